```python
import math
import jax, jax.numpy as jnp
from jax import lax
import numpy as np

D_MODEL = 1024
BATCH = 16
SEQ = 256
DEPTH = 2
DEC_BATCH = 4
DEC_SEQ = 2048
PAST_LEN = 256

GRID_W = 64
N_AB = (DEPTH + 1) // 2
N_C = DEPTH // 2
EPS = 1e-6
RW_HEADS = 8
RW_HEAD = 64
RW_DIM = RW_HEADS * RW_HEAD
W_LORA = 64
A_LORA = 64
G_LORA = 128
LNX_EPS = 64e-5
RW_IN = 3 * RW_DIM + W_LORA + A_LORA + G_LORA
RW_SPLITS = (RW_DIM, 2 * RW_DIM, 3 * RW_DIM, 3 * RW_DIM + W_LORA, 3 * RW_DIM + W_LORA + A_LORA)
MLA_HEADS = 4
QK_NOPE = 128
QK_ROPE = 64
QK_HEAD = QK_NOPE + QK_ROPE
V_HEAD = 128
Q_LORA = 256
KV_LORA = 128
MLA_DIM = MLA_HEADS * V_HEAD
MLA_IN = Q_LORA + KV_LORA + QK_ROPE
AB_IN = RW_IN + MLA_IN
ROPE_THETA = 10000.0
Q_BLOCK = 128
HY_SHORT = 3
HY_BANDS = 16
HY_EMB = 1 + 2 * HY_BANDS
HY_HIDDEN = 64
HY_TARGET = 1e-2
HY_FAST = 0.3
HY_SLOW = 1.5
PEER_KEYS = 128
PEER_N = PEER_KEYS * PEER_KEYS
PEER_HEADS = 8
PEER_DKEY = 256
PEER_TOPK = 16
TOKEN_BLOCK = 128

kernel_name = 'rwkv7_mla_hyena_peer_prefix_diffusion_step'


def rmsnorm(x, g):
    xf = x.astype(jnp.float32)
    y = xf * lax.rsqrt(jnp.mean(xf * xf, axis=-1, keepdims=True) + EPS)
    return (y * g.astype(jnp.float32)).astype(x.dtype)


def modulate(h, shift, scale):
    return h * (1.0 + scale) + shift


def centred_shift(x):
    prev = jnp.pad(x[:, :-1], ((0, 0), (1, 0), (0, 0)))
    nxt = jnp.pad(x[:, 1:], ((0, 0), (0, 1), (0, 0)))
    return 0.5 * (prev + nxt)


def centred_dwconv3(u, w, b):
    prev = jnp.pad(u[:, :-1], ((0, 0), (1, 0), (0, 0)))
    nxt = jnp.pad(u[:, 1:], ((0, 0), (0, 1), (0, 0)))
    return prev * w[0] + u * w[1] + nxt * w[2] + b


def axial_rope(L):
    rows = L // GRID_W
    row = jnp.repeat(jnp.arange(rows, dtype=jnp.float32), GRID_W)
    col = jnp.tile(jnp.arange(GRID_W, dtype=jnp.float32), rows)
    n_freq = QK_ROPE // 4
    inv = ROPE_THETA ** (-jnp.arange(n_freq, dtype=jnp.float32) / n_freq)
    ang = jnp.concatenate([row[:, None] * inv, col[:, None] * inv], axis=-1)
    return jnp.cos(ang), jnp.sin(ang)


def rope_tail(x, cos, sin):
    xn, xr = x[..., :QK_NOPE], x[..., QK_NOPE:]
    x1, x2 = xr[..., 0::2], xr[..., 1::2]
    cs = cos[None, :, None, :].astype(x.dtype)
    sn = sin[None, :, None, :].astype(x.dtype)
    rot = jnp.stack([x1 * cs - x2 * sn, x1 * sn + x2 * cs], axis=-1).reshape(xr.shape)
    return jnp.concatenate([xn, rot], axis=-1)


def attend(q, k, v):
    B, Lq, H, Dh = q.shape
    nb = Lq // Q_BLOCK
    qb = q.reshape(B, nb, Q_BLOCK, H, Dh).transpose(1, 0, 2, 3, 4)
    scale = Dh ** -0.5

    def blk(qi):
        s = jnp.einsum('bqhd,bkhd->bhqk', qi, k).astype(jnp.float32) * scale
        p = jax.nn.softmax(s, axis=-1).astype(v.dtype)
        return jnp.einsum('bhqk,bkhd->bqhd', p, v)

    o = lax.map(blk, qb)
    return o.transpose(1, 0, 2, 3, 4).reshape(B, Lq, H * v.shape[-1])


def rwkv7_scan(r, w, k, v, a, b, s0, reverse):
    def step(s, inp):
        r_t, w_t, k_t, v_t, a_t, b_t = inp
        sa = jnp.einsum('bhvk,bhk->bhv', s, a_t)
        s = s * w_t[:, :, None, :] + sa[..., None] * b_t[:, :, None, :] + v_t[..., None] * k_t[:, :, None, :]
        return s, jnp.einsum('bhvk,bhk->bhv', s, r_t)

    xs = tuple(jnp.swapaxes(t.astype(jnp.float32), 0, 1) for t in (r, w, k, v, a, b))
    s_fin, ys = lax.scan(step, s0.astype(jnp.float32), xs, reverse=reverse)
    return jnp.swapaxes(ys, 0, 1), s_fin


def mla_keys(ckv, k_pe, kv_up, kn):
    B, L, _ = ckv.shape
    kv = (ckv @ kv_up).reshape(B, L, MLA_HEADS, QK_NOPE + V_HEAD)
    k_rope = jnp.broadcast_to(k_pe[:, :, None, :], (B, L, MLA_HEADS, QK_ROPE))
    k = rmsnorm(jnp.concatenate([kv[..., :QK_NOPE], k_rope], axis=-1), kn)
    return k, kv[..., QK_NOPE:]


def rwkv_mla_mixer(h, ctx, p):
    (w_in, mu, w0, w2, a0, a2, g2, k_k, k_a, r_k, lnx_g, lnx_b,
     q_norm, q_up, kv_norm, kv_up, qn, kn, w_out) = p
    B, L, _ = h.shape
    f32 = jnp.float32
    proj = h @ w_in
    rw, mla = proj[..., :RW_IN], proj[..., RW_IN:]
    rw = rw + mu * (centred_shift(rw) - rw)
    r, k, v, wd, ad, gd = jnp.split(rw, RW_SPLITS, axis=-1)
    heads = lambda t: t.reshape(B, L, RW_HEADS, RW_HEAD)
    rh, vh = heads(r.astype(f32)), heads(v.astype(f32))
    kk = heads((k * k_k).astype(f32))
    kk = kk / jnp.maximum(jnp.linalg.norm(kk, axis=-1, keepdims=True), 1e-12)
    if ctx is None:
        z = jnp.zeros((B, RW_HEADS, RW_HEAD, RW_HEAD), f32)
        s0s = (z, z)
    else:
        s0s = (ctx[0], ctx[1])
    ys, finals, bonus = [], [], []
    for d in range(2):
        w_raw = (w0[d] + jnp.tanh(wd) @ w2[d]).astype(f32)
        decay = jnp.exp(-jnp.exp(-jax.nn.softplus(-w_raw) - 0.5))
        lr = jax.nn.sigmoid((a0[d] + ad @ a2[d]).astype(f32))
        kd = heads(k.astype(f32) * (1.0 + (lr - 1.0) * k_a))
        yd, sd = rwkv7_scan(rh, heads(decay), kd, vh, -kk, kk * heads(lr), s0s[d], d == 1)
        ys.append(yd)
        finals.append(sd)
        bonus.append(jnp.sum(rh * kd * r_k, axis=-1, keepdims=True) * vh)
    y = ys[0] + ys[1]
    mean = jnp.mean(y, axis=-1, keepdims=True)
    var = jnp.mean(jnp.square(y - mean), axis=-1, keepdims=True)
    yn = ((y - mean) * lax.rsqrt(var + LNX_EPS)).reshape(B, L, RW_DIM) * lnx_g + lnx_b
    gate = jax.nn.sigmoid(gd) @ g2
    rw_out = ((yn + (bonus[0] + bonus[1]).reshape(B, L, RW_DIM)) * gate).astype(h.dtype)
    q_c, kv_c, k_pe = jnp.split(mla, (Q_LORA, Q_LORA + KV_LORA), axis=-1)
    q = rmsnorm((rmsnorm(q_c, q_norm) @ q_up).reshape(B, L, MLA_HEADS, QK_HEAD), qn)
    ckv = rmsnorm(kv_c, kv_norm)
    k_own, v_own = mla_keys(ckv, k_pe, kv_up, kn)
    if ctx is None:
        keys, vals = k_own, v_own
    else:
        cos, sin = ctx[4]
        q = rope_tail(q, cos, sin)
        k_own = rope_tail(k_own, cos, sin)
        k_ctx, v_ctx = mla_keys(ctx[2], ctx[3], kv_up, kn)
        keys = jnp.concatenate([k_ctx, k_own], axis=1)
        vals = jnp.concatenate([v_ctx, v_own], axis=1)
    mla_out = attend(q, keys, vals)
    out = jnp.concatenate([rw_out, mla_out.astype(h.dtype)], axis=-1) @ w_out
    return out, finals[0], finals[1], ckv, k_pe


def hyena_filters(L, w1, b1, w2, b2, w3, freq):
    f32 = jnp.float32
    t = jnp.arange(L, dtype=f32)[:, None]
    t_unit = t / (L - 1)
    bands = jnp.linspace(1e-4, HY_BANDS - 1, HY_BANDS, dtype=f32)
    ang = 2.0 * math.pi * t * bands / L
    zpos = jnp.concatenate([t_unit, jnp.cos(ang), -jnp.sin(ang)], axis=-1)
    fr = freq.astype(f32)
    hid = jnp.sin(fr * (zpos @ w1.astype(f32) + b1.astype(f32)))
    hid = jnp.sin(fr * (hid @ w2.astype(f32) + b2.astype(f32)))
    filt = hid @ w3.astype(f32)
    deltas = jnp.linspace(math.log(HY_TARGET) / HY_FAST, math.log(HY_TARGET) / HY_SLOW, D_MODEL, dtype=f32)
    window = jnp.exp(-t_unit * jnp.abs(deltas))
    h_f = filt[:, :D_MODEL] * window
    h_b = filt[:, D_MODEL:] * window
    circ = jnp.concatenate([h_f, jnp.zeros((1, D_MODEL), f32), h_b[:0:-1]], axis=0)
    return circ / jnp.sum(jnp.abs(circ), axis=0, keepdims=True)


def hyena_mixer(h, p):
    w_in, b_in, conv_w, conv_b, f_w1, f_b1, f_w2, f_b2, f_w3, f_freq, bias, w_out = p
    B, L, _ = h.shape
    u = centred_dwconv3(h @ w_in + b_in, conv_w, conv_b)
    x0, x1, v = jnp.split(u, 3, axis=-1)
    zin = (x1 * v).astype(jnp.float32)
    filt_f = jnp.fft.rfft(hyena_filters(L, f_w1, f_b1, f_w2, f_b2, f_w3, f_freq), axis=0)
    conv = jnp.fft.irfft(jnp.fft.rfft(zin, n=2 * L, axis=1) * filt_f, n=2 * L, axis=1)[:, :L]
    y = conv + zin * bias.astype(jnp.float32)
    return (x0 * y.astype(h.dtype)) @ w_out


def peer_ffn(h, w_q, sub_keys, u_tab, v_tab):
    B, L, D = h.shape
    nb = (B * L) // TOKEN_BLOCK

    def blk(xb):
        q = (xb @ w_q).reshape(TOKEN_BLOCK, PEER_HEADS, 2, PEER_DKEY // 2)
        s = jnp.einsum('thpd,hpnd->thpn', q, sub_keys).astype(jnp.float32)
        s1, i1 = lax.top_k(s[:, :, 0], PEER_TOPK)
        s2, i2 = lax.top_k(s[:, :, 1], PEER_TOPK)
        cand = (s1[..., :, None] + s2[..., None, :]).reshape(TOKEN_BLOCK, PEER_HEADS, PEER_TOPK * PEER_TOPK)
        cidx = (i1[..., :, None] * PEER_KEYS + i2[..., None, :]).reshape(TOKEN_BLOCK, PEER_HEADS, PEER_TOPK * PEER_TOPK)
        top_s, pos = lax.top_k(cand, PEER_TOPK)
        idx = jnp.take_along_axis(cidx, pos, axis=-1)
        g = jax.nn.softmax(top_s, axis=-1).astype(xb.dtype)
        act = jax.nn.gelu(jnp.einsum('thkd,td->thk', u_tab[idx], xb))
        return jnp.einsum('thk,thkd->td', g * act, v_tab[idx])

    out = lax.map(blk, h.reshape(nb, TOKEN_BLOCK, D))
    return out.reshape(B, L, D)


def setup_inputs(seed: int = 0) -> dict:
    key = jax.random.key(seed)
    keys = iter(jax.random.split(key, 64))

    def nrm(shape, scale):
        return jax.random.normal(next(keys), shape, jnp.float32) * scale

    def gain(shape):
        return 1.0 + nrm(shape, 0.02)

    D = D_MODEL
    return {
        'x_prompt': nrm((BATCH, SEQ, D), 1.0),
        'x_sample': nrm((DEC_BATCH, DEC_SEQ, D), 1.0),
        'state_rwkv_fwd': nrm((DEC_BATCH, N_AB, RW_HEADS, RW_HEAD, RW_HEAD), 0.5),
        'state_rwkv_bwd': nrm((DEC_BATCH, N_AB, RW_HEADS, RW_HEAD, RW_HEAD), 0.5),
        'cache_mla_ckv': nrm((DEC_BATCH, N_AB, PAST_LEN, KV_LORA), 1.0),
        'cache_mla_kpe': nrm((DEC_BATCH, N_AB, PAST_LEN, QK_ROPE), 1.0),
        'c': nrm((DEC_BATCH, D), 1.0),
        'c_ctx': nrm((D,), 1.0),
        'norm_g': gain((DEPTH, 2, D)),
        'w_mod': nrm((DEPTH, D, 6 * D), D ** -0.5),
        'b_mod': nrm((DEPTH, 6 * D), 0.02),
        'ab_w_in': nrm((N_AB, D, AB_IN), D ** -0.5),
        'rw_mu': jax.random.uniform(next(keys), (N_AB, RW_IN), jnp.float32, 0.1, 0.9),
        'rw_w0': nrm((N_AB, 2, RW_DIM), 0.5),
        'rw_w2': nrm((N_AB, 2, W_LORA, RW_DIM), 0.1),
        'rw_a0': nrm((N_AB, 2, RW_DIM), 0.3),
        'rw_a2': nrm((N_AB, 2, A_LORA, RW_DIM), 0.1),
        'rw_g2': nrm((N_AB, G_LORA, RW_DIM), G_LORA ** -0.5),
        'rw_k_k': 0.85 + nrm((N_AB, RW_DIM), 0.02),
        'rw_k_a': gain((N_AB, RW_DIM)),
        'rw_r_k': nrm((N_AB, RW_HEADS, RW_HEAD), 0.1),
        'rw_lnx_g': gain((N_AB, RW_DIM)),
        'rw_lnx_b': nrm((N_AB, RW_DIM), 0.02),
        'mla_q_norm': gain((N_AB, Q_LORA)),
        'mla_q_up': nrm((N_AB, Q_LORA, MLA_HEADS * QK_HEAD), Q_LORA ** -0.5),
        'mla_kv_norm': gain((N_AB, KV_LORA)),
        'mla_kv_up': nrm((N_AB, KV_LORA, MLA_HEADS * (QK_NOPE + V_HEAD)), KV_LORA ** -0.5),
        'mla_qn': gain((N_AB, QK_HEAD)),
        'mla_kn': gain((N_AB, QK_HEAD)),
        'ab_w_out': nrm((N_AB, RW_DIM + MLA_DIM, D), (RW_DIM + MLA_DIM) ** -0.5),
        'hy_w_in': nrm((N_C, D, 3 * D), D ** -0.5),
        'hy_b_in': nrm((N_C, 3 * D), 0.02),
        'hy_conv_w': nrm((N_C, HY_SHORT, 3 * D), 0.5),
        'hy_conv_b': nrm((N_C, 3 * D), 0.02),
        'hy_f_w1': nrm((N_C, HY_EMB, HY_HIDDEN), HY_EMB ** -0.5),
        'hy_f_b1': nrm((N_C, HY_HIDDEN), 0.1),
        'hy_f_w2': nrm((N_C, HY_HIDDEN, HY_HIDDEN), HY_HIDDEN ** -0.5),
        'hy_f_b2': nrm((N_C, HY_HIDDEN), 0.1),
        'hy_f_w3': nrm((N_C, HY_HIDDEN, 2 * D), HY_HIDDEN ** -0.5),
        'hy_f_freq': gain((N_C, HY_HIDDEN)),
        'hy_bias': nrm((N_C, D), 0.1),
        'hy_w_out': nrm((N_C, D, D), D ** -0.5),
        'peer_w_q': nrm((DEPTH, D, PEER_HEADS * PEER_DKEY), D ** -0.5),
        'peer_keys': nrm((DEPTH, PEER_HEADS, 2, PEER_KEYS, PEER_DKEY // 2), (PEER_DKEY // 2) ** -0.5),
        'peer_u': nrm((DEPTH, PEER_N, D), D ** -0.5),
        'peer_v': nrm((DEPTH, PEER_N, D), 0.05),
    }


def reference(x_prompt, x_sample, state_rwkv_fwd, state_rwkv_bwd, cache_mla_ckv, cache_mla_kpe, c, c_ctx,
              norm_g, w_mod, b_mod, ab_w_in, rw_mu, rw_w0, rw_w2, rw_a0, rw_a2, rw_g2, rw_k_k, rw_k_a, rw_r_k,
              rw_lnx_g, rw_lnx_b, mla_q_norm, mla_q_up, mla_kv_norm, mla_kv_up, mla_qn, mla_kn, ab_w_out,
              hy_w_in, hy_b_in, hy_conv_w, hy_conv_b, hy_f_w1, hy_f_b1, hy_f_w2, hy_f_b2, hy_f_w3, hy_f_freq,
              hy_bias, hy_w_out, peer_w_q, peer_keys, peer_u, peer_v):
    rope = axial_rope(x_sample.shape[1])
    xp, xs = x_prompt, x_sample
    st_f, st_b, st_ckv, st_kpe = [], [], [], []
    for li in range(DEPTH):
        j = li // 2
        mp = jnp.split(jax.nn.silu(c_ctx) @ w_mod[li] + b_mod[li], 6, axis=-1)
        ms = [m[:, None, :] for m in jnp.split(jax.nn.silu(c) @ w_mod[li] + b_mod[li], 6, axis=-1)]
        hp = modulate(rmsnorm(xp, norm_g[li, 0]), mp[0], mp[1])
        hs = modulate(rmsnorm(xs, norm_g[li, 0]), ms[0], ms[1])
        if li % 2 == 0:
            p = (ab_w_in[j], rw_mu[j], rw_w0[j], rw_w2[j], rw_a0[j], rw_a2[j], rw_g2[j], rw_k_k[j], rw_k_a[j],
                 rw_r_k[j], rw_lnx_g[j], rw_lnx_b[j], mla_q_norm[j], mla_q_up[j], mla_kv_norm[j], mla_kv_up[j],
                 mla_qn[j], mla_kn[j], ab_w_out[j])
            op, sf, sb, ckv, kpe = rwkv_mla_mixer(hp, None, p)
            ctx = (state_rwkv_fwd[:, j], state_rwkv_bwd[:, j], cache_mla_ckv[:, j], cache_mla_kpe[:, j], rope)
            os_, _, _, _, _ = rwkv_mla_mixer(hs, ctx, p)
            st_f.append(sf)
            st_b.append(sb)
            st_ckv.append(ckv)
            st_kpe.append(kpe)
        else:
            p = (hy_w_in[j], hy_b_in[j], hy_conv_w[j], hy_conv_b[j], hy_f_w1[j], hy_f_b1[j], hy_f_w2[j],
                 hy_f_b2[j], hy_f_w3[j], hy_f_freq[j], hy_bias[j], hy_w_out[j])
            op = hyena_mixer(hp, p)
            os_ = hyena_mixer(hs, p)
        xp = xp + mp[2] * op
        xs = xs + ms[2] * os_
        pp = (peer_w_q[li], peer_keys[li], peer_u[li], peer_v[li])
        xp = xp + mp[5] * peer_ffn(modulate(rmsnorm(xp, norm_g[li, 1]), mp[3], mp[4]), *pp)
        xs = xs + ms[5] * peer_ffn(modulate(rmsnorm(xs, norm_g[li, 1]), ms[3], ms[4]), *pp)
    new_state_rwkv_fwd = jnp.stack(st_f, axis=1).astype(x_prompt.dtype)
    new_state_rwkv_bwd = jnp.stack(st_b, axis=1).astype(x_prompt.dtype)
    new_cache_mla_ckv = jnp.stack(st_ckv, axis=1)
    new_cache_mla_kpe = jnp.stack(st_kpe, axis=1)
    return (xp, xs, new_state_rwkv_fwd, new_state_rwkv_bwd, new_cache_mla_ckv, new_cache_mla_kpe)
```

```python
import math
from functools import partial

import jax
import jax.numpy as jnp
from jax import lax
from jax.experimental import pallas as pl
from jax.experimental.pallas import tpu as pltpu

D_MODEL = 1024
DEPTH = 2
GRID_W = 64
EPS = 1e-6
RW_HEADS = 8
RW_HEAD = 64
RW_DIM = RW_HEADS * RW_HEAD
W_LORA = 64
A_LORA = 64
G_LORA = 128
LNX_EPS = 64e-5
RW_IN = 3 * RW_DIM + W_LORA + A_LORA + G_LORA
RW_SPLITS = (RW_DIM, 2 * RW_DIM, 3 * RW_DIM, 3 * RW_DIM + W_LORA, 3 * RW_DIM + W_LORA + A_LORA)
MLA_HEADS = 4
QK_NOPE = 128
QK_ROPE = 64
QK_HEAD = QK_NOPE + QK_ROPE
V_HEAD = 128
Q_LORA = 256
KV_LORA = 128
MLA_DIM = MLA_HEADS * V_HEAD
ROPE_THETA = 10000.0
Q_BLOCK = 128
HY_BANDS = 16
HY_TARGET = 1e-2
HY_FAST = 0.3
HY_SLOW = 1.5
PEER_KEYS = 128
PEER_HEADS = 8
PEER_DKEY = 256
PEER_TOPK = 16
TOKEN_BLOCK = 128


def _mm_kernel(a_ref, b_ref, o_ref):
    o_ref[...] = jnp.dot(a_ref[...].astype(jnp.bfloat16), b_ref[...].astype(jnp.bfloat16),
                         preferred_element_type=jnp.float32)


def _mm(a, b, tm=512, tn=512):
    lead = a.shape[:-1]
    K = a.shape[-1]
    N = b.shape[-1]
    a2 = a.reshape(-1, K)
    M = a2.shape[0]
    tm = min(tm, M)
    tn = min(tn, N)
    if N % tn:
        tn = N
    assert M % tm == 0 and N % tn == 0
    out = pl.pallas_call(
        _mm_kernel,
        grid=(M // tm, N // tn),
        in_specs=[pl.BlockSpec((tm, K), lambda i, j: (i, 0)),
                  pl.BlockSpec((K, tn), lambda i, j: (0, j))],
        out_specs=pl.BlockSpec((tm, tn), lambda i, j: (i, j)),
        out_shape=jax.ShapeDtypeStruct((M, N), jnp.float32),
    )(a2, b)
    return out.reshape(*lead, N)


def rmsnorm(x, g):
    xf = x.astype(jnp.float32)
    y = xf * lax.rsqrt(jnp.mean(xf * xf, axis=-1, keepdims=True) + EPS)
    return (y * g.astype(jnp.float32)).astype(x.dtype)


def modulate(h, shift, scale):
    return h * (1.0 + scale) + shift


def centred_shift(x):
    prev = jnp.pad(x[:, :-1], ((0, 0), (1, 0), (0, 0)))
    nxt = jnp.pad(x[:, 1:], ((0, 0), (0, 1), (0, 0)))
    return 0.5 * (prev + nxt)


def centred_dwconv3(u, w, b):
    prev = jnp.pad(u[:, :-1], ((0, 0), (1, 0), (0, 0)))
    nxt = jnp.pad(u[:, 1:], ((0, 0), (0, 1), (0, 0)))
    return prev * w[0] + u * w[1] + nxt * w[2] + b


def axial_rope(L):
    rows = L // GRID_W
    row = jnp.repeat(jnp.arange(rows, dtype=jnp.float32), GRID_W)
    col = jnp.tile(jnp.arange(GRID_W, dtype=jnp.float32), rows)
    n_freq = QK_ROPE // 4
    inv = ROPE_THETA ** (-jnp.arange(n_freq, dtype=jnp.float32) / n_freq)
    ang = jnp.concatenate([row[:, None] * inv, col[:, None] * inv], axis=-1)
    return jnp.cos(ang), jnp.sin(ang)


def rope_tail(x, cos, sin):
    xn, xr = x[..., :QK_NOPE], x[..., QK_NOPE:]
    x1, x2 = xr[..., 0::2], xr[..., 1::2]
    cs = cos[None, :, None, :].astype(x.dtype)
    sn = sin[None, :, None, :].astype(x.dtype)
    rot = jnp.stack([x1 * cs - x2 * sn, x1 * sn + x2 * cs], axis=-1).reshape(xr.shape)
    return jnp.concatenate([xn, rot], axis=-1)


def attend(q, k, v):
    B, Lq, H, Dh = q.shape
    nb = Lq // Q_BLOCK
    qb = q.reshape(B, nb, Q_BLOCK, H, Dh).transpose(1, 0, 2, 3, 4)
    scale = Dh ** -0.5

    def blk(qi):
        s = jnp.einsum('bqhd,bkhd->bhqk', qi, k).astype(jnp.float32) * scale
        p = jax.nn.softmax(s, axis=-1).astype(v.dtype)
        return jnp.einsum('bhqk,bkhd->bqhd', p, v)

    o = lax.map(blk, qb)
    return o.transpose(1, 0, 2, 3, 4).reshape(B, Lq, H * v.shape[-1])


def rwkv7_scan(r, w, k, v, a, b, s0, reverse):
    def step(s, inp):
        r_t, w_t, k_t, v_t, a_t, b_t = inp
        sa = jnp.einsum('bhvk,bhk->bhv', s, a_t)
        s = s * w_t[:, :, None, :] + sa[..., None] * b_t[:, :, None, :] + v_t[..., None] * k_t[:, :, None, :]
        return s, jnp.einsum('bhvk,bhk->bhv', s, r_t)

    xs = tuple(jnp.swapaxes(t.astype(jnp.float32), 0, 1) for t in (r, w, k, v, a, b))
    s_fin, ys = lax.scan(step, s0.astype(jnp.float32), xs, reverse=reverse)
    return jnp.swapaxes(ys, 0, 1), s_fin


def mla_keys(ckv, k_pe, kv_up, kn):
    B, L, _ = ckv.shape
    kv = (ckv @ kv_up).reshape(B, L, MLA_HEADS, QK_NOPE + V_HEAD)
    k_rope = jnp.broadcast_to(k_pe[:, :, None, :], (B, L, MLA_HEADS, QK_ROPE))
    k = rmsnorm(jnp.concatenate([kv[..., :QK_NOPE], k_rope], axis=-1), kn)
    return k, kv[..., QK_NOPE:]


def rwkv_mla_mixer(h, ctx, p):
    (w_in, mu, w0, w2, a0, a2, g2, k_k, k_a, r_k, lnx_g, lnx_b,
     q_norm, q_up, kv_norm, kv_up, qn, kn, w_out) = p
    B, L, _ = h.shape
    f32 = jnp.float32
    proj = _mm(h, w_in)
    rw, mla = proj[..., :RW_IN], proj[..., RW_IN:]
    rw = rw + mu * (centred_shift(rw) - rw)
    r, k, v, wd, ad, gd = jnp.split(rw, RW_SPLITS, axis=-1)
    heads = lambda t: t.reshape(B, L, RW_HEADS, RW_HEAD)
    rh, vh = heads(r.astype(f32)), heads(v.astype(f32))
    kk = heads((k * k_k).astype(f32))
    kk = kk / jnp.maximum(jnp.linalg.norm(kk, axis=-1, keepdims=True), 1e-12)
    if ctx is None:
        z = jnp.zeros((B, RW_HEADS, RW_HEAD, RW_HEAD), f32)
        s0s = (z, z)
    else:
        s0s = (ctx[0], ctx[1])
    ys, finals, bonus = [], [], []
    for d in range(2):
        w_raw = (w0[d] + jnp.tanh(wd) @ w2[d]).astype(f32)
        decay = jnp.exp(-jnp.exp(-jax.nn.softplus(-w_raw) - 0.5))
        lr = jax.nn.sigmoid((a0[d] + ad @ a2[d]).astype(f32))
        kd = heads(k.astype(f32) * (1.0 + (lr - 1.0) * k_a))
        yd, sd = rwkv7_scan(rh, heads(decay), kd, vh, -kk, kk * heads(lr), s0s[d], d == 1)
        ys.append(yd)
        finals.append(sd)
        bonus.append(jnp.sum(rh * kd * r_k, axis=-1, keepdims=True) * vh)
    y = ys[0] + ys[1]
    mean = jnp.mean(y, axis=-1, keepdims=True)
    var = jnp.mean(jnp.square(y - mean), axis=-1, keepdims=True)
    yn = ((y - mean) * lax.rsqrt(var + LNX_EPS)).reshape(B, L, RW_DIM) * lnx_g + lnx_b
    gate = jax.nn.sigmoid(gd) @ g2
    rw_out = ((yn + (bonus[0] + bonus[1]).reshape(B, L, RW_DIM)) * gate).astype(h.dtype)
    q_c, kv_c, k_pe = jnp.split(mla, (Q_LORA, Q_LORA + KV_LORA), axis=-1)
    q = rmsnorm((rmsnorm(q_c, q_norm) @ q_up).reshape(B, L, MLA_HEADS, QK_HEAD), qn)
    ckv = rmsnorm(kv_c, kv_norm)
    k_own, v_own = mla_keys(ckv, k_pe, kv_up, kn)
    if ctx is None:
        keys, vals = k_own, v_own
    else:
        cos, sin = ctx[4]
        q = rope_tail(q, cos, sin)
        k_own = rope_tail(k_own, cos, sin)
        k_ctx, v_ctx = mla_keys(ctx[2], ctx[3], kv_up, kn)
        keys = jnp.concatenate([k_ctx, k_own], axis=1)
        vals = jnp.concatenate([v_ctx, v_own], axis=1)
    mla_out = attend(q, keys, vals)
    out = _mm(jnp.concatenate([rw_out, mla_out.astype(h.dtype)], axis=-1), w_out)
    return out, finals[0], finals[1], ckv, k_pe


def hyena_filters(L, w1, b1, w2, b2, w3, freq):
    f32 = jnp.float32
    t = jnp.arange(L, dtype=f32)[:, None]
    t_unit = t / (L - 1)
    bands = jnp.linspace(1e-4, HY_BANDS - 1, HY_BANDS, dtype=f32)
    ang = 2.0 * math.pi * t * bands / L
    zpos = jnp.concatenate([t_unit, jnp.cos(ang), -jnp.sin(ang)], axis=-1)
    fr = freq.astype(f32)
    hid = jnp.sin(fr * (zpos @ w1.astype(f32) + b1.astype(f32)))
    hid = jnp.sin(fr * (hid @ w2.astype(f32) + b2.astype(f32)))
    filt = hid @ w3.astype(f32)
    deltas = jnp.linspace(math.log(HY_TARGET) / HY_FAST, math.log(HY_TARGET) / HY_SLOW, D_MODEL, dtype=f32)
    window = jnp.exp(-t_unit * jnp.abs(deltas))
    h_f = filt[:, :D_MODEL] * window
    h_b = filt[:, D_MODEL:] * window
    circ = jnp.concatenate([h_f, jnp.zeros((1, D_MODEL), f32), h_b[:0:-1]], axis=0)
    return circ / jnp.sum(jnp.abs(circ), axis=0, keepdims=True)


def hyena_mixer(h, p):
    w_in, b_in, conv_w, conv_b, f_w1, f_b1, f_w2, f_b2, f_w3, f_freq, bias, w_out = p
    B, L, _ = h.shape
    u = centred_dwconv3(_mm(h, w_in) + b_in, conv_w, conv_b)
    x0, x1, v = jnp.split(u, 3, axis=-1)
    zin = (x1 * v).astype(jnp.float32)
    filt_f = jnp.fft.rfft(hyena_filters(L, f_w1, f_b1, f_w2, f_b2, f_w3, f_freq), axis=0)
    conv = jnp.fft.irfft(jnp.fft.rfft(zin, n=2 * L, axis=1) * filt_f, n=2 * L, axis=1)[:, :L]
    y = conv + zin * bias.astype(jnp.float32)
    return _mm(x0 * y.astype(h.dtype), w_out)


def peer_ffn(h, w_q, sub_keys, u_tab, v_tab):
    B, L, D = h.shape
    nb = (B * L) // TOKEN_BLOCK

    def blk(xb):
        q = (xb @ w_q).reshape(TOKEN_BLOCK, PEER_HEADS, 2, PEER_DKEY // 2)
        s = jnp.einsum('thpd,hpnd->thpn', q, sub_keys).astype(jnp.float32)
        s1, i1 = lax.top_k(s[:, :, 0], PEER_TOPK)
        s2, i2 = lax.top_k(s[:, :, 1], PEER_TOPK)
        cand = (s1[..., :, None] + s2[..., None, :]).reshape(TOKEN_BLOCK, PEER_HEADS, PEER_TOPK * PEER_TOPK)
        cidx = (i1[..., :, None] * PEER_KEYS + i2[..., None, :]).reshape(TOKEN_BLOCK, PEER_HEADS, PEER_TOPK * PEER_TOPK)
        top_s, pos = lax.top_k(cand, PEER_TOPK)
        idx = jnp.take_along_axis(cidx, pos, axis=-1)
        g = jax.nn.softmax(top_s, axis=-1).astype(xb.dtype)
        act = jax.nn.gelu(jnp.einsum('thkd,td->thk', u_tab[idx], xb))
        return jnp.einsum('thk,thkd->td', g * act, v_tab[idx])

    out = lax.map(blk, h.reshape(nb, TOKEN_BLOCK, D))
    return out.reshape(B, L, D)


def kernel(x_prompt, x_sample, state_rwkv_fwd, state_rwkv_bwd, cache_mla_ckv, cache_mla_kpe, c, c_ctx,
           norm_g, w_mod, b_mod, ab_w_in, rw_mu, rw_w0, rw_w2, rw_a0, rw_a2, rw_g2, rw_k_k, rw_k_a, rw_r_k,
           rw_lnx_g, rw_lnx_b, mla_q_norm, mla_q_up, mla_kv_norm, mla_kv_up, mla_qn, mla_kn, ab_w_out,
           hy_w_in, hy_b_in, hy_conv_w, hy_conv_b, hy_f_w1, hy_f_b1, hy_f_w2, hy_f_b2, hy_f_w3, hy_f_freq,
           hy_bias, hy_w_out, peer_w_q, peer_keys, peer_u, peer_v):
    rope = axial_rope(x_sample.shape[1])
    xp, xs = x_prompt, x_sample
    st_f, st_b, st_ckv, st_kpe = [], [], [], []
    for li in range(DEPTH):
        j = li // 2
        mp = jnp.split(jax.nn.silu(c_ctx) @ w_mod[li] + b_mod[li], 6, axis=-1)
        ms = [m[:, None, :] for m in jnp.split(jax.nn.silu(c) @ w_mod[li] + b_mod[li], 6, axis=-1)]
        hp = modulate(rmsnorm(xp, norm_g[li, 0]), mp[0], mp[1])
        hs = modulate(rmsnorm(xs, norm_g[li, 0]), ms[0], ms[1])
        if li % 2 == 0:
            p = (ab_w_in[j], rw_mu[j], rw_w0[j], rw_w2[j], rw_a0[j], rw_a2[j], rw_g2[j], rw_k_k[j], rw_k_a[j],
                 rw_r_k[j], rw_lnx_g[j], rw_lnx_b[j], mla_q_norm[j], mla_q_up[j], mla_kv_norm[j], mla_kv_up[j],
                 mla_qn[j], mla_kn[j], ab_w_out[j])
            op, sf, sb, ckv, kpe = rwkv_mla_mixer(hp, None, p)
            ctx = (state_rwkv_fwd[:, j], state_rwkv_bwd[:, j], cache_mla_ckv[:, j], cache_mla_kpe[:, j], rope)
            os_, _, _, _, _ = rwkv_mla_mixer(hs, ctx, p)
            st_f.append(sf)
            st_b.append(sb)
            st_ckv.append(ckv)
            st_kpe.append(kpe)
        else:
            p = (hy_w_in[j], hy_b_in[j], hy_conv_w[j], hy_conv_b[j], hy_f_w1[j], hy_f_b1[j], hy_f_w2[j],
                 hy_f_b2[j], hy_f_w3[j], hy_f_freq[j], hy_bias[j], hy_w_out[j])
            op = hyena_mixer(hp, p)
            os_ = hyena_mixer(hs, p)
        xp = xp + mp[2] * op
        xs = xs + ms[2] * os_
        pp = (peer_w_q[li], peer_keys[li], peer_u[li], peer_v[li])
        xp = xp + mp[5] * peer_ffn(modulate(rmsnorm(xp, norm_g[li, 1]), mp[3], mp[4]), *pp)
        xs = xs + ms[5] * peer_ffn(modulate(rmsnorm(xs, norm_g[li, 1]), ms[3], ms[4]), *pp)
    new_state_rwkv_fwd = jnp.stack(st_f, axis=1).astype(x_prompt.dtype)
    new_state_rwkv_bwd = jnp.stack(st_b, axis=1).astype(x_prompt.dtype)
    new_cache_mla_ckv = jnp.stack(st_ckv, axis=1)
    new_cache_mla_kpe = jnp.stack(st_kpe, axis=1)
    return (xp, xs, new_state_rwkv_fwd, new_state_rwkv_bwd, new_cache_mla_ckv, new_cache_mla_kpe)
```

```python
import math
from functools import partial

import jax
import jax.numpy as jnp
from jax import lax
from jax.experimental import pallas as pl
from jax.experimental.pallas import tpu as pltpu

D_MODEL = 1024
DEPTH = 2
GRID_W = 64
EPS = 1e-6
RW_HEADS = 8
RW_HEAD = 64
RW_DIM = RW_HEADS * RW_HEAD
W_LORA = 64
A_LORA = 64
G_LORA = 128
LNX_EPS = 64e-5
RW_IN = 3 * RW_DIM + W_LORA + A_LORA + G_LORA
RW_SPLITS = (RW_DIM, 2 * RW_DIM, 3 * RW_DIM, 3 * RW_DIM + W_LORA, 3 * RW_DIM + W_LORA + A_LORA)
MLA_HEADS = 4
QK_NOPE = 128
QK_ROPE = 64
QK_HEAD = QK_NOPE + QK_ROPE
V_HEAD = 128
Q_LORA = 256
KV_LORA = 128
MLA_DIM = MLA_HEADS * V_HEAD
ROPE_THETA = 10000.0
Q_BLOCK = 128
HY_BANDS = 16
HY_TARGET = 1e-2
HY_FAST = 0.3
HY_SLOW = 1.5
PEER_KEYS = 128
PEER_HEADS = 8
PEER_DKEY = 256
PEER_TOPK = 16
TOKEN_BLOCK = 128


def _mm_kernel(a_ref, b_ref, o_ref):
    o_ref[...] = jnp.dot(a_ref[...].astype(jnp.bfloat16), b_ref[...].astype(jnp.bfloat16),
                         preferred_element_type=jnp.float32)


def _mm(a, b, tm=512, tn=512):
    lead = a.shape[:-1]
    K = a.shape[-1]
    N = b.shape[-1]
    a2 = a.reshape(-1, K)
    M = a2.shape[0]
    tm = min(tm, M)
    tn = min(tn, N)
    if N % tn:
        tn = N
    assert M % tm == 0 and N % tn == 0
    out = pl.pallas_call(
        _mm_kernel,
        grid=(M // tm, N // tn),
        in_specs=[pl.BlockSpec((tm, K), lambda i, j: (i, 0)),
                  pl.BlockSpec((K, tn), lambda i, j: (0, j))],
        out_specs=pl.BlockSpec((tm, tn), lambda i, j: (i, j)),
        out_shape=jax.ShapeDtypeStruct((M, N), jnp.float32),
    )(a2, b)
    return out.reshape(*lead, N)


def rmsnorm(x, g):
    xf = x.astype(jnp.float32)
    y = xf * lax.rsqrt(jnp.mean(xf * xf, axis=-1, keepdims=True) + EPS)
    return (y * g.astype(jnp.float32)).astype(x.dtype)


def modulate(h, shift, scale):
    return h * (1.0 + scale) + shift


def centred_shift(x):
    prev = jnp.pad(x[:, :-1], ((0, 0), (1, 0), (0, 0)))
    nxt = jnp.pad(x[:, 1:], ((0, 0), (0, 1), (0, 0)))
    return 0.5 * (prev + nxt)


def centred_dwconv3(u, w, b):
    prev = jnp.pad(u[:, :-1], ((0, 0), (1, 0), (0, 0)))
    nxt = jnp.pad(u[:, 1:], ((0, 0), (0, 1), (0, 0)))
    return prev * w[0] + u * w[1] + nxt * w[2] + b


def axial_rope(L):
    rows = L // GRID_W
    row = jnp.repeat(jnp.arange(rows, dtype=jnp.float32), GRID_W)
    col = jnp.tile(jnp.arange(GRID_W, dtype=jnp.float32), rows)
    n_freq = QK_ROPE // 4
    inv = ROPE_THETA ** (-jnp.arange(n_freq, dtype=jnp.float32) / n_freq)
    ang = jnp.concatenate([row[:, None] * inv, col[:, None] * inv], axis=-1)
    return jnp.cos(ang), jnp.sin(ang)


def rope_tail(x, cos, sin):
    xn, xr = x[..., :QK_NOPE], x[..., QK_NOPE:]
    x1, x2 = xr[..., 0::2], xr[..., 1::2]
    cs = cos[None, :, None, :].astype(x.dtype)
    sn = sin[None, :, None, :].astype(x.dtype)
    rot = jnp.stack([x1 * cs - x2 * sn, x1 * sn + x2 * cs], axis=-1).reshape(xr.shape)
    return jnp.concatenate([xn, rot], axis=-1)


def attend(q, k, v):
    B, Lq, H, Dh = q.shape
    nb = Lq // Q_BLOCK
    qb = q.reshape(B, nb, Q_BLOCK, H, Dh).transpose(1, 0, 2, 3, 4)
    scale = Dh ** -0.5

    def blk(qi):
        s = jnp.einsum('bqhd,bkhd->bhqk', qi, k).astype(jnp.float32) * scale
        p = jax.nn.softmax(s, axis=-1).astype(v.dtype)
        return jnp.einsum('bhqk,bkhd->bqhd', p, v)

    o = lax.map(blk, qb)
    return o.transpose(1, 0, 2, 3, 4).reshape(B, Lq, H * v.shape[-1])


def rwkv7_scan(r, w, k, v, a, b, s0, reverse):
    def step(s, inp):
        r_t, w_t, k_t, v_t, a_t, b_t = inp
        sa = jnp.einsum('bhvk,bhk->bhv', s, a_t)
        s = s * w_t[:, :, None, :] + sa[..., None] * b_t[:, :, None, :] + v_t[..., None] * k_t[:, :, None, :]
        return s, jnp.einsum('bhvk,bhk->bhv', s, r_t)

    xs = tuple(jnp.swapaxes(t.astype(jnp.float32), 0, 1) for t in (r, w, k, v, a, b))
    s_fin, ys = lax.scan(step, s0.astype(jnp.float32), xs, reverse=reverse)
    return jnp.swapaxes(ys, 0, 1), s_fin


def mla_keys(ckv, k_pe, kv_up, kn):
    B, L, _ = ckv.shape
    kv = (ckv @ kv_up).reshape(B, L, MLA_HEADS, QK_NOPE + V_HEAD)
    k_rope = jnp.broadcast_to(k_pe[:, :, None, :], (B, L, MLA_HEADS, QK_ROPE))
    k = rmsnorm(jnp.concatenate([kv[..., :QK_NOPE], k_rope], axis=-1), kn)
    return k, kv[..., QK_NOPE:]


def rwkv_mla_mixer(h, ctx, p):
    (w_in, mu, w0, w2, a0, a2, g2, k_k, k_a, r_k, lnx_g, lnx_b,
     q_norm, q_up, kv_norm, kv_up, qn, kn, w_out) = p
    B, L, _ = h.shape
    f32 = jnp.float32
    proj = _mm(h, w_in)
    rw, mla = proj[..., :RW_IN], proj[..., RW_IN:]
    rw = rw + mu * (centred_shift(rw) - rw)
    r, k, v, wd, ad, gd = jnp.split(rw, RW_SPLITS, axis=-1)
    heads = lambda t: t.reshape(B, L, RW_HEADS, RW_HEAD)
    rh, vh = heads(r.astype(f32)), heads(v.astype(f32))
    kk = heads((k * k_k).astype(f32))
    kk = kk / jnp.maximum(jnp.linalg.norm(kk, axis=-1, keepdims=True), 1e-12)
    if ctx is None:
        z = jnp.zeros((B, RW_HEADS, RW_HEAD, RW_HEAD), f32)
        s0s = (z, z)
    else:
        s0s = (ctx[0], ctx[1])
    ys, finals, bonus = [], [], []
    for d in range(2):
        w_raw = (w0[d] + jnp.tanh(wd) @ w2[d]).astype(f32)
        decay = jnp.exp(-jnp.exp(-jax.nn.softplus(-w_raw) - 0.5))
        lr = jax.nn.sigmoid((a0[d] + ad @ a2[d]).astype(f32))
        kd = heads(k.astype(f32) * (1.0 + (lr - 1.0) * k_a))
        yd, sd = rwkv7_scan(rh, heads(decay), kd, vh, -kk, kk * heads(lr), s0s[d], d == 1)
        ys.append(yd)
        finals.append(sd)
        bonus.append(jnp.sum(rh * kd * r_k, axis=-1, keepdims=True) * vh)
    y = ys[0] + ys[1]
    mean = jnp.mean(y, axis=-1, keepdims=True)
    var = jnp.mean(jnp.square(y - mean), axis=-1, keepdims=True)
    yn = ((y - mean) * lax.rsqrt(var + LNX_EPS)).reshape(B, L, RW_DIM) * lnx_g + lnx_b
    gate = jax.nn.sigmoid(gd) @ g2
    rw_out = ((yn + (bonus[0] + bonus[1]).reshape(B, L, RW_DIM)) * gate).astype(h.dtype)
    q_c, kv_c, k_pe = jnp.split(mla, (Q_LORA, Q_LORA + KV_LORA), axis=-1)
    q = rmsnorm((rmsnorm(q_c, q_norm) @ q_up).reshape(B, L, MLA_HEADS, QK_HEAD), qn)
    ckv = rmsnorm(kv_c, kv_norm)
    k_own, v_own = mla_keys(ckv, k_pe, kv_up, kn)
    if ctx is None:
        keys, vals = k_own, v_own
    else:
        cos, sin = ctx[4]
        q = rope_tail(q, cos, sin)
        k_own = rope_tail(k_own, cos, sin)
        k_ctx, v_ctx = mla_keys(ctx[2], ctx[3], kv_up, kn)
        keys = jnp.concatenate([k_ctx, k_own], axis=1)
        vals = jnp.concatenate([v_ctx, v_own], axis=1)
    mla_out = attend(q, keys, vals)
    out = _mm(jnp.concatenate([rw_out, mla_out.astype(h.dtype)], axis=-1), w_out)
    return out, finals[0], finals[1], ckv, k_pe


def hyena_filters(L, w1, b1, w2, b2, w3, freq):
    f32 = jnp.float32
    t = jnp.arange(L, dtype=f32)[:, None]
    t_unit = t / (L - 1)
    bands = jnp.linspace(1e-4, HY_BANDS - 1, HY_BANDS, dtype=f32)
    ang = 2.0 * math.pi * t * bands / L
    zpos = jnp.concatenate([t_unit, jnp.cos(ang), -jnp.sin(ang)], axis=-1)
    fr = freq.astype(f32)
    hid = jnp.sin(fr * (zpos @ w1.astype(f32) + b1.astype(f32)))
    hid = jnp.sin(fr * (hid @ w2.astype(f32) + b2.astype(f32)))
    filt = hid @ w3.astype(f32)
    deltas = jnp.linspace(math.log(HY_TARGET) / HY_FAST, math.log(HY_TARGET) / HY_SLOW, D_MODEL, dtype=f32)
    window = jnp.exp(-t_unit * jnp.abs(deltas))
    h_f = filt[:, :D_MODEL] * window
    h_b = filt[:, D_MODEL:] * window
    circ = jnp.concatenate([h_f, jnp.zeros((1, D_MODEL), f32), h_b[:0:-1]], axis=0)
    return circ / jnp.sum(jnp.abs(circ), axis=0, keepdims=True)


def hyena_mixer(h, p):
    w_in, b_in, conv_w, conv_b, f_w1, f_b1, f_w2, f_b2, f_w3, f_freq, bias, w_out = p
    B, L, _ = h.shape
    u = centred_dwconv3(_mm(h, w_in) + b_in, conv_w, conv_b)
    x0, x1, v = jnp.split(u, 3, axis=-1)
    zin = (x1 * v).astype(jnp.float32)
    filt_f = jnp.fft.rfft(hyena_filters(L, f_w1, f_b1, f_w2, f_b2, f_w3, f_freq), axis=0)
    conv = jnp.fft.irfft(jnp.fft.rfft(zin, n=2 * L, axis=1) * filt_f, n=2 * L, axis=1)[:, :L]
    y = conv + zin * bias.astype(jnp.float32)
    return _mm(x0 * y.astype(h.dtype), w_out)


PEER_N = PEER_KEYS * PEER_KEYS
LANES = 128
SUBLANES = 8
VMEM_LIMIT = 56 * 1024 * 1024

_NT = (((1,), (1,)), ((), ()))


def _split_bf16(x):
    hi = x.astype(jnp.bfloat16)
    lo = (x - hi.astype(jnp.float32)).astype(jnp.bfloat16)
    return hi, lo


def _top_vals(s, n):
    vals = []
    for _ in range(n):
        m = jnp.max(s, axis=0, keepdims=True)
        vals.append(m)
        s = jnp.where(s == m, -jnp.inf, s)
    return vals


def _peer_route_kernel(h_ref, wq_hi_ref, wq_lo_ref, k_hi_ref, k_lo_ref,
                       s1_ref, s2_ref, e1_ref, e2_ref, tau_ref, s_scr):
    tb = h_ref.shape[0]
    half = PEER_DKEY // 2
    h_hi, h_lo = _split_bf16(h_ref[...])
    w_hi = wq_hi_ref[...]
    q = (jnp.dot(h_hi, w_hi, preferred_element_type=jnp.float32)
         + jnp.dot(h_lo, w_hi, preferred_element_type=jnp.float32)
         + jnp.dot(h_hi, wq_lo_ref[...], preferred_element_type=jnp.float32))
    for p in range(2):
        q_hi, q_lo = _split_bf16(q[:, p * half:(p + 1) * half])
        k_hi = k_hi_ref[0, p]
        s_scr[p] = (lax.dot_general(k_hi, q_hi, _NT, preferred_element_type=jnp.float32)
                    + lax.dot_general(k_lo_ref[0, p], q_hi, _NT, preferred_element_type=jnp.float32)
                    + lax.dot_general(k_hi, q_lo, _NT, preferred_element_type=jnp.float32))

    def tile(lt, carry):
        ln = pl.ds(pl.multiple_of(lt * LANES, LANES), LANES)
        s1 = s_scr[0, :, ln]
        s2 = s_scr[1, :, ln]
        v1 = _top_vals(s1, PEER_TOPK)
        v2 = jnp.concatenate(_top_vals(s2, PEER_TOPK), axis=0)
        cand = [v1[a] + v2 for a in range(PEER_TOPK)]
        c = cand
        for k in range(PEER_TOPK):
            m = c[0]
            for ci in c[1:]:
                m = jnp.maximum(m, ci)
            m = jnp.max(m, axis=0, keepdims=True)
            if k + 1 < PEER_TOPK:
                c = [jnp.where(ci == m, -jnp.inf, ci) for ci in c]
        tau = m
        top = v1[0] + v2[0:1]
        z = jnp.zeros_like(tau)
        for ci in cand:
            z = z + jnp.sum(jnp.where(ci >= tau, jnp.exp(ci - top), 0.0), axis=0, keepdims=True)
        s1_ref[0, :, ln] = s1
        s2_ref[0, :, ln] = s2
        e1_ref[0, :, ln] = jnp.exp(s1 - v1[0]) / z
        e2_ref[0, :, ln] = jnp.exp(s2 - v2[0:1])
        tau_ref[0, :, ln] = jnp.broadcast_to(tau, (8, LANES))
        return carry

    lax.fori_loop(0, tb // LANES, tile, 0)


def peer_route(h, wq_hi, wq_lo, k_hi, k_lo, tb=256):
    T = h.shape[0]
    nh = PEER_HEADS
    out_sd = jax.ShapeDtypeStruct((nh, PEER_KEYS, T), jnp.float32)
    blk = pl.BlockSpec((1, PEER_KEYS, tb), lambda t, hh: (hh, 0, t))
    return pl.pallas_call(
        _peer_route_kernel,
        grid=(T // tb, nh),
        in_specs=[pl.BlockSpec((tb, D_MODEL), lambda t, hh: (t, 0)),
                  pl.BlockSpec((D_MODEL, PEER_DKEY), lambda t, hh: (0, hh)),
                  pl.BlockSpec((D_MODEL, PEER_DKEY), lambda t, hh: (0, hh)),
                  pl.BlockSpec((1, 2, PEER_KEYS, PEER_DKEY // 2), lambda t, hh: (hh, 0, 0, 0)),
                  pl.BlockSpec((1, 2, PEER_KEYS, PEER_DKEY // 2), lambda t, hh: (hh, 0, 0, 0))],
        out_specs=[blk, blk, blk, blk, pl.BlockSpec((1, 8, tb), lambda t, hh: (hh, 0, t))],
        out_shape=[out_sd, out_sd, out_sd, out_sd, jax.ShapeDtypeStruct((nh, 8, T), jnp.float32)],
        scratch_shapes=[pltpu.VMEM((2, PEER_KEYS, tb), jnp.float32)],
        compiler_params=pltpu.CompilerParams(dimension_semantics=("arbitrary", "arbitrary"),
                                             vmem_limit_bytes=VMEM_LIMIT),
        name="peer_route",
    )(h, wq_hi, wq_lo, k_hi, k_lo)


def _gelu_tanh(x):
    return 0.5 * x * (1.0 + jnp.tanh(0.7978845608028654 * (x + 0.044715 * (x * x * x))))


def _peer_expert_kernel(h_ref, u_ref, vt_ref, s1_ref, s2_ref, e1_ref, e2_ref, tau_ref,
                        o_ref, acc_ref, a_scr, w_scr):
    c = pl.program_id(1)
    ec, tb = a_scr.shape
    n_i = ec // PEER_KEYS

    @pl.when(c == 0)
    def _():
        acc_ref[...] = jnp.zeros_like(acc_ref)

    a_scr[...] = lax.dot_general(u_ref[...], h_ref[...], _NT, preferred_element_type=jnp.float32)

    igrp = pl.ds(pl.multiple_of(c * n_i, SUBLANES), SUBLANES)

    def tile(lt, carry):
        ln = pl.ds(pl.multiple_of(lt * LANES, LANES), LANES)
        for ii in range(n_i):
            rows = slice(ii * PEER_KEYS, (ii + 1) * PEER_KEYS)
            g = jnp.zeros((PEER_KEYS, LANES), jnp.float32)
            for hh in range(PEER_HEADS):
                s1r = s1_ref[hh, igrp, ln][ii:ii + 1]
                e1r = e1_ref[hh, igrp, ln][ii:ii + 1]
                tau = tau_ref[hh, 0:1, ln]
                sel = (s1r + s2_ref[hh, :, ln]) >= tau
                g = g + jnp.where(sel, e2_ref[hh, :, ln] * e1r, 0.0)
            w_scr[rows, ln] = (g * _gelu_tanh(a_scr[rows, ln])).astype(jnp.bfloat16)
        return carry

    lax.fori_loop(0, tb // LANES, tile, 0)
    acc_ref[...] += jnp.dot(vt_ref[...], w_scr[...], preferred_element_type=jnp.float32)

    @pl.when(c == pl.num_programs(1) - 1)
    def _():
        o_ref[...] = acc_ref[...].T


def peer_experts(h_bf16, u_bf16, vt_bf16, s1, s2, e1, e2, tau, tb=512, ec=SUBLANES * PEER_KEYS):
    T = h_bf16.shape[0]
    rblk = pl.BlockSpec((PEER_HEADS, PEER_KEYS, tb), lambda t, c: (0, 0, t))
    return pl.pallas_call(
        _peer_expert_kernel,
        grid=(T // tb, PEER_N // ec),
        in_specs=[pl.BlockSpec((tb, D_MODEL), lambda t, c: (t, 0)),
                  pl.BlockSpec((ec, D_MODEL), lambda t, c: (c, 0)),
                  pl.BlockSpec((D_MODEL, ec), lambda t, c: (0, c)),
                  rblk, rblk, rblk, rblk,
                  pl.BlockSpec((PEER_HEADS, 8, tb), lambda t, c: (0, 0, t))],
        out_specs=pl.BlockSpec((tb, D_MODEL), lambda t, c: (t, 0)),
        out_shape=jax.ShapeDtypeStruct((T, D_MODEL), jnp.float32),
        scratch_shapes=[pltpu.VMEM((D_MODEL, tb), jnp.float32),
                        pltpu.VMEM((ec, tb), jnp.float32),
                        pltpu.VMEM((ec, tb), jnp.bfloat16)],
        compiler_params=pltpu.CompilerParams(dimension_semantics=("arbitrary", "arbitrary"),
                                             vmem_limit_bytes=VMEM_LIMIT),
        name="peer_experts",
    )(h_bf16, u_bf16, vt_bf16, s1, s2, e1, e2, tau)


def peer_ffn(h, w_q, sub_keys, u_tab, v_tab):
    wq_hi, wq_lo = _split_bf16(w_q)
    k_hi, k_lo = _split_bf16(sub_keys)
    s1, s2, e1, e2, tau = peer_route(h, wq_hi, wq_lo, k_hi, k_lo)
    return peer_experts(h.astype(jnp.bfloat16), u_tab.astype(jnp.bfloat16), v_tab.T.astype(jnp.bfloat16),
                        s1, s2, e1, e2, tau)


def kernel(x_prompt, x_sample, state_rwkv_fwd, state_rwkv_bwd, cache_mla_ckv, cache_mla_kpe, c, c_ctx,
           norm_g, w_mod, b_mod, ab_w_in, rw_mu, rw_w0, rw_w2, rw_a0, rw_a2, rw_g2, rw_k_k, rw_k_a, rw_r_k,
           rw_lnx_g, rw_lnx_b, mla_q_norm, mla_q_up, mla_kv_norm, mla_kv_up, mla_qn, mla_kn, ab_w_out,
           hy_w_in, hy_b_in, hy_conv_w, hy_conv_b, hy_f_w1, hy_f_b1, hy_f_w2, hy_f_b2, hy_f_w3, hy_f_freq,
           hy_bias, hy_w_out, peer_w_q, peer_keys, peer_u, peer_v):
    rope = axial_rope(x_sample.shape[1])
    xp, xs = x_prompt, x_sample
    st_f, st_b, st_ckv, st_kpe = [], [], [], []
    for li in range(DEPTH):
        j = li // 2
        mp = jnp.split(jax.nn.silu(c_ctx) @ w_mod[li] + b_mod[li], 6, axis=-1)
        ms = [m[:, None, :] for m in jnp.split(jax.nn.silu(c) @ w_mod[li] + b_mod[li], 6, axis=-1)]
        hp = modulate(rmsnorm(xp, norm_g[li, 0]), mp[0], mp[1])
        hs = modulate(rmsnorm(xs, norm_g[li, 0]), ms[0], ms[1])
        if li % 2 == 0:
            p = (ab_w_in[j], rw_mu[j], rw_w0[j], rw_w2[j], rw_a0[j], rw_a2[j], rw_g2[j], rw_k_k[j], rw_k_a[j],
                 rw_r_k[j], rw_lnx_g[j], rw_lnx_b[j], mla_q_norm[j], mla_q_up[j], mla_kv_norm[j], mla_kv_up[j],
                 mla_qn[j], mla_kn[j], ab_w_out[j])
            op, sf, sb, ckv, kpe = rwkv_mla_mixer(hp, None, p)
            ctx = (state_rwkv_fwd[:, j], state_rwkv_bwd[:, j], cache_mla_ckv[:, j], cache_mla_kpe[:, j], rope)
            os_, _, _, _, _ = rwkv_mla_mixer(hs, ctx, p)
            st_f.append(sf)
            st_b.append(sb)
            st_ckv.append(ckv)
            st_kpe.append(kpe)
        else:
            p = (hy_w_in[j], hy_b_in[j], hy_conv_w[j], hy_conv_b[j], hy_f_w1[j], hy_f_b1[j], hy_f_w2[j],
                 hy_f_b2[j], hy_f_w3[j], hy_f_freq[j], hy_bias[j], hy_w_out[j])
            op = hyena_mixer(hp, p)
            os_ = hyena_mixer(hs, p)
        xp = xp + mp[2] * op
        xs = xs + ms[2] * os_
        pp = (peer_w_q[li], peer_keys[li], peer_u[li], peer_v[li])
        gp = modulate(rmsnorm(xp, norm_g[li, 1]), mp[3], mp[4]).reshape(-1, D_MODEL)
        gs = modulate(rmsnorm(xs, norm_g[li, 1]), ms[3], ms[4]).reshape(-1, D_MODEL)
        po = peer_ffn(jnp.concatenate([gp, gs], axis=0), *pp)
        xp = xp + mp[5] * po[:gp.shape[0]].reshape(xp.shape)
        xs = xs + ms[5] * po[gp.shape[0]:].reshape(xs.shape)
    new_state_rwkv_fwd = jnp.stack(st_f, axis=1).astype(x_prompt.dtype)
    new_state_rwkv_bwd = jnp.stack(st_b, axis=1).astype(x_prompt.dtype)
    new_cache_mla_ckv = jnp.stack(st_ckv, axis=1)
    new_cache_mla_kpe = jnp.stack(st_kpe, axis=1)
    return (xp, xs, new_state_rwkv_fwd, new_state_rwkv_bwd, new_cache_mla_ckv, new_cache_mla_kpe)
```

```python
import math
from functools import partial

import jax
import jax.numpy as jnp
from jax import lax
from jax.experimental import pallas as pl
from jax.experimental.pallas import tpu as pltpu

D_MODEL = 1024
DEPTH = 2
GRID_W = 64
EPS = 1e-6
RW_HEADS = 8
RW_HEAD = 64
RW_DIM = RW_HEADS * RW_HEAD
W_LORA = 64
A_LORA = 64
G_LORA = 128
LNX_EPS = 64e-5
RW_IN = 3 * RW_DIM + W_LORA + A_LORA + G_LORA
RW_SPLITS = (RW_DIM, 2 * RW_DIM, 3 * RW_DIM, 3 * RW_DIM + W_LORA, 3 * RW_DIM + W_LORA + A_LORA)
MLA_HEADS = 4
QK_NOPE = 128
QK_ROPE = 64
QK_HEAD = QK_NOPE + QK_ROPE
V_HEAD = 128
Q_LORA = 256
KV_LORA = 128
MLA_DIM = MLA_HEADS * V_HEAD
ROPE_THETA = 10000.0
Q_BLOCK = 128
HY_BANDS = 16
HY_TARGET = 1e-2
HY_FAST = 0.3
HY_SLOW = 1.5
PEER_KEYS = 128
PEER_HEADS = 8
PEER_DKEY = 256
PEER_TOPK = 16
TOKEN_BLOCK = 128


def _mm_kernel(a_ref, b_ref, o_ref):
    o_ref[...] = jnp.dot(a_ref[...].astype(jnp.bfloat16), b_ref[...].astype(jnp.bfloat16),
                         preferred_element_type=jnp.float32)


def _mm(a, b, tm=512, tn=512):
    lead = a.shape[:-1]
    K = a.shape[-1]
    N = b.shape[-1]
    a2 = a.reshape(-1, K)
    M = a2.shape[0]
    tm = min(tm, M)
    tn = min(tn, N)
    if N % tn:
        tn = N
    assert M % tm == 0 and N % tn == 0
    out = pl.pallas_call(
        _mm_kernel,
        grid=(M // tm, N // tn),
        in_specs=[pl.BlockSpec((tm, K), lambda i, j: (i, 0)),
                  pl.BlockSpec((K, tn), lambda i, j: (0, j))],
        out_specs=pl.BlockSpec((tm, tn), lambda i, j: (i, j)),
        out_shape=jax.ShapeDtypeStruct((M, N), jnp.float32),
    )(a2, b)
    return out.reshape(*lead, N)


def rmsnorm(x, g):
    xf = x.astype(jnp.float32)
    y = xf * lax.rsqrt(jnp.mean(xf * xf, axis=-1, keepdims=True) + EPS)
    return (y * g.astype(jnp.float32)).astype(x.dtype)


def modulate(h, shift, scale):
    return h * (1.0 + scale) + shift


def centred_shift(x):
    prev = jnp.pad(x[:, :-1], ((0, 0), (1, 0), (0, 0)))
    nxt = jnp.pad(x[:, 1:], ((0, 0), (0, 1), (0, 0)))
    return 0.5 * (prev + nxt)


def centred_dwconv3(u, w, b):
    prev = jnp.pad(u[:, :-1], ((0, 0), (1, 0), (0, 0)))
    nxt = jnp.pad(u[:, 1:], ((0, 0), (0, 1), (0, 0)))
    return prev * w[0] + u * w[1] + nxt * w[2] + b


def axial_rope(L):
    rows = L // GRID_W
    row = jnp.repeat(jnp.arange(rows, dtype=jnp.float32), GRID_W)
    col = jnp.tile(jnp.arange(GRID_W, dtype=jnp.float32), rows)
    n_freq = QK_ROPE // 4
    inv = ROPE_THETA ** (-jnp.arange(n_freq, dtype=jnp.float32) / n_freq)
    ang = jnp.concatenate([row[:, None] * inv, col[:, None] * inv], axis=-1)
    return jnp.cos(ang), jnp.sin(ang)


def rope_tail(x, cos, sin):
    xn, xr = x[..., :QK_NOPE], x[..., QK_NOPE:]
    x1, x2 = xr[..., 0::2], xr[..., 1::2]
    cs = cos[None, :, None, :].astype(x.dtype)
    sn = sin[None, :, None, :].astype(x.dtype)
    rot = jnp.stack([x1 * cs - x2 * sn, x1 * sn + x2 * cs], axis=-1).reshape(xr.shape)
    return jnp.concatenate([xn, rot], axis=-1)


def attend(q, k, v):
    B, Lq, H, Dh = q.shape
    nb = Lq // Q_BLOCK
    qb = q.reshape(B, nb, Q_BLOCK, H, Dh).transpose(1, 0, 2, 3, 4)
    scale = Dh ** -0.5

    def blk(qi):
        s = jnp.einsum('bqhd,bkhd->bhqk', qi, k).astype(jnp.float32) * scale
        p = jax.nn.softmax(s, axis=-1).astype(v.dtype)
        return jnp.einsum('bhqk,bkhd->bqhd', p, v)

    o = lax.map(blk, qb)
    return o.transpose(1, 0, 2, 3, 4).reshape(B, Lq, H * v.shape[-1])


LANES = 128
SUBLANES = 8
VMEM_LIMIT = 56 * 1024 * 1024
SCAN_ACCS = 4


def _rwkv_scan_kernel(r_ref, kk_ref, v_ref, w_ref, kd_ref, b_ref, s0_ref, y_ref, sf_ref, s_scr):
    d = pl.program_id(0)
    c = pl.program_id(1)
    tc = r_ref.shape[0]
    nv = v_ref.shape[1]

    @pl.when(c == 0)
    def _():
        s_scr[...] = s0_ref[0]

    def row(ref, tt, k):
        return jnp.broadcast_to(ref[tt, pl.ds(k, 1), :], (nv, LANES))

    def row_d(ref, tt, k):
        return jnp.broadcast_to(ref[0, tt, pl.ds(k, 1), :], (nv, LANES))

    def step(t, carry):
        tt = t + d * (tc - 1 - 2 * t)
        parts = [None] * SCAN_ACCS
        for k in range(RW_HEAD):
            term = s_scr[k] * row(kk_ref, tt, k)
            parts[k % SCAN_ACCS] = term if parts[k % SCAN_ACCS] is None else parts[k % SCAN_ACCS] + term
        sa = -((parts[0] + parts[1]) + (parts[2] + parts[3]))
        vv = v_ref[tt]
        parts = [None] * SCAN_ACCS
        for k in range(RW_HEAD):
            s_new = s_scr[k] * row_d(w_ref, tt, k) + sa * row_d(b_ref, tt, k) + vv * row_d(kd_ref, tt, k)
            s_scr[k] = s_new
            term = s_new * row(r_ref, tt, k)
            parts[k % SCAN_ACCS] = term if parts[k % SCAN_ACCS] is None else parts[k % SCAN_ACCS] + term
        y_ref[0, tt] = (parts[0] + parts[1]) + (parts[2] + parts[3])
        return carry

    lax.fori_loop(0, tc, step, 0)

    @pl.when(c == pl.num_programs(1) - 1)
    def _():
        sf_ref[0] = s_scr[...]


def rwkv_scan(r, kk, v, w2, kd2, b2, s0, tc=32):
    L = r.shape[0]
    nv = v.shape[1]
    nc = L // tc
    tmap = lambda d, c: c + d * (nc - 1 - 2 * c)
    shared = lambda rows: pl.BlockSpec((tc, rows, LANES), lambda d, c: (tmap(d, c), 0, 0))
    per_dir = pl.BlockSpec((1, tc, RW_HEAD, LANES), lambda d, c: (d, tmap(d, c), 0, 0))
    state = pl.BlockSpec((1, RW_HEAD, nv, LANES), lambda d, c: (d, 0, 0, 0))
    return pl.pallas_call(
        _rwkv_scan_kernel,
        grid=(2, nc),
        in_specs=[shared(RW_HEAD), shared(RW_HEAD), shared(nv), per_dir, per_dir, per_dir, state],
        out_specs=[pl.BlockSpec((1, tc, nv, LANES), lambda d, c: (d, tmap(d, c), 0, 0)), state],
        out_shape=[jax.ShapeDtypeStruct((2, L, nv, LANES), jnp.float32),
                   jax.ShapeDtypeStruct((2, RW_HEAD, nv, LANES), jnp.float32)],
        scratch_shapes=[pltpu.VMEM((RW_HEAD, nv, LANES), jnp.float32)],
        compiler_params=pltpu.CompilerParams(dimension_semantics=("arbitrary", "arbitrary"),
                                             vmem_limit_bytes=VMEM_LIMIT),
        name="rwkv_scan",
    )(r, kk, v, w2, kd2, b2, s0)


def to_lanes(x, vsplit):
    B, L, H, N = x.shape
    y = x.transpose(1, 3, 0, 2).reshape(L, N, B * H)
    return jnp.repeat(y, vsplit, axis=-1) if vsplit > 1 else y


def v_to_lanes(x, vsplit):
    B, L, H, N = x.shape
    nv = N // vsplit
    return x.reshape(B, L, H, vsplit, nv).transpose(1, 4, 0, 2, 3).reshape(L, nv, B * H * vsplit)


def v_from_lanes(y, B, H, vsplit):
    L, nv, _ = y.shape
    return y.reshape(L, nv, B, H, vsplit).transpose(2, 0, 3, 4, 1).reshape(B, L, H, vsplit * nv)


def state_to_lanes(s, vsplit):
    B, H, N, K = s.shape
    nv = N // vsplit
    return s.reshape(B, H, vsplit, nv, K).transpose(4, 3, 0, 1, 2).reshape(K, nv, B * H * vsplit)


def state_from_lanes(s, B, H, vsplit):
    K, nv, _ = s.shape
    return s.reshape(K, nv, B, H, vsplit).transpose(2, 3, 4, 1, 0).reshape(B, H, vsplit * nv, K)


def rwkv_scan_both(rh, kk, vh, decay2, kd2, b2, s0_2, tc=32):
    B, L, H, N = rh.shape
    vsplit = LANES // (B * H)
    assert B * H * vsplit == LANES
    stack = lambda xs: jnp.stack([to_lanes(x, vsplit) for x in xs])
    y2, sf2 = rwkv_scan(to_lanes(rh, vsplit), to_lanes(kk, vsplit), v_to_lanes(vh, vsplit),
                        stack(decay2), stack(kd2), stack(b2),
                        jnp.stack([state_to_lanes(s, vsplit) for s in s0_2]), tc=tc)
    ys = [v_from_lanes(y2[d], B, H, vsplit) for d in range(2)]
    sfs = [state_from_lanes(sf2[d], B, H, vsplit) for d in range(2)]
    return ys, sfs


def mla_keys(ckv, k_pe, kv_up, kn):
    B, L, _ = ckv.shape
    kv = (ckv @ kv_up).reshape(B, L, MLA_HEADS, QK_NOPE + V_HEAD)
    k_rope = jnp.broadcast_to(k_pe[:, :, None, :], (B, L, MLA_HEADS, QK_ROPE))
    k = rmsnorm(jnp.concatenate([kv[..., :QK_NOPE], k_rope], axis=-1), kn)
    return k, kv[..., QK_NOPE:]


def rwkv_mla_mixer(h, ctx, p):
    (w_in, mu, w0, w2, a0, a2, g2, k_k, k_a, r_k, lnx_g, lnx_b,
     q_norm, q_up, kv_norm, kv_up, qn, kn, w_out) = p
    B, L, _ = h.shape
    f32 = jnp.float32
    proj = _mm(h, w_in)
    rw, mla = proj[..., :RW_IN], proj[..., RW_IN:]
    rw = rw + mu * (centred_shift(rw) - rw)
    r, k, v, wd, ad, gd = jnp.split(rw, RW_SPLITS, axis=-1)
    heads = lambda t: t.reshape(B, L, RW_HEADS, RW_HEAD)
    rh, vh = heads(r.astype(f32)), heads(v.astype(f32))
    kk = heads((k * k_k).astype(f32))
    kk = kk / jnp.maximum(jnp.linalg.norm(kk, axis=-1, keepdims=True), 1e-12)
    if ctx is None:
        z = jnp.zeros((B, RW_HEADS, RW_HEAD, RW_HEAD), f32)
        s0s = (z, z)
    else:
        s0s = (ctx[0], ctx[1])
    decays, kds, bs, bonus = [], [], [], []
    for d in range(2):
        w_raw = (w0[d] + jnp.tanh(wd) @ w2[d]).astype(f32)
        decay = jnp.exp(-jnp.exp(-jax.nn.softplus(-w_raw) - 0.5))
        lr = jax.nn.sigmoid((a0[d] + ad @ a2[d]).astype(f32))
        kd = heads(k.astype(f32) * (1.0 + (lr - 1.0) * k_a))
        decays.append(heads(decay))
        kds.append(kd)
        bs.append(kk * heads(lr))
        bonus.append(jnp.sum(rh * kd * r_k, axis=-1, keepdims=True) * vh)
    ys, finals = rwkv_scan_both(rh, kk, vh, decays, kds, bs, s0s)
    y = ys[0] + ys[1]
    mean = jnp.mean(y, axis=-1, keepdims=True)
    var = jnp.mean(jnp.square(y - mean), axis=-1, keepdims=True)
    yn = ((y - mean) * lax.rsqrt(var + LNX_EPS)).reshape(B, L, RW_DIM) * lnx_g + lnx_b
    gate = jax.nn.sigmoid(gd) @ g2
    rw_out = ((yn + (bonus[0] + bonus[1]).reshape(B, L, RW_DIM)) * gate).astype(h.dtype)
    q_c, kv_c, k_pe = jnp.split(mla, (Q_LORA, Q_LORA + KV_LORA), axis=-1)
    q = rmsnorm((rmsnorm(q_c, q_norm) @ q_up).reshape(B, L, MLA_HEADS, QK_HEAD), qn)
    ckv = rmsnorm(kv_c, kv_norm)
    k_own, v_own = mla_keys(ckv, k_pe, kv_up, kn)
    if ctx is None:
        keys, vals = k_own, v_own
    else:
        cos, sin = ctx[4]
        q = rope_tail(q, cos, sin)
        k_own = rope_tail(k_own, cos, sin)
        k_ctx, v_ctx = mla_keys(ctx[2], ctx[3], kv_up, kn)
        keys = jnp.concatenate([k_ctx, k_own], axis=1)
        vals = jnp.concatenate([v_ctx, v_own], axis=1)
    mla_out = attend(q, keys, vals)
    out = _mm(jnp.concatenate([rw_out, mla_out.astype(h.dtype)], axis=-1), w_out)
    return out, finals[0], finals[1], ckv, k_pe


def hyena_filters(L, w1, b1, w2, b2, w3, freq):
    f32 = jnp.float32
    t = jnp.arange(L, dtype=f32)[:, None]
    t_unit = t / (L - 1)
    bands = jnp.linspace(1e-4, HY_BANDS - 1, HY_BANDS, dtype=f32)
    ang = 2.0 * math.pi * t * bands / L
    zpos = jnp.concatenate([t_unit, jnp.cos(ang), -jnp.sin(ang)], axis=-1)
    fr = freq.astype(f32)
    hid = jnp.sin(fr * (zpos @ w1.astype(f32) + b1.astype(f32)))
    hid = jnp.sin(fr * (hid @ w2.astype(f32) + b2.astype(f32)))
    filt = hid @ w3.astype(f32)
    deltas = jnp.linspace(math.log(HY_TARGET) / HY_FAST, math.log(HY_TARGET) / HY_SLOW, D_MODEL, dtype=f32)
    window = jnp.exp(-t_unit * jnp.abs(deltas))
    h_f = filt[:, :D_MODEL] * window
    h_b = filt[:, D_MODEL:] * window
    circ = jnp.concatenate([h_f, jnp.zeros((1, D_MODEL), f32), h_b[:0:-1]], axis=0)
    return circ / jnp.sum(jnp.abs(circ), axis=0, keepdims=True)


def hyena_mixer(h, p):
    w_in, b_in, conv_w, conv_b, f_w1, f_b1, f_w2, f_b2, f_w3, f_freq, bias, w_out = p
    B, L, _ = h.shape
    u = centred_dwconv3(_mm(h, w_in) + b_in, conv_w, conv_b)
    x0, x1, v = jnp.split(u, 3, axis=-1)
    zin = (x1 * v).astype(jnp.float32)
    filt_f = jnp.fft.rfft(hyena_filters(L, f_w1, f_b1, f_w2, f_b2, f_w3, f_freq), axis=0)
    conv = jnp.fft.irfft(jnp.fft.rfft(zin, n=2 * L, axis=1) * filt_f, n=2 * L, axis=1)[:, :L]
    y = conv + zin * bias.astype(jnp.float32)
    return _mm(x0 * y.astype(h.dtype), w_out)


PEER_N = PEER_KEYS * PEER_KEYS

_NT = (((1,), (1,)), ((), ()))


def _split_bf16(x):
    hi = x.astype(jnp.bfloat16)
    lo = (x - hi.astype(jnp.float32)).astype(jnp.bfloat16)
    return hi, lo


def _top_vals(s, n):
    vals = []
    for _ in range(n):
        m = jnp.max(s, axis=0, keepdims=True)
        vals.append(m)
        s = jnp.where(s == m, -jnp.inf, s)
    return vals


def _peer_route_kernel(h_ref, wq_hi_ref, wq_lo_ref, k_hi_ref, k_lo_ref,
                       s1_ref, s2_ref, e1_ref, e2_ref, tau_ref, s_scr):
    tb = h_ref.shape[0]
    half = PEER_DKEY // 2
    h_hi, h_lo = _split_bf16(h_ref[...])
    w_hi = wq_hi_ref[...]
    q = (jnp.dot(h_hi, w_hi, preferred_element_type=jnp.float32)
         + jnp.dot(h_lo, w_hi, preferred_element_type=jnp.float32)
         + jnp.dot(h_hi, wq_lo_ref[...], preferred_element_type=jnp.float32))
    for p in range(2):
        q_hi, q_lo = _split_bf16(q[:, p * half:(p + 1) * half])
        k_hi = k_hi_ref[0, p]
        s_scr[p] = (lax.dot_general(k_hi, q_hi, _NT, preferred_element_type=jnp.float32)
                    + lax.dot_general(k_lo_ref[0, p], q_hi, _NT, preferred_element_type=jnp.float32)
                    + lax.dot_general(k_hi, q_lo, _NT, preferred_element_type=jnp.float32))

    def tile(lt, carry):
        ln = pl.ds(pl.multiple_of(lt * LANES, LANES), LANES)
        s1 = s_scr[0, :, ln]
        s2 = s_scr[1, :, ln]
        v1 = _top_vals(s1, PEER_TOPK)
        v2 = jnp.concatenate(_top_vals(s2, PEER_TOPK), axis=0)
        cand = [v1[a] + v2 for a in range(PEER_TOPK)]
        c = cand
        for k in range(PEER_TOPK):
            m = c[0]
            for ci in c[1:]:
                m = jnp.maximum(m, ci)
            m = jnp.max(m, axis=0, keepdims=True)
            if k + 1 < PEER_TOPK:
                c = [jnp.where(ci == m, -jnp.inf, ci) for ci in c]
        tau = m
        top = v1[0] + v2[0:1]
        z = jnp.zeros_like(tau)
        for ci in cand:
            z = z + jnp.sum(jnp.where(ci >= tau, jnp.exp(ci - top), 0.0), axis=0, keepdims=True)
        s1_ref[0, :, ln] = s1
        s2_ref[0, :, ln] = s2
        e1_ref[0, :, ln] = jnp.exp(s1 - v1[0]) / z
        e2_ref[0, :, ln] = jnp.exp(s2 - v2[0:1])
        tau_ref[0, :, ln] = jnp.broadcast_to(tau, (8, LANES))
        return carry

    lax.fori_loop(0, tb // LANES, tile, 0)


def peer_route(h, wq_hi, wq_lo, k_hi, k_lo, tb=256):
    T = h.shape[0]
    nh = PEER_HEADS
    out_sd = jax.ShapeDtypeStruct((nh, PEER_KEYS, T), jnp.float32)
    blk = pl.BlockSpec((1, PEER_KEYS, tb), lambda t, hh: (hh, 0, t))
    return pl.pallas_call(
        _peer_route_kernel,
        grid=(T // tb, nh),
        in_specs=[pl.BlockSpec((tb, D_MODEL), lambda t, hh: (t, 0)),
                  pl.BlockSpec((D_MODEL, PEER_DKEY), lambda t, hh: (0, hh)),
                  pl.BlockSpec((D_MODEL, PEER_DKEY), lambda t, hh: (0, hh)),
                  pl.BlockSpec((1, 2, PEER_KEYS, PEER_DKEY // 2), lambda t, hh: (hh, 0, 0, 0)),
                  pl.BlockSpec((1, 2, PEER_KEYS, PEER_DKEY // 2), lambda t, hh: (hh, 0, 0, 0))],
        out_specs=[blk, blk, blk, blk, pl.BlockSpec((1, 8, tb), lambda t, hh: (hh, 0, t))],
        out_shape=[out_sd, out_sd, out_sd, out_sd, jax.ShapeDtypeStruct((nh, 8, T), jnp.float32)],
        scratch_shapes=[pltpu.VMEM((2, PEER_KEYS, tb), jnp.float32)],
        compiler_params=pltpu.CompilerParams(dimension_semantics=("arbitrary", "arbitrary"),
                                             vmem_limit_bytes=VMEM_LIMIT),
        name="peer_route",
    )(h, wq_hi, wq_lo, k_hi, k_lo)


def _gelu_tanh(x):
    return 0.5 * x * (1.0 + jnp.tanh(0.7978845608028654 * (x + 0.044715 * (x * x * x))))


def _peer_expert_kernel(h_ref, u_ref, vt_ref, s1_ref, s2_ref, e1_ref, e2_ref, tau_ref,
                        o_ref, acc_ref, a_scr, w_scr):
    c = pl.program_id(1)
    ec, tb = a_scr.shape
    n_i = ec // PEER_KEYS

    @pl.when(c == 0)
    def _():
        acc_ref[...] = jnp.zeros_like(acc_ref)

    a_scr[...] = lax.dot_general(u_ref[...], h_ref[...], _NT, preferred_element_type=jnp.float32)

    igrp = pl.ds(pl.multiple_of(c * n_i, SUBLANES), SUBLANES)

    def tile(lt, carry):
        ln = pl.ds(pl.multiple_of(lt * LANES, LANES), LANES)
        for ii in range(n_i):
            rows = slice(ii * PEER_KEYS, (ii + 1) * PEER_KEYS)
            g = jnp.zeros((PEER_KEYS, LANES), jnp.float32)
            for hh in range(PEER_HEADS):
                s1r = s1_ref[hh, igrp, ln][ii:ii + 1]
                e1r = e1_ref[hh, igrp, ln][ii:ii + 1]
                tau = tau_ref[hh, 0:1, ln]
                sel = (s1r + s2_ref[hh, :, ln]) >= tau
                g = g + jnp.where(sel, e2_ref[hh, :, ln] * e1r, 0.0)
            w_scr[rows, ln] = (g * _gelu_tanh(a_scr[rows, ln])).astype(jnp.bfloat16)
        return carry

    lax.fori_loop(0, tb // LANES, tile, 0)
    acc_ref[...] += jnp.dot(vt_ref[...], w_scr[...], preferred_element_type=jnp.float32)

    @pl.when(c == pl.num_programs(1) - 1)
    def _():
        o_ref[...] = acc_ref[...].T


def peer_experts(h_bf16, u_bf16, vt_bf16, s1, s2, e1, e2, tau, tb=512, ec=SUBLANES * PEER_KEYS):
    T = h_bf16.shape[0]
    rblk = pl.BlockSpec((PEER_HEADS, PEER_KEYS, tb), lambda t, c: (0, 0, t))
    return pl.pallas_call(
        _peer_expert_kernel,
        grid=(T // tb, PEER_N // ec),
        in_specs=[pl.BlockSpec((tb, D_MODEL), lambda t, c: (t, 0)),
                  pl.BlockSpec((ec, D_MODEL), lambda t, c: (c, 0)),
                  pl.BlockSpec((D_MODEL, ec), lambda t, c: (0, c)),
                  rblk, rblk, rblk, rblk,
                  pl.BlockSpec((PEER_HEADS, 8, tb), lambda t, c: (0, 0, t))],
        out_specs=pl.BlockSpec((tb, D_MODEL), lambda t, c: (t, 0)),
        out_shape=jax.ShapeDtypeStruct((T, D_MODEL), jnp.float32),
        scratch_shapes=[pltpu.VMEM((D_MODEL, tb), jnp.float32),
                        pltpu.VMEM((ec, tb), jnp.float32),
                        pltpu.VMEM((ec, tb), jnp.bfloat16)],
        compiler_params=pltpu.CompilerParams(dimension_semantics=("arbitrary", "arbitrary"),
                                             vmem_limit_bytes=VMEM_LIMIT),
        name="peer_experts",
    )(h_bf16, u_bf16, vt_bf16, s1, s2, e1, e2, tau)


def peer_ffn(h, w_q, sub_keys, u_tab, v_tab):
    wq_hi, wq_lo = _split_bf16(w_q)
    k_hi, k_lo = _split_bf16(sub_keys)
    s1, s2, e1, e2, tau = peer_route(h, wq_hi, wq_lo, k_hi, k_lo)
    return peer_experts(h.astype(jnp.bfloat16), u_tab.astype(jnp.bfloat16), v_tab.T.astype(jnp.bfloat16),
                        s1, s2, e1, e2, tau)


def kernel(x_prompt, x_sample, state_rwkv_fwd, state_rwkv_bwd, cache_mla_ckv, cache_mla_kpe, c, c_ctx,
           norm_g, w_mod, b_mod, ab_w_in, rw_mu, rw_w0, rw_w2, rw_a0, rw_a2, rw_g2, rw_k_k, rw_k_a, rw_r_k,
           rw_lnx_g, rw_lnx_b, mla_q_norm, mla_q_up, mla_kv_norm, mla_kv_up, mla_qn, mla_kn, ab_w_out,
           hy_w_in, hy_b_in, hy_conv_w, hy_conv_b, hy_f_w1, hy_f_b1, hy_f_w2, hy_f_b2, hy_f_w3, hy_f_freq,
           hy_bias, hy_w_out, peer_w_q, peer_keys, peer_u, peer_v):
    rope = axial_rope(x_sample.shape[1])
    xp, xs = x_prompt, x_sample
    st_f, st_b, st_ckv, st_kpe = [], [], [], []
    for li in range(DEPTH):
        j = li // 2
        mp = jnp.split(jax.nn.silu(c_ctx) @ w_mod[li] + b_mod[li], 6, axis=-1)
        ms = [m[:, None, :] for m in jnp.split(jax.nn.silu(c) @ w_mod[li] + b_mod[li], 6, axis=-1)]
        hp = modulate(rmsnorm(xp, norm_g[li, 0]), mp[0], mp[1])
        hs = modulate(rmsnorm(xs, norm_g[li, 0]), ms[0], ms[1])
        if li % 2 == 0:
            p = (ab_w_in[j], rw_mu[j], rw_w0[j], rw_w2[j], rw_a0[j], rw_a2[j], rw_g2[j], rw_k_k[j], rw_k_a[j],
                 rw_r_k[j], rw_lnx_g[j], rw_lnx_b[j], mla_q_norm[j], mla_q_up[j], mla_kv_norm[j], mla_kv_up[j],
                 mla_qn[j], mla_kn[j], ab_w_out[j])
            op, sf, sb, ckv, kpe = rwkv_mla_mixer(hp, None, p)
            ctx = (state_rwkv_fwd[:, j], state_rwkv_bwd[:, j], cache_mla_ckv[:, j], cache_mla_kpe[:, j], rope)
            os_, _, _, _, _ = rwkv_mla_mixer(hs, ctx, p)
            st_f.append(sf)
            st_b.append(sb)
            st_ckv.append(ckv)
            st_kpe.append(kpe)
        else:
            p = (hy_w_in[j], hy_b_in[j], hy_conv_w[j], hy_conv_b[j], hy_f_w1[j], hy_f_b1[j], hy_f_w2[j],
                 hy_f_b2[j], hy_f_w3[j], hy_f_freq[j], hy_bias[j], hy_w_out[j])
            op = hyena_mixer(hp, p)
            os_ = hyena_mixer(hs, p)
        xp = xp + mp[2] * op
        xs = xs + ms[2] * os_
        pp = (peer_w_q[li], peer_keys[li], peer_u[li], peer_v[li])
        gp = modulate(rmsnorm(xp, norm_g[li, 1]), mp[3], mp[4]).reshape(-1, D_MODEL)
        gs = modulate(rmsnorm(xs, norm_g[li, 1]), ms[3], ms[4]).reshape(-1, D_MODEL)
        po = peer_ffn(jnp.concatenate([gp, gs], axis=0), *pp)
        xp = xp + mp[5] * po[:gp.shape[0]].reshape(xp.shape)
        xs = xs + ms[5] * po[gp.shape[0]:].reshape(xs.shape)
    new_state_rwkv_fwd = jnp.stack(st_f, axis=1).astype(x_prompt.dtype)
    new_state_rwkv_bwd = jnp.stack(st_b, axis=1).astype(x_prompt.dtype)
    new_cache_mla_ckv = jnp.stack(st_ckv, axis=1)
    new_cache_mla_kpe = jnp.stack(st_kpe, axis=1)
    return (xp, xs, new_state_rwkv_fwd, new_state_rwkv_bwd, new_cache_mla_ckv, new_cache_mla_kpe)
```

```python
import math
from functools import partial

import jax
import jax.numpy as jnp
from jax import lax
from jax.experimental import pallas as pl
from jax.experimental.pallas import tpu as pltpu

D_MODEL = 1024
DEPTH = 2
GRID_W = 64
EPS = 1e-6
RW_HEADS = 8
RW_HEAD = 64
RW_DIM = RW_HEADS * RW_HEAD
W_LORA = 64
A_LORA = 64
G_LORA = 128
LNX_EPS = 64e-5
RW_IN = 3 * RW_DIM + W_LORA + A_LORA + G_LORA
RW_SPLITS = (RW_DIM, 2 * RW_DIM, 3 * RW_DIM, 3 * RW_DIM + W_LORA, 3 * RW_DIM + W_LORA + A_LORA)
MLA_HEADS = 4
QK_NOPE = 128
QK_ROPE = 64
QK_HEAD = QK_NOPE + QK_ROPE
V_HEAD = 128
Q_LORA = 256
KV_LORA = 128
MLA_DIM = MLA_HEADS * V_HEAD
ROPE_THETA = 10000.0
Q_BLOCK = 128
HY_BANDS = 16
HY_TARGET = 1e-2
HY_FAST = 0.3
HY_SLOW = 1.5
PEER_KEYS = 128
PEER_HEADS = 8
PEER_DKEY = 256
PEER_TOPK = 16
TOKEN_BLOCK = 128


def _mm_kernel(a_ref, b_ref, o_ref):
    o_ref[...] = jnp.dot(a_ref[...].astype(jnp.bfloat16), b_ref[...].astype(jnp.bfloat16),
                         preferred_element_type=jnp.float32)


def _mm(a, b, tm=512, tn=512):
    lead = a.shape[:-1]
    K = a.shape[-1]
    N = b.shape[-1]
    a2 = a.reshape(-1, K)
    M = a2.shape[0]
    tm = min(tm, M)
    tn = min(tn, N)
    if N % tn:
        tn = N
    assert M % tm == 0 and N % tn == 0
    out = pl.pallas_call(
        _mm_kernel,
        grid=(M // tm, N // tn),
        in_specs=[pl.BlockSpec((tm, K), lambda i, j: (i, 0)),
                  pl.BlockSpec((K, tn), lambda i, j: (0, j))],
        out_specs=pl.BlockSpec((tm, tn), lambda i, j: (i, j)),
        out_shape=jax.ShapeDtypeStruct((M, N), jnp.float32),
    )(a2, b)
    return out.reshape(*lead, N)


def rmsnorm(x, g):
    xf = x.astype(jnp.float32)
    y = xf * lax.rsqrt(jnp.mean(xf * xf, axis=-1, keepdims=True) + EPS)
    return (y * g.astype(jnp.float32)).astype(x.dtype)


def modulate(h, shift, scale):
    return h * (1.0 + scale) + shift


def centred_shift(x):
    prev = jnp.pad(x[:, :-1], ((0, 0), (1, 0), (0, 0)))
    nxt = jnp.pad(x[:, 1:], ((0, 0), (0, 1), (0, 0)))
    return 0.5 * (prev + nxt)


def centred_dwconv3(u, w, b):
    prev = jnp.pad(u[:, :-1], ((0, 0), (1, 0), (0, 0)))
    nxt = jnp.pad(u[:, 1:], ((0, 0), (0, 1), (0, 0)))
    return prev * w[0] + u * w[1] + nxt * w[2] + b


def axial_rope(L):
    rows = L // GRID_W
    row = jnp.repeat(jnp.arange(rows, dtype=jnp.float32), GRID_W)
    col = jnp.tile(jnp.arange(GRID_W, dtype=jnp.float32), rows)
    n_freq = QK_ROPE // 4
    inv = ROPE_THETA ** (-jnp.arange(n_freq, dtype=jnp.float32) / n_freq)
    ang = jnp.concatenate([row[:, None] * inv, col[:, None] * inv], axis=-1)
    return jnp.cos(ang), jnp.sin(ang)


def rope_tail(x, cos, sin):
    xn, xr = x[..., :QK_NOPE], x[..., QK_NOPE:]
    x1, x2 = xr[..., 0::2], xr[..., 1::2]
    cs = cos[None, :, None, :].astype(x.dtype)
    sn = sin[None, :, None, :].astype(x.dtype)
    rot = jnp.stack([x1 * cs - x2 * sn, x1 * sn + x2 * cs], axis=-1).reshape(xr.shape)
    return jnp.concatenate([xn, rot], axis=-1)


LANES = 128
SUBLANES = 8
VMEM_LIMIT = 56 * 1024 * 1024
_NT = (((1,), (1,)), ((), ()))


def _split_bf16(x):
    hi = x.astype(jnp.bfloat16)
    lo = (x - hi.astype(jnp.float32)).astype(jnp.bfloat16)
    return hi, lo


def _mla_attn_kernel(q_ref, k_ref, v_ref, o_ref):
    scale = QK_HEAD ** -0.5
    for h in range(MLA_HEADS):
        q = q_ref[0, :, h * QK_HEAD:(h + 1) * QK_HEAD].astype(jnp.bfloat16)
        k = k_ref[0, :, h * QK_HEAD:(h + 1) * QK_HEAD].astype(jnp.bfloat16)
        s = lax.dot_general(q, k, _NT, preferred_element_type=jnp.float32) * scale
        m = jnp.max(s, axis=-1, keepdims=True)
        p = jnp.exp(s - m)
        l = jnp.sum(p, axis=-1, keepdims=True)
        v = v_ref[0, :, h * V_HEAD:(h + 1) * V_HEAD].astype(jnp.bfloat16)
        o = jnp.dot((p / l).astype(jnp.bfloat16), v, preferred_element_type=jnp.float32)
        o_ref[0, :, h * V_HEAD:(h + 1) * V_HEAD] = o


def attend(q, k, v, tq=256):
    B, Lq = q.shape[:2]
    Lk = k.shape[1]
    tq = min(tq, Lq)
    return pl.pallas_call(
        _mla_attn_kernel, grid=(B, Lq // tq),
        in_specs=[pl.BlockSpec((1, tq, MLA_HEADS * QK_HEAD), lambda b, i: (b, i, 0)),
                  pl.BlockSpec((1, Lk, MLA_HEADS * QK_HEAD), lambda b, i: (b, 0, 0)),
                  pl.BlockSpec((1, Lk, MLA_HEADS * V_HEAD), lambda b, i: (b, 0, 0))],
        out_specs=pl.BlockSpec((1, tq, MLA_HEADS * V_HEAD), lambda b, i: (b, i, 0)),
        out_shape=jax.ShapeDtypeStruct((B, Lq, MLA_HEADS * V_HEAD), jnp.float32),
        compiler_params=pltpu.CompilerParams(dimension_semantics=("arbitrary", "arbitrary"),
                                             vmem_limit_bytes=VMEM_LIMIT),
        name="mla_attend",
    )(q.reshape(B, Lq, -1), k.reshape(B, Lk, -1), v.reshape(B, Lk, -1))


def _dot3(a_hi, a_lo, b_hi, b_lo):
    f32 = jnp.float32
    return (jnp.dot(a_hi, b_hi, preferred_element_type=f32) + jnp.dot(a_lo, b_hi, preferred_element_type=f32)
            + jnp.dot(a_hi, b_lo, preferred_element_type=f32))


def _dft_pair_kernel(c_hi, c_lo, s_hi, s_lo, x_hi, x_lo, oc_ref, os_ref):
    oc_ref[0] = _dot3(c_hi[...], c_lo[...], x_hi[0], x_lo[0])
    os_ref[0] = _dot3(s_hi[...], s_lo[...], x_hi[0], x_lo[0])


def _dft_sum_kernel(c_hi, c_lo, s_hi, s_lo, x_hi, x_lo, y_hi, y_lo, o_ref):
    o_ref[0] = _dot3(c_hi[...], c_lo[...], x_hi[0], x_lo[0]) + _dot3(s_hi[...], s_lo[...], y_hi[0], y_lo[0])


def dft_tables(L):
    f = jnp.arange(L, dtype=jnp.int32)
    ph = (f[:, None] * f[None, :]) % (2 * L)
    ang = ph.astype(jnp.float32) * (math.pi / L)
    c = jnp.cos(ang)
    s = -jnp.sin(ang)
    alt = jnp.where(f % 2 == 0, 1.0, -1.0).astype(jnp.float32)
    s_ana = jnp.where(f[:, None] == 0, alt[None, :], s)
    s_syn = jnp.where(f[None, :] == 0, alt[:, None], s)
    cc = _split_bf16(c)
    return cc + _split_bf16(s_ana), cc + _split_bf16(s_syn)


def _dft_call(body, tabs, xs, n_out, name, rb=256, nb=1024):
    B, L, N = xs[0].shape
    rb = min(rb, L)
    nb = min(nb, N)
    tab = pl.BlockSpec((rb, L), lambda b, n, i: (i, 0))
    xin = pl.BlockSpec((1, L, nb), lambda b, n, i: (b, 0, n))
    out = pl.BlockSpec((1, rb, nb), lambda b, n, i: (b, i, n))
    sd = jax.ShapeDtypeStruct((B, L, N), jnp.float32)
    halves = [h for x in xs for h in _split_bf16(x)]
    return pl.pallas_call(
        body, grid=(B, N // nb, L // rb),
        in_specs=[tab] * 4 + [xin] * len(halves),
        out_specs=[out] * n_out if n_out > 1 else out,
        out_shape=[sd] * n_out if n_out > 1 else sd,
        compiler_params=pltpu.CompilerParams(dimension_semantics=("arbitrary",) * 3,
                                             vmem_limit_bytes=VMEM_LIMIT),
        name=name,
    )(*tabs, *halves)


def hyena_longconv(zin, circ):
    B, L, D = zin.shape
    tabs, tabs_syn = dft_tables(L)
    alt = jnp.where(jnp.arange(L) % 2 == 0, 1.0, -1.0).astype(jnp.float32)[:, None]
    fc, fs = _dft_call(_dft_pair_kernel, tabs, [jnp.concatenate([circ[:L], circ[L:]], axis=-1)[None]], 2, "dft_filter")
    hr = fc[0, :, :D] + alt * fc[0, :, D:]
    hi = fs[0, :, :D] + alt * fs[0, :, D:]
    zc, zs = _dft_call(_dft_pair_kernel, tabs, [zin], 2, "dft_analysis")
    row0 = (jnp.arange(L) == 0)[None, :, None]
    yr = jnp.where(row0, zc * hr * (0.5 / L), (zc * hr - zs * hi) * (1.0 / L))
    yi = jnp.where(row0, zs * hi * (0.5 / L), (zc * hi + zs * hr) * (1.0 / L))
    return _dft_call(_dft_sum_kernel, tabs_syn, [yr, yi], 1, "dft_synthesis")


SCAN_ACCS = 4


def _rwkv_scan_kernel(r_ref, kk_ref, v_ref, w_ref, kd_ref, b_ref, s0_ref, y_ref, sf_ref, s_scr):
    d = pl.program_id(0)
    c = pl.program_id(1)
    tc = r_ref.shape[0]
    nv = v_ref.shape[1]

    @pl.when(c == 0)
    def _():
        s_scr[...] = s0_ref[0]

    def row(ref, tt, k):
        return jnp.broadcast_to(ref[tt, pl.ds(k, 1), :], (nv, LANES))

    def row_d(ref, tt, k):
        return jnp.broadcast_to(ref[0, tt, pl.ds(k, 1), :], (nv, LANES))

    def step(t, carry):
        tt = t + d * (tc - 1 - 2 * t)
        parts = [None] * SCAN_ACCS
        for k in range(RW_HEAD):
            term = s_scr[k] * row(kk_ref, tt, k)
            parts[k % SCAN_ACCS] = term if parts[k % SCAN_ACCS] is None else parts[k % SCAN_ACCS] + term
        sa = -((parts[0] + parts[1]) + (parts[2] + parts[3]))
        vv = v_ref[tt]
        parts = [None] * SCAN_ACCS
        for k in range(RW_HEAD):
            s_new = s_scr[k] * row_d(w_ref, tt, k) + sa * row_d(b_ref, tt, k) + vv * row_d(kd_ref, tt, k)
            s_scr[k] = s_new
            term = s_new * row(r_ref, tt, k)
            parts[k % SCAN_ACCS] = term if parts[k % SCAN_ACCS] is None else parts[k % SCAN_ACCS] + term
        y_ref[0, tt] = (parts[0] + parts[1]) + (parts[2] + parts[3])
        return carry

    lax.fori_loop(0, tc, step, 0)

    @pl.when(c == pl.num_programs(1) - 1)
    def _():
        sf_ref[0] = s_scr[...]


def rwkv_scan(r, kk, v, w2, kd2, b2, s0, tc=32):
    L = r.shape[0]
    nv = v.shape[1]
    nc = L // tc
    tmap = lambda d, c: c + d * (nc - 1 - 2 * c)
    shared = lambda rows: pl.BlockSpec((tc, rows, LANES), lambda d, c: (tmap(d, c), 0, 0))
    per_dir = pl.BlockSpec((1, tc, RW_HEAD, LANES), lambda d, c: (d, tmap(d, c), 0, 0))
    state = pl.BlockSpec((1, RW_HEAD, nv, LANES), lambda d, c: (d, 0, 0, 0))
    return pl.pallas_call(
        _rwkv_scan_kernel,
        grid=(2, nc),
        in_specs=[shared(RW_HEAD), shared(RW_HEAD), shared(nv), per_dir, per_dir, per_dir, state],
        out_specs=[pl.BlockSpec((1, tc, nv, LANES), lambda d, c: (d, tmap(d, c), 0, 0)), state],
        out_shape=[jax.ShapeDtypeStruct((2, L, nv, LANES), jnp.float32),
                   jax.ShapeDtypeStruct((2, RW_HEAD, nv, LANES), jnp.float32)],
        scratch_shapes=[pltpu.VMEM((RW_HEAD, nv, LANES), jnp.float32)],
        compiler_params=pltpu.CompilerParams(dimension_semantics=("arbitrary", "arbitrary"),
                                             vmem_limit_bytes=VMEM_LIMIT),
        name="rwkv_scan",
    )(r, kk, v, w2, kd2, b2, s0)


def to_lanes(x, vsplit):
    B, L, H, N = x.shape
    y = x.transpose(1, 3, 0, 2).reshape(L, N, B * H)
    return jnp.repeat(y, vsplit, axis=-1) if vsplit > 1 else y


def v_to_lanes(x, vsplit):
    B, L, H, N = x.shape
    nv = N // vsplit
    return x.reshape(B, L, H, vsplit, nv).transpose(1, 4, 0, 2, 3).reshape(L, nv, B * H * vsplit)


def v_from_lanes(y, B, H, vsplit):
    L, nv, _ = y.shape
    return y.reshape(L, nv, B, H, vsplit).transpose(2, 0, 3, 4, 1).reshape(B, L, H, vsplit * nv)


def state_to_lanes(s, vsplit):
    B, H, N, K = s.shape
    nv = N // vsplit
    return s.reshape(B, H, vsplit, nv, K).transpose(4, 3, 0, 1, 2).reshape(K, nv, B * H * vsplit)


def state_from_lanes(s, B, H, vsplit):
    K, nv, _ = s.shape
    return s.reshape(K, nv, B, H, vsplit).transpose(2, 3, 4, 1, 0).reshape(B, H, vsplit * nv, K)


def rwkv_scan_both(rh, kk, vh, decay2, kd2, b2, s0_2, tc=32):
    B, L, H, N = rh.shape
    vsplit = LANES // (B * H)
    assert B * H * vsplit == LANES
    stack = lambda xs: jnp.stack([to_lanes(x, vsplit) for x in xs])
    y2, sf2 = rwkv_scan(to_lanes(rh, vsplit), to_lanes(kk, vsplit), v_to_lanes(vh, vsplit),
                        stack(decay2), stack(kd2), stack(b2),
                        jnp.stack([state_to_lanes(s, vsplit) for s in s0_2]), tc=tc)
    ys = [v_from_lanes(y2[d], B, H, vsplit) for d in range(2)]
    sfs = [state_from_lanes(sf2[d], B, H, vsplit) for d in range(2)]
    return ys, sfs


def mla_keys(ckv, k_pe, kv_up, kn):
    B, L, _ = ckv.shape
    kv = (ckv @ kv_up).reshape(B, L, MLA_HEADS, QK_NOPE + V_HEAD)
    k_rope = jnp.broadcast_to(k_pe[:, :, None, :], (B, L, MLA_HEADS, QK_ROPE))
    k = rmsnorm(jnp.concatenate([kv[..., :QK_NOPE], k_rope], axis=-1), kn)
    return k, kv[..., QK_NOPE:]


def rwkv_mla_mixer(h, ctx, p):
    (w_in, mu, w0, w2, a0, a2, g2, k_k, k_a, r_k, lnx_g, lnx_b,
     q_norm, q_up, kv_norm, kv_up, qn, kn, w_out) = p
    B, L, _ = h.shape
    f32 = jnp.float32
    proj = _mm(h, w_in)
    rw, mla = proj[..., :RW_IN], proj[..., RW_IN:]
    rw = rw + mu * (centred_shift(rw) - rw)
    r, k, v, wd, ad, gd = jnp.split(rw, RW_SPLITS, axis=-1)
    heads = lambda t: t.reshape(B, L, RW_HEADS, RW_HEAD)
    rh, vh = heads(r.astype(f32)), heads(v.astype(f32))
    kk = heads((k * k_k).astype(f32))
    kk = kk / jnp.maximum(jnp.linalg.norm(kk, axis=-1, keepdims=True), 1e-12)
    if ctx is None:
        z = jnp.zeros((B, RW_HEADS, RW_HEAD, RW_HEAD), f32)
        s0s = (z, z)
    else:
        s0s = (ctx[0], ctx[1])
    decays, kds, bs, bonus = [], [], [], []
    for d in range(2):
        w_raw = (w0[d] + jnp.tanh(wd) @ w2[d]).astype(f32)
        decay = jnp.exp(-jnp.exp(-jax.nn.softplus(-w_raw) - 0.5))
        lr = jax.nn.sigmoid((a0[d] + ad @ a2[d]).astype(f32))
        kd = heads(k.astype(f32) * (1.0 + (lr - 1.0) * k_a))
        decays.append(heads(decay))
        kds.append(kd)
        bs.append(kk * heads(lr))
        bonus.append(jnp.sum(rh * kd * r_k, axis=-1, keepdims=True) * vh)
    ys, finals = rwkv_scan_both(rh, kk, vh, decays, kds, bs, s0s)
    y = ys[0] + ys[1]
    mean = jnp.mean(y, axis=-1, keepdims=True)
    var = jnp.mean(jnp.square(y - mean), axis=-1, keepdims=True)
    yn = ((y - mean) * lax.rsqrt(var + LNX_EPS)).reshape(B, L, RW_DIM) * lnx_g + lnx_b
    gate = jax.nn.sigmoid(gd) @ g2
    rw_out = ((yn + (bonus[0] + bonus[1]).reshape(B, L, RW_DIM)) * gate).astype(h.dtype)
    q_c, kv_c, k_pe = jnp.split(mla, (Q_LORA, Q_LORA + KV_LORA), axis=-1)
    q = rmsnorm((rmsnorm(q_c, q_norm) @ q_up).reshape(B, L, MLA_HEADS, QK_HEAD), qn)
    ckv = rmsnorm(kv_c, kv_norm)
    k_own, v_own = mla_keys(ckv, k_pe, kv_up, kn)
    if ctx is None:
        keys, vals = k_own, v_own
    else:
        cos, sin = ctx[4]
        q = rope_tail(q, cos, sin)
        k_own = rope_tail(k_own, cos, sin)
        k_ctx, v_ctx = mla_keys(ctx[2], ctx[3], kv_up, kn)
        keys = jnp.concatenate([k_ctx, k_own], axis=1)
        vals = jnp.concatenate([v_ctx, v_own], axis=1)
    mla_out = attend(q, keys, vals)
    out = _mm(jnp.concatenate([rw_out, mla_out.astype(h.dtype)], axis=-1), w_out)
    return out, finals[0], finals[1], ckv, k_pe


def hyena_filters(L, w1, b1, w2, b2, w3, freq):
    f32 = jnp.float32
    t = jnp.arange(L, dtype=f32)[:, None]
    t_unit = t / (L - 1)
    bands = jnp.linspace(1e-4, HY_BANDS - 1, HY_BANDS, dtype=f32)
    ang = 2.0 * math.pi * t * bands / L
    zpos = jnp.concatenate([t_unit, jnp.cos(ang), -jnp.sin(ang)], axis=-1)
    fr = freq.astype(f32)
    hid = jnp.sin(fr * (zpos @ w1.astype(f32) + b1.astype(f32)))
    hid = jnp.sin(fr * (hid @ w2.astype(f32) + b2.astype(f32)))
    filt = hid @ w3.astype(f32)
    deltas = jnp.linspace(math.log(HY_TARGET) / HY_FAST, math.log(HY_TARGET) / HY_SLOW, D_MODEL, dtype=f32)
    window = jnp.exp(-t_unit * jnp.abs(deltas))
    h_f = filt[:, :D_MODEL] * window
    h_b = filt[:, D_MODEL:] * window
    circ = jnp.concatenate([h_f, jnp.zeros((1, D_MODEL), f32), h_b[:0:-1]], axis=0)
    return circ / jnp.sum(jnp.abs(circ), axis=0, keepdims=True)


def hyena_mixer(h, p):
    w_in, b_in, conv_w, conv_b, f_w1, f_b1, f_w2, f_b2, f_w3, f_freq, bias, w_out = p
    B, L, _ = h.shape
    u = centred_dwconv3(_mm(h, w_in) + b_in, conv_w, conv_b)
    x0, x1, v = jnp.split(u, 3, axis=-1)
    zin = (x1 * v).astype(jnp.float32)
    conv = hyena_longconv(zin, hyena_filters(L, f_w1, f_b1, f_w2, f_b2, f_w3, f_freq))
    y = conv + zin * bias.astype(jnp.float32)
    return _mm(x0 * y.astype(h.dtype), w_out)


PEER_N = PEER_KEYS * PEER_KEYS


def _top_vals(s, n):
    vals = []
    for _ in range(n):
        m = jnp.max(s, axis=0, keepdims=True)
        vals.append(m)
        s = jnp.where(s == m, -jnp.inf, s)
    return vals


def _peer_route_kernel(h_ref, wq_hi_ref, wq_lo_ref, k_hi_ref, k_lo_ref,
                       s1_ref, s2_ref, e1_ref, e2_ref, tau_ref, s_scr):
    tb = h_ref.shape[0]
    half = PEER_DKEY // 2
    h_hi, h_lo = _split_bf16(h_ref[...])
    w_hi = wq_hi_ref[...]
    q = (jnp.dot(h_hi, w_hi, preferred_element_type=jnp.float32)
         + jnp.dot(h_lo, w_hi, preferred_element_type=jnp.float32)
         + jnp.dot(h_hi, wq_lo_ref[...], preferred_element_type=jnp.float32))
    for p in range(2):
        q_hi, q_lo = _split_bf16(q[:, p * half:(p + 1) * half])
        k_hi = k_hi_ref[0, p]
        s_scr[p] = (lax.dot_general(k_hi, q_hi, _NT, preferred_element_type=jnp.float32)
                    + lax.dot_general(k_lo_ref[0, p], q_hi, _NT, preferred_element_type=jnp.float32)
                    + lax.dot_general(k_hi, q_lo, _NT, preferred_element_type=jnp.float32))

    def tile(lt, carry):
        ln = pl.ds(pl.multiple_of(lt * LANES, LANES), LANES)
        s1 = s_scr[0, :, ln]
        s2 = s_scr[1, :, ln]
        v1 = _top_vals(s1, PEER_TOPK)
        v2 = jnp.concatenate(_top_vals(s2, PEER_TOPK), axis=0)
        cand = [v1[a] + v2 for a in range(PEER_TOPK)]
        c = cand
        for k in range(PEER_TOPK):
            m = c[0]
            for ci in c[1:]:
                m = jnp.maximum(m, ci)
            m = jnp.max(m, axis=0, keepdims=True)
            if k + 1 < PEER_TOPK:
                c = [jnp.where(ci == m, -jnp.inf, ci) for ci in c]
        tau = m
        top = v1[0] + v2[0:1]
        z = jnp.zeros_like(tau)
        for ci in cand:
            z = z + jnp.sum(jnp.where(ci >= tau, jnp.exp(ci - top), 0.0), axis=0, keepdims=True)
        s1_ref[0, :, ln] = s1
        s2_ref[0, :, ln] = s2
        e1_ref[0, :, ln] = jnp.exp(s1 - v1[0]) / z
        e2_ref[0, :, ln] = jnp.exp(s2 - v2[0:1])
        tau_ref[0, :, ln] = jnp.broadcast_to(tau, (8, LANES))
        return carry

    lax.fori_loop(0, tb // LANES, tile, 0)


def peer_route(h, wq_hi, wq_lo, k_hi, k_lo, tb=256):
    T = h.shape[0]
    nh = PEER_HEADS
    out_sd = jax.ShapeDtypeStruct((nh, PEER_KEYS, T), jnp.float32)
    blk = pl.BlockSpec((1, PEER_KEYS, tb), lambda t, hh: (hh, 0, t))
    return pl.pallas_call(
        _peer_route_kernel,
        grid=(T // tb, nh),
        in_specs=[pl.BlockSpec((tb, D_MODEL), lambda t, hh: (t, 0)),
                  pl.BlockSpec((D_MODEL, PEER_DKEY), lambda t, hh: (0, hh)),
                  pl.BlockSpec((D_MODEL, PEER_DKEY), lambda t, hh: (0, hh)),
                  pl.BlockSpec((1, 2, PEER_KEYS, PEER_DKEY // 2), lambda t, hh: (hh, 0, 0, 0)),
                  pl.BlockSpec((1, 2, PEER_KEYS, PEER_DKEY // 2), lambda t, hh: (hh, 0, 0, 0))],
        out_specs=[blk, blk, blk, blk, pl.BlockSpec((1, 8, tb), lambda t, hh: (hh, 0, t))],
        out_shape=[out_sd, out_sd, out_sd, out_sd, jax.ShapeDtypeStruct((nh, 8, T), jnp.float32)],
        scratch_shapes=[pltpu.VMEM((2, PEER_KEYS, tb), jnp.float32)],
        compiler_params=pltpu.CompilerParams(dimension_semantics=("arbitrary", "arbitrary"),
                                             vmem_limit_bytes=VMEM_LIMIT),
        name="peer_route",
    )(h, wq_hi, wq_lo, k_hi, k_lo)


def _gelu_tanh(x):
    return 0.5 * x * (1.0 + jnp.tanh(0.7978845608028654 * (x + 0.044715 * (x * x * x))))


def _peer_expert_kernel(h_ref, u_ref, vt_ref, s1_ref, s2_ref, e1_ref, e2_ref, tau_ref,
                        o_ref, acc_ref, a_scr, w_scr):
    c = pl.program_id(1)
    ec, tb = a_scr.shape
    n_i = ec // PEER_KEYS

    @pl.when(c == 0)
    def _():
        acc_ref[...] = jnp.zeros_like(acc_ref)

    a_scr[...] = lax.dot_general(u_ref[...], h_ref[...], _NT, preferred_element_type=jnp.float32)

    igrp = pl.ds(pl.multiple_of(c * n_i, SUBLANES), SUBLANES)

    def tile(lt, carry):
        ln = pl.ds(pl.multiple_of(lt * LANES, LANES), LANES)
        for ii in range(n_i):
            rows = slice(ii * PEER_KEYS, (ii + 1) * PEER_KEYS)
            g = jnp.zeros((PEER_KEYS, LANES), jnp.float32)
            for hh in range(PEER_HEADS):
                s1r = s1_ref[hh, igrp, ln][ii:ii + 1]
                e1r = e1_ref[hh, igrp, ln][ii:ii + 1]
                tau = tau_ref[hh, 0:1, ln]
                sel = (s1r + s2_ref[hh, :, ln]) >= tau
                g = g + jnp.where(sel, e2_ref[hh, :, ln] * e1r, 0.0)
            w_scr[rows, ln] = (g * _gelu_tanh(a_scr[rows, ln])).astype(jnp.bfloat16)
        return carry

    lax.fori_loop(0, tb // LANES, tile, 0)
    acc_ref[...] += jnp.dot(vt_ref[...], w_scr[...], preferred_element_type=jnp.float32)

    @pl.when(c == pl.num_programs(1) - 1)
    def _():
        o_ref[...] = acc_ref[...].T


def peer_experts(h_bf16, u_bf16, vt_bf16, s1, s2, e1, e2, tau, tb=512, ec=SUBLANES * PEER_KEYS):
    T = h_bf16.shape[0]
    rblk = pl.BlockSpec((PEER_HEADS, PEER_KEYS, tb), lambda t, c: (0, 0, t))
    return pl.pallas_call(
        _peer_expert_kernel,
        grid=(T // tb, PEER_N // ec),
        in_specs=[pl.BlockSpec((tb, D_MODEL), lambda t, c: (t, 0)),
                  pl.BlockSpec((ec, D_MODEL), lambda t, c: (c, 0)),
                  pl.BlockSpec((D_MODEL, ec), lambda t, c: (0, c)),
                  rblk, rblk, rblk, rblk,
                  pl.BlockSpec((PEER_HEADS, 8, tb), lambda t, c: (0, 0, t))],
        out_specs=pl.BlockSpec((tb, D_MODEL), lambda t, c: (t, 0)),
        out_shape=jax.ShapeDtypeStruct((T, D_MODEL), jnp.float32),
        scratch_shapes=[pltpu.VMEM((D_MODEL, tb), jnp.float32),
                        pltpu.VMEM((ec, tb), jnp.float32),
                        pltpu.VMEM((ec, tb), jnp.bfloat16)],
        compiler_params=pltpu.CompilerParams(dimension_semantics=("arbitrary", "arbitrary"),
                                             vmem_limit_bytes=VMEM_LIMIT),
        name="peer_experts",
    )(h_bf16, u_bf16, vt_bf16, s1, s2, e1, e2, tau)


def peer_ffn(h, w_q, sub_keys, u_tab, v_tab):
    wq_hi, wq_lo = _split_bf16(w_q)
    k_hi, k_lo = _split_bf16(sub_keys)
    s1, s2, e1, e2, tau = peer_route(h, wq_hi, wq_lo, k_hi, k_lo)
    return peer_experts(h.astype(jnp.bfloat16), u_tab.astype(jnp.bfloat16), v_tab.T.astype(jnp.bfloat16),
                        s1, s2, e1, e2, tau)


def kernel(x_prompt, x_sample, state_rwkv_fwd, state_rwkv_bwd, cache_mla_ckv, cache_mla_kpe, c, c_ctx,
           norm_g, w_mod, b_mod, ab_w_in, rw_mu, rw_w0, rw_w2, rw_a0, rw_a2, rw_g2, rw_k_k, rw_k_a, rw_r_k,
           rw_lnx_g, rw_lnx_b, mla_q_norm, mla_q_up, mla_kv_norm, mla_kv_up, mla_qn, mla_kn, ab_w_out,
           hy_w_in, hy_b_in, hy_conv_w, hy_conv_b, hy_f_w1, hy_f_b1, hy_f_w2, hy_f_b2, hy_f_w3, hy_f_freq,
           hy_bias, hy_w_out, peer_w_q, peer_keys, peer_u, peer_v):
    rope = axial_rope(x_sample.shape[1])
    xp, xs = x_prompt, x_sample
    st_f, st_b, st_ckv, st_kpe = [], [], [], []
    for li in range(DEPTH):
        j = li // 2
        mp = jnp.split(jax.nn.silu(c_ctx) @ w_mod[li] + b_mod[li], 6, axis=-1)
        ms = [m[:, None, :] for m in jnp.split(jax.nn.silu(c) @ w_mod[li] + b_mod[li], 6, axis=-1)]
        hp = modulate(rmsnorm(xp, norm_g[li, 0]), mp[0], mp[1])
        hs = modulate(rmsnorm(xs, norm_g[li, 0]), ms[0], ms[1])
        if li % 2 == 0:
            p = (ab_w_in[j], rw_mu[j], rw_w0[j], rw_w2[j], rw_a0[j], rw_a2[j], rw_g2[j], rw_k_k[j], rw_k_a[j],
                 rw_r_k[j], rw_lnx_g[j], rw_lnx_b[j], mla_q_norm[j], mla_q_up[j], mla_kv_norm[j], mla_kv_up[j],
                 mla_qn[j], mla_kn[j], ab_w_out[j])
            op, sf, sb, ckv, kpe = rwkv_mla_mixer(hp, None, p)
            ctx = (state_rwkv_fwd[:, j], state_rwkv_bwd[:, j], cache_mla_ckv[:, j], cache_mla_kpe[:, j], rope)
            os_, _, _, _, _ = rwkv_mla_mixer(hs, ctx, p)
            st_f.append(sf)
            st_b.append(sb)
            st_ckv.append(ckv)
            st_kpe.append(kpe)
        else:
            p = (hy_w_in[j], hy_b_in[j], hy_conv_w[j], hy_conv_b[j], hy_f_w1[j], hy_f_b1[j], hy_f_w2[j],
                 hy_f_b2[j], hy_f_w3[j], hy_f_freq[j], hy_bias[j], hy_w_out[j])
            op = hyena_mixer(hp, p)
            os_ = hyena_mixer(hs, p)
        xp = xp + mp[2] * op
        xs = xs + ms[2] * os_
        pp = (peer_w_q[li], peer_keys[li], peer_u[li], peer_v[li])
        gp = modulate(rmsnorm(xp, norm_g[li, 1]), mp[3], mp[4]).reshape(-1, D_MODEL)
        gs = modulate(rmsnorm(xs, norm_g[li, 1]), ms[3], ms[4]).reshape(-1, D_MODEL)
        po = peer_ffn(jnp.concatenate([gp, gs], axis=0), *pp)
        xp = xp + mp[5] * po[:gp.shape[0]].reshape(xp.shape)
        xs = xs + ms[5] * po[gp.shape[0]:].reshape(xs.shape)
    new_state_rwkv_fwd = jnp.stack(st_f, axis=1).astype(x_prompt.dtype)
    new_state_rwkv_bwd = jnp.stack(st_b, axis=1).astype(x_prompt.dtype)
    new_cache_mla_ckv = jnp.stack(st_ckv, axis=1)
    new_cache_mla_kpe = jnp.stack(st_kpe, axis=1)
    return (xp, xs, new_state_rwkv_fwd, new_state_rwkv_bwd, new_cache_mla_ckv, new_cache_mla_kpe)
```

```python
import math
from functools import partial

import jax
import jax.numpy as jnp
from jax import lax
from jax.experimental import pallas as pl
from jax.experimental.pallas import tpu as pltpu

D_MODEL = 1024
DEPTH = 2
GRID_W = 64
EPS = 1e-6
RW_HEADS = 8
RW_HEAD = 64
RW_DIM = RW_HEADS * RW_HEAD
W_LORA = 64
A_LORA = 64
G_LORA = 128
LNX_EPS = 64e-5
RW_IN = 3 * RW_DIM + W_LORA + A_LORA + G_LORA
RW_SPLITS = (RW_DIM, 2 * RW_DIM, 3 * RW_DIM, 3 * RW_DIM + W_LORA, 3 * RW_DIM + W_LORA + A_LORA)
MLA_HEADS = 4
QK_NOPE = 128
QK_ROPE = 64
QK_HEAD = QK_NOPE + QK_ROPE
V_HEAD = 128
Q_LORA = 256
KV_LORA = 128
MLA_DIM = MLA_HEADS * V_HEAD
ROPE_THETA = 10000.0
Q_BLOCK = 128
HY_BANDS = 16
HY_TARGET = 1e-2
HY_FAST = 0.3
HY_SLOW = 1.5
PEER_KEYS = 128
PEER_HEADS = 8
PEER_DKEY = 256
PEER_TOPK = 16
TOKEN_BLOCK = 128


def _mm_kernel(a_ref, b_ref, o_ref):
    o_ref[...] = jnp.dot(a_ref[...].astype(jnp.bfloat16), b_ref[...].astype(jnp.bfloat16),
                         preferred_element_type=jnp.float32)


def _mm(a, b, tm=512, tn=512):
    lead = a.shape[:-1]
    K = a.shape[-1]
    N = b.shape[-1]
    a2 = a.reshape(-1, K)
    M = a2.shape[0]
    tm = min(tm, M)
    tn = min(tn, N)
    if N % tn:
        tn = N
    assert M % tm == 0 and N % tn == 0
    out = pl.pallas_call(
        _mm_kernel,
        grid=(M // tm, N // tn),
        in_specs=[pl.BlockSpec((tm, K), lambda i, j: (i, 0)),
                  pl.BlockSpec((K, tn), lambda i, j: (0, j))],
        out_specs=pl.BlockSpec((tm, tn), lambda i, j: (i, j)),
        out_shape=jax.ShapeDtypeStruct((M, N), jnp.float32),
    )(a2, b)
    return out.reshape(*lead, N)


def rmsnorm(x, g):
    xf = x.astype(jnp.float32)
    y = xf * lax.rsqrt(jnp.mean(xf * xf, axis=-1, keepdims=True) + EPS)
    return (y * g.astype(jnp.float32)).astype(x.dtype)


def modulate(h, shift, scale):
    return h * (1.0 + scale) + shift


def centred_shift(x):
    prev = jnp.pad(x[:, :-1], ((0, 0), (1, 0), (0, 0)))
    nxt = jnp.pad(x[:, 1:], ((0, 0), (0, 1), (0, 0)))
    return 0.5 * (prev + nxt)


def centred_dwconv3(u, w, b):
    prev = jnp.pad(u[:, :-1], ((0, 0), (1, 0), (0, 0)))
    nxt = jnp.pad(u[:, 1:], ((0, 0), (0, 1), (0, 0)))
    return prev * w[0] + u * w[1] + nxt * w[2] + b


def axial_rope(L):
    rows = L // GRID_W
    row = jnp.repeat(jnp.arange(rows, dtype=jnp.float32), GRID_W)
    col = jnp.tile(jnp.arange(GRID_W, dtype=jnp.float32), rows)
    n_freq = QK_ROPE // 4
    inv = ROPE_THETA ** (-jnp.arange(n_freq, dtype=jnp.float32) / n_freq)
    ang = jnp.concatenate([row[:, None] * inv, col[:, None] * inv], axis=-1)
    return jnp.cos(ang), jnp.sin(ang)


def rope_tail(x, cos, sin):
    xn, xr = x[..., :QK_NOPE], x[..., QK_NOPE:]
    x1, x2 = xr[..., 0::2], xr[..., 1::2]
    cs = cos[None, :, None, :].astype(x.dtype)
    sn = sin[None, :, None, :].astype(x.dtype)
    rot = jnp.stack([x1 * cs - x2 * sn, x1 * sn + x2 * cs], axis=-1).reshape(xr.shape)
    return jnp.concatenate([xn, rot], axis=-1)


LANES = 128
SUBLANES = 8
VMEM_LIMIT = 56 * 1024 * 1024
_NT = (((1,), (1,)), ((), ()))


def _split_bf16(x):
    hi = x.astype(jnp.bfloat16)
    lo = (x - hi.astype(jnp.float32)).astype(jnp.bfloat16)
    return hi, lo


def _mla_attn_kernel(q_ref, k_ref, v_ref, o_ref):
    scale = QK_HEAD ** -0.5
    for h in range(MLA_HEADS):
        q = q_ref[0, :, h * QK_HEAD:(h + 1) * QK_HEAD].astype(jnp.bfloat16)
        k = k_ref[0, :, h * QK_HEAD:(h + 1) * QK_HEAD].astype(jnp.bfloat16)
        s = lax.dot_general(q, k, _NT, preferred_element_type=jnp.float32) * scale
        m = jnp.max(s, axis=-1, keepdims=True)
        p = jnp.exp(s - m)
        l = jnp.sum(p, axis=-1, keepdims=True)
        v = v_ref[0, :, h * V_HEAD:(h + 1) * V_HEAD].astype(jnp.bfloat16)
        o = jnp.dot((p / l).astype(jnp.bfloat16), v, preferred_element_type=jnp.float32)
        o_ref[0, :, h * V_HEAD:(h + 1) * V_HEAD] = o


def attend(q, k, v, tq=256):
    B, Lq = q.shape[:2]
    Lk = k.shape[1]
    tq = min(tq, Lq)
    return pl.pallas_call(
        _mla_attn_kernel, grid=(B, Lq // tq),
        in_specs=[pl.BlockSpec((1, tq, MLA_HEADS * QK_HEAD), lambda b, i: (b, i, 0)),
                  pl.BlockSpec((1, Lk, MLA_HEADS * QK_HEAD), lambda b, i: (b, 0, 0)),
                  pl.BlockSpec((1, Lk, MLA_HEADS * V_HEAD), lambda b, i: (b, 0, 0))],
        out_specs=pl.BlockSpec((1, tq, MLA_HEADS * V_HEAD), lambda b, i: (b, i, 0)),
        out_shape=jax.ShapeDtypeStruct((B, Lq, MLA_HEADS * V_HEAD), jnp.float32),
        compiler_params=pltpu.CompilerParams(dimension_semantics=("arbitrary", "arbitrary"),
                                             vmem_limit_bytes=VMEM_LIMIT),
        name="mla_attend",
    )(q.reshape(B, Lq, -1), k.reshape(B, Lk, -1), v.reshape(B, Lk, -1))


def _dot3(a_hi, a_lo, b_hi, b_lo):
    f32 = jnp.float32
    return (jnp.dot(a_hi, b_hi, preferred_element_type=f32) + jnp.dot(a_lo, b_hi, preferred_element_type=f32)
            + jnp.dot(a_hi, b_lo, preferred_element_type=f32))


def _dft_pair_kernel(c_hi, c_lo, s_hi, s_lo, x_hi, x_lo, oc_ref, os_ref):
    oc_ref[0] = _dot3(c_hi[...], c_lo[...], x_hi[0], x_lo[0])
    os_ref[0] = _dot3(s_hi[...], s_lo[...], x_hi[0], x_lo[0])


def _dft_sum_kernel(c_hi, c_lo, s_hi, s_lo, x_hi, x_lo, y_hi, y_lo, o_ref):
    o_ref[0] = _dot3(c_hi[...], c_lo[...], x_hi[0], x_lo[0]) + _dot3(s_hi[...], s_lo[...], y_hi[0], y_lo[0])


def dft_tables(L):
    f = jnp.arange(L, dtype=jnp.int32)
    ph = (f[:, None] * f[None, :]) % (2 * L)
    ang = ph.astype(jnp.float32) * (math.pi / L)
    c = jnp.cos(ang)
    s = -jnp.sin(ang)
    alt = jnp.where(f % 2 == 0, 1.0, -1.0).astype(jnp.float32)
    s_ana = jnp.where(f[:, None] == 0, alt[None, :], s)
    s_syn = jnp.where(f[None, :] == 0, alt[:, None], s)
    cc = _split_bf16(c)
    return cc + _split_bf16(s_ana), cc + _split_bf16(s_syn)


def _dft_call(body, tabs, xs, n_out, name, rb=256, nb=1024):
    B, L, N = xs[0].shape
    rb = min(rb, L)
    nb = min(nb, N)
    tab = pl.BlockSpec((rb, L), lambda b, n, i: (i, 0))
    xin = pl.BlockSpec((1, L, nb), lambda b, n, i: (b, 0, n))
    out = pl.BlockSpec((1, rb, nb), lambda b, n, i: (b, i, n))
    sd = jax.ShapeDtypeStruct((B, L, N), jnp.float32)
    halves = [h for x in xs for h in _split_bf16(x)]
    return pl.pallas_call(
        body, grid=(B, N // nb, L // rb),
        in_specs=[tab] * 4 + [xin] * len(halves),
        out_specs=[out] * n_out if n_out > 1 else out,
        out_shape=[sd] * n_out if n_out > 1 else sd,
        compiler_params=pltpu.CompilerParams(dimension_semantics=("arbitrary",) * 3,
                                             vmem_limit_bytes=VMEM_LIMIT),
        name=name,
    )(*tabs, *halves)


def hyena_longconv(zin, circ):
    B, L, D = zin.shape
    tabs, tabs_syn = dft_tables(L)
    alt = jnp.where(jnp.arange(L) % 2 == 0, 1.0, -1.0).astype(jnp.float32)[:, None]
    fc, fs = _dft_call(_dft_pair_kernel, tabs, [jnp.concatenate([circ[:L], circ[L:]], axis=-1)[None]], 2, "dft_filter")
    hr = fc[0, :, :D] + alt * fc[0, :, D:]
    hi = fs[0, :, :D] + alt * fs[0, :, D:]
    zc, zs = _dft_call(_dft_pair_kernel, tabs, [zin], 2, "dft_analysis")
    row0 = (jnp.arange(L) == 0)[None, :, None]
    yr = jnp.where(row0, zc * hr * (0.5 / L), (zc * hr - zs * hi) * (1.0 / L))
    yi = jnp.where(row0, zs * hi * (0.5 / L), (zc * hi + zs * hr) * (1.0 / L))
    return _dft_call(_dft_sum_kernel, tabs_syn, [yr, yi], 1, "dft_synthesis")


SCAN_ACCS = 4


def _rwkv_scan_kernel(r_ref, kk_ref, v_ref, w_ref, kd_ref, b_ref, s0_ref, y_ref, sf_ref, s_scr):
    d = pl.program_id(0)
    c = pl.program_id(1)
    tc = r_ref.shape[0]
    nv = v_ref.shape[1]

    @pl.when(c == 0)
    def _():
        s_scr[...] = s0_ref[0]

    def row(ref, tt, k):
        return jnp.broadcast_to(ref[tt, pl.ds(k, 1), :], (nv, LANES))

    def row_d(ref, tt, k):
        return jnp.broadcast_to(ref[0, tt, pl.ds(k, 1), :], (nv, LANES))

    def step(t, carry):
        tt = t + d * (tc - 1 - 2 * t)
        parts = [None] * SCAN_ACCS
        for k in range(RW_HEAD):
            term = s_scr[k] * row(kk_ref, tt, k)
            parts[k % SCAN_ACCS] = term if parts[k % SCAN_ACCS] is None else parts[k % SCAN_ACCS] + term
        sa = -((parts[0] + parts[1]) + (parts[2] + parts[3]))
        vv = v_ref[tt]
        parts = [None] * SCAN_ACCS
        for k in range(RW_HEAD):
            s_new = s_scr[k] * row_d(w_ref, tt, k) + sa * row_d(b_ref, tt, k) + vv * row_d(kd_ref, tt, k)
            s_scr[k] = s_new
            term = s_new * row(r_ref, tt, k)
            parts[k % SCAN_ACCS] = term if parts[k % SCAN_ACCS] is None else parts[k % SCAN_ACCS] + term
        y_ref[0, tt] = (parts[0] + parts[1]) + (parts[2] + parts[3])
        return carry

    lax.fori_loop(0, tc, step, 0)

    @pl.when(c == pl.num_programs(1) - 1)
    def _():
        sf_ref[0] = s_scr[...]


def rwkv_scan(r, kk, v, w2, kd2, b2, s0, tc=32):
    L = r.shape[0]
    nv = v.shape[1]
    nc = L // tc
    tmap = lambda d, c: c + d * (nc - 1 - 2 * c)
    shared = lambda rows: pl.BlockSpec((tc, rows, LANES), lambda d, c: (tmap(d, c), 0, 0))
    per_dir = pl.BlockSpec((1, tc, RW_HEAD, LANES), lambda d, c: (d, tmap(d, c), 0, 0))
    state = pl.BlockSpec((1, RW_HEAD, nv, LANES), lambda d, c: (d, 0, 0, 0))
    return pl.pallas_call(
        _rwkv_scan_kernel,
        grid=(2, nc),
        in_specs=[shared(RW_HEAD), shared(RW_HEAD), shared(nv), per_dir, per_dir, per_dir, state],
        out_specs=[pl.BlockSpec((1, tc, nv, LANES), lambda d, c: (d, tmap(d, c), 0, 0)), state],
        out_shape=[jax.ShapeDtypeStruct((2, L, nv, LANES), jnp.float32),
                   jax.ShapeDtypeStruct((2, RW_HEAD, nv, LANES), jnp.float32)],
        scratch_shapes=[pltpu.VMEM((RW_HEAD, nv, LANES), jnp.float32)],
        compiler_params=pltpu.CompilerParams(dimension_semantics=("arbitrary", "arbitrary"),
                                             vmem_limit_bytes=VMEM_LIMIT),
        name="rwkv_scan",
    )(r, kk, v, w2, kd2, b2, s0)


def to_lanes(x, vsplit):
    B, L, H, N = x.shape
    y = x.transpose(1, 3, 0, 2).reshape(L, N, B * H)
    return jnp.repeat(y, vsplit, axis=-1) if vsplit > 1 else y


def v_to_lanes(x, vsplit):
    B, L, H, N = x.shape
    nv = N // vsplit
    return x.reshape(B, L, H, vsplit, nv).transpose(1, 4, 0, 2, 3).reshape(L, nv, B * H * vsplit)


def v_from_lanes(y, B, H, vsplit):
    L, nv, _ = y.shape
    return y.reshape(L, nv, B, H, vsplit).transpose(2, 0, 3, 4, 1).reshape(B, L, H, vsplit * nv)


def state_to_lanes(s, vsplit):
    B, H, N, K = s.shape
    nv = N // vsplit
    return s.reshape(B, H, vsplit, nv, K).transpose(4, 3, 0, 1, 2).reshape(K, nv, B * H * vsplit)


def state_from_lanes(s, B, H, vsplit):
    K, nv, _ = s.shape
    return s.reshape(K, nv, B, H, vsplit).transpose(2, 3, 4, 1, 0).reshape(B, H, vsplit * nv, K)


def rwkv_scan_both(rh, kk, vh, decay2, kd2, b2, s0_2, tc=32):
    B, L, H, N = rh.shape
    vsplit = LANES // (B * H)
    assert B * H * vsplit == LANES
    stack = lambda xs: jnp.stack([to_lanes(x, vsplit) for x in xs])
    y2, sf2 = rwkv_scan(to_lanes(rh, vsplit), to_lanes(kk, vsplit), v_to_lanes(vh, vsplit),
                        stack(decay2), stack(kd2), stack(b2),
                        jnp.stack([state_to_lanes(s, vsplit) for s in s0_2]), tc=tc)
    ys = [v_from_lanes(y2[d], B, H, vsplit) for d in range(2)]
    sfs = [state_from_lanes(sf2[d], B, H, vsplit) for d in range(2)]
    return ys, sfs


def mla_keys(ckv, k_pe, kv_up, kn):
    B, L, _ = ckv.shape
    kv = (ckv @ kv_up).reshape(B, L, MLA_HEADS, QK_NOPE + V_HEAD)
    k_rope = jnp.broadcast_to(k_pe[:, :, None, :], (B, L, MLA_HEADS, QK_ROPE))
    k = rmsnorm(jnp.concatenate([kv[..., :QK_NOPE], k_rope], axis=-1), kn)
    return k, kv[..., QK_NOPE:]


def rwkv_mla_mixer(h, ctx, p):
    (w_in, mu, w0, w2, a0, a2, g2, k_k, k_a, r_k, lnx_g, lnx_b,
     q_norm, q_up, kv_norm, kv_up, qn, kn, w_out) = p
    B, L, _ = h.shape
    f32 = jnp.float32
    proj = _mm(h, w_in)
    rw, mla = proj[..., :RW_IN], proj[..., RW_IN:]
    rw = rw + mu * (centred_shift(rw) - rw)
    r, k, v, wd, ad, gd = jnp.split(rw, RW_SPLITS, axis=-1)
    heads = lambda t: t.reshape(B, L, RW_HEADS, RW_HEAD)
    rh, vh = heads(r.astype(f32)), heads(v.astype(f32))
    kk = heads((k * k_k).astype(f32))
    kk = kk / jnp.maximum(jnp.linalg.norm(kk, axis=-1, keepdims=True), 1e-12)
    if ctx is None:
        z = jnp.zeros((B, RW_HEADS, RW_HEAD, RW_HEAD), f32)
        s0s = (z, z)
    else:
        s0s = (ctx[0], ctx[1])
    decays, kds, bs, bonus = [], [], [], []
    for d in range(2):
        w_raw = (w0[d] + jnp.tanh(wd) @ w2[d]).astype(f32)
        decay = jnp.exp(-jnp.exp(-jax.nn.softplus(-w_raw) - 0.5))
        lr = jax.nn.sigmoid((a0[d] + ad @ a2[d]).astype(f32))
        kd = heads(k.astype(f32) * (1.0 + (lr - 1.0) * k_a))
        decays.append(heads(decay))
        kds.append(kd)
        bs.append(kk * heads(lr))
        bonus.append(jnp.sum(rh * kd * r_k, axis=-1, keepdims=True) * vh)
    ys, finals = rwkv_scan_both(rh, kk, vh, decays, kds, bs, s0s)
    y = ys[0] + ys[1]
    mean = jnp.mean(y, axis=-1, keepdims=True)
    var = jnp.mean(jnp.square(y - mean), axis=-1, keepdims=True)
    yn = ((y - mean) * lax.rsqrt(var + LNX_EPS)).reshape(B, L, RW_DIM) * lnx_g + lnx_b
    gate = jax.nn.sigmoid(gd) @ g2
    rw_out = ((yn + (bonus[0] + bonus[1]).reshape(B, L, RW_DIM)) * gate).astype(h.dtype)
    q_c, kv_c, k_pe = jnp.split(mla, (Q_LORA, Q_LORA + KV_LORA), axis=-1)
    q = rmsnorm((rmsnorm(q_c, q_norm) @ q_up).reshape(B, L, MLA_HEADS, QK_HEAD), qn)
    ckv = rmsnorm(kv_c, kv_norm)
    k_own, v_own = mla_keys(ckv, k_pe, kv_up, kn)
    if ctx is None:
        keys, vals = k_own, v_own
    else:
        cos, sin = ctx[4]
        q = rope_tail(q, cos, sin)
        k_own = rope_tail(k_own, cos, sin)
        k_ctx, v_ctx = mla_keys(ctx[2], ctx[3], kv_up, kn)
        keys = jnp.concatenate([k_ctx, k_own], axis=1)
        vals = jnp.concatenate([v_ctx, v_own], axis=1)
    mla_out = attend(q, keys, vals)
    out = _mm(jnp.concatenate([rw_out, mla_out.astype(h.dtype)], axis=-1), w_out)
    return out, finals[0], finals[1], ckv, k_pe


def hyena_filters(L, w1, b1, w2, b2, w3, freq):
    f32 = jnp.float32
    t = jnp.arange(L, dtype=f32)[:, None]
    t_unit = t / (L - 1)
    bands = jnp.linspace(1e-4, HY_BANDS - 1, HY_BANDS, dtype=f32)
    ang = 2.0 * math.pi * t * bands / L
    zpos = jnp.concatenate([t_unit, jnp.cos(ang), -jnp.sin(ang)], axis=-1)
    fr = freq.astype(f32)
    hid = jnp.sin(fr * (zpos @ w1.astype(f32) + b1.astype(f32)))
    hid = jnp.sin(fr * (hid @ w2.astype(f32) + b2.astype(f32)))
    filt = hid @ w3.astype(f32)
    deltas = jnp.linspace(math.log(HY_TARGET) / HY_FAST, math.log(HY_TARGET) / HY_SLOW, D_MODEL, dtype=f32)
    window = jnp.exp(-t_unit * jnp.abs(deltas))
    h_f = filt[:, :D_MODEL] * window
    h_b = filt[:, D_MODEL:] * window
    circ = jnp.concatenate([h_f, jnp.zeros((1, D_MODEL), f32), h_b[:0:-1]], axis=0)
    return circ / jnp.sum(jnp.abs(circ), axis=0, keepdims=True)


def hyena_mixer(h, p):
    w_in, b_in, conv_w, conv_b, f_w1, f_b1, f_w2, f_b2, f_w3, f_freq, bias, w_out = p
    B, L, _ = h.shape
    u = centred_dwconv3(_mm(h, w_in) + b_in, conv_w, conv_b)
    x0, x1, v = jnp.split(u, 3, axis=-1)
    zin = (x1 * v).astype(jnp.float32)
    conv = hyena_longconv(zin, hyena_filters(L, f_w1, f_b1, f_w2, f_b2, f_w3, f_freq))
    y = conv + zin * bias.astype(jnp.float32)
    return _mm(x0 * y.astype(h.dtype), w_out)


PEER_N = PEER_KEYS * PEER_KEYS
GATE_LANES = 2 * LANES


def _top_vals(s, n, with_rank):
    vals = []
    rank = jnp.full(s.shape, float(n), jnp.float32) if with_rank else None
    for a in range(n):
        m = jnp.max(s, axis=0, keepdims=True)
        vals.append(m)
        hit = s == m
        if with_rank:
            rank = jnp.where(hit, float(a), rank)
        s = jnp.where(hit, -jnp.inf, s)
    return vals, rank


def _peer_route_kernel(h_ref, wq_hi_ref, wq_lo_ref, k_hi_ref, k_lo_ref,
                       n1_ref, e1_ref, r2_ref, e2_ref, s_scr):
    tb = h_ref.shape[0]
    half = PEER_DKEY // 2
    h_hi, h_lo = _split_bf16(h_ref[...])
    w_hi = wq_hi_ref[...]
    q = (jnp.dot(h_hi, w_hi, preferred_element_type=jnp.float32)
         + jnp.dot(h_lo, w_hi, preferred_element_type=jnp.float32)
         + jnp.dot(h_hi, wq_lo_ref[...], preferred_element_type=jnp.float32))
    for p in range(2):
        q_hi, q_lo = _split_bf16(q[:, p * half:(p + 1) * half])
        k_hi = k_hi_ref[0, p]
        s_scr[p] = (lax.dot_general(k_hi, q_hi, _NT, preferred_element_type=jnp.float32)
                    + lax.dot_general(k_lo_ref[0, p], q_hi, _NT, preferred_element_type=jnp.float32)
                    + lax.dot_general(k_hi, q_lo, _NT, preferred_element_type=jnp.float32))

    K = PEER_TOPK
    G = SUBLANES

    def tile(lt, carry):
        ln = pl.ds(pl.multiple_of(lt * LANES, LANES), LANES)
        s1 = s_scr[0, :, ln]
        s2 = s_scr[1, :, ln]
        v1, _ = _top_vals(s1, K, False)
        v2l, r2 = _top_vals(s2, K, True)
        v2 = jnp.concatenate(v2l, axis=0)
        cand = [v1[0] + v2[:G], v1[0] + v2[G:]]
        cand += [v1[a] + v2[:G] for a in range(1, G)]
        cand += [jnp.concatenate(v1[G:], axis=0) + v2[0:1]]
        c = cand
        for k in range(K):
            m = c[0]
            for ci in c[1:]:
                m = jnp.maximum(m, ci)
            m = jnp.max(m, axis=0, keepdims=True)
            if k + 1 < K:
                c = [jnp.where(ci == m, -jnp.inf, ci) for ci in c]
        tau = m
        top = v1[0] + v2[0:1]
        keep = [ci >= tau for ci in cand]
        z = jnp.zeros_like(tau)
        for ci, ki in zip(cand, keep):
            z = z + jnp.sum(jnp.where(ki, jnp.exp(ci - top), 0.0), axis=0, keepdims=True)
        cnt = [jnp.sum(jnp.where(ki, 1.0, 0.0), axis=0, keepdims=True) for ki in keep[:G + 1]]
        tail = jnp.where(keep[G + 1], 1.0, 0.0)
        n_a = [cnt[0] + cnt[1]] + cnt[2:] + [tail[a:a + 1] for a in range(G)]
        n1 = jnp.zeros_like(s1)
        for a in range(K):
            n1 = jnp.where(s1 == v1[a], n_a[a], n1)
        n1_ref[0, :, ln] = n1
        e1_ref[0, :, ln] = jnp.exp(s1 - v1[0]) / z
        r2_ref[0, :, ln] = r2.astype(jnp.bfloat16)
        e2_ref[0, :, ln] = jnp.exp(s2 - v2[0:1]).astype(jnp.bfloat16)
        return carry

    lax.fori_loop(0, tb // LANES, tile, 0)


def peer_route(h, wq_hi, wq_lo, k_hi, k_lo, tb=256):
    T = h.shape[0]
    nh = PEER_HEADS
    blk = pl.BlockSpec((1, PEER_KEYS, tb), lambda t, hh: (hh, 0, t))
    sd = lambda dt: jax.ShapeDtypeStruct((nh, PEER_KEYS, T), dt)
    return pl.pallas_call(
        _peer_route_kernel,
        grid=(T // tb, nh),
        in_specs=[pl.BlockSpec((tb, D_MODEL), lambda t, hh: (t, 0)),
                  pl.BlockSpec((D_MODEL, PEER_DKEY), lambda t, hh: (0, hh)),
                  pl.BlockSpec((D_MODEL, PEER_DKEY), lambda t, hh: (0, hh)),
                  pl.BlockSpec((1, 2, PEER_KEYS, PEER_DKEY // 2), lambda t, hh: (hh, 0, 0, 0)),
                  pl.BlockSpec((1, 2, PEER_KEYS, PEER_DKEY // 2), lambda t, hh: (hh, 0, 0, 0))],
        out_specs=[blk, blk, blk, blk],
        out_shape=[sd(jnp.float32), sd(jnp.float32), sd(jnp.bfloat16), sd(jnp.bfloat16)],
        scratch_shapes=[pltpu.VMEM((2, PEER_KEYS, tb), jnp.float32)],
        compiler_params=pltpu.CompilerParams(dimension_semantics=("arbitrary", "arbitrary"),
                                             vmem_limit_bytes=VMEM_LIMIT),
        name="peer_route",
    )(h, wq_hi, wq_lo, k_hi, k_lo)


def _gelu_tanh(x):
    hx = 0.5 * x
    return hx * jnp.tanh(x * (x * x * (0.7978845608028654 * 0.044715) + 0.7978845608028654)) + hx


def _peer_expert_kernel(h_ref, u_ref, vt_even_ref, vt_prev_ref, vt_last_ref, n1_ref, e1_ref, r2_ref, e2_ref,
                        o_ref, acc_ref, a0_scr, a1_scr, w0_scr, w1_scr):
    c = pl.program_id(1)
    ec, tb = a0_scr.shape
    n_i = ec // PEER_KEYS
    bf16 = jnp.bfloat16
    f32 = jnp.float32

    @pl.when(c == 0)
    def _():
        acc_ref[...] = jnp.zeros_like(acc_ref)
        w1_scr[...] = jnp.zeros_like(w1_scr)

    def gate_times_act(a_scr, w_scr, chunk):
        igrp = pl.ds(pl.multiple_of(chunk * n_i, SUBLANES), SUBLANES)
        for lt in range(tb // GATE_LANES):
            ln = slice(lt * GATE_LANES, (lt + 1) * GATE_LANES)
            for ii in range(n_i):
                rows = slice(ii * PEER_KEYS, (ii + 1) * PEER_KEYS)
                g = jnp.zeros((PEER_KEYS, GATE_LANES), bf16)
                for hh in range(PEER_HEADS):
                    n1 = jnp.broadcast_to(n1_ref[hh, igrp, ln][ii:ii + 1], (PEER_KEYS, GATE_LANES)).astype(bf16)
                    e1 = jnp.broadcast_to(e1_ref[hh, igrp, ln][ii:ii + 1], (PEER_KEYS, GATE_LANES)).astype(bf16)
                    g = g + jnp.where(r2_ref[hh, :, ln] < n1, e2_ref[hh, :, ln] * e1, jnp.zeros((), bf16))
                w_scr[rows, ln] = g * _gelu_tanh(a_scr[rows, ln])

    h = h_ref[...]
    a0_scr[...] = lax.dot_general(u_ref[:ec], h, _NT, preferred_element_type=f32).astype(bf16)
    acc_ref[...] += jnp.dot(vt_prev_ref[...], w1_scr[...], preferred_element_type=f32)
    gate_times_act(a0_scr, w0_scr, 2 * c)
    a1_scr[...] = lax.dot_general(u_ref[ec:], h, _NT, preferred_element_type=f32).astype(bf16)
    acc_ref[...] += jnp.dot(vt_even_ref[...], w0_scr[...], preferred_element_type=f32)
    gate_times_act(a1_scr, w1_scr, 2 * c + 1)

    @pl.when(c == pl.num_programs(1) - 1)
    def _():
        o_ref[...] = (acc_ref[...] + jnp.dot(vt_last_ref[...], w1_scr[...], preferred_element_type=f32)).T


def peer_experts(h_bf16, u_bf16, vt_bf16, n1, e1, r2, e2, tb=512, ec=SUBLANES * PEER_KEYS):
    T = h_bf16.shape[0]
    n_steps = PEER_N // (2 * ec)
    rblk = pl.BlockSpec((PEER_HEADS, PEER_KEYS, tb), lambda t, c: (0, 0, t))
    vt_blk = lambda chunk_of: pl.BlockSpec((D_MODEL, ec), lambda t, c: (0, chunk_of(c)))
    return pl.pallas_call(
        _peer_expert_kernel,
        grid=(T // tb, n_steps),
        in_specs=[pl.BlockSpec((tb, D_MODEL), lambda t, c: (t, 0)),
                  pl.BlockSpec((2 * ec, D_MODEL), lambda t, c: (c, 0)),
                  vt_blk(lambda c: 2 * c),
                  vt_blk(lambda c: jnp.maximum(2 * c - 1, 0)),
                  vt_blk(lambda c: 2 * n_steps - 1),
                  rblk, rblk, rblk, rblk],
        out_specs=pl.BlockSpec((tb, D_MODEL), lambda t, c: (t, 0)),
        out_shape=jax.ShapeDtypeStruct((T, D_MODEL), jnp.float32),
        scratch_shapes=[pltpu.VMEM((D_MODEL, tb), jnp.float32)] + [pltpu.VMEM((ec, tb), jnp.bfloat16)] * 4,
        compiler_params=pltpu.CompilerParams(dimension_semantics=("arbitrary", "arbitrary"),
                                             vmem_limit_bytes=VMEM_LIMIT),
        name="peer_experts",
    )(h_bf16, u_bf16, vt_bf16, vt_bf16, vt_bf16, n1, e1, r2, e2)


def peer_ffn(h, w_q, sub_keys, u_tab, v_tab):
    wq_hi, wq_lo = _split_bf16(w_q)
    k_hi, k_lo = _split_bf16(sub_keys)
    n1, e1, r2, e2 = peer_route(h, wq_hi, wq_lo, k_hi, k_lo)
    return peer_experts(h.astype(jnp.bfloat16), u_tab.astype(jnp.bfloat16), v_tab.T.astype(jnp.bfloat16),
                        n1, e1, r2, e2)


def kernel(x_prompt, x_sample, state_rwkv_fwd, state_rwkv_bwd, cache_mla_ckv, cache_mla_kpe, c, c_ctx,
           norm_g, w_mod, b_mod, ab_w_in, rw_mu, rw_w0, rw_w2, rw_a0, rw_a2, rw_g2, rw_k_k, rw_k_a, rw_r_k,
           rw_lnx_g, rw_lnx_b, mla_q_norm, mla_q_up, mla_kv_norm, mla_kv_up, mla_qn, mla_kn, ab_w_out,
           hy_w_in, hy_b_in, hy_conv_w, hy_conv_b, hy_f_w1, hy_f_b1, hy_f_w2, hy_f_b2, hy_f_w3, hy_f_freq,
           hy_bias, hy_w_out, peer_w_q, peer_keys, peer_u, peer_v):
    rope = axial_rope(x_sample.shape[1])
    xp, xs = x_prompt, x_sample
    st_f, st_b, st_ckv, st_kpe = [], [], [], []
    for li in range(DEPTH):
        j = li // 2
        mp = jnp.split(jax.nn.silu(c_ctx) @ w_mod[li] + b_mod[li], 6, axis=-1)
        ms = [m[:, None, :] for m in jnp.split(jax.nn.silu(c) @ w_mod[li] + b_mod[li], 6, axis=-1)]
        hp = modulate(rmsnorm(xp, norm_g[li, 0]), mp[0], mp[1])
        hs = modulate(rmsnorm(xs, norm_g[li, 0]), ms[0], ms[1])
        if li % 2 == 0:
            p = (ab_w_in[j], rw_mu[j], rw_w0[j], rw_w2[j], rw_a0[j], rw_a2[j], rw_g2[j], rw_k_k[j], rw_k_a[j],
                 rw_r_k[j], rw_lnx_g[j], rw_lnx_b[j], mla_q_norm[j], mla_q_up[j], mla_kv_norm[j], mla_kv_up[j],
                 mla_qn[j], mla_kn[j], ab_w_out[j])
            op, sf, sb, ckv, kpe = rwkv_mla_mixer(hp, None, p)
            ctx = (state_rwkv_fwd[:, j], state_rwkv_bwd[:, j], cache_mla_ckv[:, j], cache_mla_kpe[:, j], rope)
            os_, _, _, _, _ = rwkv_mla_mixer(hs, ctx, p)
            st_f.append(sf)
            st_b.append(sb)
            st_ckv.append(ckv)
            st_kpe.append(kpe)
        else:
            p = (hy_w_in[j], hy_b_in[j], hy_conv_w[j], hy_conv_b[j], hy_f_w1[j], hy_f_b1[j], hy_f_w2[j],
                 hy_f_b2[j], hy_f_w3[j], hy_f_freq[j], hy_bias[j], hy_w_out[j])
            op = hyena_mixer(hp, p)
            os_ = hyena_mixer(hs, p)
        xp = xp + mp[2] * op
        xs = xs + ms[2] * os_
        pp = (peer_w_q[li], peer_keys[li], peer_u[li], peer_v[li])
        gp = modulate(rmsnorm(xp, norm_g[li, 1]), mp[3], mp[4]).reshape(-1, D_MODEL)
        gs = modulate(rmsnorm(xs, norm_g[li, 1]), ms[3], ms[4]).reshape(-1, D_MODEL)
        po = peer_ffn(jnp.concatenate([gp, gs], axis=0), *pp)
        xp = xp + mp[5] * po[:gp.shape[0]].reshape(xp.shape)
        xs = xs + ms[5] * po[gp.shape[0]:].reshape(xs.shape)
    new_state_rwkv_fwd = jnp.stack(st_f, axis=1).astype(x_prompt.dtype)
    new_state_rwkv_bwd = jnp.stack(st_b, axis=1).astype(x_prompt.dtype)
    new_cache_mla_ckv = jnp.stack(st_ckv, axis=1)
    new_cache_mla_kpe = jnp.stack(st_kpe, axis=1)
    return (xp, xs, new_state_rwkv_fwd, new_state_rwkv_bwd, new_cache_mla_ckv, new_cache_mla_kpe)
```

```python
import math
from functools import partial

import jax
import jax.numpy as jnp
from jax import lax
from jax.experimental import pallas as pl
from jax.experimental.pallas import tpu as pltpu

D_MODEL = 1024
DEPTH = 2
GRID_W = 64
EPS = 1e-6
RW_HEADS = 8
RW_HEAD = 64
RW_DIM = RW_HEADS * RW_HEAD
W_LORA = 64
A_LORA = 64
G_LORA = 128
LNX_EPS = 64e-5
RW_IN = 3 * RW_DIM + W_LORA + A_LORA + G_LORA
RW_SPLITS = (RW_DIM, 2 * RW_DIM, 3 * RW_DIM, 3 * RW_DIM + W_LORA, 3 * RW_DIM + W_LORA + A_LORA)
MLA_HEADS = 4
QK_NOPE = 128
QK_ROPE = 64
QK_HEAD = QK_NOPE + QK_ROPE
V_HEAD = 128
Q_LORA = 256
KV_LORA = 128
MLA_DIM = MLA_HEADS * V_HEAD
ROPE_THETA = 10000.0
Q_BLOCK = 128
HY_BANDS = 16
HY_TARGET = 1e-2
HY_FAST = 0.3
HY_SLOW = 1.5
PEER_KEYS = 128
PEER_HEADS = 8
PEER_DKEY = 256
PEER_TOPK = 16
TOKEN_BLOCK = 128


def _mm_kernel(a_ref, b_ref, o_ref, *, split):
    if split:
        a_hi, a_lo = _split_bf16(a_ref[...])
        b_hi, b_lo = _split_bf16(b_ref[...])
        o_ref[...] = _dot3(a_hi, a_lo, b_hi, b_lo)
    else:
        o_ref[...] = jnp.dot(a_ref[...].astype(jnp.bfloat16), b_ref[...].astype(jnp.bfloat16),
                             preferred_element_type=jnp.float32)


def _mm(a, b, tm=512, tn=512, split=False):
    lead = a.shape[:-1]
    K = a.shape[-1]
    N = b.shape[-1]
    a2 = a.reshape(-1, K)
    if K % LANES:
        kp = -K % LANES
        a2 = jnp.pad(a2, ((0, 0), (0, kp)))
        b = jnp.pad(b, ((0, kp), (0, 0)))
        K += kp
    M = a2.shape[0]
    tm = min(tm, M)
    tn = min(tn, N)
    if N % tn:
        tn = N
    assert M % tm == 0 and N % tn == 0
    out = pl.pallas_call(
        partial(_mm_kernel, split=split),
        grid=(M // tm, N // tn),
        in_specs=[pl.BlockSpec((tm, K), lambda i, j: (i, 0)),
                  pl.BlockSpec((K, tn), lambda i, j: (0, j))],
        out_specs=pl.BlockSpec((tm, tn), lambda i, j: (i, j)),
        out_shape=jax.ShapeDtypeStruct((M, N), jnp.float32),
        name="matmul",
    )(a2, b)
    return out.reshape(*lead, N)


def rmsnorm(x, g):
    xf = x.astype(jnp.float32)
    y = xf * lax.rsqrt(jnp.mean(xf * xf, axis=-1, keepdims=True) + EPS)
    return (y * g.astype(jnp.float32)).astype(x.dtype)


def modulate(h, shift, scale):
    return h * (1.0 + scale) + shift


def centred_shift(x):
    prev = jnp.pad(x[:, :-1], ((0, 0), (1, 0), (0, 0)))
    nxt = jnp.pad(x[:, 1:], ((0, 0), (0, 1), (0, 0)))
    return 0.5 * (prev + nxt)


def centred_dwconv3(u, w, b):
    prev = jnp.pad(u[:, :-1], ((0, 0), (1, 0), (0, 0)))
    nxt = jnp.pad(u[:, 1:], ((0, 0), (0, 1), (0, 0)))
    return prev * w[0] + u * w[1] + nxt * w[2] + b


def axial_rope(L):
    rows = L // GRID_W
    row = jnp.repeat(jnp.arange(rows, dtype=jnp.float32), GRID_W)
    col = jnp.tile(jnp.arange(GRID_W, dtype=jnp.float32), rows)
    n_freq = QK_ROPE // 4
    inv = ROPE_THETA ** (-jnp.arange(n_freq, dtype=jnp.float32) / n_freq)
    ang = jnp.concatenate([row[:, None] * inv, col[:, None] * inv], axis=-1)
    return jnp.cos(ang), jnp.sin(ang)


def rope_tail(x, cos, sin):
    xn, xr = x[..., :QK_NOPE], x[..., QK_NOPE:]
    x1, x2 = xr[..., 0::2], xr[..., 1::2]
    cs = cos[None, :, None, :].astype(x.dtype)
    sn = sin[None, :, None, :].astype(x.dtype)
    rot = jnp.stack([x1 * cs - x2 * sn, x1 * sn + x2 * cs], axis=-1).reshape(xr.shape)
    return jnp.concatenate([xn, rot], axis=-1)


LANES = 128
SUBLANES = 8
VMEM_LIMIT = 56 * 1024 * 1024
_NT = (((1,), (1,)), ((), ()))


def _split_bf16(x):
    hi = x.astype(jnp.bfloat16)
    lo = (x - hi.astype(jnp.float32)).astype(jnp.bfloat16)
    return hi, lo


def _mla_attn_kernel(q_ref, k_ref, v_ref, o_ref):
    scale = QK_HEAD ** -0.5
    for h in range(MLA_HEADS):
        q = q_ref[0, :, h * QK_HEAD:(h + 1) * QK_HEAD].astype(jnp.bfloat16)
        k = k_ref[0, :, h * QK_HEAD:(h + 1) * QK_HEAD].astype(jnp.bfloat16)
        s = lax.dot_general(q, k, _NT, preferred_element_type=jnp.float32) * scale
        m = jnp.max(s, axis=-1, keepdims=True)
        p = jnp.exp(s - m)
        l = jnp.sum(p, axis=-1, keepdims=True)
        v = v_ref[0, :, h * V_HEAD:(h + 1) * V_HEAD].astype(jnp.bfloat16)
        o = jnp.dot((p / l).astype(jnp.bfloat16), v, preferred_element_type=jnp.float32)
        o_ref[0, :, h * V_HEAD:(h + 1) * V_HEAD] = o


def attend(q, k, v, tq=256):
    B, Lq = q.shape[:2]
    Lk = k.shape[1]
    tq = min(tq, Lq)
    return pl.pallas_call(
        _mla_attn_kernel, grid=(B, Lq // tq),
        in_specs=[pl.BlockSpec((1, tq, MLA_HEADS * QK_HEAD), lambda b, i: (b, i, 0)),
                  pl.BlockSpec((1, Lk, MLA_HEADS * QK_HEAD), lambda b, i: (b, 0, 0)),
                  pl.BlockSpec((1, Lk, MLA_HEADS * V_HEAD), lambda b, i: (b, 0, 0))],
        out_specs=pl.BlockSpec((1, tq, MLA_HEADS * V_HEAD), lambda b, i: (b, i, 0)),
        out_shape=jax.ShapeDtypeStruct((B, Lq, MLA_HEADS * V_HEAD), jnp.float32),
        compiler_params=pltpu.CompilerParams(dimension_semantics=("arbitrary", "arbitrary"),
                                             vmem_limit_bytes=VMEM_LIMIT),
        name="mla_attend",
    )(q.reshape(B, Lq, -1), k.reshape(B, Lk, -1), v.reshape(B, Lk, -1))


def _dot3(a_hi, a_lo, b_hi, b_lo):
    f32 = jnp.float32
    return (jnp.dot(a_hi, b_hi, preferred_element_type=f32) + jnp.dot(a_lo, b_hi, preferred_element_type=f32)
            + jnp.dot(a_hi, b_lo, preferred_element_type=f32))


def _dft_split_input(x_refs, scr_refs):
    @pl.when(pl.program_id(2) == 0)
    def _():
        for x_ref, (hi_scr, lo_scr) in zip(x_refs, scr_refs):
            hi_scr[...], lo_scr[...] = _split_bf16(x_ref[0])


def _dft_pair_kernel(c_hi, c_lo, s_hi, s_lo, x_ref, oc_ref, os_ref, x_hi, x_lo):
    _dft_split_input([x_ref], [(x_hi, x_lo)])
    oc_ref[0] = _dot3(c_hi[...], c_lo[...], x_hi[...], x_lo[...])
    os_ref[0] = _dot3(s_hi[...], s_lo[...], x_hi[...], x_lo[...])


def _dft_spectral_kernel(c_hi, c_lo, s_hi, s_lo, x_ref, hr_ref, hi_ref, yr_ref, yi_ref, x_hi, x_lo):
    _dft_split_input([x_ref], [(x_hi, x_lo)])
    rb, L = c_hi.shape
    zc = _dot3(c_hi[...], c_lo[...], x_hi[...], x_lo[...])
    zs = _dot3(s_hi[...], s_lo[...], x_hi[...], x_lo[...])
    hr = hr_ref[...]
    hi = hi_ref[...]
    f = pl.program_id(2) * rb + lax.broadcasted_iota(jnp.int32, zc.shape, 0)
    yr_ref[0] = jnp.where(f == 0, zc * hr * (0.5 / L), (zc * hr - zs * hi) * (1.0 / L))
    yi_ref[0] = jnp.where(f == 0, zs * hi * (0.5 / L), (zc * hi + zs * hr) * (1.0 / L))


def _dft_sum_kernel(c_hi, c_lo, s_hi, s_lo, x_ref, y_ref, o_ref, x_hi, x_lo, y_hi, y_lo):
    _dft_split_input([x_ref, y_ref], [(x_hi, x_lo), (y_hi, y_lo)])
    o_ref[0] = (_dot3(c_hi[...], c_lo[...], x_hi[...], x_lo[...])
                + _dot3(s_hi[...], s_lo[...], y_hi[...], y_lo[...]))


def dft_tables(L):
    f = jnp.arange(L, dtype=jnp.int32)
    ph = (f[:, None] * f[None, :]) % (2 * L)
    ang = ph.astype(jnp.float32) * (math.pi / L)
    c = jnp.cos(ang)
    s = -jnp.sin(ang)
    alt = jnp.where(f % 2 == 0, 1.0, -1.0).astype(jnp.float32)
    s_ana = jnp.where(f[:, None] == 0, alt[None, :], s)
    s_syn = jnp.where(f[None, :] == 0, alt[:, None], s)
    cc = _split_bf16(c)
    return cc + _split_bf16(s_ana), cc + _split_bf16(s_syn)


def _dft_call(body, tabs, xs, n_out, name, row_inputs=(), rb=256, nb=512):
    B, L, N = xs[0].shape
    rb = min(rb, L)
    nb = min(nb, N)
    tab = pl.BlockSpec((rb, L), lambda b, n, i: (i, 0))
    xin = pl.BlockSpec((1, L, nb), lambda b, n, i: (b, 0, n))
    rin = pl.BlockSpec((rb, nb), lambda b, n, i: (i, n))
    out = pl.BlockSpec((1, rb, nb), lambda b, n, i: (b, i, n))
    sd = jax.ShapeDtypeStruct((B, L, N), jnp.float32)
    return pl.pallas_call(
        body, grid=(B, N // nb, L // rb),
        in_specs=[tab] * 4 + [xin] * len(xs) + [rin] * len(row_inputs),
        out_specs=[out] * n_out if n_out > 1 else out,
        out_shape=[sd] * n_out if n_out > 1 else sd,
        scratch_shapes=[pltpu.VMEM((L, nb), jnp.bfloat16)] * (2 * len(xs)),
        compiler_params=pltpu.CompilerParams(dimension_semantics=("arbitrary",) * 3,
                                             vmem_limit_bytes=VMEM_LIMIT),
        name=name,
    )(*tabs, *xs, *row_inputs)


def hyena_longconv(zin, circ):
    B, L, D = zin.shape
    tabs, tabs_syn = dft_tables(L)
    alt = jnp.where(jnp.arange(L) % 2 == 0, 1.0, -1.0).astype(jnp.float32)[:, None]
    fc, fs = _dft_call(_dft_pair_kernel, tabs, [jnp.concatenate([circ[:L], circ[L:]], axis=-1)[None]], 2, "dft_filter")
    hr = fc[0, :, :D] + alt * fc[0, :, D:]
    hi = fs[0, :, :D] + alt * fs[0, :, D:]
    yr, yi = _dft_call(_dft_spectral_kernel, tabs, [zin], 2, "dft_analysis", row_inputs=(hr, hi))
    return _dft_call(_dft_sum_kernel, tabs_syn, [yr, yi], 1, "dft_synthesis")


SCAN_ACCS = 4


def _rwkv_scan_kernel(r_ref, kk_ref, v_ref, w_ref, kd_ref, b_ref, s0_ref, y_ref, sf_ref, s_scr):
    d = pl.program_id(0)
    c = pl.program_id(1)
    tc = r_ref.shape[0]
    nv = v_ref.shape[1]

    @pl.when(c == 0)
    def _():
        s_scr[...] = s0_ref[0]

    def row(ref, tt, k):
        return jnp.broadcast_to(ref[tt, pl.ds(k, 1), :], (nv, LANES))

    def row_d(ref, tt, k):
        return jnp.broadcast_to(ref[0, tt, pl.ds(k, 1), :], (nv, LANES))

    def step(t, carry):
        tt = t + d * (tc - 1 - 2 * t)
        parts = [None] * SCAN_ACCS
        for k in range(RW_HEAD):
            term = s_scr[k] * row(kk_ref, tt, k)
            parts[k % SCAN_ACCS] = term if parts[k % SCAN_ACCS] is None else parts[k % SCAN_ACCS] + term
        sa = -((parts[0] + parts[1]) + (parts[2] + parts[3]))
        vv = v_ref[tt]
        parts = [None] * SCAN_ACCS
        for k in range(RW_HEAD):
            s_new = s_scr[k] * row_d(w_ref, tt, k) + sa * row_d(b_ref, tt, k) + vv * row_d(kd_ref, tt, k)
            s_scr[k] = s_new
            term = s_new * row(r_ref, tt, k)
            parts[k % SCAN_ACCS] = term if parts[k % SCAN_ACCS] is None else parts[k % SCAN_ACCS] + term
        y_ref[0, tt] = (parts[0] + parts[1]) + (parts[2] + parts[3])
        return carry

    lax.fori_loop(0, tc, step, 0)

    @pl.when(c == pl.num_programs(1) - 1)
    def _():
        sf_ref[0] = s_scr[...]


def rwkv_scan(r, kk, v, w2, kd2, b2, s0, tc=32):
    L = r.shape[0]
    nv = v.shape[1]
    nc = L // tc
    tmap = lambda d, c: c + d * (nc - 1 - 2 * c)
    shared = lambda rows: pl.BlockSpec((tc, rows, LANES), lambda d, c: (tmap(d, c), 0, 0))
    per_dir = pl.BlockSpec((1, tc, RW_HEAD, LANES), lambda d, c: (d, tmap(d, c), 0, 0))
    state = pl.BlockSpec((1, RW_HEAD, nv, LANES), lambda d, c: (d, 0, 0, 0))
    return pl.pallas_call(
        _rwkv_scan_kernel,
        grid=(2, nc),
        in_specs=[shared(RW_HEAD), shared(RW_HEAD), shared(nv), per_dir, per_dir, per_dir, state],
        out_specs=[pl.BlockSpec((1, tc, nv, LANES), lambda d, c: (d, tmap(d, c), 0, 0)), state],
        out_shape=[jax.ShapeDtypeStruct((2, L, nv, LANES), jnp.float32),
                   jax.ShapeDtypeStruct((2, RW_HEAD, nv, LANES), jnp.float32)],
        scratch_shapes=[pltpu.VMEM((RW_HEAD, nv, LANES), jnp.float32)],
        compiler_params=pltpu.CompilerParams(dimension_semantics=("arbitrary", "arbitrary"),
                                             vmem_limit_bytes=VMEM_LIMIT),
        name="rwkv_scan",
    )(r, kk, v, w2, kd2, b2, s0)


def to_lanes(x, vsplit):
    B, L, H, N = x.shape
    y = x.transpose(1, 3, 0, 2).reshape(L, N, B * H)
    return jnp.repeat(y, vsplit, axis=-1) if vsplit > 1 else y


def v_to_lanes(x, vsplit):
    B, L, H, N = x.shape
    nv = N // vsplit
    return x.reshape(B, L, H, vsplit, nv).transpose(1, 4, 0, 2, 3).reshape(L, nv, B * H * vsplit)


def v_from_lanes(y, B, H, vsplit):
    L, nv, _ = y.shape
    return y.reshape(L, nv, B, H, vsplit).transpose(2, 0, 3, 4, 1).reshape(B, L, H, vsplit * nv)


def state_to_lanes(s, vsplit):
    B, H, N, K = s.shape
    nv = N // vsplit
    return s.reshape(B, H, vsplit, nv, K).transpose(4, 3, 0, 1, 2).reshape(K, nv, B * H * vsplit)


def state_from_lanes(s, B, H, vsplit):
    K, nv, _ = s.shape
    return s.reshape(K, nv, B, H, vsplit).transpose(2, 3, 4, 1, 0).reshape(B, H, vsplit * nv, K)


def rwkv_scan_both(rh, kk, vh, decay2, kd2, b2, s0_2, tc=32):
    B, L, H, N = rh.shape
    vsplit = LANES // (B * H)
    assert B * H * vsplit == LANES
    stack = lambda xs: jnp.stack([to_lanes(x, vsplit) for x in xs])
    y2, sf2 = rwkv_scan(to_lanes(rh, vsplit), to_lanes(kk, vsplit), v_to_lanes(vh, vsplit),
                        stack(decay2), stack(kd2), stack(b2),
                        jnp.stack([state_to_lanes(s, vsplit) for s in s0_2]), tc=tc)
    ys = [v_from_lanes(y2[d], B, H, vsplit) for d in range(2)]
    sfs = [state_from_lanes(sf2[d], B, H, vsplit) for d in range(2)]
    return ys, sfs


def mla_keys(ckv, k_pe, kv_up, kn):
    B, L, _ = ckv.shape
    kv = _mm(ckv, kv_up).reshape(B, L, MLA_HEADS, QK_NOPE + V_HEAD)
    k_rope = jnp.broadcast_to(k_pe[:, :, None, :], (B, L, MLA_HEADS, QK_ROPE))
    k = rmsnorm(jnp.concatenate([kv[..., :QK_NOPE], k_rope], axis=-1), kn)
    return k, kv[..., QK_NOPE:]


def rwkv_mla_mixer(h, ctx, p):
    (w_in, mu, w0, w2, a0, a2, g2, k_k, k_a, r_k, lnx_g, lnx_b,
     q_norm, q_up, kv_norm, kv_up, qn, kn, w_out) = p
    B, L, _ = h.shape
    f32 = jnp.float32
    proj = _mm(h, w_in)
    rw, mla = proj[..., :RW_IN], proj[..., RW_IN:]
    rw = rw + mu * (centred_shift(rw) - rw)
    r, k, v, wd, ad, gd = jnp.split(rw, RW_SPLITS, axis=-1)
    heads = lambda t: t.reshape(B, L, RW_HEADS, RW_HEAD)
    rh, vh = heads(r.astype(f32)), heads(v.astype(f32))
    kk = heads((k * k_k).astype(f32))
    kk = kk / jnp.maximum(jnp.linalg.norm(kk, axis=-1, keepdims=True), 1e-12)
    if ctx is None:
        z = jnp.zeros((B, RW_HEADS, RW_HEAD, RW_HEAD), f32)
        s0s = (z, z)
    else:
        s0s = (ctx[0], ctx[1])
    decays, kds, bs, bonus = [], [], [], []
    for d in range(2):
        w_raw = (w0[d] + _mm(jnp.tanh(wd), w2[d])).astype(f32)
        decay = jnp.exp(-jnp.exp(-jax.nn.softplus(-w_raw) - 0.5))
        lr = jax.nn.sigmoid((a0[d] + _mm(ad, a2[d])).astype(f32))
        kd = heads(k.astype(f32) * (1.0 + (lr - 1.0) * k_a))
        decays.append(heads(decay))
        kds.append(kd)
        bs.append(kk * heads(lr))
        bonus.append(jnp.sum(rh * kd * r_k, axis=-1, keepdims=True) * vh)
    ys, finals = rwkv_scan_both(rh, kk, vh, decays, kds, bs, s0s)
    y = ys[0] + ys[1]
    mean = jnp.mean(y, axis=-1, keepdims=True)
    var = jnp.mean(jnp.square(y - mean), axis=-1, keepdims=True)
    yn = ((y - mean) * lax.rsqrt(var + LNX_EPS)).reshape(B, L, RW_DIM) * lnx_g + lnx_b
    gate = _mm(jax.nn.sigmoid(gd), g2)
    rw_out = ((yn + (bonus[0] + bonus[1]).reshape(B, L, RW_DIM)) * gate).astype(h.dtype)
    q_c, kv_c, k_pe = jnp.split(mla, (Q_LORA, Q_LORA + KV_LORA), axis=-1)
    q = rmsnorm(_mm(rmsnorm(q_c, q_norm), q_up).reshape(B, L, MLA_HEADS, QK_HEAD), qn)
    ckv = rmsnorm(kv_c, kv_norm)
    k_own, v_own = mla_keys(ckv, k_pe, kv_up, kn)
    if ctx is None:
        keys, vals = k_own, v_own
    else:
        cos, sin = ctx[4]
        q = rope_tail(q, cos, sin)
        k_own = rope_tail(k_own, cos, sin)
        k_ctx, v_ctx = mla_keys(ctx[2], ctx[3], kv_up, kn)
        keys = jnp.concatenate([k_ctx, k_own], axis=1)
        vals = jnp.concatenate([v_ctx, v_own], axis=1)
    mla_out = attend(q, keys, vals)
    out = _mm(jnp.concatenate([rw_out, mla_out.astype(h.dtype)], axis=-1), w_out)
    return out, finals[0], finals[1], ckv, k_pe


def hyena_filters(L, w1, b1, w2, b2, w3, freq):
    f32 = jnp.float32
    u = jnp.arange(2 * L, dtype=jnp.int32)[:, None]
    t = jnp.where(u < L, u, 2 * L - u).astype(f32)
    t_unit = t / (L - 1)
    bands = jnp.linspace(1e-4, HY_BANDS - 1, HY_BANDS, dtype=f32)
    ang = 2.0 * math.pi * t * bands / L
    zpos = jnp.concatenate([t_unit, jnp.cos(ang), -jnp.sin(ang)], axis=-1)
    fr = freq.astype(f32)
    hid = jnp.sin(fr * (_mm(zpos, w1.astype(f32), split=True) + b1.astype(f32)))
    hid = jnp.sin(fr * (_mm(hid, w2.astype(f32), split=True) + b2.astype(f32)))
    filt = _mm(hid, w3.astype(f32), split=True)
    deltas = jnp.linspace(math.log(HY_TARGET) / HY_FAST, math.log(HY_TARGET) / HY_SLOW, D_MODEL, dtype=f32)
    window = jnp.exp(-t_unit * jnp.abs(deltas))
    circ = jnp.where(u < L, filt[:, :D_MODEL], filt[:, D_MODEL:]) * window
    circ = jnp.where(u == L, 0.0, circ)
    return circ / jnp.sum(jnp.abs(circ), axis=0, keepdims=True)


def hyena_mixer(h, p):
    w_in, b_in, conv_w, conv_b, f_w1, f_b1, f_w2, f_b2, f_w3, f_freq, bias, w_out = p
    B, L, _ = h.shape
    u = centred_dwconv3(_mm(h, w_in) + b_in, conv_w, conv_b)
    x0, x1, v = jnp.split(u, 3, axis=-1)
    zin = (x1 * v).astype(jnp.float32)
    conv = hyena_longconv(zin, hyena_filters(L, f_w1, f_b1, f_w2, f_b2, f_w3, f_freq))
    y = conv + zin * bias.astype(jnp.float32)
    return _mm(x0 * y.astype(h.dtype), w_out)


PEER_N = PEER_KEYS * PEER_KEYS
GATE_LANES = 2 * LANES


def _top_vals(s, n, with_rank):
    vals = []
    rank = jnp.full(s.shape, float(n), jnp.float32) if with_rank else None
    for a in range(n):
        m = jnp.max(s, axis=0, keepdims=True)
        vals.append(m)
        hit = s == m
        if with_rank:
            rank = jnp.where(hit, float(a), rank)
        s = jnp.where(hit, -jnp.inf, s)
    return vals, rank


def _peer_route_kernel(h_ref, wq_hi_ref, wq_lo_ref, k_hi_ref, k_lo_ref,
                       n1_ref, e1_ref, r2_ref, e2_ref, s_scr):
    tb = h_ref.shape[0]
    half = PEER_DKEY // 2
    n_tiles = tb // LANES
    h_hi, h_lo = _split_bf16(h_ref[...])
    q_hi, q_lo = _split_bf16(_dot3(h_hi, h_lo, wq_hi_ref[...], wq_lo_ref[...]))
    for hh in range(PEER_HEADS):
        for p in range(2):
            cols = slice((2 * hh + p) * half, (2 * hh + p + 1) * half)
            k_hi = k_hi_ref[hh, p]
            s_scr[hh, p] = (lax.dot_general(k_hi, q_hi[:, cols], _NT, preferred_element_type=jnp.float32)
                            + lax.dot_general(k_lo_ref[hh, p], q_hi[:, cols], _NT, preferred_element_type=jnp.float32)
                            + lax.dot_general(k_hi, q_lo[:, cols], _NT, preferred_element_type=jnp.float32))

    K = PEER_TOPK
    G = SUBLANES

    def tile(it, carry):
        hh = it // n_tiles
        ln = pl.ds(pl.multiple_of((it % n_tiles) * LANES, LANES), LANES)
        s1 = s_scr[hh, 0, :, ln]
        s2 = s_scr[hh, 1, :, ln]
        v1, _ = _top_vals(s1, K, False)
        v2l, r2 = _top_vals(s2, K, True)
        v2 = jnp.concatenate(v2l, axis=0)
        cand = [v1[0] + v2[:G], v1[0] + v2[G:]]
        cand += [v1[a] + v2[:G] for a in range(1, G)]
        cand += [jnp.concatenate(v1[G:], axis=0) + v2[0:1]]
        c = cand
        for k in range(K):
            m = c[0]
            for ci in c[1:]:
                m = jnp.maximum(m, ci)
            m = jnp.max(m, axis=0, keepdims=True)
            if k + 1 < K:
                c = [jnp.where(ci == m, -jnp.inf, ci) for ci in c]
        tau = m
        top = v1[0] + v2[0:1]
        keep = [ci >= tau for ci in cand]
        z = jnp.zeros_like(tau)
        for ci, ki in zip(cand, keep):
            z = z + jnp.sum(jnp.where(ki, jnp.exp(ci - top), 0.0), axis=0, keepdims=True)
        cnt = [jnp.sum(jnp.where(ki, 1.0, 0.0), axis=0, keepdims=True) for ki in keep[:G + 1]]
        tail = jnp.where(keep[G + 1], 1.0, 0.0)
        n_a = [cnt[0] + cnt[1]] + cnt[2:] + [tail[a:a + 1] for a in range(G)]
        n1 = jnp.zeros_like(s1)
        for a in range(K):
            n1 = jnp.where(s1 == v1[a], n_a[a], n1)
        n1_ref[hh, :, ln] = n1
        e1_ref[hh, :, ln] = jnp.exp(s1 - v1[0]) / z
        r2_ref[hh, :, ln] = r2.astype(jnp.bfloat16)
        e2_ref[hh, :, ln] = jnp.exp(s2 - v2[0:1]).astype(jnp.bfloat16)
        return carry

    lax.fori_loop(0, PEER_HEADS * n_tiles, tile, 0)


def peer_route(h, wq_hi, wq_lo, k_hi, k_lo, tb=256):
    T = h.shape[0]
    nh = PEER_HEADS
    blk = pl.BlockSpec((nh, PEER_KEYS, tb), lambda t: (0, 0, t))
    sd = lambda dt: jax.ShapeDtypeStruct((nh, PEER_KEYS, T), dt)
    whole = lambda a: pl.BlockSpec(a.shape, lambda t: (0,) * a.ndim)
    return pl.pallas_call(
        _peer_route_kernel,
        grid=(T // tb,),
        in_specs=[pl.BlockSpec((tb, D_MODEL), lambda t: (t, 0)),
                  whole(wq_hi), whole(wq_lo), whole(k_hi), whole(k_lo)],
        out_specs=[blk, blk, blk, blk],
        out_shape=[sd(jnp.float32), sd(jnp.float32), sd(jnp.bfloat16), sd(jnp.bfloat16)],
        scratch_shapes=[pltpu.VMEM((nh, 2, PEER_KEYS, tb), jnp.float32)],
        compiler_params=pltpu.CompilerParams(dimension_semantics=("arbitrary",),
                                             vmem_limit_bytes=VMEM_LIMIT),
        name="peer_route",
    )(h, wq_hi, wq_lo, k_hi, k_lo)


def _gelu_tanh(x):
    hx = 0.5 * x
    return hx * jnp.tanh(x * (x * x * (0.7978845608028654 * 0.044715) + 0.7978845608028654)) + hx


def _peer_expert_kernel(h_ref, u_ref, vt_even_ref, vt_prev_ref, vt_last_ref, n1_ref, e1_ref, r2_ref, e2_ref,
                        o_ref, acc_ref, a0_scr, a1_scr, w0_scr, w1_scr):
    c = pl.program_id(1)
    ec, tb = a0_scr.shape
    n_i = ec // PEER_KEYS
    bf16 = jnp.bfloat16
    f32 = jnp.float32

    @pl.when(c == 0)
    def _():
        acc_ref[...] = jnp.zeros_like(acc_ref)
        w1_scr[...] = jnp.zeros_like(w1_scr)

    def gate_times_act(a_scr, w_scr, chunk):
        igrp = pl.ds(pl.multiple_of(chunk * n_i, SUBLANES), SUBLANES)
        for lt in range(tb // GATE_LANES):
            ln = slice(lt * GATE_LANES, (lt + 1) * GATE_LANES)
            for ii in range(n_i):
                rows = slice(ii * PEER_KEYS, (ii + 1) * PEER_KEYS)
                g = jnp.zeros((PEER_KEYS, GATE_LANES), bf16)
                for hh in range(PEER_HEADS):
                    n1 = jnp.broadcast_to(n1_ref[hh, igrp, ln][ii:ii + 1], (PEER_KEYS, GATE_LANES)).astype(bf16)
                    e1 = jnp.broadcast_to(e1_ref[hh, igrp, ln][ii:ii + 1], (PEER_KEYS, GATE_LANES)).astype(bf16)
                    g = g + jnp.where(r2_ref[hh, :, ln] < n1, e2_ref[hh, :, ln] * e1, jnp.zeros((), bf16))
                w_scr[rows, ln] = g * _gelu_tanh(a_scr[rows, ln])

    h = h_ref[...]
    a0_scr[...] = lax.dot_general(u_ref[:ec], h, _NT, preferred_element_type=f32).astype(bf16)
    acc_ref[...] += jnp.dot(vt_prev_ref[...], w1_scr[...], preferred_element_type=f32)
    gate_times_act(a0_scr, w0_scr, 2 * c)
    a1_scr[...] = lax.dot_general(u_ref[ec:], h, _NT, preferred_element_type=f32).astype(bf16)
    acc_ref[...] += jnp.dot(vt_even_ref[...], w0_scr[...], preferred_element_type=f32)
    gate_times_act(a1_scr, w1_scr, 2 * c + 1)

    @pl.when(c == pl.num_programs(1) - 1)
    def _():
        o_ref[...] = (acc_ref[...] + jnp.dot(vt_last_ref[...], w1_scr[...], preferred_element_type=f32)).T


def peer_experts(h_bf16, u_bf16, vt_bf16, n1, e1, r2, e2, tb=512, ec=SUBLANES * PEER_KEYS):
    T = h_bf16.shape[0]
    n_steps = PEER_N // (2 * ec)
    rblk = pl.BlockSpec((PEER_HEADS, PEER_KEYS, tb), lambda t, c: (0, 0, t))
    vt_blk = lambda chunk_of: pl.BlockSpec((D_MODEL, ec), lambda t, c: (0, chunk_of(c)))
    return pl.pallas_call(
        _peer_expert_kernel,
        grid=(T // tb, n_steps),
        in_specs=[pl.BlockSpec((tb, D_MODEL), lambda t, c: (t, 0)),
                  pl.BlockSpec((2 * ec, D_MODEL), lambda t, c: (c, 0)),
                  vt_blk(lambda c: 2 * c),
                  vt_blk(lambda c: jnp.maximum(2 * c - 1, 0)),
                  vt_blk(lambda c: 2 * n_steps - 1),
                  rblk, rblk, rblk, rblk],
        out_specs=pl.BlockSpec((tb, D_MODEL), lambda t, c: (t, 0)),
        out_shape=jax.ShapeDtypeStruct((T, D_MODEL), jnp.float32),
        scratch_shapes=[pltpu.VMEM((D_MODEL, tb), jnp.float32)] + [pltpu.VMEM((ec, tb), jnp.bfloat16)] * 4,
        compiler_params=pltpu.CompilerParams(dimension_semantics=("arbitrary", "arbitrary"),
                                             vmem_limit_bytes=VMEM_LIMIT),
        name="peer_experts",
    )(h_bf16, u_bf16, vt_bf16, vt_bf16, vt_bf16, n1, e1, r2, e2)


def peer_ffn(h, w_q, sub_keys, u_tab, v_tab):
    wq_hi, wq_lo = _split_bf16(w_q)
    k_hi, k_lo = _split_bf16(sub_keys)
    n1, e1, r2, e2 = peer_route(h, wq_hi, wq_lo, k_hi, k_lo)
    return peer_experts(h.astype(jnp.bfloat16), u_tab.astype(jnp.bfloat16), v_tab.T.astype(jnp.bfloat16),
                        n1, e1, r2, e2)


def kernel(x_prompt, x_sample, state_rwkv_fwd, state_rwkv_bwd, cache_mla_ckv, cache_mla_kpe, c, c_ctx,
           norm_g, w_mod, b_mod, ab_w_in, rw_mu, rw_w0, rw_w2, rw_a0, rw_a2, rw_g2, rw_k_k, rw_k_a, rw_r_k,
           rw_lnx_g, rw_lnx_b, mla_q_norm, mla_q_up, mla_kv_norm, mla_kv_up, mla_qn, mla_kn, ab_w_out,
           hy_w_in, hy_b_in, hy_conv_w, hy_conv_b, hy_f_w1, hy_f_b1, hy_f_w2, hy_f_b2, hy_f_w3, hy_f_freq,
           hy_bias, hy_w_out, peer_w_q, peer_keys, peer_u, peer_v):
    rope = axial_rope(x_sample.shape[1])
    xp, xs = x_prompt, x_sample
    st_f, st_b, st_ckv, st_kpe = [], [], [], []
    for li in range(DEPTH):
        j = li // 2
        mp = jnp.split(_mm(jax.nn.silu(c_ctx)[None], w_mod[li])[0] + b_mod[li], 6, axis=-1)
        ms = [m[:, None, :] for m in jnp.split(_mm(jax.nn.silu(c), w_mod[li]) + b_mod[li], 6, axis=-1)]
        hp = modulate(rmsnorm(xp, norm_g[li, 0]), mp[0], mp[1])
        hs = modulate(rmsnorm(xs, norm_g[li, 0]), ms[0], ms[1])
        if li % 2 == 0:
            p = (ab_w_in[j], rw_mu[j], rw_w0[j], rw_w2[j], rw_a0[j], rw_a2[j], rw_g2[j], rw_k_k[j], rw_k_a[j],
                 rw_r_k[j], rw_lnx_g[j], rw_lnx_b[j], mla_q_norm[j], mla_q_up[j], mla_kv_norm[j], mla_kv_up[j],
                 mla_qn[j], mla_kn[j], ab_w_out[j])
            op, sf, sb, ckv, kpe = rwkv_mla_mixer(hp, None, p)
            ctx = (state_rwkv_fwd[:, j], state_rwkv_bwd[:, j], cache_mla_ckv[:, j], cache_mla_kpe[:, j], rope)
            os_, _, _, _, _ = rwkv_mla_mixer(hs, ctx, p)
            st_f.append(sf)
            st_b.append(sb)
            st_ckv.append(ckv)
            st_kpe.append(kpe)
        else:
            p = (hy_w_in[j], hy_b_in[j], hy_conv_w[j], hy_conv_b[j], hy_f_w1[j], hy_f_b1[j], hy_f_w2[j],
                 hy_f_b2[j], hy_f_w3[j], hy_f_freq[j], hy_bias[j], hy_w_out[j])
            op = hyena_mixer(hp, p)
            os_ = hyena_mixer(hs, p)
        xp = xp + mp[2] * op
        xs = xs + ms[2] * os_
        pp = (peer_w_q[li], peer_keys[li], peer_u[li], peer_v[li])
        gp = modulate(rmsnorm(xp, norm_g[li, 1]), mp[3], mp[4]).reshape(-1, D_MODEL)
        gs = modulate(rmsnorm(xs, norm_g[li, 1]), ms[3], ms[4]).reshape(-1, D_MODEL)
        po = peer_ffn(jnp.concatenate([gp, gs], axis=0), *pp)
        xp = xp + mp[5] * po[:gp.shape[0]].reshape(xp.shape)
        xs = xs + ms[5] * po[gp.shape[0]:].reshape(xs.shape)
    new_state_rwkv_fwd = jnp.stack(st_f, axis=1).astype(x_prompt.dtype)
    new_state_rwkv_bwd = jnp.stack(st_b, axis=1).astype(x_prompt.dtype)
    new_cache_mla_ckv = jnp.stack(st_ckv, axis=1)
    new_cache_mla_kpe = jnp.stack(st_kpe, axis=1)
    return (xp, xs, new_state_rwkv_fwd, new_state_rwkv_bwd, new_cache_mla_ckv, new_cache_mla_kpe)
```

```python
import math
from functools import partial

import jax
import jax.numpy as jnp
from jax import lax
from jax.experimental import pallas as pl
from jax.experimental.pallas import tpu as pltpu

D_MODEL = 1024
DEPTH = 2
GRID_W = 64
EPS = 1e-6
RW_HEADS = 8
RW_HEAD = 64
RW_DIM = RW_HEADS * RW_HEAD
W_LORA = 64
A_LORA = 64
G_LORA = 128
LNX_EPS = 64e-5
RW_IN = 3 * RW_DIM + W_LORA + A_LORA + G_LORA
RW_SPLITS = (RW_DIM, 2 * RW_DIM, 3 * RW_DIM, 3 * RW_DIM + W_LORA, 3 * RW_DIM + W_LORA + A_LORA)
MLA_HEADS = 4
QK_NOPE = 128
QK_ROPE = 64
QK_HEAD = QK_NOPE + QK_ROPE
V_HEAD = 128
Q_LORA = 256
KV_LORA = 128
MLA_DIM = MLA_HEADS * V_HEAD
ROPE_THETA = 10000.0
HY_BANDS = 16
HY_TARGET = 1e-2
HY_FAST = 0.3
HY_SLOW = 1.5
PEER_KEYS = 128
PEER_HEADS = 8
PEER_DKEY = 256
PEER_TOPK = 16

LANES = 128
SUBLANES = 8
VMEM_LIMIT = 56 * 1024 * 1024
_NT = (((1,), (1,)), ((), ()))


def _split_bf16(x):
    hi = x.astype(jnp.bfloat16)
    lo = (x - hi.astype(jnp.float32)).astype(jnp.bfloat16)
    return hi, lo


def _dot3(a_hi, a_lo, b_hi, b_lo):
    f32 = jnp.float32
    return (jnp.dot(a_hi, b_hi, preferred_element_type=f32) + jnp.dot(a_lo, b_hi, preferred_element_type=f32)
            + jnp.dot(a_hi, b_lo, preferred_element_type=f32))


def _mm_kernel(a_ref, b_ref, o_ref, *, split):
    if split:
        a_hi, a_lo = _split_bf16(a_ref[...])
        b_hi, b_lo = _split_bf16(b_ref[...])
        o_ref[...] = _dot3(a_hi, a_lo, b_hi, b_lo)
    else:
        o_ref[...] = jnp.dot(a_ref[...].astype(jnp.bfloat16), b_ref[...].astype(jnp.bfloat16),
                             preferred_element_type=jnp.float32)


def _mm(a, b, tm=512, tn=512, split=False):
    lead = a.shape[:-1]
    K = a.shape[-1]
    N = b.shape[-1]
    a2 = a.reshape(-1, K)
    if K % LANES:
        kp = -K % LANES
        a2 = jnp.pad(a2, ((0, 0), (0, kp)))
        b = jnp.pad(b, ((0, kp), (0, 0)))
        K += kp
    M = a2.shape[0]
    tm = min(tm, M)
    tn = min(tn, N)
    if N % tn:
        tn = N
    assert M % tm == 0 and N % tn == 0
    out = pl.pallas_call(
        partial(_mm_kernel, split=split),
        grid=(M // tm, N // tn),
        in_specs=[pl.BlockSpec((tm, K), lambda i, j: (i, 0)),
                  pl.BlockSpec((K, tn), lambda i, j: (0, j))],
        out_specs=pl.BlockSpec((tm, tn), lambda i, j: (i, j)),
        out_shape=jax.ShapeDtypeStruct((M, N), jnp.float32),
        name="matmul",
    )(a2, b)
    return out.reshape(*lead, N)


def rmsnorm(x, g):
    xf = x.astype(jnp.float32)
    y = xf * lax.rsqrt(jnp.mean(xf * xf, axis=-1, keepdims=True) + EPS)
    return (y * g.astype(jnp.float32)).astype(x.dtype)


def centred_shift(x):
    prev = jnp.pad(x[:, :-1], ((0, 0), (1, 0), (0, 0)))
    nxt = jnp.pad(x[:, 1:], ((0, 0), (0, 1), (0, 0)))
    return 0.5 * (prev + nxt)


def centred_dwconv3(u, w, b):
    prev = jnp.pad(u[:, :-1], ((0, 0), (1, 0), (0, 0)))
    nxt = jnp.pad(u[:, 1:], ((0, 0), (0, 1), (0, 0)))
    return prev * w[0] + u * w[1] + nxt * w[2] + b


def axial_rope(L):
    rows = L // GRID_W
    row = jnp.repeat(jnp.arange(rows, dtype=jnp.float32), GRID_W)
    col = jnp.tile(jnp.arange(GRID_W, dtype=jnp.float32), rows)
    n_freq = QK_ROPE // 4
    inv = ROPE_THETA ** (-jnp.arange(n_freq, dtype=jnp.float32) / n_freq)
    ang = jnp.concatenate([row[:, None] * inv, col[:, None] * inv], axis=-1)
    return jnp.cos(ang), jnp.sin(ang)


def rope_tail(x, cos, sin):
    xn, xr = x[..., :QK_NOPE], x[..., QK_NOPE:]
    x1, x2 = xr[..., 0::2], xr[..., 1::2]
    cs = cos[None, :, None, :].astype(x.dtype)
    sn = sin[None, :, None, :].astype(x.dtype)
    rot = jnp.stack([x1 * cs - x2 * sn, x1 * sn + x2 * cs], axis=-1).reshape(xr.shape)
    return jnp.concatenate([xn, rot], axis=-1)


def _mla_attn_kernel(q_ref, k_ref, v_ref, o_ref):
    scale = QK_HEAD ** -0.5
    for h in range(MLA_HEADS):
        q = q_ref[0, :, h * QK_HEAD:(h + 1) * QK_HEAD].astype(jnp.bfloat16)
        k = k_ref[0, :, h * QK_HEAD:(h + 1) * QK_HEAD].astype(jnp.bfloat16)
        s = lax.dot_general(q, k, _NT, preferred_element_type=jnp.float32) * scale
        m = jnp.max(s, axis=-1, keepdims=True)
        p = jnp.exp(s - m)
        l = jnp.sum(p, axis=-1, keepdims=True)
        v = v_ref[0, :, h * V_HEAD:(h + 1) * V_HEAD].astype(jnp.bfloat16)
        o = jnp.dot((p / l).astype(jnp.bfloat16), v, preferred_element_type=jnp.float32)
        o_ref[0, :, h * V_HEAD:(h + 1) * V_HEAD] = o


def attend(q, k, v, tq=256):
    B, Lq = q.shape[:2]
    Lk = k.shape[1]
    tq = min(tq, Lq)
    return pl.pallas_call(
        _mla_attn_kernel, grid=(B, Lq // tq),
        in_specs=[pl.BlockSpec((1, tq, MLA_HEADS * QK_HEAD), lambda b, i: (b, i, 0)),
                  pl.BlockSpec((1, Lk, MLA_HEADS * QK_HEAD), lambda b, i: (b, 0, 0)),
                  pl.BlockSpec((1, Lk, MLA_HEADS * V_HEAD), lambda b, i: (b, 0, 0))],
        out_specs=pl.BlockSpec((1, tq, MLA_HEADS * V_HEAD), lambda b, i: (b, i, 0)),
        out_shape=jax.ShapeDtypeStruct((B, Lq, MLA_HEADS * V_HEAD), jnp.float32),
        compiler_params=pltpu.CompilerParams(dimension_semantics=("arbitrary", "arbitrary"),
                                             vmem_limit_bytes=VMEM_LIMIT),
        name="mla_attend",
    )(q.reshape(B, Lq, -1), k.reshape(B, Lk, -1), v.reshape(B, Lk, -1))


def _dft_split_input(x_refs, scr_refs):
    @pl.when(pl.program_id(2) == 0)
    def _():
        for x_ref, (hi_scr, lo_scr) in zip(x_refs, scr_refs):
            hi_scr[...], lo_scr[...] = _split_bf16(x_ref[0])


def _dft_pair_kernel(c_hi, c_lo, s_hi, s_lo, x_ref, oc_ref, os_ref, x_hi, x_lo):
    _dft_split_input([x_ref], [(x_hi, x_lo)])
    oc_ref[0] = _dot3(c_hi[...], c_lo[...], x_hi[...], x_lo[...])
    os_ref[0] = _dot3(s_hi[...], s_lo[...], x_hi[...], x_lo[...])


def _dft_spectral_kernel(c_hi, c_lo, s_hi, s_lo, x_ref, hr_ref, hi_ref, yr_ref, yi_ref, x_hi, x_lo):
    _dft_split_input([x_ref], [(x_hi, x_lo)])
    rb, L = c_hi.shape
    zc = _dot3(c_hi[...], c_lo[...], x_hi[...], x_lo[...])
    zs = _dot3(s_hi[...], s_lo[...], x_hi[...], x_lo[...])
    hr = hr_ref[...]
    hi = hi_ref[...]
    f = pl.program_id(2) * rb + lax.broadcasted_iota(jnp.int32, zc.shape, 0)
    yr_ref[0] = jnp.where(f == 0, zc * hr * (0.5 / L), (zc * hr - zs * hi) * (1.0 / L))
    yi_ref[0] = jnp.where(f == 0, zs * hi * (0.5 / L), (zc * hi + zs * hr) * (1.0 / L))


def _dft_sum_kernel(c_hi, c_lo, s_hi, s_lo, x_ref, y_ref, o_ref, x_hi, x_lo, y_hi, y_lo):
    _dft_split_input([x_ref, y_ref], [(x_hi, x_lo), (y_hi, y_lo)])
    o_ref[0] = (_dot3(c_hi[...], c_lo[...], x_hi[...], x_lo[...])
                + _dot3(s_hi[...], s_lo[...], y_hi[...], y_lo[...]))


def dft_tables(L):
    f = jnp.arange(L, dtype=jnp.int32)
    ph = (f[:, None] * f[None, :]) % (2 * L)
    ang = ph.astype(jnp.float32) * (math.pi / L)
    c = jnp.cos(ang)
    s = -jnp.sin(ang)
    alt = jnp.where(f % 2 == 0, 1.0, -1.0).astype(jnp.float32)
    s_ana = jnp.where(f[:, None] == 0, alt[None, :], s)
    s_syn = jnp.where(f[None, :] == 0, alt[:, None], s)
    cc = _split_bf16(c)
    return cc + _split_bf16(s_ana), cc + _split_bf16(s_syn)


def _dft_call(body, tabs, xs, n_out, name, row_inputs=(), rb=256, nb=512):
    B, L, N = xs[0].shape
    rb = min(rb, L)
    nb = min(nb, N)
    tab = pl.BlockSpec((rb, L), lambda b, n, i: (i, 0))
    xin = pl.BlockSpec((1, L, nb), lambda b, n, i: (b, 0, n))
    rin = pl.BlockSpec((rb, nb), lambda b, n, i: (i, n))
    out = pl.BlockSpec((1, rb, nb), lambda b, n, i: (b, i, n))
    sd = jax.ShapeDtypeStruct((B, L, N), jnp.float32)
    return pl.pallas_call(
        body, grid=(B, N // nb, L // rb),
        in_specs=[tab] * 4 + [xin] * len(xs) + [rin] * len(row_inputs),
        out_specs=[out] * n_out if n_out > 1 else out,
        out_shape=[sd] * n_out if n_out > 1 else sd,
        scratch_shapes=[pltpu.VMEM((L, nb), jnp.bfloat16)] * (2 * len(xs)),
        compiler_params=pltpu.CompilerParams(dimension_semantics=("arbitrary",) * 3,
                                             vmem_limit_bytes=VMEM_LIMIT),
        name=name,
    )(*tabs, *xs, *row_inputs)


def hyena_longconv(zin, circ):
    B, L, D = zin.shape
    tabs, tabs_syn = dft_tables(L)
    alt = jnp.where(jnp.arange(L) % 2 == 0, 1.0, -1.0).astype(jnp.float32)[:, None]
    fc, fs = _dft_call(_dft_pair_kernel, tabs, [jnp.concatenate([circ[:L], circ[L:]], axis=-1)[None]], 2, "dft_filter")
    hr = fc[0, :, :D] + alt * fc[0, :, D:]
    hi = fs[0, :, :D] + alt * fs[0, :, D:]
    yr, yi = _dft_call(_dft_spectral_kernel, tabs, [zin], 2, "dft_analysis", row_inputs=(hr, hi))
    return _dft_call(_dft_sum_kernel, tabs_syn, [yr, yi], 1, "dft_synthesis")


SCAN_ACCS = 4


def _rwkv_scan_kernel(r_ref, kk_ref, v_ref, w_ref, kd_ref, b_ref, s0_ref, y_ref, sf_ref, s_scr):
    d = pl.program_id(0)
    c = pl.program_id(1)
    tc = r_ref.shape[0]
    nv = v_ref.shape[1]

    @pl.when(c == 0)
    def _():
        s_scr[...] = s0_ref[0]

    def row(ref, tt, k):
        return jnp.broadcast_to(ref[tt, pl.ds(k, 1), :], (nv, LANES))

    def row_d(ref, tt, k):
        return jnp.broadcast_to(ref[0, tt, pl.ds(k, 1), :], (nv, LANES))

    def step(t, carry):
        tt = t + d * (tc - 1 - 2 * t)
        parts = [None] * SCAN_ACCS
        for k in range(RW_HEAD):
            term = s_scr[k] * row(kk_ref, tt, k)
            parts[k % SCAN_ACCS] = term if parts[k % SCAN_ACCS] is None else parts[k % SCAN_ACCS] + term
        sa = -((parts[0] + parts[1]) + (parts[2] + parts[3]))
        vv = v_ref[tt]
        parts = [None] * SCAN_ACCS
        for k in range(RW_HEAD):
            s_new = s_scr[k] * row_d(w_ref, tt, k) + sa * row_d(b_ref, tt, k) + vv * row_d(kd_ref, tt, k)
            s_scr[k] = s_new
            term = s_new * row(r_ref, tt, k)
            parts[k % SCAN_ACCS] = term if parts[k % SCAN_ACCS] is None else parts[k % SCAN_ACCS] + term
        y_ref[0, tt] = (parts[0] + parts[1]) + (parts[2] + parts[3])
        return carry

    lax.fori_loop(0, tc, step, 0)

    @pl.when(c == pl.num_programs(1) - 1)
    def _():
        sf_ref[0] = s_scr[...]


def rwkv_scan(r, kk, v, w2, kd2, b2, s0, tc=32):
    L = r.shape[0]
    nv = v.shape[1]
    nc = L // tc
    tmap = lambda d, c: c + d * (nc - 1 - 2 * c)
    shared = lambda rows: pl.BlockSpec((tc, rows, LANES), lambda d, c: (tmap(d, c), 0, 0))
    per_dir = pl.BlockSpec((1, tc, RW_HEAD, LANES), lambda d, c: (d, tmap(d, c), 0, 0))
    state = pl.BlockSpec((1, RW_HEAD, nv, LANES), lambda d, c: (d, 0, 0, 0))
    return pl.pallas_call(
        _rwkv_scan_kernel,
        grid=(2, nc),
        in_specs=[shared(RW_HEAD), shared(RW_HEAD), shared(nv), per_dir, per_dir, per_dir, state],
        out_specs=[pl.BlockSpec((1, tc, nv, LANES), lambda d, c: (d, tmap(d, c), 0, 0)), state],
        out_shape=[jax.ShapeDtypeStruct((2, L, nv, LANES), jnp.float32),
                   jax.ShapeDtypeStruct((2, RW_HEAD, nv, LANES), jnp.float32)],
        scratch_shapes=[pltpu.VMEM((RW_HEAD, nv, LANES), jnp.float32)],
        compiler_params=pltpu.CompilerParams(dimension_semantics=("arbitrary", "arbitrary"),
                                             vmem_limit_bytes=VMEM_LIMIT),
        name="rwkv_scan",
    )(r, kk, v, w2, kd2, b2, s0)


def to_lanes(x, vsplit):
    B, L, H, N = x.shape
    return jnp.tile(x.transpose(1, 3, 0, 2).reshape(L, N, B * H), (1, 1, vsplit))


def v_to_lanes(x, vsplit):
    B, L, H, N = x.shape
    nv = N // vsplit
    return x.reshape(B, L, H, vsplit, nv).transpose(1, 4, 3, 0, 2).reshape(L, nv, vsplit * B * H)


def v_from_lanes(y, B, H, vsplit):
    L, nv, _ = y.shape
    return y.reshape(L, nv, vsplit, B, H).transpose(3, 0, 4, 2, 1).reshape(B, L, H, vsplit * nv)


def state_to_lanes(s, vsplit):
    B, H, N, K = s.shape
    nv = N // vsplit
    return s.reshape(B, H, vsplit, nv, K).transpose(4, 3, 2, 0, 1).reshape(K, nv, vsplit * B * H)


def state_from_lanes(s, B, H, vsplit):
    K, nv, _ = s.shape
    return s.reshape(K, nv, vsplit, B, H).transpose(3, 4, 2, 1, 0).reshape(B, H, vsplit * nv, K)


def rwkv_scan_both(rh, kk, vh, decay2, kd2, b2, s0_2, tc=32):
    B, L, H, N = rh.shape
    vsplit = LANES // (B * H)
    assert B * H * vsplit == LANES
    stack = lambda xs: jnp.stack([to_lanes(x, vsplit) for x in xs])
    y2, sf2 = rwkv_scan(to_lanes(rh, vsplit), to_lanes(kk, vsplit), v_to_lanes(vh, vsplit),
                        stack(decay2), stack(kd2), stack(b2),
                        jnp.stack([state_to_lanes(s, vsplit) for s in s0_2]), tc=tc)
    ys = [v_from_lanes(y2[d], B, H, vsplit) for d in range(2)]
    sfs = [state_from_lanes(sf2[d], B, H, vsplit) for d in range(2)]
    return ys, sfs


def _rms_mod(x, g, shift, scale):
    y = x * lax.rsqrt(jnp.mean(x * x, axis=-1, keepdims=True) + EPS)
    return (y * g) * (1.0 + scale) + shift


def tile_mod(m_prompt, m_sample, n_prompt, n_sample_each, tm):
    reps = [jnp.broadcast_to(m_prompt[None], (n_prompt // tm, m_prompt.shape[-1])),
            jnp.repeat(m_sample, n_sample_each // tm, axis=0)]
    return jnp.concatenate(reps, axis=0)[:, None, :]


def _norm_mod_matmul_kernel(x_ref, g_ref, sh_ref, sc_ref, w_ref, b_ref, o_ref):
    h = _rms_mod(x_ref[...], g_ref[...], sh_ref[0], sc_ref[0])
    o_ref[...] = jnp.dot(h.astype(jnp.bfloat16), w_ref[...], preferred_element_type=jnp.float32) + b_ref[...]


def norm_mod_matmul(x, g, shift_t, scale_t, w, b, tm):
    T, D = x.shape
    N = w.shape[1]
    mod = pl.BlockSpec((1, 1, D), lambda i: (i, 0, 0))
    return pl.pallas_call(
        _norm_mod_matmul_kernel, grid=(T // tm,),
        in_specs=[pl.BlockSpec((tm, D), lambda i: (i, 0)), pl.BlockSpec((1, D), lambda i: (0, 0)), mod, mod,
                  pl.BlockSpec((D, N), lambda i: (0, 0)), pl.BlockSpec((1, N), lambda i: (0, 0))],
        out_specs=pl.BlockSpec((tm, N), lambda i: (i, 0)),
        out_shape=jax.ShapeDtypeStruct((T, N), jnp.float32),
        compiler_params=pltpu.CompilerParams(dimension_semantics=("arbitrary",), vmem_limit_bytes=VMEM_LIMIT),
        name="norm_mod_matmul",
    )(x, g[None], shift_t, scale_t, w.astype(jnp.bfloat16), b[None])


def _matmul_residual_kernel(a_ref, w_ref, x_ref, gate_ref, o_ref):
    o_ref[...] = x_ref[...] + gate_ref[0] * jnp.dot(a_ref[...].astype(jnp.bfloat16), w_ref[...],
                                                    preferred_element_type=jnp.float32)


def matmul_residual(a, w, x, gate_t, tm):
    T, K = a.shape
    D = w.shape[1]
    return pl.pallas_call(
        _matmul_residual_kernel, grid=(T // tm,),
        in_specs=[pl.BlockSpec((tm, K), lambda i: (i, 0)), pl.BlockSpec((K, D), lambda i: (0, 0)),
                  pl.BlockSpec((tm, D), lambda i: (i, 0)), pl.BlockSpec((1, 1, D), lambda i: (i, 0, 0))],
        out_specs=pl.BlockSpec((tm, D), lambda i: (i, 0)),
        out_shape=jax.ShapeDtypeStruct((T, D), jnp.float32),
        compiler_params=pltpu.CompilerParams(dimension_semantics=("arbitrary",), vmem_limit_bytes=VMEM_LIMIT),
        name="matmul_residual",
    )(a, w.astype(jnp.bfloat16), x, gate_t)


def mla_keys(ckv, k_pe, kv_up, kn):
    B, L, _ = ckv.shape
    kv = _mm(ckv, kv_up).reshape(B, L, MLA_HEADS, QK_NOPE + V_HEAD)
    k_rope = jnp.broadcast_to(k_pe[:, :, None, :], (B, L, MLA_HEADS, QK_ROPE))
    k = rmsnorm(jnp.concatenate([kv[..., :QK_NOPE], k_rope], axis=-1), kn)
    return k, kv[..., QK_NOPE:]


def rwkv_mla_mixer(proj, ctx, p):
    (mu, w0, w2, a0, a2, g2, k_k, k_a, r_k, lnx_g, lnx_b,
     q_norm, q_up, kv_norm, kv_up, qn, kn) = p
    B, L, _ = proj.shape
    f32 = jnp.float32
    rw, mla = proj[..., :RW_IN], proj[..., RW_IN:]
    rw = rw + mu * (centred_shift(rw) - rw)
    r, k, v, wd, ad, gd = jnp.split(rw, RW_SPLITS, axis=-1)
    heads = lambda t: t.reshape(B, L, RW_HEADS, RW_HEAD)
    rh, vh = heads(r.astype(f32)), heads(v.astype(f32))
    kk = heads((k * k_k).astype(f32))
    kk = kk / jnp.maximum(jnp.linalg.norm(kk, axis=-1, keepdims=True), 1e-12)
    if ctx is None:
        z = jnp.zeros((B, RW_HEADS, RW_HEAD, RW_HEAD), f32)
        s0s = (z, z)
    else:
        s0s = (ctx[0], ctx[1])
    decays, kds, bs, bonus = [], [], [], []
    for d in range(2):
        w_raw = (w0[d] + _mm(jnp.tanh(wd), w2[d])).astype(f32)
        decay = jnp.exp(-jnp.exp(-jax.nn.softplus(-w_raw) - 0.5))
        lr = jax.nn.sigmoid((a0[d] + _mm(ad, a2[d])).astype(f32))
        kd = heads(k.astype(f32) * (1.0 + (lr - 1.0) * k_a))
        decays.append(heads(decay))
        kds.append(kd)
        bs.append(kk * heads(lr))
        bonus.append(jnp.sum(rh * kd * r_k, axis=-1, keepdims=True) * vh)
    ys, finals = rwkv_scan_both(rh, kk, vh, decays, kds, bs, s0s)
    y = ys[0] + ys[1]
    mean = jnp.mean(y, axis=-1, keepdims=True)
    var = jnp.mean(jnp.square(y - mean), axis=-1, keepdims=True)
    yn = ((y - mean) * lax.rsqrt(var + LNX_EPS)).reshape(B, L, RW_DIM) * lnx_g + lnx_b
    gate = _mm(jax.nn.sigmoid(gd), g2)
    rw_out = (yn + (bonus[0] + bonus[1]).reshape(B, L, RW_DIM)) * gate
    q_c, kv_c, k_pe = jnp.split(mla, (Q_LORA, Q_LORA + KV_LORA), axis=-1)
    q = rmsnorm(_mm(rmsnorm(q_c, q_norm), q_up).reshape(B, L, MLA_HEADS, QK_HEAD), qn)
    ckv = rmsnorm(kv_c, kv_norm)
    k_own, v_own = mla_keys(ckv, k_pe, kv_up, kn)
    if ctx is None:
        keys, vals = k_own, v_own
    else:
        cos, sin = ctx[4]
        q = rope_tail(q, cos, sin)
        k_own = rope_tail(k_own, cos, sin)
        k_ctx, v_ctx = mla_keys(ctx[2], ctx[3], kv_up, kn)
        keys = jnp.concatenate([k_ctx, k_own], axis=1)
        vals = jnp.concatenate([v_ctx, v_own], axis=1)
    mla_out = attend(q, keys, vals)
    return jnp.concatenate([rw_out, mla_out], axis=-1), finals[0], finals[1], ckv, k_pe


def hyena_filters(L, w1, b1, w2, b2, w3, freq):
    f32 = jnp.float32
    u = jnp.arange(2 * L, dtype=jnp.int32)[:, None]
    t = jnp.where(u < L, u, 2 * L - u).astype(f32)
    t_unit = t / (L - 1)
    bands = jnp.linspace(1e-4, HY_BANDS - 1, HY_BANDS, dtype=f32)
    ang = 2.0 * math.pi * t * bands / L
    zpos = jnp.concatenate([t_unit, jnp.cos(ang), -jnp.sin(ang)], axis=-1)
    fr = freq.astype(f32)
    hid = jnp.sin(fr * (_mm(zpos, w1.astype(f32), split=True) + b1.astype(f32)))
    hid = jnp.sin(fr * (_mm(hid, w2.astype(f32), split=True) + b2.astype(f32)))
    filt = _mm(hid, w3.astype(f32), split=True)
    deltas = jnp.linspace(math.log(HY_TARGET) / HY_FAST, math.log(HY_TARGET) / HY_SLOW, D_MODEL, dtype=f32)
    window = jnp.exp(-t_unit * jnp.abs(deltas))
    circ = jnp.where(u < L, filt[:, :D_MODEL], filt[:, D_MODEL:]) * window
    circ = jnp.where(u == L, 0.0, circ)
    return circ / jnp.sum(jnp.abs(circ), axis=0, keepdims=True)


def hyena_mixer(proj, p):
    conv_w, conv_b, f_w1, f_b1, f_w2, f_b2, f_w3, f_freq, bias = p
    B, L, _ = proj.shape
    u = centred_dwconv3(proj, conv_w, conv_b)
    x0, x1, v = jnp.split(u, 3, axis=-1)
    zin = (x1 * v).astype(jnp.float32)
    conv = hyena_longconv(zin, hyena_filters(L, f_w1, f_b1, f_w2, f_b2, f_w3, f_freq))
    y = conv + zin * bias.astype(jnp.float32)
    return x0 * y


PEER_N = PEER_KEYS * PEER_KEYS
GATE_LANES = 2 * LANES
PEER_ROUTE_TOKENS = 256
PEER_EXPERT_TOKENS = 512
ROW_TILE = 512


def _top_vals(s, n, with_rank):
    vals = []
    rank = jnp.full(s.shape, float(n), jnp.float32) if with_rank else None
    for a in range(n):
        m = jnp.max(s, axis=0, keepdims=True)
        vals.append(m)
        hit = s == m
        if with_rank:
            rank = jnp.where(hit, float(a), rank)
        s = jnp.where(hit, -jnp.inf, s)
    return vals, rank


def _peer_route_kernel(x_ref, g_ref, sh_ref, sc_ref, wq_hi_ref, wq_lo_ref, k_hi_ref, k_lo_ref,
                       n1_ref, e1_ref, r2_ref, e2_ref, h_ref, s_scr):
    tb = x_ref.shape[0]
    half = PEER_DKEY // 2
    n_tiles = tb // LANES
    h_hi, h_lo = _split_bf16(_rms_mod(x_ref[...], g_ref[...], sh_ref[0], sc_ref[0]))
    h_ref[...] = h_hi
    q_hi, q_lo = _split_bf16(_dot3(h_hi, h_lo, wq_hi_ref[...], wq_lo_ref[...]))
    for hh in range(PEER_HEADS):
        for p in range(2):
            cols = slice((2 * hh + p) * half, (2 * hh + p + 1) * half)
            k_hi = k_hi_ref[hh, p]
            s_scr[hh, p] = (lax.dot_general(k_hi, q_hi[:, cols], _NT, preferred_element_type=jnp.float32)
                            + lax.dot_general(k_lo_ref[hh, p], q_hi[:, cols], _NT, preferred_element_type=jnp.float32)
                            + lax.dot_general(k_hi, q_lo[:, cols], _NT, preferred_element_type=jnp.float32))

    K = PEER_TOPK
    G = SUBLANES

    def tile(it, carry):
        hh = it // n_tiles
        ln = pl.ds(pl.multiple_of((it % n_tiles) * LANES, LANES), LANES)
        s1 = s_scr[hh, 0, :, ln]
        s2 = s_scr[hh, 1, :, ln]
        v1, _ = _top_vals(s1, K, False)
        v2l, r2 = _top_vals(s2, K, True)
        v2 = jnp.concatenate(v2l, axis=0)
        cand = [v1[0] + v2[:G], v1[0] + v2[G:]]
        cand += [v1[a] + v2[:G] for a in range(1, G)]
        cand += [jnp.concatenate(v1[G:], axis=0) + v2[0:1]]
        c = cand
        for k in range(K):
            m = c[0]
            for ci in c[1:]:
                m = jnp.maximum(m, ci)
            m = jnp.max(m, axis=0, keepdims=True)
            if k + 1 < K:
                c = [jnp.where(ci == m, -jnp.inf, ci) for ci in c]
        tau = m
        top = v1[0] + v2[0:1]
        keep = [ci >= tau for ci in cand]
        z = jnp.zeros_like(tau)
        for ci, ki in zip(cand, keep):
            z = z + jnp.sum(jnp.where(ki, jnp.exp(ci - top), 0.0), axis=0, keepdims=True)
        cnt = [jnp.sum(jnp.where(ki, 1.0, 0.0), axis=0, keepdims=True) for ki in keep[:G + 1]]
        tail = jnp.where(keep[G + 1], 1.0, 0.0)
        n_a = [cnt[0] + cnt[1]] + cnt[2:] + [tail[a:a + 1] for a in range(G)]
        n1 = jnp.zeros_like(s1)
        for a in range(K):
            n1 = jnp.where(s1 == v1[a], n_a[a], n1)
        n1_ref[hh, :, ln] = n1
        e1_ref[hh, :, ln] = jnp.exp(s1 - v1[0]) / z
        r2_ref[hh, :, ln] = r2.astype(jnp.bfloat16)
        e2_ref[hh, :, ln] = jnp.exp(s2 - v2[0:1]).astype(jnp.bfloat16)
        return carry

    lax.fori_loop(0, PEER_HEADS * n_tiles, tile, 0, unroll=2)


def peer_route(x, g, shift_t, scale_t, wq_hi, wq_lo, k_hi, k_lo, tb=PEER_ROUTE_TOKENS):
    T = x.shape[0]
    nh = PEER_HEADS
    blk = pl.BlockSpec((nh, PEER_KEYS, tb), lambda t: (0, 0, t))
    sd = lambda dt: jax.ShapeDtypeStruct((nh, PEER_KEYS, T), dt)
    whole = lambda a: pl.BlockSpec(a.shape, lambda t: (0,) * a.ndim)
    mod = pl.BlockSpec((1, 1, D_MODEL), lambda t: (t, 0, 0))
    return pl.pallas_call(
        _peer_route_kernel,
        grid=(T // tb,),
        in_specs=[pl.BlockSpec((tb, D_MODEL), lambda t: (t, 0)), whole(g), mod, mod,
                  whole(wq_hi), whole(wq_lo), whole(k_hi), whole(k_lo)],
        out_specs=[blk, blk, blk, blk, pl.BlockSpec((tb, D_MODEL), lambda t: (t, 0))],
        out_shape=[sd(jnp.float32), sd(jnp.float32), sd(jnp.bfloat16), sd(jnp.bfloat16),
                   jax.ShapeDtypeStruct((T, D_MODEL), jnp.bfloat16)],
        scratch_shapes=[pltpu.VMEM((nh, 2, PEER_KEYS, tb), jnp.float32)],
        compiler_params=pltpu.CompilerParams(dimension_semantics=("arbitrary",),
                                             vmem_limit_bytes=VMEM_LIMIT),
        name="peer_route",
    )(x, g, shift_t, scale_t, wq_hi, wq_lo, k_hi, k_lo)


def _gelu_tanh(x):
    hx = 0.5 * x
    return hx * jnp.tanh(x * (x * x * (0.7978845608028654 * 0.044715) + 0.7978845608028654)) + hx


def _peer_expert_kernel(h_ref, u_ref, vt_even_ref, vt_prev_ref, vt_last_ref, n1_ref, e1_ref, r2_ref, e2_ref,
                        x_ref, gate_ref, o_ref, acc_ref, a0_scr, a1_scr, w0_scr, w1_scr):
    c = pl.program_id(1)
    ec, tb = a0_scr.shape
    n_i = ec // PEER_KEYS
    bf16 = jnp.bfloat16
    f32 = jnp.float32

    @pl.when(c == 0)
    def _():
        acc_ref[...] = jnp.zeros_like(acc_ref)
        w1_scr[...] = jnp.zeros_like(w1_scr)

    def gate_times_act(a_scr, w_scr, chunk):
        igrp = pl.ds(pl.multiple_of(chunk * n_i, SUBLANES), SUBLANES)
        for lt in range(tb // GATE_LANES):
            ln = slice(lt * GATE_LANES, (lt + 1) * GATE_LANES)
            for ii in range(n_i):
                rows = slice(ii * PEER_KEYS, (ii + 1) * PEER_KEYS)
                g = jnp.zeros((PEER_KEYS, GATE_LANES), bf16)
                for hh in range(PEER_HEADS):
                    n1 = jnp.broadcast_to(n1_ref[hh, igrp, ln][ii:ii + 1], (PEER_KEYS, GATE_LANES)).astype(bf16)
                    e1 = jnp.broadcast_to(e1_ref[hh, igrp, ln][ii:ii + 1], (PEER_KEYS, GATE_LANES)).astype(bf16)
                    g = g + jnp.where(r2_ref[hh, :, ln] < n1, e2_ref[hh, :, ln] * e1, jnp.zeros((), bf16))
                w_scr[rows, ln] = g * _gelu_tanh(a_scr[rows, ln])

    h = h_ref[...]
    a0_scr[...] = lax.dot_general(u_ref[:ec], h, _NT, preferred_element_type=f32).astype(bf16)
    acc_ref[...] += jnp.dot(vt_prev_ref[...], w1_scr[...], preferred_element_type=f32)
    gate_times_act(a0_scr, w0_scr, 2 * c)
    a1_scr[...] = lax.dot_general(u_ref[ec:], h, _NT, preferred_element_type=f32).astype(bf16)
    acc_ref[...] += jnp.dot(vt_even_ref[...], w0_scr[...], preferred_element_type=f32)
    gate_times_act(a1_scr, w1_scr, 2 * c + 1)

    @pl.when(c == pl.num_programs(1) - 1)
    def _():
        y = acc_ref[...] + jnp.dot(vt_last_ref[...], w1_scr[...], preferred_element_type=f32)
        o_ref[...] = x_ref[...] + gate_ref[0] * y.T


def peer_experts(h_bf16, u_bf16, vt_bf16, n1, e1, r2, e2, x, gate_t, tb=PEER_EXPERT_TOKENS, ec=SUBLANES * PEER_KEYS):
    T = h_bf16.shape[0]
    n_steps = PEER_N // (2 * ec)
    rblk = pl.BlockSpec((PEER_HEADS, PEER_KEYS, tb), lambda t, c: (0, 0, t))
    vt_blk = lambda chunk_of: pl.BlockSpec((D_MODEL, ec), lambda t, c: (0, chunk_of(c)))
    return pl.pallas_call(
        _peer_expert_kernel,
        grid=(T // tb, n_steps),
        in_specs=[pl.BlockSpec((tb, D_MODEL), lambda t, c: (t, 0)),
                  pl.BlockSpec((2 * ec, D_MODEL), lambda t, c: (c, 0)),
                  vt_blk(lambda c: 2 * c),
                  vt_blk(lambda c: jnp.maximum(2 * c - 1, 0)),
                  vt_blk(lambda c: 2 * n_steps - 1),
                  rblk, rblk, rblk, rblk,
                  pl.BlockSpec((tb, D_MODEL), lambda t, c: (t, 0)),
                  pl.BlockSpec((1, 1, D_MODEL), lambda t, c: (t, 0, 0))],
        out_specs=pl.BlockSpec((tb, D_MODEL), lambda t, c: (t, 0)),
        out_shape=jax.ShapeDtypeStruct((T, D_MODEL), jnp.float32),
        scratch_shapes=[pltpu.VMEM((D_MODEL, tb), jnp.float32)] + [pltpu.VMEM((ec, tb), jnp.bfloat16)] * 4,
        compiler_params=pltpu.CompilerParams(dimension_semantics=("arbitrary", "arbitrary"),
                                             vmem_limit_bytes=VMEM_LIMIT),
        name="peer_experts",
    )(h_bf16, u_bf16, vt_bf16, vt_bf16, vt_bf16, n1, e1, r2, e2, x, gate_t)


def peer_block(x, g, mods_route, mods_expert, w_q, sub_keys, u_tab, v_tab):
    wq_hi, wq_lo = _split_bf16(w_q)
    k_hi, k_lo = _split_bf16(sub_keys)
    n1, e1, r2, e2, h_bf16 = peer_route(x, g[None], mods_route[0], mods_route[1], wq_hi, wq_lo, k_hi, k_lo)
    return peer_experts(h_bf16, u_tab.astype(jnp.bfloat16), v_tab.T.astype(jnp.bfloat16),
                        n1, e1, r2, e2, x, mods_expert)


def kernel(x_prompt, x_sample, state_rwkv_fwd, state_rwkv_bwd, cache_mla_ckv, cache_mla_kpe, c, c_ctx,
           norm_g, w_mod, b_mod, ab_w_in, rw_mu, rw_w0, rw_w2, rw_a0, rw_a2, rw_g2, rw_k_k, rw_k_a, rw_r_k,
           rw_lnx_g, rw_lnx_b, mla_q_norm, mla_q_up, mla_kv_norm, mla_kv_up, mla_qn, mla_kn, ab_w_out,
           hy_w_in, hy_b_in, hy_conv_w, hy_conv_b, hy_f_w1, hy_f_b1, hy_f_w2, hy_f_b2, hy_f_w3, hy_f_freq,
           hy_bias, hy_w_out, peer_w_q, peer_keys, peer_u, peer_v):
    rope = axial_rope(x_sample.shape[1])
    D = D_MODEL
    n_p = x_prompt.shape[0] * x_prompt.shape[1]
    Bs, Ls = x_sample.shape[:2]
    x = jnp.concatenate([x_prompt.reshape(n_p, D), x_sample.reshape(Bs * Ls, D)], axis=0)
    split_rows = lambda a: (a[:n_p].reshape(x_prompt.shape[:2] + a.shape[1:]), a[n_p:].reshape((Bs, Ls) + a.shape[1:]))
    merge_rows = lambda ap, as_: jnp.concatenate([ap.reshape(n_p, -1), as_.reshape(Bs * Ls, -1)], axis=0)
    st_f, st_b, st_ckv, st_kpe = [], [], [], []
    for li in range(DEPTH):
        j = li // 2
        mp = jnp.split(_mm(jax.nn.silu(c_ctx)[None], w_mod[li])[0] + b_mod[li], 6, axis=-1)
        ms = jnp.split(_mm(jax.nn.silu(c), w_mod[li]) + b_mod[li], 6, axis=-1)
        mod = lambda i, tm: tile_mod(mp[i], ms[i], n_p, Ls, tm)
        if li % 2 == 0:
            p = (rw_mu[j], rw_w0[j], rw_w2[j], rw_a0[j], rw_a2[j], rw_g2[j], rw_k_k[j], rw_k_a[j],
                 rw_r_k[j], rw_lnx_g[j], rw_lnx_b[j], mla_q_norm[j], mla_q_up[j], mla_kv_norm[j], mla_kv_up[j],
                 mla_qn[j], mla_kn[j])
            proj = norm_mod_matmul(x, norm_g[li, 0], mod(0, ROW_TILE), mod(1, ROW_TILE), ab_w_in[j],
                                   jnp.zeros((ab_w_in.shape[-1],), jnp.float32), ROW_TILE)
            proj_p, proj_s = split_rows(proj)
            op, sf, sb, ckv, kpe = rwkv_mla_mixer(proj_p, None, p)
            ctx = (state_rwkv_fwd[:, j], state_rwkv_bwd[:, j], cache_mla_ckv[:, j], cache_mla_kpe[:, j], rope)
            os_, _, _, _, _ = rwkv_mla_mixer(proj_s, ctx, p)
            st_f.append(sf)
            st_b.append(sb)
            st_ckv.append(ckv)
            st_kpe.append(kpe)
            w_out = ab_w_out[j]
        else:
            p = (hy_conv_w[j], hy_conv_b[j], hy_f_w1[j], hy_f_b1[j], hy_f_w2[j],
                 hy_f_b2[j], hy_f_w3[j], hy_f_freq[j], hy_bias[j])
            proj = norm_mod_matmul(x, norm_g[li, 0], mod(0, ROW_TILE), mod(1, ROW_TILE), hy_w_in[j], hy_b_in[j],
                                   ROW_TILE)
            proj_p, proj_s = split_rows(proj)
            op = hyena_mixer(proj_p, p)
            os_ = hyena_mixer(proj_s, p)
            w_out = hy_w_out[j]
        x = matmul_residual(merge_rows(op, os_), w_out, x, mod(2, ROW_TILE), ROW_TILE)
        x = peer_block(x, norm_g[li, 1], (mod(3, PEER_ROUTE_TOKENS), mod(4, PEER_ROUTE_TOKENS)),
                       mod(5, PEER_EXPERT_TOKENS), peer_w_q[li], peer_keys[li], peer_u[li], peer_v[li])
    xp, xs = split_rows(x)
    new_state_rwkv_fwd = jnp.stack(st_f, axis=1).astype(x_prompt.dtype)
    new_state_rwkv_bwd = jnp.stack(st_b, axis=1).astype(x_prompt.dtype)
    new_cache_mla_ckv = jnp.stack(st_ckv, axis=1)
    new_cache_mla_kpe = jnp.stack(st_kpe, axis=1)
    return (xp, xs, new_state_rwkv_fwd, new_state_rwkv_bwd, new_cache_mla_ckv, new_cache_mla_kpe)
```

```python
import math
from functools import lru_cache, partial

import jax
import jax.numpy as jnp
import numpy as np
from jax import lax
from jax.experimental import pallas as pl
from jax.experimental.pallas import tpu as pltpu

D_MODEL = 1024
DEPTH = 2
GRID_W = 64
EPS = 1e-6
RW_HEADS = 8
RW_HEAD = 64
RW_DIM = RW_HEADS * RW_HEAD
W_LORA = 64
A_LORA = 64
G_LORA = 128
LNX_EPS = 64e-5
RW_IN = 3 * RW_DIM + W_LORA + A_LORA + G_LORA
RW_SPLITS = (RW_DIM, 2 * RW_DIM, 3 * RW_DIM, 3 * RW_DIM + W_LORA, 3 * RW_DIM + W_LORA + A_LORA)
MLA_HEADS = 4
QK_NOPE = 128
QK_ROPE = 64
QK_HEAD = QK_NOPE + QK_ROPE
V_HEAD = 128
Q_LORA = 256
KV_LORA = 128
MLA_DIM = MLA_HEADS * V_HEAD
ROPE_THETA = 10000.0
HY_BANDS = 16
HY_TARGET = 1e-2
HY_FAST = 0.3
HY_SLOW = 1.5
PEER_KEYS = 128
PEER_HEADS = 8
PEER_DKEY = 256
PEER_TOPK = 16

LANES = 128
SUBLANES = 8
VMEM_LIMIT = 56 * 1024 * 1024
_NT = (((1,), (1,)), ((), ()))


def _split_bf16(x):
    hi = x.astype(jnp.bfloat16)
    lo = (x - hi.astype(jnp.float32)).astype(jnp.bfloat16)
    return hi, lo


def _dot3(a_hi, a_lo, b_hi, b_lo):
    f32 = jnp.float32
    return (jnp.dot(a_hi, b_hi, preferred_element_type=f32) + jnp.dot(a_lo, b_hi, preferred_element_type=f32)
            + jnp.dot(a_hi, b_lo, preferred_element_type=f32))


def _mm_kernel(a_ref, b_ref, o_ref, *, split):
    if split:
        a_hi, a_lo = _split_bf16(a_ref[...])
        b_hi, b_lo = _split_bf16(b_ref[...])
        o_ref[...] = _dot3(a_hi, a_lo, b_hi, b_lo)
    else:
        o_ref[...] = jnp.dot(a_ref[...].astype(jnp.bfloat16), b_ref[...].astype(jnp.bfloat16),
                             preferred_element_type=jnp.float32)


def _mm(a, b, tm=512, tn=512, split=False):
    lead = a.shape[:-1]
    K = a.shape[-1]
    N = b.shape[-1]
    a2 = a.reshape(-1, K)
    if K % LANES:
        kp = -K % LANES
        a2 = jnp.pad(a2, ((0, 0), (0, kp)))
        b = jnp.pad(b, ((0, kp), (0, 0)))
        K += kp
    M = a2.shape[0]
    tm = min(tm, M)
    tn = min(tn, N)
    if N % tn:
        tn = N
    assert M % tm == 0 and N % tn == 0
    out = pl.pallas_call(
        partial(_mm_kernel, split=split),
        grid=(M // tm, N // tn),
        in_specs=[pl.BlockSpec((tm, K), lambda i, j: (i, 0)),
                  pl.BlockSpec((K, tn), lambda i, j: (0, j))],
        out_specs=pl.BlockSpec((tm, tn), lambda i, j: (i, j)),
        out_shape=jax.ShapeDtypeStruct((M, N), jnp.float32),
        name="matmul",
    )(a2, b)
    return out.reshape(*lead, N)


def rmsnorm(x, g):
    xf = x.astype(jnp.float32)
    y = xf * lax.rsqrt(jnp.mean(xf * xf, axis=-1, keepdims=True) + EPS)
    return (y * g.astype(jnp.float32)).astype(x.dtype)


def centred_dwconv3(u, w, b):
    prev = jnp.pad(u[:, :-1], ((0, 0), (1, 0), (0, 0)))
    nxt = jnp.pad(u[:, 1:], ((0, 0), (0, 1), (0, 0)))
    return prev * w[0] + u * w[1] + nxt * w[2] + b


def axial_rope(L):
    rows = L // GRID_W
    row = jnp.repeat(jnp.arange(rows, dtype=jnp.float32), GRID_W)
    col = jnp.tile(jnp.arange(GRID_W, dtype=jnp.float32), rows)
    n_freq = QK_ROPE // 4
    inv = ROPE_THETA ** (-jnp.arange(n_freq, dtype=jnp.float32) / n_freq)
    ang = jnp.concatenate([row[:, None] * inv, col[:, None] * inv], axis=-1)
    return jnp.cos(ang), jnp.sin(ang)


def rope_tail(x, cos, sin):
    xn, xr = x[..., :QK_NOPE], x[..., QK_NOPE:]
    x1, x2 = xr[..., 0::2], xr[..., 1::2]
    cs = cos[None, :, None, :].astype(x.dtype)
    sn = sin[None, :, None, :].astype(x.dtype)
    rot = jnp.stack([x1 * cs - x2 * sn, x1 * sn + x2 * cs], axis=-1).reshape(xr.shape)
    return jnp.concatenate([xn, rot], axis=-1)


def _mla_attn_kernel(q_ref, k_ref, v_ref, o_ref):
    scale = QK_HEAD ** -0.5
    for h in range(MLA_HEADS):
        q = q_ref[0, :, h * QK_HEAD:(h + 1) * QK_HEAD].astype(jnp.bfloat16)
        k = k_ref[0, :, h * QK_HEAD:(h + 1) * QK_HEAD].astype(jnp.bfloat16)
        s = lax.dot_general(q, k, _NT, preferred_element_type=jnp.float32) * scale
        m = jnp.max(s, axis=-1, keepdims=True)
        p = jnp.exp(s - m)
        l = jnp.sum(p, axis=-1, keepdims=True)
        v = v_ref[0, :, h * V_HEAD:(h + 1) * V_HEAD].astype(jnp.bfloat16)
        o = jnp.dot((p / l).astype(jnp.bfloat16), v, preferred_element_type=jnp.float32)
        o_ref[0, :, h * V_HEAD:(h + 1) * V_HEAD] = o


def attend(q, k, v, tq=256):
    B, Lq = q.shape[:2]
    Lk = k.shape[1]
    tq = min(tq, Lq)
    return pl.pallas_call(
        _mla_attn_kernel, grid=(B, Lq // tq),
        in_specs=[pl.BlockSpec((1, tq, MLA_HEADS * QK_HEAD), lambda b, i: (b, i, 0)),
                  pl.BlockSpec((1, Lk, MLA_HEADS * QK_HEAD), lambda b, i: (b, 0, 0)),
                  pl.BlockSpec((1, Lk, MLA_HEADS * V_HEAD), lambda b, i: (b, 0, 0))],
        out_specs=pl.BlockSpec((1, tq, MLA_HEADS * V_HEAD), lambda b, i: (b, i, 0)),
        out_shape=jax.ShapeDtypeStruct((B, Lq, MLA_HEADS * V_HEAD), jnp.float32),
        compiler_params=pltpu.CompilerParams(dimension_semantics=("arbitrary", "arbitrary"),
                                             vmem_limit_bytes=VMEM_LIMIT),
        name="mla_attend",
    )(q.reshape(B, Lq, -1), k.reshape(B, Lk, -1), v.reshape(B, Lk, -1))


def _dft_split_input(x_refs, scr_refs):
    @pl.when(pl.program_id(2) == 0)
    def _():
        for x_ref, (hi_scr, lo_scr) in zip(x_refs, scr_refs):
            hi_scr[...], lo_scr[...] = _split_bf16(x_ref[0])


def _dft_pair_kernel(c_hi, c_lo, s_hi, s_lo, x_ref, oc_ref, os_ref, x_hi, x_lo):
    _dft_split_input([x_ref], [(x_hi, x_lo)])
    oc_ref[0] = _dot3(c_hi[...], c_lo[...], x_hi[...], x_lo[...])
    os_ref[0] = _dot3(s_hi[...], s_lo[...], x_hi[...], x_lo[...])


def _dft_spectral_kernel(c_hi, c_lo, s_hi, s_lo, x_ref, hr_ref, hi_ref, yr_ref, yi_ref, x_hi, x_lo):
    _dft_split_input([x_ref], [(x_hi, x_lo)])
    rb, L = c_hi.shape
    zc = _dot3(c_hi[...], c_lo[...], x_hi[...], x_lo[...])
    zs = _dot3(s_hi[...], s_lo[...], x_hi[...], x_lo[...])
    hr = hr_ref[...]
    hi = hi_ref[...]
    f = pl.program_id(2) * rb + lax.broadcasted_iota(jnp.int32, zc.shape, 0)
    yr_ref[0] = jnp.where(f == 0, zc * hr * (0.5 / L), (zc * hr - zs * hi) * (1.0 / L))
    yi_ref[0] = jnp.where(f == 0, zs * hi * (0.5 / L), (zc * hi + zs * hr) * (1.0 / L))


def _dft_sum_kernel(c_hi, c_lo, s_hi, s_lo, x_ref, y_ref, o_ref, x_hi, x_lo, y_hi, y_lo):
    _dft_split_input([x_ref, y_ref], [(x_hi, x_lo), (y_hi, y_lo)])
    o_ref[0] = (_dot3(c_hi[...], c_lo[...], x_hi[...], x_lo[...])
                + _dot3(s_hi[...], s_lo[...], y_hi[...], y_lo[...]))


@lru_cache(maxsize=None)
def dft_tables(L):
    f = np.arange(L, dtype=np.int64)
    ang = ((f[:, None] * f[None, :]) % (2 * L)).astype(np.float64) * (math.pi / L)
    c = np.cos(ang).astype(np.float32)
    s = (-np.sin(ang)).astype(np.float32)
    alt = np.where(f % 2 == 0, 1.0, -1.0).astype(np.float32)
    s_ana = np.where(f[:, None] == 0, alt[None, :], s)
    s_syn = np.where(f[None, :] == 0, alt[:, None], s)

    def split(x):
        hi = x.astype(jnp.bfloat16)
        return hi, (x - hi.astype(np.float32)).astype(jnp.bfloat16)

    cc = split(c)
    return cc + split(s_ana), cc + split(s_syn)


def _dft_call(body, tabs, xs, n_out, name, row_inputs=(), rb=256, nb=512):
    B, L, N = xs[0].shape
    rb = min(rb, L)
    nb = min(nb, N)
    tab = pl.BlockSpec((rb, L), lambda b, n, i: (i, 0))
    xin = pl.BlockSpec((1, L, nb), lambda b, n, i: (b, 0, n))
    rin = pl.BlockSpec((rb, nb), lambda b, n, i: (i, n))
    out = pl.BlockSpec((1, rb, nb), lambda b, n, i: (b, i, n))
    sd = jax.ShapeDtypeStruct((B, L, N), jnp.float32)
    return pl.pallas_call(
        body, grid=(B, N // nb, L // rb),
        in_specs=[tab] * 4 + [xin] * len(xs) + [rin] * len(row_inputs),
        out_specs=[out] * n_out if n_out > 1 else out,
        out_shape=[sd] * n_out if n_out > 1 else sd,
        scratch_shapes=[pltpu.VMEM((L, nb), jnp.bfloat16)] * (2 * len(xs)),
        compiler_params=pltpu.CompilerParams(dimension_semantics=("arbitrary",) * 3,
                                             vmem_limit_bytes=VMEM_LIMIT),
        name=name,
    )(*tabs, *xs, *row_inputs)


def hyena_longconv(zin, circ):
    B, L, D = zin.shape
    tabs, tabs_syn = dft_tables(L)
    alt = jnp.where(jnp.arange(L) % 2 == 0, 1.0, -1.0).astype(jnp.float32)[:, None]
    fc, fs = _dft_call(_dft_pair_kernel, tabs, [jnp.concatenate([circ[:L], circ[L:]], axis=-1)[None]], 2, "dft_filter")
    hr = fc[0, :, :D] + alt * fc[0, :, D:]
    hi = fs[0, :, :D] + alt * fs[0, :, D:]
    yr, yi = _dft_call(_dft_spectral_kernel, tabs, [zin], 2, "dft_analysis", row_inputs=(hr, hi))
    return _dft_call(_dft_sum_kernel, tabs_syn, [yr, yi], 1, "dft_synthesis")


SCAN_ACCS = 4


def _rwkv_scan_kernel(r_ref, kk_ref, v_ref, w_ref, kd_ref, b_ref, s0_ref, y_ref, sf_ref, s_scr):
    d = pl.program_id(0)
    c = pl.program_id(1)
    tc = r_ref.shape[0]
    nv = v_ref.shape[1]

    @pl.when(c == 0)
    def _():
        s_scr[...] = s0_ref[0]

    def row(ref, tt, k):
        return jnp.broadcast_to(ref[tt, pl.ds(k, 1), :], (nv, LANES))

    def row_d(ref, tt, k):
        return jnp.broadcast_to(ref[0, tt, pl.ds(k, 1), :], (nv, LANES))

    def step(t, carry):
        tt = t + d * (tc - 1 - 2 * t)
        parts = [None] * SCAN_ACCS
        for k in range(RW_HEAD):
            term = s_scr[k] * row(kk_ref, tt, k)
            parts[k % SCAN_ACCS] = term if parts[k % SCAN_ACCS] is None else parts[k % SCAN_ACCS] + term
        sa = -((parts[0] + parts[1]) + (parts[2] + parts[3]))
        vv = v_ref[tt]
        parts = [None] * SCAN_ACCS
        for k in range(RW_HEAD):
            s_new = s_scr[k] * row_d(w_ref, tt, k) + sa * row_d(b_ref, tt, k) + vv * row_d(kd_ref, tt, k)
            s_scr[k] = s_new
            term = s_new * row(r_ref, tt, k)
            parts[k % SCAN_ACCS] = term if parts[k % SCAN_ACCS] is None else parts[k % SCAN_ACCS] + term
        y_ref[0, tt] = (parts[0] + parts[1]) + (parts[2] + parts[3])
        return carry

    lax.fori_loop(0, tc, step, 0)

    @pl.when(c == pl.num_programs(1) - 1)
    def _():
        sf_ref[0] = s_scr[...]


def rwkv_scan(r, kk, v, w2, kd2, b2, s0, tc=32):
    L = r.shape[0]
    nv = v.shape[1]
    nc = L // tc
    tmap = lambda d, c: c + d * (nc - 1 - 2 * c)
    shared = lambda rows: pl.BlockSpec((tc, rows, LANES), lambda d, c: (tmap(d, c), 0, 0))
    per_dir = pl.BlockSpec((1, tc, RW_HEAD, LANES), lambda d, c: (d, tmap(d, c), 0, 0))
    state = pl.BlockSpec((1, RW_HEAD, nv, LANES), lambda d, c: (d, 0, 0, 0))
    return pl.pallas_call(
        _rwkv_scan_kernel,
        grid=(2, nc),
        in_specs=[shared(RW_HEAD), shared(RW_HEAD), shared(nv), per_dir, per_dir, per_dir, state],
        out_specs=[pl.BlockSpec((1, tc, nv, LANES), lambda d, c: (d, tmap(d, c), 0, 0)), state],
        out_shape=[jax.ShapeDtypeStruct((2, L, nv, LANES), jnp.float32),
                   jax.ShapeDtypeStruct((2, RW_HEAD, nv, LANES), jnp.float32)],
        scratch_shapes=[pltpu.VMEM((RW_HEAD, nv, LANES), jnp.float32)],
        compiler_params=pltpu.CompilerParams(dimension_semantics=("arbitrary", "arbitrary"),
                                             vmem_limit_bytes=VMEM_LIMIT),
        name="rwkv_scan",
    )(r, kk, v, w2, kd2, b2, s0)


def to_lanes(x, vsplit):
    B, L, H, N = x.shape
    return jnp.tile(x.transpose(1, 3, 0, 2).reshape(L, N, B * H), (1, 1, vsplit))


def v_to_lanes(x, vsplit):
    B, L, H, N = x.shape
    nv = N // vsplit
    return x.reshape(B, L, H, vsplit, nv).transpose(1, 4, 3, 0, 2).reshape(L, nv, vsplit * B * H)


def v_from_lanes(y, B, H, vsplit):
    L, nv, _ = y.shape
    return y.reshape(L, nv, vsplit, B, H).transpose(3, 0, 4, 2, 1).reshape(B, L, H, vsplit * nv)


def state_to_lanes(s, vsplit):
    B, H, N, K = s.shape
    nv = N // vsplit
    return s.reshape(B, H, vsplit, nv, K).transpose(4, 3, 2, 0, 1).reshape(K, nv, vsplit * B * H)


def state_from_lanes(s, B, H, vsplit):
    K, nv, _ = s.shape
    return s.reshape(K, nv, vsplit, B, H).transpose(3, 4, 2, 1, 0).reshape(B, H, vsplit * nv, K)


def rwkv_scan_both(rh, kk, vh, decay2, kd2, b2, s0_2, tc=32):
    B, L, H, N = rh.shape
    vsplit = LANES // (B * H)
    assert B * H * vsplit == LANES
    stack = lambda xs: jnp.stack([to_lanes(x, vsplit) for x in xs])
    y2, sf2 = rwkv_scan(to_lanes(rh, vsplit), to_lanes(kk, vsplit), v_to_lanes(vh, vsplit),
                        stack(decay2), stack(kd2), stack(b2),
                        jnp.stack([state_to_lanes(s, vsplit) for s in s0_2]), tc=tc)
    ys = [v_from_lanes(y2[d], B, H, vsplit) for d in range(2)]
    sfs = [state_from_lanes(sf2[d], B, H, vsplit) for d in range(2)]
    return ys, sfs


def _rms_mod(x, g, shift, scale):
    y = x * lax.rsqrt(jnp.mean(x * x, axis=-1, keepdims=True) + EPS)
    return (y * g) * (1.0 + scale) + shift


def tile_mod(m_prompt, m_sample, n_prompt, n_sample_each, tm):
    reps = [jnp.broadcast_to(m_prompt[None], (n_prompt // tm, m_prompt.shape[-1])),
            jnp.repeat(m_sample, n_sample_each // tm, axis=0)]
    return jnp.concatenate(reps, axis=0)[:, None, :]


def _norm_mod_matmul_kernel(x_ref, g_ref, sh_ref, sc_ref, w_ref, b_ref, o_ref):
    h = _rms_mod(x_ref[...], g_ref[...], sh_ref[0], sc_ref[0])
    o_ref[...] = jnp.dot(h.astype(jnp.bfloat16), w_ref[...], preferred_element_type=jnp.float32) + b_ref[...]


def norm_mod_matmul(x, g, shift_t, scale_t, w, b, tm):
    T, D = x.shape
    N = w.shape[1]
    mod = pl.BlockSpec((1, 1, D), lambda i: (i, 0, 0))
    return pl.pallas_call(
        _norm_mod_matmul_kernel, grid=(T // tm,),
        in_specs=[pl.BlockSpec((tm, D), lambda i: (i, 0)), pl.BlockSpec((1, D), lambda i: (0, 0)), mod, mod,
                  pl.BlockSpec((D, N), lambda i: (0, 0)), pl.BlockSpec((1, N), lambda i: (0, 0))],
        out_specs=pl.BlockSpec((tm, N), lambda i: (i, 0)),
        out_shape=jax.ShapeDtypeStruct((T, N), jnp.float32),
        compiler_params=pltpu.CompilerParams(dimension_semantics=("arbitrary",), vmem_limit_bytes=VMEM_LIMIT),
        name="norm_mod_matmul",
    )(x, g[None], shift_t, scale_t, w.astype(jnp.bfloat16), b[None])


def _matmul_residual_kernel(a_ref, w_ref, x_ref, gate_ref, o_ref):
    o_ref[...] = x_ref[...] + gate_ref[0] * jnp.dot(a_ref[...].astype(jnp.bfloat16), w_ref[...],
                                                    preferred_element_type=jnp.float32)


def matmul_residual(a, w, x, gate_t, tm):
    T, K = a.shape
    D = w.shape[1]
    return pl.pallas_call(
        _matmul_residual_kernel, grid=(T // tm,),
        in_specs=[pl.BlockSpec((tm, K), lambda i: (i, 0)), pl.BlockSpec((K, D), lambda i: (0, 0)),
                  pl.BlockSpec((tm, D), lambda i: (i, 0)), pl.BlockSpec((1, 1, D), lambda i: (i, 0, 0))],
        out_specs=pl.BlockSpec((tm, D), lambda i: (i, 0)),
        out_shape=jax.ShapeDtypeStruct((T, D), jnp.float32),
        compiler_params=pltpu.CompilerParams(dimension_semantics=("arbitrary",), vmem_limit_bytes=VMEM_LIMIT),
        name="matmul_residual",
    )(a, w.astype(jnp.bfloat16), x, gate_t)


def _head_sums(x, ones_bd):
    hi, lo = _split_bf16(x)
    return (jnp.dot(hi, ones_bd, preferred_element_type=jnp.float32)
            + jnp.dot(lo, ones_bd, preferred_element_type=jnp.float32))


def _rwkv_prep_kernel(proj_ref, prev_ref, next_ref, first_ref, last_ref, mu_ref, w0_ref, w2_ref, a0_ref, a2_ref,
                      g2_ref, kk_ref, ka_ref, rk_ref, ones_ref,
                      r_out, kk_out, v_out, w_out0, w_out1, kd_out0, kd_out1, b_out0, b_out1, gate_out, bonus_out):
    tm = proj_ref.shape[0]
    bf16 = jnp.bfloat16
    f32 = jnp.float32
    x = proj_ref[:, :RW_IN]
    row = lax.broadcasted_iota(jnp.int32, (tm, 1), 0)
    prev_row = prev_ref[SUBLANES - 1:SUBLANES, :RW_IN] * (1.0 - first_ref[0, :, :1])
    next_row = next_ref[0:1, :RW_IN] * (1.0 - last_ref[0, :, :1])
    prev = jnp.where(row == 0, prev_row, pltpu.roll(x, 1, 0))
    nxt = jnp.where(row == tm - 1, next_row, pltpu.roll(x, tm - 1, 0))
    rw = x + mu_ref[...] * (0.5 * (prev + nxt) - x)
    r = rw[:, RW_SPLITS[0] - RW_DIM:RW_SPLITS[0]]
    k = rw[:, RW_SPLITS[0]:RW_SPLITS[1]]
    v = rw[:, RW_SPLITS[1]:RW_SPLITS[2]]
    wd = rw[:, RW_SPLITS[2]:RW_SPLITS[3]]
    ad = rw[:, RW_SPLITS[3]:RW_SPLITS[4]]
    gd = rw[:, RW_SPLITS[4]:]
    ones_bd = ones_ref[...]
    kk = k * kk_ref[...]
    kk = kk / jnp.maximum(jnp.sqrt(_head_sums(kk * kk, ones_bd)), 1e-12)
    tanh_wd = jnp.tanh(wd).astype(bf16)
    ad16 = ad.astype(bf16)
    rk = r * rk_ref[...]
    bonus = jnp.zeros_like(r)
    for d, (w_o, kd_o, b_o) in enumerate(((w_out0, kd_out0, b_out0), (w_out1, kd_out1, b_out1))):
        w_raw = w0_ref[d:d + 1, :] + jnp.dot(tanh_wd, w2_ref[d], preferred_element_type=f32)
        w_o[...] = jnp.exp(-jnp.exp(-jax.nn.softplus(-w_raw) - 0.5))
        lr = jax.nn.sigmoid(a0_ref[d:d + 1, :] + jnp.dot(ad16, a2_ref[d], preferred_element_type=f32))
        kd = k * (1.0 + (lr - 1.0) * ka_ref[...])
        kd_o[...] = kd
        b_o[...] = kk * lr
        bonus = bonus + _head_sums(rk * kd, ones_bd)
    r_out[...] = r
    kk_out[...] = kk
    v_out[...] = v
    gate_out[...] = jnp.dot(jax.nn.sigmoid(gd).astype(bf16), g2_ref[...], preferred_element_type=f32)
    bonus_out[...] = bonus * v


def _head_ones():
    h = np.arange(RW_DIM) // RW_HEAD
    return jnp.asarray(h[:, None] == h[None, :], jnp.bfloat16)


def rwkv_prep(proj, seq_len_of_tile, mu, w0, w2, a0, a2, g2, k_k, k_a, r_k, tm):
    T, C = proj.shape
    n_tiles = T // tm
    hb = tm // SUBLANES
    first = np.zeros((n_tiles, 1, LANES), np.float32)
    last = np.zeros((n_tiles, 1, LANES), np.float32)
    start = 0
    for i, L in enumerate(seq_len_of_tile):
        first[i] = float(start % L == 0)
        last[i] = float((start + tm) % L == 0)
        start = (start + tm) % L
    row2 = lambda a: a.reshape(1, -1)
    whole = lambda a: pl.BlockSpec(a.shape, lambda i: (0,) * a.ndim)
    flag = pl.BlockSpec((1, 1, LANES), lambda i: (i, 0, 0))
    out = pl.BlockSpec((tm, RW_DIM), lambda i: (i, 0))
    consts = (row2(mu), w0, w2.astype(jnp.bfloat16), a0, a2.astype(jnp.bfloat16), g2.astype(jnp.bfloat16),
              row2(k_k), row2(k_a), row2(r_k), _head_ones())
    return pl.pallas_call(
        _rwkv_prep_kernel, grid=(n_tiles,),
        in_specs=[pl.BlockSpec((tm, C), lambda i: (i, 0)),
                  pl.BlockSpec((SUBLANES, C), lambda i: (jnp.maximum(i * hb - 1, 0), 0)),
                  pl.BlockSpec((SUBLANES, C), lambda i: (jnp.minimum((i + 1) * hb, T // SUBLANES - 1), 0)),
                  flag, flag] + [whole(a) for a in consts],
        out_specs=[out] * 11,
        out_shape=[jax.ShapeDtypeStruct((T, RW_DIM), jnp.float32)] * 11,
        compiler_params=pltpu.CompilerParams(dimension_semantics=("arbitrary",), vmem_limit_bytes=VMEM_LIMIT),
        name="rwkv_prep",
    )(proj, proj, proj, jnp.asarray(first), jnp.asarray(last), *consts)


def _rwkv_out_kernel(y0_ref, y1_ref, bonus_ref, gate_ref, mla_ref, lng_ref, lnb_ref, ones_ref, w_ref, x_ref, mod_ref,
                     o_ref):
    bf16 = jnp.bfloat16
    f32 = jnp.float32
    ones_bd = ones_ref[...]
    y = y0_ref[...] + y1_ref[...]
    dlt = y - _head_sums(y, ones_bd) * (1.0 / RW_HEAD)
    var = _head_sums(dlt * dlt, ones_bd) * (1.0 / RW_HEAD)
    yn = dlt * lax.rsqrt(var + LNX_EPS) * lng_ref[...] + lnb_ref[...]
    rw_out = (yn + bonus_ref[...]) * gate_ref[...]
    upd = (jnp.dot(rw_out.astype(bf16), w_ref[:RW_DIM], preferred_element_type=f32)
           + jnp.dot(mla_ref[...].astype(bf16), w_ref[RW_DIM:], preferred_element_type=f32))
    o_ref[...] = x_ref[...] + mod_ref[0] * upd


def rwkv_out_residual(y0, y1, bonus, gate, mla_out, lnx_g, lnx_b, w_out, x, gate_t, tm):
    T, D = x.shape
    row = lambda n: pl.BlockSpec((tm, n), lambda i: (i, 0))
    whole = lambda a: pl.BlockSpec(a.shape, lambda i: (0,) * a.ndim)
    consts = (lnx_g.reshape(1, -1), lnx_b.reshape(1, -1), _head_ones(), w_out.astype(jnp.bfloat16))
    return pl.pallas_call(
        _rwkv_out_kernel, grid=(T // tm,),
        in_specs=[row(RW_DIM)] * 4 + [row(MLA_DIM)] + [whole(a) for a in consts]
                 + [row(D), pl.BlockSpec((1, 1, D), lambda i: (i, 0, 0))],
        out_specs=row(D),
        out_shape=jax.ShapeDtypeStruct((T, D), jnp.float32),
        compiler_params=pltpu.CompilerParams(dimension_semantics=("arbitrary",), vmem_limit_bytes=VMEM_LIMIT),
        name="rwkv_out_residual",
    )(y0, y1, bonus, gate, mla_out, *consts, x, gate_t)


def mla_keys(ckv, k_pe, kv_up, kn):
    B, L, _ = ckv.shape
    kv = _mm(ckv, kv_up).reshape(B, L, MLA_HEADS, QK_NOPE + V_HEAD)
    k_rope = jnp.broadcast_to(k_pe[:, :, None, :], (B, L, MLA_HEADS, QK_ROPE))
    k = rmsnorm(jnp.concatenate([kv[..., :QK_NOPE], k_rope], axis=-1), kn)
    return k, kv[..., QK_NOPE:]


def rwkv_scans(prep, B, L, s0s):
    heads = lambda t: t.reshape(B, L, RW_HEADS, RW_HEAD)
    r, kk, v, w_0, w_1, kd_0, kd_1, b_0, b_1 = [heads(t) for t in prep]
    if s0s is None:
        z = jnp.zeros((B, RW_HEADS, RW_HEAD, RW_HEAD), jnp.float32)
        s0s = (z, z)
    ys, finals = rwkv_scan_both(r, kk, v, [w_0, w_1], [kd_0, kd_1], [b_0, b_1], s0s)
    return [y.reshape(B * L, RW_DIM) for y in ys], finals


def mla_mixer(mla, ctx, p):
    q_norm, q_up, kv_norm, kv_up, qn, kn = p
    B, L, _ = mla.shape
    q_c, kv_c, k_pe = jnp.split(mla, (Q_LORA, Q_LORA + KV_LORA), axis=-1)
    q = rmsnorm(_mm(rmsnorm(q_c, q_norm), q_up).reshape(B, L, MLA_HEADS, QK_HEAD), qn)
    ckv = rmsnorm(kv_c, kv_norm)
    k_own, v_own = mla_keys(ckv, k_pe, kv_up, kn)
    if ctx is None:
        keys, vals = k_own, v_own
    else:
        cos, sin = ctx[2]
        q = rope_tail(q, cos, sin)
        k_own = rope_tail(k_own, cos, sin)
        k_ctx, v_ctx = mla_keys(ctx[0], ctx[1], kv_up, kn)
        keys = jnp.concatenate([k_ctx, k_own], axis=1)
        vals = jnp.concatenate([v_ctx, v_own], axis=1)
    return attend(q, keys, vals), ckv, k_pe


def hyena_filters(L, w1, b1, w2, b2, w3, freq):
    f32 = jnp.float32
    u = jnp.arange(2 * L, dtype=jnp.int32)[:, None]
    t = jnp.where(u < L, u, 2 * L - u).astype(f32)
    t_unit = t / (L - 1)
    bands = jnp.linspace(1e-4, HY_BANDS - 1, HY_BANDS, dtype=f32)
    ang = 2.0 * math.pi * t * bands / L
    zpos = jnp.concatenate([t_unit, jnp.cos(ang), -jnp.sin(ang)], axis=-1)
    fr = freq.astype(f32)
    hid = jnp.sin(fr * (_mm(zpos, w1.astype(f32), split=True) + b1.astype(f32)))
    hid = jnp.sin(fr * (_mm(hid, w2.astype(f32), split=True) + b2.astype(f32)))
    filt = _mm(hid, w3.astype(f32), split=True)
    deltas = jnp.linspace(math.log(HY_TARGET) / HY_FAST, math.log(HY_TARGET) / HY_SLOW, D_MODEL, dtype=f32)
    window = jnp.exp(-t_unit * jnp.abs(deltas))
    circ = jnp.where(u < L, filt[:, :D_MODEL], filt[:, D_MODEL:]) * window
    circ = jnp.where(u == L, 0.0, circ)
    return circ / jnp.sum(jnp.abs(circ), axis=0, keepdims=True)


def hyena_mixer(proj, p):
    conv_w, conv_b, f_w1, f_b1, f_w2, f_b2, f_w3, f_freq, bias = p
    B, L, _ = proj.shape
    u = centred_dwconv3(proj, conv_w, conv_b)
    x0, x1, v = jnp.split(u, 3, axis=-1)
    zin = (x1 * v).astype(jnp.float32)
    conv = hyena_longconv(zin, hyena_filters(L, f_w1, f_b1, f_w2, f_b2, f_w3, f_freq))
    y = conv + zin * bias.astype(jnp.float32)
    return x0 * y


PEER_N = PEER_KEYS * PEER_KEYS
GATE_LANES = 2 * LANES
PEER_ROUTE_TOKENS = 256
PEER_EXPERT_TOKENS = 512
ROW_TILE = 512
SEQ_TILE = 256


def _top_vals(s, n, with_rank):
    vals = []
    rank = jnp.full(s.shape, float(n), jnp.float32) if with_rank else None
    for a in range(n):
        m = jnp.max(s, axis=0, keepdims=True)
        vals.append(m)
        hit = s == m
        if with_rank:
            rank = jnp.where(hit, float(a), rank)
        s = jnp.where(hit, -jnp.inf, s)
    return vals, rank


def _peer_route_kernel(x_ref, g_ref, sh_ref, sc_ref, wq_hi_ref, wq_lo_ref, k_hi_ref, k_lo_ref,
                       n1_ref, e1_ref, r2_ref, e2_ref, h_ref, s_scr):
    tb = x_ref.shape[0]
    half = PEER_DKEY // 2
    n_tiles = tb // LANES
    h_hi, h_lo = _split_bf16(_rms_mod(x_ref[...], g_ref[...], sh_ref[0], sc_ref[0]))
    h_ref[...] = h_hi
    q_hi, q_lo = _split_bf16(_dot3(h_hi, h_lo, wq_hi_ref[...], wq_lo_ref[...]))
    for hh in range(PEER_HEADS):
        for p in range(2):
            cols = slice((2 * hh + p) * half, (2 * hh + p + 1) * half)
            k_hi = k_hi_ref[hh, p]
            s_scr[hh, p] = (lax.dot_general(k_hi, q_hi[:, cols], _NT, preferred_element_type=jnp.float32)
                            + lax.dot_general(k_lo_ref[hh, p], q_hi[:, cols], _NT, preferred_element_type=jnp.float32)
                            + lax.dot_general(k_hi, q_lo[:, cols], _NT, preferred_element_type=jnp.float32))

    K = PEER_TOPK
    G = SUBLANES

    def tile(it, carry):
        hh = it // n_tiles
        ln = pl.ds(pl.multiple_of((it % n_tiles) * LANES, LANES), LANES)
        s1 = s_scr[hh, 0, :, ln]
        s2 = s_scr[hh, 1, :, ln]
        v1, _ = _top_vals(s1, K, False)
        v2l, r2 = _top_vals(s2, K, True)
        v2 = jnp.concatenate(v2l, axis=0)
        cand = [v1[0] + v2[:G], v1[0] + v2[G:]]
        cand += [v1[a] + v2[:G] for a in range(1, G)]
        cand += [jnp.concatenate(v1[G:], axis=0) + v2[0:1]]
        c = cand
        for k in range(K):
            m = c[0]
            for ci in c[1:]:
                m = jnp.maximum(m, ci)
            m = jnp.max(m, axis=0, keepdims=True)
            if k + 1 < K:
                c = [jnp.where(ci == m, -jnp.inf, ci) for ci in c]
        tau = m
        top = v1[0] + v2[0:1]
        keep = [ci >= tau for ci in cand]
        z = jnp.zeros_like(tau)
        for ci, ki in zip(cand, keep):
            z = z + jnp.sum(jnp.where(ki, jnp.exp(ci - top), 0.0), axis=0, keepdims=True)
        cnt = [jnp.sum(jnp.where(ki, 1.0, 0.0), axis=0, keepdims=True) for ki in keep[:G + 1]]
        tail = jnp.where(keep[G + 1], 1.0, 0.0)
        n_a = [cnt[0] + cnt[1]] + cnt[2:] + [tail[a:a + 1] for a in range(G)]
        n1 = jnp.zeros_like(s1)
        for a in range(K):
            n1 = jnp.where(s1 == v1[a], n_a[a], n1)
        n1_ref[hh, :, ln] = n1
        e1_ref[hh, :, ln] = jnp.exp(s1 - v1[0]) / z
        r2_ref[hh, :, ln] = r2.astype(jnp.bfloat16)
        e2_ref[hh, :, ln] = jnp.exp(s2 - v2[0:1]).astype(jnp.bfloat16)
        return carry

    lax.fori_loop(0, PEER_HEADS * n_tiles, tile, 0, unroll=2)


def peer_route(x, g, shift_t, scale_t, wq_hi, wq_lo, k_hi, k_lo, tb=PEER_ROUTE_TOKENS):
    T = x.shape[0]
    nh = PEER_HEADS
    blk = pl.BlockSpec((nh, PEER_KEYS, tb), lambda t: (0, 0, t))
    sd = lambda dt: jax.ShapeDtypeStruct((nh, PEER_KEYS, T), dt)
    whole = lambda a: pl.BlockSpec(a.shape, lambda t: (0,) * a.ndim)
    mod = pl.BlockSpec((1, 1, D_MODEL), lambda t: (t, 0, 0))
    return pl.pallas_call(
        _peer_route_kernel,
        grid=(T // tb,),
        in_specs=[pl.BlockSpec((tb, D_MODEL), lambda t: (t, 0)), whole(g), mod, mod,
                  whole(wq_hi), whole(wq_lo), whole(k_hi), whole(k_lo)],
        out_specs=[blk, blk, blk, blk, pl.BlockSpec((tb, D_MODEL), lambda t: (t, 0))],
        out_shape=[sd(jnp.float32), sd(jnp.float32), sd(jnp.bfloat16), sd(jnp.bfloat16),
                   jax.ShapeDtypeStruct((T, D_MODEL), jnp.bfloat16)],
        scratch_shapes=[pltpu.VMEM((nh, 2, PEER_KEYS, tb), jnp.float32)],
        compiler_params=pltpu.CompilerParams(dimension_semantics=("arbitrary",),
                                             vmem_limit_bytes=VMEM_LIMIT),
        name="peer_route",
    )(x, g, shift_t, scale_t, wq_hi, wq_lo, k_hi, k_lo)


def _gelu_tanh(x):
    hx = 0.5 * x
    return hx * jnp.tanh(x * (x * x * (0.7978845608028654 * 0.044715) + 0.7978845608028654)) + hx


def _peer_expert_kernel(h_ref, u_ref, vt_even_ref, vt_prev_ref, vt_last_ref, n1_ref, e1_ref, r2_ref, e2_ref,
                        x_ref, gate_ref, o_ref, acc_ref, a0_scr, a1_scr, w0_scr, w1_scr):
    c = pl.program_id(1)
    ec, tb = a0_scr.shape
    n_i = ec // PEER_KEYS
    bf16 = jnp.bfloat16
    f32 = jnp.float32

    @pl.when(c == 0)
    def _():
        acc_ref[...] = jnp.zeros_like(acc_ref)
        w1_scr[...] = jnp.zeros_like(w1_scr)

    def gate_times_act(a_scr, w_scr, chunk):
        igrp = pl.ds(pl.multiple_of(chunk * n_i, SUBLANES), SUBLANES)
        for lt in range(tb // GATE_LANES):
            ln = slice(lt * GATE_LANES, (lt + 1) * GATE_LANES)
            for ii in range(n_i):
                rows = slice(ii * PEER_KEYS, (ii + 1) * PEER_KEYS)
                g = jnp.zeros((PEER_KEYS, GATE_LANES), bf16)
                for hh in range(PEER_HEADS):
                    n1 = jnp.broadcast_to(n1_ref[hh, igrp, ln][ii:ii + 1], (PEER_KEYS, GATE_LANES)).astype(bf16)
                    e1 = jnp.broadcast_to(e1_ref[hh, igrp, ln][ii:ii + 1], (PEER_KEYS, GATE_LANES)).astype(bf16)
                    g = g + jnp.where(r2_ref[hh, :, ln] < n1, e2_ref[hh, :, ln] * e1, jnp.zeros((), bf16))
                w_scr[rows, ln] = g * _gelu_tanh(a_scr[rows, ln])

    h = h_ref[...]
    a0_scr[...] = lax.dot_general(u_ref[:ec], h, _NT, preferred_element_type=f32).astype(bf16)
    acc_ref[...] += jnp.dot(vt_prev_ref[...], w1_scr[...], preferred_element_type=f32)
    gate_times_act(a0_scr, w0_scr, 2 * c)
    a1_scr[...] = lax.dot_general(u_ref[ec:], h, _NT, preferred_element_type=f32).astype(bf16)
    acc_ref[...] += jnp.dot(vt_even_ref[...], w0_scr[...], preferred_element_type=f32)
    gate_times_act(a1_scr, w1_scr, 2 * c + 1)

    @pl.when(c == pl.num_programs(1) - 1)
    def _():
        y = acc_ref[...] + jnp.dot(vt_last_ref[...], w1_scr[...], preferred_element_type=f32)
        o_ref[...] = x_ref[...] + gate_ref[0] * y.T


def peer_experts(h_bf16, u_bf16, vt_bf16, n1, e1, r2, e2, x, gate_t, tb=PEER_EXPERT_TOKENS, ec=SUBLANES * PEER_KEYS):
    T = h_bf16.shape[0]
    n_steps = PEER_N // (2 * ec)
    rblk = pl.BlockSpec((PEER_HEADS, PEER_KEYS, tb), lambda t, c: (0, 0, t))
    vt_blk = lambda chunk_of: pl.BlockSpec((D_MODEL, ec), lambda t, c: (0, chunk_of(c)))
    return pl.pallas_call(
        _peer_expert_kernel,
        grid=(T // tb, n_steps),
        in_specs=[pl.BlockSpec((tb, D_MODEL), lambda t, c: (t, 0)),
                  pl.BlockSpec((2 * ec, D_MODEL), lambda t, c: (c, 0)),
                  vt_blk(lambda c: 2 * c),
                  vt_blk(lambda c: jnp.maximum(2 * c - 1, 0)),
                  vt_blk(lambda c: 2 * n_steps - 1),
                  rblk, rblk, rblk, rblk,
                  pl.BlockSpec((tb, D_MODEL), lambda t, c: (t, 0)),
                  pl.BlockSpec((1, 1, D_MODEL), lambda t, c: (t, 0, 0))],
        out_specs=pl.BlockSpec((tb, D_MODEL), lambda t, c: (t, 0)),
        out_shape=jax.ShapeDtypeStruct((T, D_MODEL), jnp.float32),
        scratch_shapes=[pltpu.VMEM((D_MODEL, tb), jnp.float32)] + [pltpu.VMEM((ec, tb), jnp.bfloat16)] * 4,
        compiler_params=pltpu.CompilerParams(dimension_semantics=("arbitrary", "arbitrary"),
                                             vmem_limit_bytes=VMEM_LIMIT),
        name="peer_experts",
    )(h_bf16, u_bf16, vt_bf16, vt_bf16, vt_bf16, n1, e1, r2, e2, x, gate_t)


def peer_block(x, g, mods_route, mods_expert, w_q, sub_keys, u_tab, v_tab):
    wq_hi, wq_lo = _split_bf16(w_q)
    k_hi, k_lo = _split_bf16(sub_keys)
    n1, e1, r2, e2, h_bf16 = peer_route(x, g[None], mods_route[0], mods_route[1], wq_hi, wq_lo, k_hi, k_lo)
    return peer_experts(h_bf16, u_tab.astype(jnp.bfloat16), v_tab.T.astype(jnp.bfloat16),
                        n1, e1, r2, e2, x, mods_expert)


def kernel(x_prompt, x_sample, state_rwkv_fwd, state_rwkv_bwd, cache_mla_ckv, cache_mla_kpe, c, c_ctx,
           norm_g, w_mod, b_mod, ab_w_in, rw_mu, rw_w0, rw_w2, rw_a0, rw_a2, rw_g2, rw_k_k, rw_k_a, rw_r_k,
           rw_lnx_g, rw_lnx_b, mla_q_norm, mla_q_up, mla_kv_norm, mla_kv_up, mla_qn, mla_kn, ab_w_out,
           hy_w_in, hy_b_in, hy_conv_w, hy_conv_b, hy_f_w1, hy_f_b1, hy_f_w2, hy_f_b2, hy_f_w3, hy_f_freq,
           hy_bias, hy_w_out, peer_w_q, peer_keys, peer_u, peer_v):
    rope = axial_rope(x_sample.shape[1])
    D = D_MODEL
    n_p = x_prompt.shape[0] * x_prompt.shape[1]
    Bs, Ls = x_sample.shape[:2]
    x = jnp.concatenate([x_prompt.reshape(n_p, D), x_sample.reshape(Bs * Ls, D)], axis=0)
    split_rows = lambda a: (a[:n_p].reshape(x_prompt.shape[:2] + a.shape[1:]), a[n_p:].reshape((Bs, Ls) + a.shape[1:]))
    merge_rows = lambda ap, as_: jnp.concatenate([ap.reshape(n_p, -1), as_.reshape(Bs * Ls, -1)], axis=0)
    st_f, st_b, st_ckv, st_kpe = [], [], [], []
    for li in range(DEPTH):
        j = li // 2
        mp = jnp.split(_mm(jax.nn.silu(c_ctx)[None], w_mod[li])[0] + b_mod[li], 6, axis=-1)
        ms = jnp.split(_mm(jax.nn.silu(c), w_mod[li]) + b_mod[li], 6, axis=-1)
        mod = lambda i, tm: tile_mod(mp[i], ms[i], n_p, Ls, tm)
        if li % 2 == 0:
            proj = norm_mod_matmul(x, norm_g[li, 0], mod(0, ROW_TILE), mod(1, ROW_TILE), ab_w_in[j],
                                   jnp.zeros((ab_w_in.shape[-1],), jnp.float32), ROW_TILE)
            tile_lens = [x_prompt.shape[1]] * (n_p // SEQ_TILE) + [Ls] * (Bs * Ls // SEQ_TILE)
            prep = rwkv_prep(proj, tile_lens, rw_mu[j], rw_w0[j], rw_w2[j], rw_a0[j], rw_a2[j], rw_g2[j],
                             rw_k_k[j], rw_k_a[j], rw_r_k[j], SEQ_TILE)
            ys_p, (sf, sb) = rwkv_scans([t[:n_p] for t in prep[:9]], x_prompt.shape[0], x_prompt.shape[1], None)
            ys_s, _ = rwkv_scans([t[n_p:] for t in prep[:9]], Bs, Ls, (state_rwkv_fwd[:, j], state_rwkv_bwd[:, j]))
            mp_ = (mla_q_norm[j], mla_q_up[j], mla_kv_norm[j], mla_kv_up[j], mla_qn[j], mla_kn[j])
            mla_p, mla_s = split_rows(proj[:, RW_IN:])
            att_p, ckv, kpe = mla_mixer(mla_p, None, mp_)
            att_s, _, _ = mla_mixer(mla_s, (cache_mla_ckv[:, j], cache_mla_kpe[:, j], rope), mp_)
            st_f.append(sf)
            st_b.append(sb)
            st_ckv.append(ckv)
            st_kpe.append(kpe)
            x = rwkv_out_residual(jnp.concatenate([ys_p[0], ys_s[0]], axis=0), jnp.concatenate([ys_p[1], ys_s[1]], axis=0),
                                  prep[10], prep[9], merge_rows(att_p, att_s), rw_lnx_g[j], rw_lnx_b[j], ab_w_out[j],
                                  x, mod(2, ROW_TILE), ROW_TILE)
        else:
            p = (hy_conv_w[j], hy_conv_b[j], hy_f_w1[j], hy_f_b1[j], hy_f_w2[j],
                 hy_f_b2[j], hy_f_w3[j], hy_f_freq[j], hy_bias[j])
            proj = norm_mod_matmul(x, norm_g[li, 0], mod(0, ROW_TILE), mod(1, ROW_TILE), hy_w_in[j], hy_b_in[j],
                                   ROW_TILE)
            proj_p, proj_s = split_rows(proj)
            op = hyena_mixer(proj_p, p)
            os_ = hyena_mixer(proj_s, p)
            x = matmul_residual(merge_rows(op, os_), hy_w_out[j], x, mod(2, ROW_TILE), ROW_TILE)
        x = peer_block(x, norm_g[li, 1], (mod(3, PEER_ROUTE_TOKENS), mod(4, PEER_ROUTE_TOKENS)),
                       mod(5, PEER_EXPERT_TOKENS), peer_w_q[li], peer_keys[li], peer_u[li], peer_v[li])
    xp, xs = split_rows(x)
    new_state_rwkv_fwd = jnp.stack(st_f, axis=1).astype(x_prompt.dtype)
    new_state_rwkv_bwd = jnp.stack(st_b, axis=1).astype(x_prompt.dtype)
    new_cache_mla_ckv = jnp.stack(st_ckv, axis=1)
    new_cache_mla_kpe = jnp.stack(st_kpe, axis=1)
    return (xp, xs, new_state_rwkv_fwd, new_state_rwkv_bwd, new_cache_mla_ckv, new_cache_mla_kpe)
```

```python
import math
from functools import lru_cache, partial

import jax
import jax.numpy as jnp
import numpy as np
from jax import lax
from jax.experimental import pallas as pl
from jax.experimental.pallas import tpu as pltpu

D_MODEL = 1024
DEPTH = 2
GRID_W = 64
EPS = 1e-6
RW_HEADS = 8
RW_HEAD = 64
RW_DIM = RW_HEADS * RW_HEAD
W_LORA = 64
A_LORA = 64
G_LORA = 128
LNX_EPS = 64e-5
RW_IN = 3 * RW_DIM + W_LORA + A_LORA + G_LORA
RW_SPLITS = (RW_DIM, 2 * RW_DIM, 3 * RW_DIM, 3 * RW_DIM + W_LORA, 3 * RW_DIM + W_LORA + A_LORA)
MLA_HEADS = 4
QK_NOPE = 128
QK_ROPE = 64
QK_HEAD = QK_NOPE + QK_ROPE
V_HEAD = 128
Q_LORA = 256
KV_LORA = 128
MLA_DIM = MLA_HEADS * V_HEAD
ROPE_THETA = 10000.0
HY_BANDS = 16
HY_TARGET = 1e-2
HY_FAST = 0.3
HY_SLOW = 1.5
PEER_KEYS = 128
PEER_HEADS = 8
PEER_DKEY = 256
PEER_TOPK = 16

LANES = 128
SUBLANES = 8
VMEM_LIMIT = 56 * 1024 * 1024
_NT = (((1,), (1,)), ((), ()))


def _split_bf16(x):
    hi = x.astype(jnp.bfloat16)
    lo = (x - hi.astype(jnp.float32)).astype(jnp.bfloat16)
    return hi, lo


def _dot3(a_hi, a_lo, b_hi, b_lo):
    f32 = jnp.float32
    return (jnp.dot(a_hi, b_hi, preferred_element_type=f32) + jnp.dot(a_lo, b_hi, preferred_element_type=f32)
            + jnp.dot(a_hi, b_lo, preferred_element_type=f32))


def _mm_kernel(a_ref, b_ref, o_ref, *, split):
    if split:
        a_hi, a_lo = _split_bf16(a_ref[...])
        b_hi, b_lo = _split_bf16(b_ref[...])
        o_ref[...] = _dot3(a_hi, a_lo, b_hi, b_lo)
    else:
        o_ref[...] = jnp.dot(a_ref[...].astype(jnp.bfloat16), b_ref[...].astype(jnp.bfloat16),
                             preferred_element_type=jnp.float32)


def _mm(a, b, tm=512, tn=512, split=False):
    lead = a.shape[:-1]
    K = a.shape[-1]
    N = b.shape[-1]
    a2 = a.reshape(-1, K)
    if K % LANES:
        kp = -K % LANES
        a2 = jnp.pad(a2, ((0, 0), (0, kp)))
        b = jnp.pad(b, ((0, kp), (0, 0)))
        K += kp
    M = a2.shape[0]
    tm = min(tm, M)
    tn = min(tn, N)
    if N % tn:
        tn = N
    assert M % tm == 0 and N % tn == 0
    out = pl.pallas_call(
        partial(_mm_kernel, split=split),
        grid=(M // tm, N // tn),
        in_specs=[pl.BlockSpec((tm, K), lambda i, j: (i, 0)),
                  pl.BlockSpec((K, tn), lambda i, j: (0, j))],
        out_specs=pl.BlockSpec((tm, tn), lambda i, j: (i, j)),
        out_shape=jax.ShapeDtypeStruct((M, N), jnp.float32),
        name="matmul",
    )(a2, b)
    return out.reshape(*lead, N)


def rmsnorm(x, g):
    xf = x.astype(jnp.float32)
    y = xf * lax.rsqrt(jnp.mean(xf * xf, axis=-1, keepdims=True) + EPS)
    return (y * g.astype(jnp.float32)).astype(x.dtype)


def centred_dwconv3(u, w, b):
    prev = jnp.pad(u[:, :-1], ((0, 0), (1, 0), (0, 0)))
    nxt = jnp.pad(u[:, 1:], ((0, 0), (0, 1), (0, 0)))
    return prev * w[0] + u * w[1] + nxt * w[2] + b


def axial_rope(L):
    rows = L // GRID_W
    row = jnp.repeat(jnp.arange(rows, dtype=jnp.float32), GRID_W)
    col = jnp.tile(jnp.arange(GRID_W, dtype=jnp.float32), rows)
    n_freq = QK_ROPE // 4
    inv = ROPE_THETA ** (-jnp.arange(n_freq, dtype=jnp.float32) / n_freq)
    ang = jnp.concatenate([row[:, None] * inv, col[:, None] * inv], axis=-1)
    return jnp.cos(ang), jnp.sin(ang)


def rope_tail(x, cos, sin):
    xn, xr = x[..., :QK_NOPE], x[..., QK_NOPE:]
    x1, x2 = xr[..., 0::2], xr[..., 1::2]
    cs = cos[None, :, None, :].astype(x.dtype)
    sn = sin[None, :, None, :].astype(x.dtype)
    rot = jnp.stack([x1 * cs - x2 * sn, x1 * sn + x2 * cs], axis=-1).reshape(xr.shape)
    return jnp.concatenate([xn, rot], axis=-1)


def _mla_attn_kernel(q_ref, k_ref, v_ref, o_ref):
    scale = QK_HEAD ** -0.5
    for h in range(MLA_HEADS):
        q = q_ref[0, :, h * QK_HEAD:(h + 1) * QK_HEAD].astype(jnp.bfloat16)
        k = k_ref[0, :, h * QK_HEAD:(h + 1) * QK_HEAD].astype(jnp.bfloat16)
        s = lax.dot_general(q, k, _NT, preferred_element_type=jnp.float32) * scale
        m = jnp.max(s, axis=-1, keepdims=True)
        p = jnp.exp(s - m)
        l = jnp.sum(p, axis=-1, keepdims=True)
        v = v_ref[0, :, h * V_HEAD:(h + 1) * V_HEAD].astype(jnp.bfloat16)
        o = jnp.dot((p / l).astype(jnp.bfloat16), v, preferred_element_type=jnp.float32)
        o_ref[0, :, h * V_HEAD:(h + 1) * V_HEAD] = o


def attend(q, k, v, tq=256):
    B, Lq = q.shape[:2]
    Lk = k.shape[1]
    tq = min(tq, Lq)
    return pl.pallas_call(
        _mla_attn_kernel, grid=(B, Lq // tq),
        in_specs=[pl.BlockSpec((1, tq, MLA_HEADS * QK_HEAD), lambda b, i: (b, i, 0)),
                  pl.BlockSpec((1, Lk, MLA_HEADS * QK_HEAD), lambda b, i: (b, 0, 0)),
                  pl.BlockSpec((1, Lk, MLA_HEADS * V_HEAD), lambda b, i: (b, 0, 0))],
        out_specs=pl.BlockSpec((1, tq, MLA_HEADS * V_HEAD), lambda b, i: (b, i, 0)),
        out_shape=jax.ShapeDtypeStruct((B, Lq, MLA_HEADS * V_HEAD), jnp.float32),
        compiler_params=pltpu.CompilerParams(dimension_semantics=("arbitrary", "arbitrary"),
                                             vmem_limit_bytes=VMEM_LIMIT),
        name="mla_attend",
    )(q.reshape(B, Lq, -1), k.reshape(B, Lk, -1), v.reshape(B, Lk, -1))


def _dft_split_input(x_refs, scr_refs):
    @pl.when(pl.program_id(2) == 0)
    def _():
        for x_ref, (hi_scr, lo_scr) in zip(x_refs, scr_refs):
            hi_scr[...], lo_scr[...] = _split_bf16(x_ref[0])


def _dft_pair_kernel(c_hi, c_lo, s_hi, s_lo, x_ref, oc_ref, os_ref, x_hi, x_lo):
    _dft_split_input([x_ref], [(x_hi, x_lo)])
    oc_ref[0] = _dot3(c_hi[...], c_lo[...], x_hi[...], x_lo[...])
    os_ref[0] = _dot3(s_hi[...], s_lo[...], x_hi[...], x_lo[...])


def _dft_spectral_kernel(c_hi, c_lo, s_hi, s_lo, x_ref, hr_ref, hi_ref, yr_ref, yi_ref, x_hi, x_lo):
    _dft_split_input([x_ref], [(x_hi, x_lo)])
    rb, L = c_hi.shape
    zc = _dot3(c_hi[...], c_lo[...], x_hi[...], x_lo[...])
    zs = _dot3(s_hi[...], s_lo[...], x_hi[...], x_lo[...])
    hr = hr_ref[...]
    hi = hi_ref[...]
    f = pl.program_id(2) * rb + lax.broadcasted_iota(jnp.int32, zc.shape, 0)
    yr_ref[0] = jnp.where(f == 0, zc * hr * (0.5 / L), (zc * hr - zs * hi) * (1.0 / L))
    yi_ref[0] = jnp.where(f == 0, zs * hi * (0.5 / L), (zc * hi + zs * hr) * (1.0 / L))


def _dft_sum_kernel(c_hi, c_lo, s_hi, s_lo, x_ref, y_ref, o_ref, x_hi, x_lo, y_hi, y_lo):
    _dft_split_input([x_ref, y_ref], [(x_hi, x_lo), (y_hi, y_lo)])
    o_ref[0] = (_dot3(c_hi[...], c_lo[...], x_hi[...], x_lo[...])
                + _dot3(s_hi[...], s_lo[...], y_hi[...], y_lo[...]))


@lru_cache(maxsize=None)
def dft_tables(L):
    f = np.arange(L, dtype=np.int64)
    ang = ((f[:, None] * f[None, :]) % (2 * L)).astype(np.float64) * (math.pi / L)
    c = np.cos(ang).astype(np.float32)
    s = (-np.sin(ang)).astype(np.float32)
    alt = np.where(f % 2 == 0, 1.0, -1.0).astype(np.float32)
    s_ana = np.where(f[:, None] == 0, alt[None, :], s)
    s_syn = np.where(f[None, :] == 0, alt[:, None], s)

    def split(x):
        hi = x.astype(jnp.bfloat16)
        return hi, (x - hi.astype(np.float32)).astype(jnp.bfloat16)

    cc = split(c)
    return cc + split(s_ana), cc + split(s_syn)


def _dft_call(body, tabs, xs, n_out, name, row_inputs=(), rb=256, nb=512):
    B, L, N = xs[0].shape
    rb = min(rb, L)
    nb = min(nb, N)
    tab = pl.BlockSpec((rb, L), lambda b, n, i: (i, 0))
    xin = pl.BlockSpec((1, L, nb), lambda b, n, i: (b, 0, n))
    rin = pl.BlockSpec((rb, nb), lambda b, n, i: (i, n))
    out = pl.BlockSpec((1, rb, nb), lambda b, n, i: (b, i, n))
    sd = jax.ShapeDtypeStruct((B, L, N), jnp.float32)
    return pl.pallas_call(
        body, grid=(B, N // nb, L // rb),
        in_specs=[tab] * 4 + [xin] * len(xs) + [rin] * len(row_inputs),
        out_specs=[out] * n_out if n_out > 1 else out,
        out_shape=[sd] * n_out if n_out > 1 else sd,
        scratch_shapes=[pltpu.VMEM((L, nb), jnp.bfloat16)] * (2 * len(xs)),
        compiler_params=pltpu.CompilerParams(dimension_semantics=("arbitrary",) * 3,
                                             vmem_limit_bytes=VMEM_LIMIT),
        name=name,
    )(*tabs, *xs, *row_inputs)


def hyena_longconv(zin, circ):
    B, L, D = zin.shape
    tabs, tabs_syn = dft_tables(L)
    alt = jnp.where(jnp.arange(L) % 2 == 0, 1.0, -1.0).astype(jnp.float32)[:, None]
    fc, fs = _dft_call(_dft_pair_kernel, tabs, [jnp.concatenate([circ[:L], circ[L:]], axis=-1)[None]], 2, "dft_filter")
    hr = fc[0, :, :D] + alt * fc[0, :, D:]
    hi = fs[0, :, :D] + alt * fs[0, :, D:]
    yr, yi = _dft_call(_dft_spectral_kernel, tabs, [zin], 2, "dft_analysis", row_inputs=(hr, hi))
    return _dft_call(_dft_sum_kernel, tabs_syn, [yr, yi], 1, "dft_synthesis")


SCAN_ACCS = 4


def _rwkv_scan_kernel(r_ref, kk_ref, v_ref, w_ref, kd_ref, b_ref, s0_ref, y_ref, sf_ref, s_scr):
    d = pl.program_id(0)
    c = pl.program_id(1)
    tc = r_ref.shape[0]
    nv = v_ref.shape[1]

    @pl.when(c == 0)
    def _():
        s_scr[...] = s0_ref[0]

    def row(ref, tt, k):
        return jnp.broadcast_to(ref[tt, pl.ds(k, 1), :], (nv, LANES))

    def row_d(ref, tt, k):
        return jnp.broadcast_to(ref[0, tt, pl.ds(k, 1), :], (nv, LANES))

    def step(t, carry):
        tt = t + d * (tc - 1 - 2 * t)
        parts = [None] * SCAN_ACCS
        for k in range(RW_HEAD):
            term = s_scr[k] * row(kk_ref, tt, k)
            parts[k % SCAN_ACCS] = term if parts[k % SCAN_ACCS] is None else parts[k % SCAN_ACCS] + term
        sa = -((parts[0] + parts[1]) + (parts[2] + parts[3]))
        vv = v_ref[tt]
        parts = [None] * SCAN_ACCS
        for k in range(RW_HEAD):
            s_new = s_scr[k] * row_d(w_ref, tt, k) + sa * row_d(b_ref, tt, k) + vv * row_d(kd_ref, tt, k)
            s_scr[k] = s_new
            term = s_new * row(r_ref, tt, k)
            parts[k % SCAN_ACCS] = term if parts[k % SCAN_ACCS] is None else parts[k % SCAN_ACCS] + term
        y_ref[0, tt] = (parts[0] + parts[1]) + (parts[2] + parts[3])
        return carry

    lax.fori_loop(0, tc, step, 0)

    @pl.when(c == pl.num_programs(1) - 1)
    def _():
        sf_ref[0] = s_scr[...]


def rwkv_scan(r, kk, v, w2, kd2, b2, s0, tc=32):
    L = r.shape[0]
    nv = v.shape[1]
    nc = L // tc
    tmap = lambda d, c: c + d * (nc - 1 - 2 * c)
    shared = lambda rows: pl.BlockSpec((tc, rows, LANES), lambda d, c: (tmap(d, c), 0, 0))
    per_dir = pl.BlockSpec((1, tc, RW_HEAD, LANES), lambda d, c: (d, tmap(d, c), 0, 0))
    state = pl.BlockSpec((1, RW_HEAD, nv, LANES), lambda d, c: (d, 0, 0, 0))
    return pl.pallas_call(
        _rwkv_scan_kernel,
        grid=(2, nc),
        in_specs=[shared(RW_HEAD), shared(RW_HEAD), shared(nv), per_dir, per_dir, per_dir, state],
        out_specs=[pl.BlockSpec((1, tc, nv, LANES), lambda d, c: (d, tmap(d, c), 0, 0)), state],
        out_shape=[jax.ShapeDtypeStruct((2, L, nv, LANES), jnp.float32),
                   jax.ShapeDtypeStruct((2, RW_HEAD, nv, LANES), jnp.float32)],
        scratch_shapes=[pltpu.VMEM((RW_HEAD, nv, LANES), jnp.float32)],
        compiler_params=pltpu.CompilerParams(dimension_semantics=("arbitrary", "arbitrary"),
                                             vmem_limit_bytes=VMEM_LIMIT),
        name="rwkv_scan",
    )(r, kk, v, w2, kd2, b2, s0)


def to_lanes(x, vsplit):
    B, L, H, N = x.shape
    return jnp.tile(x.transpose(1, 3, 0, 2).reshape(L, N, B * H), (1, 1, vsplit))


def v_to_lanes(x, vsplit):
    B, L, H, N = x.shape
    nv = N // vsplit
    return x.reshape(B, L, H, vsplit, nv).transpose(1, 4, 3, 0, 2).reshape(L, nv, vsplit * B * H)


def v_from_lanes(y, B, H, vsplit):
    L, nv, _ = y.shape
    return y.reshape(L, nv, vsplit, B, H).transpose(3, 0, 4, 2, 1).reshape(B, L, H, vsplit * nv)


def state_to_lanes(s, vsplit):
    B, H, N, K = s.shape
    nv = N // vsplit
    return s.reshape(B, H, vsplit, nv, K).transpose(4, 3, 2, 0, 1).reshape(K, nv, vsplit * B * H)


def state_from_lanes(s, B, H, vsplit):
    K, nv, _ = s.shape
    return s.reshape(K, nv, vsplit, B, H).transpose(3, 4, 2, 1, 0).reshape(B, H, vsplit * nv, K)


def rwkv_scan_both(rh, kk, vh, decay2, kd2, b2, s0_2, tc=32):
    B, L, H, N = rh.shape
    vsplit = LANES // (B * H)
    assert B * H * vsplit == LANES
    stack = lambda xs: jnp.stack([to_lanes(x, vsplit) for x in xs])
    y2, sf2 = rwkv_scan(to_lanes(rh, vsplit), to_lanes(kk, vsplit), v_to_lanes(vh, vsplit),
                        stack(decay2), stack(kd2), stack(b2),
                        jnp.stack([state_to_lanes(s, vsplit) for s in s0_2]), tc=tc)
    ys = [v_from_lanes(y2[d], B, H, vsplit) for d in range(2)]
    sfs = [state_from_lanes(sf2[d], B, H, vsplit) for d in range(2)]
    return ys, sfs


def _rms_mod(x, g, shift, scale):
    y = x * lax.rsqrt(jnp.mean(x * x, axis=-1, keepdims=True) + EPS)
    return (y * g) * (1.0 + scale) + shift


def tile_mod(m, n_rows, tm):
    return jnp.repeat(m, n_rows // m.shape[0] // tm, axis=0)[:, None, :]


def _norm_mod_matmul_kernel(x_ref, g_ref, sh_ref, sc_ref, w_ref, b_ref, o_ref):
    h = _rms_mod(x_ref[...], g_ref[...], sh_ref[0], sc_ref[0])
    o_ref[...] = jnp.dot(h.astype(jnp.bfloat16), w_ref[...], preferred_element_type=jnp.float32) + b_ref[...]


def norm_mod_matmul(x, g, shift_t, scale_t, w, b, tm):
    T, D = x.shape
    N = w.shape[1]
    mod = pl.BlockSpec((1, 1, D), lambda i: (i, 0, 0))
    return pl.pallas_call(
        _norm_mod_matmul_kernel, grid=(T // tm,),
        in_specs=[pl.BlockSpec((tm, D), lambda i: (i, 0)), pl.BlockSpec((1, D), lambda i: (0, 0)), mod, mod,
                  pl.BlockSpec((D, N), lambda i: (0, 0)), pl.BlockSpec((1, N), lambda i: (0, 0))],
        out_specs=pl.BlockSpec((tm, N), lambda i: (i, 0)),
        out_shape=jax.ShapeDtypeStruct((T, N), jnp.float32),
        compiler_params=pltpu.CompilerParams(dimension_semantics=("arbitrary",), vmem_limit_bytes=VMEM_LIMIT),
        name="norm_mod_matmul",
    )(x, g[None], shift_t, scale_t, w.astype(jnp.bfloat16), b[None])


def _matmul_residual_kernel(a_ref, w_ref, x_ref, gate_ref, o_ref):
    o_ref[...] = x_ref[...] + gate_ref[0] * jnp.dot(a_ref[...].astype(jnp.bfloat16), w_ref[...],
                                                    preferred_element_type=jnp.float32)


def matmul_residual(a, w, x, gate_t, tm):
    T, K = a.shape
    D = w.shape[1]
    return pl.pallas_call(
        _matmul_residual_kernel, grid=(T // tm,),
        in_specs=[pl.BlockSpec((tm, K), lambda i: (i, 0)), pl.BlockSpec((K, D), lambda i: (0, 0)),
                  pl.BlockSpec((tm, D), lambda i: (i, 0)), pl.BlockSpec((1, 1, D), lambda i: (i, 0, 0))],
        out_specs=pl.BlockSpec((tm, D), lambda i: (i, 0)),
        out_shape=jax.ShapeDtypeStruct((T, D), jnp.float32),
        compiler_params=pltpu.CompilerParams(dimension_semantics=("arbitrary",), vmem_limit_bytes=VMEM_LIMIT),
        name="matmul_residual",
    )(a, w.astype(jnp.bfloat16), x, gate_t)


def _head_sums(x, ones_bd):
    hi, lo = _split_bf16(x)
    return (jnp.dot(hi, ones_bd, preferred_element_type=jnp.float32)
            + jnp.dot(lo, ones_bd, preferred_element_type=jnp.float32))


def _rwkv_prep_kernel(proj_ref, prev_ref, next_ref, first_ref, last_ref, mu_ref, w0_ref, w2_ref, a0_ref, a2_ref,
                      g2_ref, kk_ref, ka_ref, rk_ref, ones_ref,
                      r_out, kk_out, v_out, w_out0, w_out1, kd_out0, kd_out1, b_out0, b_out1, gate_out, bonus_out):
    tm = proj_ref.shape[0]
    bf16 = jnp.bfloat16
    f32 = jnp.float32
    x = proj_ref[:, :RW_IN]
    row = lax.broadcasted_iota(jnp.int32, (tm, 1), 0)
    prev_row = prev_ref[SUBLANES - 1:SUBLANES, :RW_IN] * (1.0 - first_ref[0, :, :1])
    next_row = next_ref[0:1, :RW_IN] * (1.0 - last_ref[0, :, :1])
    prev = jnp.where(row == 0, prev_row, pltpu.roll(x, 1, 0))
    nxt = jnp.where(row == tm - 1, next_row, pltpu.roll(x, tm - 1, 0))
    rw = x + mu_ref[...] * (0.5 * (prev + nxt) - x)
    r = rw[:, RW_SPLITS[0] - RW_DIM:RW_SPLITS[0]]
    k = rw[:, RW_SPLITS[0]:RW_SPLITS[1]]
    v = rw[:, RW_SPLITS[1]:RW_SPLITS[2]]
    wd = rw[:, RW_SPLITS[2]:RW_SPLITS[3]]
    ad = rw[:, RW_SPLITS[3]:RW_SPLITS[4]]
    gd = rw[:, RW_SPLITS[4]:]
    ones_bd = ones_ref[...]
    kk = k * kk_ref[...]
    kk = kk / jnp.maximum(jnp.sqrt(_head_sums(kk * kk, ones_bd)), 1e-12)
    tanh_wd = jnp.tanh(wd).astype(bf16)
    ad16 = ad.astype(bf16)
    rk = r * rk_ref[...]
    bonus = jnp.zeros_like(r)
    for d, (w_o, kd_o, b_o) in enumerate(((w_out0, kd_out0, b_out0), (w_out1, kd_out1, b_out1))):
        w_raw = w0_ref[d:d + 1, :] + jnp.dot(tanh_wd, w2_ref[d], preferred_element_type=f32)
        w_o[...] = jnp.exp(-jnp.exp(-jax.nn.softplus(-w_raw) - 0.5))
        lr = jax.nn.sigmoid(a0_ref[d:d + 1, :] + jnp.dot(ad16, a2_ref[d], preferred_element_type=f32))
        kd = k * (1.0 + (lr - 1.0) * ka_ref[...])
        kd_o[...] = kd
        b_o[...] = kk * lr
        bonus = bonus + _head_sums(rk * kd, ones_bd)
    r_out[...] = r
    kk_out[...] = kk
    v_out[...] = v
    gate_out[...] = jnp.dot(jax.nn.sigmoid(gd).astype(bf16), g2_ref[...], preferred_element_type=f32)
    bonus_out[...] = bonus * v


def _head_ones():
    h = np.arange(RW_DIM) // RW_HEAD
    return jnp.asarray(h[:, None] == h[None, :], jnp.bfloat16)


def rwkv_prep(proj, L, mu, w0, w2, a0, a2, g2, k_k, k_a, r_k, tm):
    T, C = proj.shape
    n_tiles = T // tm
    hb = tm // SUBLANES
    start = np.arange(n_tiles) * tm % L
    first = np.broadcast_to((start == 0).astype(np.float32)[:, None, None], (n_tiles, 1, LANES))
    last = np.broadcast_to((start + tm == L).astype(np.float32)[:, None, None], (n_tiles, 1, LANES))
    row2 = lambda a: a.reshape(1, -1)
    whole = lambda a: pl.BlockSpec(a.shape, lambda i: (0,) * a.ndim)
    flag = pl.BlockSpec((1, 1, LANES), lambda i: (i, 0, 0))
    out = pl.BlockSpec((tm, RW_DIM), lambda i: (i, 0))
    consts = (row2(mu), w0, w2.astype(jnp.bfloat16), a0, a2.astype(jnp.bfloat16), g2.astype(jnp.bfloat16),
              row2(k_k), row2(k_a), row2(r_k), _head_ones())
    return pl.pallas_call(
        _rwkv_prep_kernel, grid=(n_tiles,),
        in_specs=[pl.BlockSpec((tm, C), lambda i: (i, 0)),
                  pl.BlockSpec((SUBLANES, C), lambda i: (jnp.maximum(i * hb - 1, 0), 0)),
                  pl.BlockSpec((SUBLANES, C), lambda i: (jnp.minimum((i + 1) * hb, T // SUBLANES - 1), 0)),
                  flag, flag] + [whole(a) for a in consts],
        out_specs=[out] * 11,
        out_shape=[jax.ShapeDtypeStruct((T, RW_DIM), jnp.float32)] * 11,
        compiler_params=pltpu.CompilerParams(dimension_semantics=("arbitrary",), vmem_limit_bytes=VMEM_LIMIT),
        name="rwkv_prep",
    )(proj, proj, proj, jnp.asarray(first), jnp.asarray(last), *consts)


def _rwkv_out_kernel(y0_ref, y1_ref, bonus_ref, gate_ref, mla_ref, lng_ref, lnb_ref, ones_ref, w_ref, x_ref, mod_ref,
                     o_ref):
    bf16 = jnp.bfloat16
    f32 = jnp.float32
    ones_bd = ones_ref[...]
    y = y0_ref[...] + y1_ref[...]
    dlt = y - _head_sums(y, ones_bd) * (1.0 / RW_HEAD)
    var = _head_sums(dlt * dlt, ones_bd) * (1.0 / RW_HEAD)
    yn = dlt * lax.rsqrt(var + LNX_EPS) * lng_ref[...] + lnb_ref[...]
    rw_out = (yn + bonus_ref[...]) * gate_ref[...]
    upd = (jnp.dot(rw_out.astype(bf16), w_ref[:RW_DIM], preferred_element_type=f32)
           + jnp.dot(mla_ref[...].astype(bf16), w_ref[RW_DIM:], preferred_element_type=f32))
    o_ref[...] = x_ref[...] + mod_ref[0] * upd


def rwkv_out_residual(y0, y1, bonus, gate, mla_out, lnx_g, lnx_b, w_out, x, gate_t, tm):
    T, D = x.shape
    row = lambda n: pl.BlockSpec((tm, n), lambda i: (i, 0))
    whole = lambda a: pl.BlockSpec(a.shape, lambda i: (0,) * a.ndim)
    consts = (lnx_g.reshape(1, -1), lnx_b.reshape(1, -1), _head_ones(), w_out.astype(jnp.bfloat16))
    return pl.pallas_call(
        _rwkv_out_kernel, grid=(T // tm,),
        in_specs=[row(RW_DIM)] * 4 + [row(MLA_DIM)] + [whole(a) for a in consts]
                 + [row(D), pl.BlockSpec((1, 1, D), lambda i: (i, 0, 0))],
        out_specs=row(D),
        out_shape=jax.ShapeDtypeStruct((T, D), jnp.float32),
        compiler_params=pltpu.CompilerParams(dimension_semantics=("arbitrary",), vmem_limit_bytes=VMEM_LIMIT),
        name="rwkv_out_residual",
    )(y0, y1, bonus, gate, mla_out, *consts, x, gate_t)


def mla_keys(ckv, k_pe, kv_up, kn):
    B, L, _ = ckv.shape
    kv = _mm(ckv, kv_up).reshape(B, L, MLA_HEADS, QK_NOPE + V_HEAD)
    k_rope = jnp.broadcast_to(k_pe[:, :, None, :], (B, L, MLA_HEADS, QK_ROPE))
    k = rmsnorm(jnp.concatenate([kv[..., :QK_NOPE], k_rope], axis=-1), kn)
    return k, kv[..., QK_NOPE:]


def rwkv_scans(prep, B, L, s0s):
    heads = lambda t: t.reshape(B, L, RW_HEADS, RW_HEAD)
    r, kk, v, w_0, w_1, kd_0, kd_1, b_0, b_1 = [heads(t) for t in prep]
    if s0s is None:
        z = jnp.zeros((B, RW_HEADS, RW_HEAD, RW_HEAD), jnp.float32)
        s0s = (z, z)
    ys, finals = rwkv_scan_both(r, kk, v, [w_0, w_1], [kd_0, kd_1], [b_0, b_1], s0s)
    return [y.reshape(B * L, RW_DIM) for y in ys], finals


def mla_mixer(mla, ctx, p):
    q_norm, q_up, kv_norm, kv_up, qn, kn = p
    B, L, _ = mla.shape
    q_c, kv_c, k_pe = jnp.split(mla, (Q_LORA, Q_LORA + KV_LORA), axis=-1)
    q = rmsnorm(_mm(rmsnorm(q_c, q_norm), q_up).reshape(B, L, MLA_HEADS, QK_HEAD), qn)
    ckv = rmsnorm(kv_c, kv_norm)
    k_own, v_own = mla_keys(ckv, k_pe, kv_up, kn)
    if ctx is None:
        keys, vals = k_own, v_own
    else:
        cos, sin = ctx[2]
        q = rope_tail(q, cos, sin)
        k_own = rope_tail(k_own, cos, sin)
        k_ctx, v_ctx = mla_keys(ctx[0], ctx[1], kv_up, kn)
        keys = jnp.concatenate([k_ctx, k_own], axis=1)
        vals = jnp.concatenate([v_ctx, v_own], axis=1)
    return attend(q, keys, vals), ckv, k_pe


def hyena_filters(L, w1, b1, w2, b2, w3, freq):
    f32 = jnp.float32
    u = jnp.arange(2 * L, dtype=jnp.int32)[:, None]
    t = jnp.where(u < L, u, 2 * L - u).astype(f32)
    t_unit = t / (L - 1)
    bands = jnp.linspace(1e-4, HY_BANDS - 1, HY_BANDS, dtype=f32)
    ang = 2.0 * math.pi * t * bands / L
    zpos = jnp.concatenate([t_unit, jnp.cos(ang), -jnp.sin(ang)], axis=-1)
    fr = freq.astype(f32)
    hid = jnp.sin(fr * (_mm(zpos, w1.astype(f32), split=True) + b1.astype(f32)))
    hid = jnp.sin(fr * (_mm(hid, w2.astype(f32), split=True) + b2.astype(f32)))
    filt = _mm(hid, w3.astype(f32), split=True)
    deltas = jnp.linspace(math.log(HY_TARGET) / HY_FAST, math.log(HY_TARGET) / HY_SLOW, D_MODEL, dtype=f32)
    window = jnp.exp(-t_unit * jnp.abs(deltas))
    circ = jnp.where(u < L, filt[:, :D_MODEL], filt[:, D_MODEL:]) * window
    circ = jnp.where(u == L, 0.0, circ)
    return circ / jnp.sum(jnp.abs(circ), axis=0, keepdims=True)


def hyena_mixer(proj, p):
    conv_w, conv_b, f_w1, f_b1, f_w2, f_b2, f_w3, f_freq, bias = p
    B, L, _ = proj.shape
    u = centred_dwconv3(proj, conv_w, conv_b)
    x0, x1, v = jnp.split(u, 3, axis=-1)
    zin = (x1 * v).astype(jnp.float32)
    conv = hyena_longconv(zin, hyena_filters(L, f_w1, f_b1, f_w2, f_b2, f_w3, f_freq))
    y = conv + zin * bias.astype(jnp.float32)
    return x0 * y


PEER_N = PEER_KEYS * PEER_KEYS
GATE_LANES = 2 * LANES
PEER_ROUTE_TOKENS = 256
PEER_EXPERT_TOKENS = 512
ROW_TILE = 512
SEQ_TILE = 256


def _top_vals(s, n, with_rank):
    vals = []
    rank = jnp.full(s.shape, float(n), jnp.float32) if with_rank else None
    for a in range(n):
        m = jnp.max(s, axis=0, keepdims=True)
        vals.append(m)
        hit = s == m
        if with_rank:
            rank = jnp.where(hit, float(a), rank)
        s = jnp.where(hit, -jnp.inf, s)
    return vals, rank


def _peer_route_kernel(x_ref, g_ref, sh_ref, sc_ref, wq_hi_ref, wq_lo_ref, k_hi_ref, k_lo_ref,
                       n1_ref, e1_ref, r2_ref, e2_ref, h_ref, s_scr):
    tb = x_ref.shape[0]
    half = PEER_DKEY // 2
    n_tiles = tb // LANES
    h_hi, h_lo = _split_bf16(_rms_mod(x_ref[...], g_ref[...], sh_ref[0], sc_ref[0]))
    h_ref[...] = h_hi
    q_hi, q_lo = _split_bf16(_dot3(h_hi, h_lo, wq_hi_ref[...], wq_lo_ref[...]))
    for hh in range(PEER_HEADS):
        for p in range(2):
            cols = slice((2 * hh + p) * half, (2 * hh + p + 1) * half)
            k_hi = k_hi_ref[hh, p]
            s_scr[hh, p] = (lax.dot_general(k_hi, q_hi[:, cols], _NT, preferred_element_type=jnp.float32)
                            + lax.dot_general(k_lo_ref[hh, p], q_hi[:, cols], _NT, preferred_element_type=jnp.float32)
                            + lax.dot_general(k_hi, q_lo[:, cols], _NT, preferred_element_type=jnp.float32))

    K = PEER_TOPK
    G = SUBLANES

    def tile(it, carry):
        hh = it // n_tiles
        ln = pl.ds(pl.multiple_of((it % n_tiles) * LANES, LANES), LANES)
        s1 = s_scr[hh, 0, :, ln]
        s2 = s_scr[hh, 1, :, ln]
        v1, _ = _top_vals(s1, K, False)
        v2l, r2 = _top_vals(s2, K, True)
        v2 = jnp.concatenate(v2l, axis=0)
        cand = [v1[0] + v2[:G], v1[0] + v2[G:]]
        cand += [v1[a] + v2[:G] for a in range(1, G)]
        cand += [jnp.concatenate(v1[G:], axis=0) + v2[0:1]]
        c = cand
        for k in range(K):
            m = c[0]
            for ci in c[1:]:
                m = jnp.maximum(m, ci)
            m = jnp.max(m, axis=0, keepdims=True)
            if k + 1 < K:
                c = [jnp.where(ci == m, -jnp.inf, ci) for ci in c]
        tau = m
        top = v1[0] + v2[0:1]
        keep = [ci >= tau for ci in cand]
        z = jnp.zeros_like(tau)
        for ci, ki in zip(cand, keep):
            z = z + jnp.sum(jnp.where(ki, jnp.exp(ci - top), 0.0), axis=0, keepdims=True)
        cnt = [jnp.sum(jnp.where(ki, 1.0, 0.0), axis=0, keepdims=True) for ki in keep[:G + 1]]
        tail = jnp.where(keep[G + 1], 1.0, 0.0)
        n_a = [cnt[0] + cnt[1]] + cnt[2:] + [tail[a:a + 1] for a in range(G)]
        n1 = jnp.zeros_like(s1)
        for a in range(K):
            n1 = jnp.where(s1 == v1[a], n_a[a], n1)
        n1_ref[hh, :, ln] = n1
        e1_ref[hh, :, ln] = jnp.exp(s1 - v1[0]) / z
        r2_ref[hh, :, ln] = r2.astype(jnp.bfloat16)
        e2_ref[hh, :, ln] = jnp.exp(s2 - v2[0:1]).astype(jnp.bfloat16)
        return carry

    lax.fori_loop(0, PEER_HEADS * n_tiles, tile, 0, unroll=2)


def peer_route(x, g, shift_t, scale_t, wq_hi, wq_lo, k_hi, k_lo, tb=PEER_ROUTE_TOKENS):
    T = x.shape[0]
    nh = PEER_HEADS
    blk = pl.BlockSpec((nh, PEER_KEYS, tb), lambda t: (0, 0, t))
    sd = lambda dt: jax.ShapeDtypeStruct((nh, PEER_KEYS, T), dt)
    whole = lambda a: pl.BlockSpec(a.shape, lambda t: (0,) * a.ndim)
    mod = pl.BlockSpec((1, 1, D_MODEL), lambda t: (t, 0, 0))
    return pl.pallas_call(
        _peer_route_kernel,
        grid=(T // tb,),
        in_specs=[pl.BlockSpec((tb, D_MODEL), lambda t: (t, 0)), whole(g), mod, mod,
                  whole(wq_hi), whole(wq_lo), whole(k_hi), whole(k_lo)],
        out_specs=[blk, blk, blk, blk, pl.BlockSpec((tb, D_MODEL), lambda t: (t, 0))],
        out_shape=[sd(jnp.float32), sd(jnp.float32), sd(jnp.bfloat16), sd(jnp.bfloat16),
                   jax.ShapeDtypeStruct((T, D_MODEL), jnp.bfloat16)],
        scratch_shapes=[pltpu.VMEM((nh, 2, PEER_KEYS, tb), jnp.float32)],
        compiler_params=pltpu.CompilerParams(dimension_semantics=("arbitrary",),
                                             vmem_limit_bytes=VMEM_LIMIT),
        name="peer_route",
    )(x, g, shift_t, scale_t, wq_hi, wq_lo, k_hi, k_lo)


def _gelu_tanh(x):
    hx = 0.5 * x
    return hx * jnp.tanh(x * (x * x * (0.7978845608028654 * 0.044715) + 0.7978845608028654)) + hx


def _peer_expert_kernel(h_ref, u_ref, vt_even_ref, vt_prev_ref, vt_last_ref, n1_ref, e1_ref, r2_ref, e2_ref,
                        x_ref, gate_ref, o_ref, acc_ref, a0_scr, a1_scr, w0_scr, w1_scr):
    c = pl.program_id(1)
    ec, tb = a0_scr.shape
    n_i = ec // PEER_KEYS
    bf16 = jnp.bfloat16
    f32 = jnp.float32

    @pl.when(c == 0)
    def _():
        acc_ref[...] = jnp.zeros_like(acc_ref)
        w1_scr[...] = jnp.zeros_like(w1_scr)

    def gate_times_act(a_scr, w_scr, chunk):
        igrp = pl.ds(pl.multiple_of(chunk * n_i, SUBLANES), SUBLANES)
        for lt in range(tb // GATE_LANES):
            ln = slice(lt * GATE_LANES, (lt + 1) * GATE_LANES)
            for ii in range(n_i):
                rows = slice(ii * PEER_KEYS, (ii + 1) * PEER_KEYS)
                g = jnp.zeros((PEER_KEYS, GATE_LANES), bf16)
                for hh in range(PEER_HEADS):
                    n1 = jnp.broadcast_to(n1_ref[hh, igrp, ln][ii:ii + 1], (PEER_KEYS, GATE_LANES)).astype(bf16)
                    e1 = jnp.broadcast_to(e1_ref[hh, igrp, ln][ii:ii + 1], (PEER_KEYS, GATE_LANES)).astype(bf16)
                    g = g + jnp.where(r2_ref[hh, :, ln] < n1, e2_ref[hh, :, ln] * e1, jnp.zeros((), bf16))
                w_scr[rows, ln] = g * _gelu_tanh(a_scr[rows, ln])

    h = h_ref[...]
    a0_scr[...] = lax.dot_general(u_ref[:ec], h, _NT, preferred_element_type=f32).astype(bf16)
    acc_ref[...] += jnp.dot(vt_prev_ref[...], w1_scr[...], preferred_element_type=f32)
    gate_times_act(a0_scr, w0_scr, 2 * c)
    a1_scr[...] = lax.dot_general(u_ref[ec:], h, _NT, preferred_element_type=f32).astype(bf16)
    acc_ref[...] += jnp.dot(vt_even_ref[...], w0_scr[...], preferred_element_type=f32)
    gate_times_act(a1_scr, w1_scr, 2 * c + 1)

    @pl.when(c == pl.num_programs(1) - 1)
    def _():
        y = acc_ref[...] + jnp.dot(vt_last_ref[...], w1_scr[...], preferred_element_type=f32)
        o_ref[...] = x_ref[...] + gate_ref[0] * y.T


def peer_experts(h_bf16, u_bf16, vt_bf16, n1, e1, r2, e2, x, gate_t, tb=PEER_EXPERT_TOKENS, ec=SUBLANES * PEER_KEYS):
    T = h_bf16.shape[0]
    n_steps = PEER_N // (2 * ec)
    rblk = pl.BlockSpec((PEER_HEADS, PEER_KEYS, tb), lambda t, c: (0, 0, t))
    vt_blk = lambda chunk_of: pl.BlockSpec((D_MODEL, ec), lambda t, c: (0, chunk_of(c)))
    return pl.pallas_call(
        _peer_expert_kernel,
        grid=(T // tb, n_steps),
        in_specs=[pl.BlockSpec((tb, D_MODEL), lambda t, c: (t, 0)),
                  pl.BlockSpec((2 * ec, D_MODEL), lambda t, c: (c, 0)),
                  vt_blk(lambda c: 2 * c),
                  vt_blk(lambda c: jnp.maximum(2 * c - 1, 0)),
                  vt_blk(lambda c: 2 * n_steps - 1),
                  rblk, rblk, rblk, rblk,
                  pl.BlockSpec((tb, D_MODEL), lambda t, c: (t, 0)),
                  pl.BlockSpec((1, 1, D_MODEL), lambda t, c: (t, 0, 0))],
        out_specs=pl.BlockSpec((tb, D_MODEL), lambda t, c: (t, 0)),
        out_shape=jax.ShapeDtypeStruct((T, D_MODEL), jnp.float32),
        scratch_shapes=[pltpu.VMEM((D_MODEL, tb), jnp.float32)] + [pltpu.VMEM((ec, tb), jnp.bfloat16)] * 4,
        compiler_params=pltpu.CompilerParams(dimension_semantics=("arbitrary", "arbitrary"),
                                             vmem_limit_bytes=VMEM_LIMIT),
        name="peer_experts",
    )(h_bf16, u_bf16, vt_bf16, vt_bf16, vt_bf16, n1, e1, r2, e2, x, gate_t)


def peer_weights(w_q, sub_keys, u_tab, v_tab):
    return _split_bf16(w_q) + _split_bf16(sub_keys) + (u_tab.astype(jnp.bfloat16), v_tab.T.astype(jnp.bfloat16))


def peer_block(x, g, mods_route, mods_expert, weights):
    wq_hi, wq_lo, k_hi, k_lo, u_bf16, vt_bf16 = weights
    n1, e1, r2, e2, h_bf16 = peer_route(x, g[None], mods_route[0], mods_route[1], wq_hi, wq_lo, k_hi, k_lo)
    return peer_experts(h_bf16, u_bf16, vt_bf16, n1, e1, r2, e2, x, mods_expert)


def kernel(x_prompt, x_sample, state_rwkv_fwd, state_rwkv_bwd, cache_mla_ckv, cache_mla_kpe, c, c_ctx,
           norm_g, w_mod, b_mod, ab_w_in, rw_mu, rw_w0, rw_w2, rw_a0, rw_a2, rw_g2, rw_k_k, rw_k_a, rw_r_k,
           rw_lnx_g, rw_lnx_b, mla_q_norm, mla_q_up, mla_kv_norm, mla_kv_up, mla_qn, mla_kn, ab_w_out,
           hy_w_in, hy_b_in, hy_conv_w, hy_conv_b, hy_f_w1, hy_f_b1, hy_f_w2, hy_f_b2, hy_f_w3, hy_f_freq,
           hy_bias, hy_w_out, peer_w_q, peer_keys, peer_u, peer_v):
    rope = axial_rope(x_sample.shape[1])
    D = D_MODEL
    groups = [dict(x=x_prompt.reshape(-1, D), B=x_prompt.shape[0], L=x_prompt.shape[1]),
              dict(x=x_sample.reshape(-1, D), B=x_sample.shape[0], L=x_sample.shape[1])]
    st_f, st_b, st_ckv, st_kpe = [], [], [], []
    for li in range(DEPTH):
        j = li // 2
        groups[0]["mod"] = (_mm(jax.nn.silu(c_ctx)[None], w_mod[li]) + b_mod[li]).reshape(1, 6, D)
        groups[1]["mod"] = (_mm(jax.nn.silu(c), w_mod[li]) + b_mod[li]).reshape(-1, 6, D)
        peer_w = peer_weights(peer_w_q[li], peer_keys[li], peer_u[li], peer_v[li])
        for gi, g in enumerate(groups):
            x, B, L = g["x"], g["B"], g["L"]
            mod = lambda i, tm: tile_mod(g["mod"][:, i], B * L, tm)
            if li % 2 == 0:
                proj = norm_mod_matmul(x, norm_g[li, 0], mod(0, ROW_TILE), mod(1, ROW_TILE), ab_w_in[j],
                                       jnp.zeros((ab_w_in.shape[-1],), jnp.float32), ROW_TILE)
                prep = rwkv_prep(proj, L, rw_mu[j], rw_w0[j], rw_w2[j], rw_a0[j], rw_a2[j], rw_g2[j],
                                 rw_k_k[j], rw_k_a[j], rw_r_k[j], SEQ_TILE)
                s0s = None if gi == 0 else (state_rwkv_fwd[:, j], state_rwkv_bwd[:, j])
                ys, (sf, sb) = rwkv_scans(prep[:9], B, L, s0s)
                ctx = None if gi == 0 else (cache_mla_ckv[:, j], cache_mla_kpe[:, j], rope)
                att, ckv, kpe = mla_mixer(proj[:, RW_IN:].reshape(B, L, -1), ctx,
                                          (mla_q_norm[j], mla_q_up[j], mla_kv_norm[j], mla_kv_up[j], mla_qn[j], mla_kn[j]))
                if gi == 0:
                    st_f.append(sf)
                    st_b.append(sb)
                    st_ckv.append(ckv)
                    st_kpe.append(kpe)
                x = rwkv_out_residual(ys[0], ys[1], prep[10], prep[9], att.reshape(B * L, -1), rw_lnx_g[j], rw_lnx_b[j],
                                      ab_w_out[j], x, mod(2, ROW_TILE), ROW_TILE)
            else:
                p = (hy_conv_w[j], hy_conv_b[j], hy_f_w1[j], hy_f_b1[j], hy_f_w2[j],
                     hy_f_b2[j], hy_f_w3[j], hy_f_freq[j], hy_bias[j])
                proj = norm_mod_matmul(x, norm_g[li, 0], mod(0, ROW_TILE), mod(1, ROW_TILE), hy_w_in[j], hy_b_in[j],
                                       ROW_TILE)
                op = hyena_mixer(proj.reshape(B, L, -1), p)
                x = matmul_residual(op.reshape(B * L, -1), hy_w_out[j], x, mod(2, ROW_TILE), ROW_TILE)
            g["x"] = peer_block(x, norm_g[li, 1], (mod(3, PEER_ROUTE_TOKENS), mod(4, PEER_ROUTE_TOKENS)),
                                mod(5, PEER_EXPERT_TOKENS), peer_w)
    xp = groups[0]["x"].reshape(x_prompt.shape)
    xs = groups[1]["x"].reshape(x_sample.shape)
    new_state_rwkv_fwd = jnp.stack(st_f, axis=1).astype(x_prompt.dtype)
    new_state_rwkv_bwd = jnp.stack(st_b, axis=1).astype(x_prompt.dtype)
    new_cache_mla_ckv = jnp.stack(st_ckv, axis=1)
    new_cache_mla_kpe = jnp.stack(st_kpe, axis=1)
    return (xp, xs, new_state_rwkv_fwd, new_state_rwkv_bwd, new_cache_mla_ckv, new_cache_mla_kpe)
```

```python
import math
from functools import lru_cache, partial

import jax
import jax.numpy as jnp
import numpy as np
from jax import lax
from jax.experimental import pallas as pl
from jax.experimental.pallas import tpu as pltpu

D_MODEL = 1024
DEPTH = 2
GRID_W = 64
EPS = 1e-6
RW_HEADS = 8
RW_HEAD = 64
RW_DIM = RW_HEADS * RW_HEAD
W_LORA = 64
A_LORA = 64
G_LORA = 128
LNX_EPS = 64e-5
RW_IN = 3 * RW_DIM + W_LORA + A_LORA + G_LORA
RW_SPLITS = (RW_DIM, 2 * RW_DIM, 3 * RW_DIM, 3 * RW_DIM + W_LORA, 3 * RW_DIM + W_LORA + A_LORA)
MLA_HEADS = 4
QK_NOPE = 128
QK_ROPE = 64
QK_HEAD = QK_NOPE + QK_ROPE
V_HEAD = 128
Q_LORA = 256
KV_LORA = 128
MLA_DIM = MLA_HEADS * V_HEAD
ROPE_THETA = 10000.0
HY_BANDS = 16
HY_TARGET = 1e-2
HY_FAST = 0.3
HY_SLOW = 1.5
PEER_KEYS = 128
PEER_HEADS = 8
PEER_DKEY = 256
PEER_TOPK = 16

LANES = 128
SUBLANES = 8
VMEM_LIMIT = 56 * 1024 * 1024
_NT = (((1,), (1,)), ((), ()))


def _split_bf16(x):
    hi = x.astype(jnp.bfloat16)
    lo = (x - hi.astype(jnp.float32)).astype(jnp.bfloat16)
    return hi, lo


def _dot3(a_hi, a_lo, b_hi, b_lo):
    f32 = jnp.float32
    return (jnp.dot(a_hi, b_hi, preferred_element_type=f32) + jnp.dot(a_lo, b_hi, preferred_element_type=f32)
            + jnp.dot(a_hi, b_lo, preferred_element_type=f32))


def _mm_kernel(a_ref, b_ref, o_ref, *, split):
    if split:
        a_hi, a_lo = _split_bf16(a_ref[...])
        b_hi, b_lo = _split_bf16(b_ref[...])
        o_ref[...] = _dot3(a_hi, a_lo, b_hi, b_lo)
    else:
        o_ref[...] = jnp.dot(a_ref[...].astype(jnp.bfloat16), b_ref[...].astype(jnp.bfloat16),
                             preferred_element_type=jnp.float32)


def _mm(a, b, tm=512, tn=512, split=False):
    lead = a.shape[:-1]
    K = a.shape[-1]
    N = b.shape[-1]
    a2 = a.reshape(-1, K)
    if K % LANES:
        kp = -K % LANES
        a2 = jnp.pad(a2, ((0, 0), (0, kp)))
        b = jnp.pad(b, ((0, kp), (0, 0)))
        K += kp
    M = a2.shape[0]
    tm = min(tm, M)
    tn = min(tn, N)
    if N % tn:
        tn = N
    assert M % tm == 0 and N % tn == 0
    out = pl.pallas_call(
        partial(_mm_kernel, split=split),
        grid=(M // tm, N // tn),
        in_specs=[pl.BlockSpec((tm, K), lambda i, j: (i, 0)),
                  pl.BlockSpec((K, tn), lambda i, j: (0, j))],
        out_specs=pl.BlockSpec((tm, tn), lambda i, j: (i, j)),
        out_shape=jax.ShapeDtypeStruct((M, N), jnp.float32),
        name="matmul",
    )(a2, b)
    return out.reshape(*lead, N)


def rmsnorm(x, g):
    xf = x.astype(jnp.float32)
    y = xf * lax.rsqrt(jnp.mean(xf * xf, axis=-1, keepdims=True) + EPS)
    return (y * g.astype(jnp.float32)).astype(x.dtype)


def axial_rope(L):
    rows = L // GRID_W
    row = jnp.repeat(jnp.arange(rows, dtype=jnp.float32), GRID_W)
    col = jnp.tile(jnp.arange(GRID_W, dtype=jnp.float32), rows)
    n_freq = QK_ROPE // 4
    inv = ROPE_THETA ** (-jnp.arange(n_freq, dtype=jnp.float32) / n_freq)
    ang = jnp.concatenate([row[:, None] * inv, col[:, None] * inv], axis=-1)
    return jnp.cos(ang), jnp.sin(ang)


def rope_tail(x, cos, sin):
    xn, xr = x[..., :QK_NOPE], x[..., QK_NOPE:]
    x1, x2 = xr[..., 0::2], xr[..., 1::2]
    cs = cos[None, :, None, :].astype(x.dtype)
    sn = sin[None, :, None, :].astype(x.dtype)
    rot = jnp.stack([x1 * cs - x2 * sn, x1 * sn + x2 * cs], axis=-1).reshape(xr.shape)
    return jnp.concatenate([xn, rot], axis=-1)


def _mla_attn_kernel(q_ref, k_ref, v_ref, o_ref):
    scale = QK_HEAD ** -0.5
    for h in range(MLA_HEADS):
        q = q_ref[0, :, h * QK_HEAD:(h + 1) * QK_HEAD].astype(jnp.bfloat16)
        k = k_ref[0, :, h * QK_HEAD:(h + 1) * QK_HEAD].astype(jnp.bfloat16)
        s = lax.dot_general(q, k, _NT, preferred_element_type=jnp.float32) * scale
        m = jnp.max(s, axis=-1, keepdims=True)
        p = jnp.exp(s - m)
        l = jnp.sum(p, axis=-1, keepdims=True)
        v = v_ref[0, :, h * V_HEAD:(h + 1) * V_HEAD].astype(jnp.bfloat16)
        o = jnp.dot((p / l).astype(jnp.bfloat16), v, preferred_element_type=jnp.float32)
        o_ref[0, :, h * V_HEAD:(h + 1) * V_HEAD] = o


def attend(q, k, v, tq=256):
    B, Lq = q.shape[:2]
    Lk = k.shape[1]
    tq = min(tq, Lq)
    return pl.pallas_call(
        _mla_attn_kernel, grid=(B, Lq // tq),
        in_specs=[pl.BlockSpec((1, tq, MLA_HEADS * QK_HEAD), lambda b, i: (b, i, 0)),
                  pl.BlockSpec((1, Lk, MLA_HEADS * QK_HEAD), lambda b, i: (b, 0, 0)),
                  pl.BlockSpec((1, Lk, MLA_HEADS * V_HEAD), lambda b, i: (b, 0, 0))],
        out_specs=pl.BlockSpec((1, tq, MLA_HEADS * V_HEAD), lambda b, i: (b, i, 0)),
        out_shape=jax.ShapeDtypeStruct((B, Lq, MLA_HEADS * V_HEAD), jnp.float32),
        compiler_params=pltpu.CompilerParams(dimension_semantics=("arbitrary", "arbitrary"),
                                             vmem_limit_bytes=VMEM_LIMIT),
        name="mla_attend",
    )(q.reshape(B, Lq, -1), k.reshape(B, Lk, -1), v.reshape(B, Lk, -1))


def _first_row_block(fn):
    pl.when(pl.program_id(2) == 0)(fn)


def _dft_pair_kernel(c_hi, c_lo, s_hi, s_lo, x_ref, oc_ref, os_ref, x_hi, x_lo):
    def _():
        x_hi[...], x_lo[...] = _split_bf16(x_ref[0])
    _first_row_block(_)
    oc_ref[0] = _dot3(c_hi[...], c_lo[...], x_hi[...], x_lo[...])
    os_ref[0] = _dot3(s_hi[...], s_lo[...], x_hi[...], x_lo[...])


def _dft_spectral_kernel(c_tab, s_tab, x_ref, hr_ref, hi_ref, yr_ref, yi_ref, x16):
    def _():
        x16[...] = x_ref[0].astype(jnp.bfloat16)
    _first_row_block(_)
    rb, L = c_tab.shape
    zc = jnp.dot(c_tab[...], x16[...], preferred_element_type=jnp.float32)
    zs = jnp.dot(s_tab[...], x16[...], preferred_element_type=jnp.float32)
    hr = hr_ref[...]
    hi = hi_ref[...]
    f = pl.program_id(2) * rb + lax.broadcasted_iota(jnp.int32, zc.shape, 0)
    yr_ref[0] = jnp.where(f == 0, zc * hr * (0.5 / L), (zc * hr - zs * hi) * (1.0 / L))
    yi_ref[0] = jnp.where(f == 0, zs * hi * (0.5 / L), (zc * hi + zs * hr) * (1.0 / L))


def _dft_sum_kernel(c_tab, s_tab, x_ref, y_ref, o_ref, x16, y16):
    def _():
        x16[...] = x_ref[0].astype(jnp.bfloat16)
        y16[...] = y_ref[0].astype(jnp.bfloat16)
    _first_row_block(_)
    o_ref[0] = (jnp.dot(c_tab[...], x16[...], preferred_element_type=jnp.float32)
                + jnp.dot(s_tab[...], y16[...], preferred_element_type=jnp.float32))


@lru_cache(maxsize=None)
def dft_tables(L):
    f = np.arange(L, dtype=np.int64)
    ang = ((f[:, None] * f[None, :]) % (2 * L)).astype(np.float64) * (math.pi / L)
    c = np.cos(ang).astype(np.float32)
    s = (-np.sin(ang)).astype(np.float32)
    alt = np.where(f % 2 == 0, 1.0, -1.0).astype(np.float32)
    s_ana = np.where(f[:, None] == 0, alt[None, :], s)
    s_syn = np.where(f[None, :] == 0, alt[:, None], s)

    def split(x):
        hi = x.astype(jnp.bfloat16)
        return hi, (x - hi.astype(np.float32)).astype(jnp.bfloat16)

    cc = split(c)
    return cc + split(s_ana), cc + split(s_syn)


def _dft_call(body, tabs, xs, n_out, name, row_inputs=(), rb=256, nb=512):
    B, L, N = xs[0].shape
    rb = min(rb, L)
    nb = min(nb, N)
    tab = pl.BlockSpec((rb, L), lambda b, n, i: (i, 0))
    xin = pl.BlockSpec((1, L, nb), lambda b, n, i: (b, 0, n))
    rin = pl.BlockSpec((rb, nb), lambda b, n, i: (i, n))
    out = pl.BlockSpec((1, rb, nb), lambda b, n, i: (b, i, n))
    sd = jax.ShapeDtypeStruct((B, L, N), jnp.float32)
    return pl.pallas_call(
        body, grid=(B, N // nb, L // rb),
        in_specs=[tab] * len(tabs) + [xin] * len(xs) + [rin] * len(row_inputs),
        out_specs=[out] * n_out if n_out > 1 else out,
        out_shape=[sd] * n_out if n_out > 1 else sd,
        scratch_shapes=[pltpu.VMEM((L, nb), jnp.bfloat16)] * (len(tabs) // 2 * len(xs)),
        compiler_params=pltpu.CompilerParams(dimension_semantics=("arbitrary",) * 3,
                                             vmem_limit_bytes=VMEM_LIMIT),
        name=name,
    )(*tabs, *xs, *row_inputs)


def hyena_longconv(zin, circ):
    B, L, D = zin.shape
    tabs, tabs_syn = dft_tables(L)
    alt = jnp.where(jnp.arange(L) % 2 == 0, 1.0, -1.0).astype(jnp.float32)[:, None]
    fc, fs = _dft_call(_dft_pair_kernel, tabs, [jnp.concatenate([circ[:L], circ[L:]], axis=-1)[None]], 2, "dft_filter")
    hr = fc[0, :, :D] + alt * fc[0, :, D:]
    hi = fs[0, :, :D] + alt * fs[0, :, D:]
    yr, yi = _dft_call(_dft_spectral_kernel, tabs[0::2], [zin], 2, "dft_analysis", row_inputs=(hr, hi))
    return _dft_call(_dft_sum_kernel, tabs_syn[0::2], [yr, yi], 1, "dft_synthesis")


SCAN_ACCS = 4


def _rwkv_scan_kernel(r_ref, kk_ref, v_ref, w_ref, kd_ref, b_ref, s0_ref, y_ref, sf_ref, s_scr, *, reverse):
    c = pl.program_id(0)
    tc = r_ref.shape[0]
    nv = v_ref.shape[1]

    @pl.when(c == 0)
    def _():
        s_scr[...] = s0_ref[...]

    def row(ref, tt, k):
        return jnp.broadcast_to(ref[tt, pl.ds(k, 1), :], (nv, LANES))

    def step(t, carry):
        tt = tc - 1 - t if reverse else t
        parts = [None] * SCAN_ACCS
        for k in range(RW_HEAD):
            term = s_scr[k] * row(kk_ref, tt, k)
            parts[k % SCAN_ACCS] = term if parts[k % SCAN_ACCS] is None else parts[k % SCAN_ACCS] + term
        sa = -((parts[0] + parts[1]) + (parts[2] + parts[3]))
        vv = v_ref[tt]
        parts = [None] * SCAN_ACCS
        for k in range(RW_HEAD):
            s_new = s_scr[k] * row(w_ref, tt, k) + sa * row(b_ref, tt, k) + vv * row(kd_ref, tt, k)
            s_scr[k] = s_new
            term = s_new * row(r_ref, tt, k)
            parts[k % SCAN_ACCS] = term if parts[k % SCAN_ACCS] is None else parts[k % SCAN_ACCS] + term
        y_ref[tt] = (parts[0] + parts[1]) + (parts[2] + parts[3])
        return carry

    lax.fori_loop(0, tc, step, 0)

    @pl.when(c == pl.num_programs(0) - 1)
    def _():
        sf_ref[...] = s_scr[...]


def rwkv_scan(r, kk, v, w, kd, b, s0, reverse, tc=32):
    L = r.shape[0]
    nv = v.shape[1]
    nc = L // tc
    tmap = (lambda c: nc - 1 - c) if reverse else (lambda c: c)
    seq = lambda rows: pl.BlockSpec((tc, rows, LANES), lambda c: (tmap(c), 0, 0))
    state = pl.BlockSpec((RW_HEAD, nv, LANES), lambda c: (0, 0, 0))
    return pl.pallas_call(
        partial(_rwkv_scan_kernel, reverse=reverse),
        grid=(nc,),
        in_specs=[seq(RW_HEAD), seq(RW_HEAD), seq(nv), seq(RW_HEAD), seq(RW_HEAD), seq(RW_HEAD), state],
        out_specs=[seq(nv), state],
        out_shape=[jax.ShapeDtypeStruct((L, nv, LANES), jnp.float32),
                   jax.ShapeDtypeStruct((RW_HEAD, nv, LANES), jnp.float32)],
        scratch_shapes=[pltpu.VMEM((RW_HEAD, nv, LANES), jnp.float32)],
        compiler_params=pltpu.CompilerParams(dimension_semantics=("arbitrary",), vmem_limit_bytes=VMEM_LIMIT),
        name="rwkv_scan",
    )(r, kk, v, w, kd, b, s0)


def to_lanes(x, vsplit):
    B, L, H, N = x.shape
    return jnp.tile(x.transpose(1, 3, 0, 2).reshape(L, N, B * H), (1, 1, vsplit))


def v_to_lanes(x, vsplit):
    B, L, H, N = x.shape
    nv = N // vsplit
    return x.reshape(B, L, H, vsplit, nv).transpose(1, 4, 3, 0, 2).reshape(L, nv, vsplit * B * H)


def v_from_lanes(y, B, H, vsplit):
    L, nv, _ = y.shape
    return y.reshape(L, nv, vsplit, B, H).transpose(3, 0, 4, 2, 1).reshape(B, L, H, vsplit * nv)


def state_to_lanes(s, vsplit):
    B, H, N, K = s.shape
    nv = N // vsplit
    return s.reshape(B, H, vsplit, nv, K).transpose(4, 3, 2, 0, 1).reshape(K, nv, vsplit * B * H)


def state_from_lanes(s, B, H, vsplit):
    K, nv, _ = s.shape
    return s.reshape(K, nv, vsplit, B, H).transpose(3, 4, 2, 1, 0).reshape(B, H, vsplit * nv, K)


def rwkv_scan_both(rh, kk, vh, decay2, kd2, b2, s0_2, tc=32):
    B, L, H, N = rh.shape
    vsplit = LANES // (B * H)
    assert B * H * vsplit == LANES
    r_l, kk_l, v_l = to_lanes(rh, vsplit), to_lanes(kk, vsplit), v_to_lanes(vh, vsplit)
    ys, sfs = [], []
    for d in range(2):
        y, sf = rwkv_scan(r_l, kk_l, v_l, to_lanes(decay2[d], vsplit), to_lanes(kd2[d], vsplit),
                          to_lanes(b2[d], vsplit), state_to_lanes(s0_2[d], vsplit), reverse=(d == 1), tc=tc)
        ys.append(v_from_lanes(y, B, H, vsplit))
        sfs.append(state_from_lanes(sf, B, H, vsplit))
    return ys, sfs


def _rms_mod(x, g, shift, scale):
    y = x * lax.rsqrt(jnp.mean(x * x, axis=-1, keepdims=True) + EPS)
    return (y * g) * (1.0 + scale) + shift


def tile_mod(m, n_rows, tm):
    return jnp.repeat(m, n_rows // m.shape[0] // tm, axis=0)[:, None, :]


def _norm_mod_matmul_kernel(x_ref, g_ref, sh_ref, sc_ref, w_ref, b_ref, o_ref):
    h = _rms_mod(x_ref[...], g_ref[...], sh_ref[0], sc_ref[0])
    o_ref[...] = jnp.dot(h.astype(jnp.bfloat16), w_ref[...], preferred_element_type=jnp.float32) + b_ref[...]


def norm_mod_matmul(x, g, shift_t, scale_t, w, b, tm):
    T, D = x.shape
    N = w.shape[1]
    mod = pl.BlockSpec((1, 1, D), lambda i: (i, 0, 0))
    return pl.pallas_call(
        _norm_mod_matmul_kernel, grid=(T // tm,),
        in_specs=[pl.BlockSpec((tm, D), lambda i: (i, 0)), pl.BlockSpec((1, D), lambda i: (0, 0)), mod, mod,
                  pl.BlockSpec((D, N), lambda i: (0, 0)), pl.BlockSpec((1, N), lambda i: (0, 0))],
        out_specs=pl.BlockSpec((tm, N), lambda i: (i, 0)),
        out_shape=jax.ShapeDtypeStruct((T, N), jnp.float32),
        compiler_params=pltpu.CompilerParams(dimension_semantics=("arbitrary",), vmem_limit_bytes=VMEM_LIMIT),
        name="norm_mod_matmul",
    )(x, g[None], shift_t, scale_t, w.astype(jnp.bfloat16), b[None])


def _head_sums(x, ones_bd):
    hi, lo = _split_bf16(x)
    return (jnp.dot(hi, ones_bd, preferred_element_type=jnp.float32)
            + jnp.dot(lo, ones_bd, preferred_element_type=jnp.float32))


def _with_neighbours(x_ref, prev_ref, next_ref, first_ref, last_ref, ncols):
    tm = x_ref.shape[0]
    x = x_ref[:, :ncols]
    row = lax.broadcasted_iota(jnp.int32, (tm, 1), 0)
    prev_row = prev_ref[SUBLANES - 1:SUBLANES, :ncols] * (1.0 - first_ref[0, :, :1])
    next_row = next_ref[0:1, :ncols] * (1.0 - last_ref[0, :, :1])
    prev = jnp.where(row == 0, prev_row, pltpu.roll(x, 1, 0))
    nxt = jnp.where(row == tm - 1, next_row, pltpu.roll(x, tm - 1, 0))
    return x, prev, nxt


def _neighbour_specs(T, C, L, tm):
    n_tiles = T // tm
    hb = tm // SUBLANES
    start = np.arange(n_tiles) * tm % L
    first = np.broadcast_to((start == 0).astype(np.float32)[:, None, None], (n_tiles, 1, LANES))
    last = np.broadcast_to((start + tm == L).astype(np.float32)[:, None, None], (n_tiles, 1, LANES))
    flag = pl.BlockSpec((1, 1, LANES), lambda i: (i, 0, 0))
    specs = [pl.BlockSpec((tm, C), lambda i: (i, 0)),
             pl.BlockSpec((SUBLANES, C), lambda i: (jnp.maximum(i * hb - 1, 0), 0)),
             pl.BlockSpec((SUBLANES, C), lambda i: (jnp.minimum((i + 1) * hb, T // SUBLANES - 1), 0)),
             flag, flag]
    return specs, (jnp.asarray(first), jnp.asarray(last))


def _rwkv_prep_kernel(proj_ref, prev_ref, next_ref, first_ref, last_ref, mu_ref, w0_ref, w2_ref, a0_ref, a2_ref,
                      g2_ref, kk_ref, ka_ref, rk_ref, ones_ref,
                      r_out, kk_out, v_out, w_out0, w_out1, kd_out0, kd_out1, b_out0, b_out1, gate_out, bonus_out):
    bf16 = jnp.bfloat16
    f32 = jnp.float32
    x, prev, nxt = _with_neighbours(proj_ref, prev_ref, next_ref, first_ref, last_ref, RW_IN)
    rw = x + mu_ref[...] * (0.5 * (prev + nxt) - x)
    r = rw[:, RW_SPLITS[0] - RW_DIM:RW_SPLITS[0]]
    k = rw[:, RW_SPLITS[0]:RW_SPLITS[1]]
    v = rw[:, RW_SPLITS[1]:RW_SPLITS[2]]
    wd = rw[:, RW_SPLITS[2]:RW_SPLITS[3]]
    ad = rw[:, RW_SPLITS[3]:RW_SPLITS[4]]
    gd = rw[:, RW_SPLITS[4]:]
    ones_bd = ones_ref[...]
    kk = k * kk_ref[...]
    kk = kk / jnp.maximum(jnp.sqrt(_head_sums(kk * kk, ones_bd)), 1e-12)
    tanh_wd = jnp.tanh(wd).astype(bf16)
    ad16 = ad.astype(bf16)
    rk = r * rk_ref[...]
    bonus = jnp.zeros_like(r)
    for d, (w_o, kd_o, b_o) in enumerate(((w_out0, kd_out0, b_out0), (w_out1, kd_out1, b_out1))):
        w_raw = w0_ref[d:d + 1, :] + jnp.dot(tanh_wd, w2_ref[d], preferred_element_type=f32)
        w_o[...] = jnp.exp(-jnp.exp(-jax.nn.softplus(-w_raw) - 0.5))
        lr = jax.nn.sigmoid(a0_ref[d:d + 1, :] + jnp.dot(ad16, a2_ref[d], preferred_element_type=f32))
        kd = k * (1.0 + (lr - 1.0) * ka_ref[...])
        kd_o[...] = kd
        b_o[...] = kk * lr
        bonus = bonus + _head_sums(rk * kd, ones_bd)
    r_out[...] = r
    kk_out[...] = kk
    v_out[...] = v
    gate_out[...] = jnp.dot(jax.nn.sigmoid(gd).astype(bf16), g2_ref[...], preferred_element_type=f32)
    bonus_out[...] = bonus * v


def _head_ones():
    h = np.arange(RW_DIM) // RW_HEAD
    return jnp.asarray(h[:, None] == h[None, :], jnp.bfloat16)


def rwkv_prep(proj, L, mu, w0, w2, a0, a2, g2, k_k, k_a, r_k, tm):
    T, C = proj.shape
    nb_specs, flags = _neighbour_specs(T, C, L, tm)
    row2 = lambda a: a.reshape(1, -1)
    whole = lambda a: pl.BlockSpec(a.shape, lambda i: (0,) * a.ndim)
    out = pl.BlockSpec((tm, RW_DIM), lambda i: (i, 0))
    consts = (row2(mu), w0, w2.astype(jnp.bfloat16), a0, a2.astype(jnp.bfloat16), g2.astype(jnp.bfloat16),
              row2(k_k), row2(k_a), row2(r_k), _head_ones())
    return pl.pallas_call(
        _rwkv_prep_kernel, grid=(T // tm,),
        in_specs=nb_specs + [whole(a) for a in consts],
        out_specs=[out] * 11,
        out_shape=[jax.ShapeDtypeStruct((T, RW_DIM), jnp.float32)] * 11,
        compiler_params=pltpu.CompilerParams(dimension_semantics=("arbitrary",), vmem_limit_bytes=VMEM_LIMIT),
        name="rwkv_prep",
    )(proj, proj, proj, *flags, *consts)


def _rwkv_out_kernel(y0_ref, y1_ref, bonus_ref, gate_ref, mla_ref, lng_ref, lnb_ref, ones_ref, w_ref, x_ref, mod_ref,
                     o_ref):
    bf16 = jnp.bfloat16
    f32 = jnp.float32
    ones_bd = ones_ref[...]
    y = y0_ref[...] + y1_ref[...]
    dlt = y - _head_sums(y, ones_bd) * (1.0 / RW_HEAD)
    var = _head_sums(dlt * dlt, ones_bd) * (1.0 / RW_HEAD)
    yn = dlt * lax.rsqrt(var + LNX_EPS) * lng_ref[...] + lnb_ref[...]
    rw_out = (yn + bonus_ref[...]) * gate_ref[...]
    upd = (jnp.dot(rw_out.astype(bf16), w_ref[:RW_DIM], preferred_element_type=f32)
           + jnp.dot(mla_ref[...].astype(bf16), w_ref[RW_DIM:], preferred_element_type=f32))
    o_ref[...] = x_ref[...] + mod_ref[0] * upd


def rwkv_out_residual(y0, y1, bonus, gate, mla_out, lnx_g, lnx_b, w_out, x, gate_t, tm):
    T, D = x.shape
    row = lambda n: pl.BlockSpec((tm, n), lambda i: (i, 0))
    whole = lambda a: pl.BlockSpec(a.shape, lambda i: (0,) * a.ndim)
    consts = (lnx_g.reshape(1, -1), lnx_b.reshape(1, -1), _head_ones(), w_out.astype(jnp.bfloat16))
    return pl.pallas_call(
        _rwkv_out_kernel, grid=(T // tm,),
        in_specs=[row(RW_DIM)] * 4 + [row(MLA_DIM)] + [whole(a) for a in consts]
                 + [row(D), pl.BlockSpec((1, 1, D), lambda i: (i, 0, 0))],
        out_specs=row(D),
        out_shape=jax.ShapeDtypeStruct((T, D), jnp.float32),
        compiler_params=pltpu.CompilerParams(dimension_semantics=("arbitrary",), vmem_limit_bytes=VMEM_LIMIT),
        name="rwkv_out_residual",
    )(y0, y1, bonus, gate, mla_out, *consts, x, gate_t)


def mla_keys(ckv, k_pe, kv_up, kn):
    B, L, _ = ckv.shape
    kv = _mm(ckv, kv_up).reshape(B, L, MLA_HEADS, QK_NOPE + V_HEAD)
    k_rope = jnp.broadcast_to(k_pe[:, :, None, :], (B, L, MLA_HEADS, QK_ROPE))
    k = rmsnorm(jnp.concatenate([kv[..., :QK_NOPE], k_rope], axis=-1), kn)
    return k, kv[..., QK_NOPE:]


def rwkv_scans(prep, B, L, s0s):
    heads = lambda t: t.reshape(B, L, RW_HEADS, RW_HEAD)
    r, kk, v, w_0, w_1, kd_0, kd_1, b_0, b_1 = [heads(t) for t in prep]
    if s0s is None:
        z = jnp.zeros((B, RW_HEADS, RW_HEAD, RW_HEAD), jnp.float32)
        s0s = (z, z)
    ys, finals = rwkv_scan_both(r, kk, v, [w_0, w_1], [kd_0, kd_1], [b_0, b_1], s0s)
    return [y.reshape(B * L, RW_DIM) for y in ys], finals


def mla_mixer(mla, ctx, p):
    q_norm, q_up, kv_norm, kv_up, qn, kn = p
    B, L, _ = mla.shape
    q_c, kv_c, k_pe = jnp.split(mla, (Q_LORA, Q_LORA + KV_LORA), axis=-1)
    q = rmsnorm(_mm(rmsnorm(q_c, q_norm), q_up).reshape(B, L, MLA_HEADS, QK_HEAD), qn)
    ckv = rmsnorm(kv_c, kv_norm)
    k_own, v_own = mla_keys(ckv, k_pe, kv_up, kn)
    if ctx is None:
        keys, vals = k_own, v_own
    else:
        cos, sin = ctx[2]
        q = rope_tail(q, cos, sin)
        k_own = rope_tail(k_own, cos, sin)
        k_ctx, v_ctx = mla_keys(ctx[0], ctx[1], kv_up, kn)
        keys = jnp.concatenate([k_ctx, k_own], axis=1)
        vals = jnp.concatenate([v_ctx, v_own], axis=1)
    return attend(q, keys, vals), ckv, k_pe


def _hyena_prep_kernel(proj_ref, prev_ref, next_ref, first_ref, last_ref, cw_ref, cb_ref, x0_ref, z_ref):
    D = x0_ref.shape[1]
    x, prev, nxt = _with_neighbours(proj_ref, prev_ref, next_ref, first_ref, last_ref, proj_ref.shape[1])
    u = prev * cw_ref[0:1, :] + x * cw_ref[1:2, :] + nxt * cw_ref[2:3, :] + cb_ref[...]
    x0_ref[...] = u[:, :D]
    z_ref[...] = u[:, D:2 * D] * u[:, 2 * D:]


def hyena_prep(proj, L, conv_w, conv_b, tm):
    T, C = proj.shape
    D = C // 3
    nb_specs, flags = _neighbour_specs(T, C, L, tm)
    out = pl.BlockSpec((tm, D), lambda i: (i, 0))
    return pl.pallas_call(
        _hyena_prep_kernel, grid=(T // tm,),
        in_specs=nb_specs + [pl.BlockSpec((3, C), lambda i: (0, 0)), pl.BlockSpec((1, C), lambda i: (0, 0))],
        out_specs=[out, out],
        out_shape=[jax.ShapeDtypeStruct((T, D), jnp.float32)] * 2,
        compiler_params=pltpu.CompilerParams(dimension_semantics=("arbitrary",), vmem_limit_bytes=VMEM_LIMIT),
        name="hyena_prep",
    )(proj, proj, proj, *flags, conv_w, conv_b[None])


def _hyena_out_kernel(x0_ref, conv_ref, z_ref, bias_ref, w_ref, x_ref, mod_ref, o_ref):
    a = x0_ref[...] * (conv_ref[...] + z_ref[...] * bias_ref[...])
    o_ref[...] = x_ref[...] + mod_ref[0] * jnp.dot(a.astype(jnp.bfloat16), w_ref[...],
                                                   preferred_element_type=jnp.float32)


def hyena_out_residual(x0, conv, zin, bias, w_out, x, gate_t, tm):
    T, D = x.shape
    row = pl.BlockSpec((tm, D), lambda i: (i, 0))
    return pl.pallas_call(
        _hyena_out_kernel, grid=(T // tm,),
        in_specs=[row, row, row, pl.BlockSpec((1, D), lambda i: (0, 0)), pl.BlockSpec((D, D), lambda i: (0, 0)),
                  row, pl.BlockSpec((1, 1, D), lambda i: (i, 0, 0))],
        out_specs=row,
        out_shape=jax.ShapeDtypeStruct((T, D), jnp.float32),
        compiler_params=pltpu.CompilerParams(dimension_semantics=("arbitrary",), vmem_limit_bytes=VMEM_LIMIT),
        name="hyena_out_residual",
    )(x0, conv, zin, bias[None], w_out.astype(jnp.bfloat16), x, gate_t)


def hyena_filters(L, w1, b1, w2, b2, w3, freq):
    f32 = jnp.float32
    u = jnp.arange(2 * L, dtype=jnp.int32)[:, None]
    t = jnp.where(u < L, u, 2 * L - u).astype(f32)
    t_unit = t / (L - 1)
    bands = jnp.linspace(1e-4, HY_BANDS - 1, HY_BANDS, dtype=f32)
    ang = 2.0 * math.pi * t * bands / L
    zpos = jnp.concatenate([t_unit, jnp.cos(ang), -jnp.sin(ang)], axis=-1)
    fr = freq.astype(f32)
    hid = jnp.sin(fr * (_mm(zpos, w1.astype(f32), split=True) + b1.astype(f32)))
    hid = jnp.sin(fr * (_mm(hid, w2.astype(f32), split=True) + b2.astype(f32)))
    filt = _mm(hid, w3.astype(f32), split=True)
    deltas = jnp.linspace(math.log(HY_TARGET) / HY_FAST, math.log(HY_TARGET) / HY_SLOW, D_MODEL, dtype=f32)
    window = jnp.exp(-t_unit * jnp.abs(deltas))
    circ = jnp.where(u < L, filt[:, :D_MODEL], filt[:, D_MODEL:]) * window
    circ = jnp.where(u == L, 0.0, circ)
    return circ / jnp.sum(jnp.abs(circ), axis=0, keepdims=True)


PEER_N = PEER_KEYS * PEER_KEYS
GATE_LANES = 2 * LANES
PEER_ROUTE_TOKENS = 256
PEER_EXPERT_TOKENS = 512
ROW_TILE = 512
SEQ_TILE = 256


def _top_vals(s, n, with_rank):
    vals = []
    rank = jnp.full(s.shape, float(n), jnp.float32) if with_rank else None
    for a in range(n):
        m = jnp.max(s, axis=0, keepdims=True)
        vals.append(m)
        hit = s == m
        if with_rank:
            rank = jnp.where(hit, float(a), rank)
        s = jnp.where(hit, -jnp.inf, s)
    return vals, rank


def _peer_route_kernel(x_ref, g_ref, sh_ref, sc_ref, wq_hi_ref, wq_lo_ref, k_hi_ref, k_lo_ref,
                       n1_ref, e1_ref, r2_ref, e2_ref, h_ref, s_scr):
    tb = x_ref.shape[0]
    half = PEER_DKEY // 2
    n_tiles = tb // LANES
    h_hi, h_lo = _split_bf16(_rms_mod(x_ref[...], g_ref[...], sh_ref[0], sc_ref[0]))
    h_ref[...] = h_hi
    q_hi, q_lo = _split_bf16(_dot3(h_hi, h_lo, wq_hi_ref[...], wq_lo_ref[...]))
    for hh in range(PEER_HEADS):
        for p in range(2):
            cols = slice((2 * hh + p) * half, (2 * hh + p + 1) * half)
            k_hi = k_hi_ref[hh, p]
            s_scr[hh, p] = (lax.dot_general(k_hi, q_hi[:, cols], _NT, preferred_element_type=jnp.float32)
                            + lax.dot_general(k_lo_ref[hh, p], q_hi[:, cols], _NT, preferred_element_type=jnp.float32)
                            + lax.dot_general(k_hi, q_lo[:, cols], _NT, preferred_element_type=jnp.float32))

    K = PEER_TOPK
    G = SUBLANES

    def tile(it, carry):
        hh = it // n_tiles
        ln = pl.ds(pl.multiple_of((it % n_tiles) * LANES, LANES), LANES)
        s1 = s_scr[hh, 0, :, ln]
        s2 = s_scr[hh, 1, :, ln]
        v1, _ = _top_vals(s1, K, False)
        v2l, r2 = _top_vals(s2, K, True)
        v2 = jnp.concatenate(v2l, axis=0)
        cand = [v1[0] + v2[:G], v1[0] + v2[G:]]
        cand += [v1[a] + v2[:G] for a in range(1, G)]
        cand += [jnp.concatenate(v1[G:], axis=0) + v2[0:1]]
        c = cand
        for k in range(K):
            m = c[0]
            for ci in c[1:]:
                m = jnp.maximum(m, ci)
            m = jnp.max(m, axis=0, keepdims=True)
            if k + 1 < K:
                c = [jnp.where(ci == m, -jnp.inf, ci) for ci in c]
        tau = m
        top = v1[0] + v2[0:1]
        keep = [ci >= tau for ci in cand]
        z = jnp.zeros_like(tau)
        for ci, ki in zip(cand, keep):
            z = z + jnp.sum(jnp.where(ki, jnp.exp(ci - top), 0.0), axis=0, keepdims=True)
        cnt = [jnp.sum(jnp.where(ki, 1.0, 0.0), axis=0, keepdims=True) for ki in keep[:G + 1]]
        tail = jnp.where(keep[G + 1], 1.0, 0.0)
        n_a = [cnt[0] + cnt[1]] + cnt[2:] + [tail[a:a + 1] for a in range(G)]
        n1 = jnp.zeros_like(s1)
        for a in range(K):
            n1 = jnp.where(s1 == v1[a], n_a[a], n1)
        n1_ref[hh, :, ln] = n1
        e1_ref[hh, :, ln] = jnp.exp(s1 - v1[0]) / z
        r2_ref[hh, :, ln] = r2.astype(jnp.bfloat16)
        e2_ref[hh, :, ln] = jnp.exp(s2 - v2[0:1]).astype(jnp.bfloat16)
        return carry

    lax.fori_loop(0, PEER_HEADS * n_tiles, tile, 0, unroll=2)


def peer_route(x, g, shift_t, scale_t, wq_hi, wq_lo, k_hi, k_lo, tb=PEER_ROUTE_TOKENS):
    T = x.shape[0]
    nh = PEER_HEADS
    blk = pl.BlockSpec((nh, PEER_KEYS, tb), lambda t: (0, 0, t))
    sd = lambda dt: jax.ShapeDtypeStruct((nh, PEER_KEYS, T), dt)
    whole = lambda a: pl.BlockSpec(a.shape, lambda t: (0,) * a.ndim)
    mod = pl.BlockSpec((1, 1, D_MODEL), lambda t: (t, 0, 0))
    return pl.pallas_call(
        _peer_route_kernel,
        grid=(T // tb,),
        in_specs=[pl.BlockSpec((tb, D_MODEL), lambda t: (t, 0)), whole(g), mod, mod,
                  whole(wq_hi), whole(wq_lo), whole(k_hi), whole(k_lo)],
        out_specs=[blk, blk, blk, blk, pl.BlockSpec((tb, D_MODEL), lambda t: (t, 0))],
        out_shape=[sd(jnp.float32), sd(jnp.float32), sd(jnp.bfloat16), sd(jnp.bfloat16),
                   jax.ShapeDtypeStruct((T, D_MODEL), jnp.bfloat16)],
        scratch_shapes=[pltpu.VMEM((nh, 2, PEER_KEYS, tb), jnp.float32)],
        compiler_params=pltpu.CompilerParams(dimension_semantics=("arbitrary",),
                                             vmem_limit_bytes=VMEM_LIMIT),
        name="peer_route",
    )(x, g, shift_t, scale_t, wq_hi, wq_lo, k_hi, k_lo)


def _gelu_tanh(x):
    hx = 0.5 * x
    return hx * jnp.tanh(x * (x * x * (0.7978845608028654 * 0.044715) + 0.7978845608028654)) + hx


def _peer_expert_kernel(h_ref, u_ref, vt_even_ref, vt_prev_ref, vt_last_ref, n1_ref, e1_ref, r2_ref, e2_ref,
                        x_ref, gate_ref, o_ref, acc_ref, a0_scr, a1_scr, w0_scr, w1_scr):
    c = pl.program_id(1)
    ec, tb = a0_scr.shape
    n_i = ec // PEER_KEYS
    bf16 = jnp.bfloat16
    f32 = jnp.float32

    @pl.when(c == 0)
    def _():
        acc_ref[...] = jnp.zeros_like(acc_ref)
        w1_scr[...] = jnp.zeros_like(w1_scr)

    def gate_times_act(a_scr, w_scr, chunk):
        igrp = pl.ds(pl.multiple_of(chunk * n_i, SUBLANES), SUBLANES)
        for lt in range(tb // GATE_LANES):
            ln = slice(lt * GATE_LANES, (lt + 1) * GATE_LANES)
            for ii in range(n_i):
                rows = slice(ii * PEER_KEYS, (ii + 1) * PEER_KEYS)
                g = jnp.zeros((PEER_KEYS, GATE_LANES), bf16)
                for hh in range(PEER_HEADS):
                    n1 = jnp.broadcast_to(n1_ref[hh, igrp, ln][ii:ii + 1], (PEER_KEYS, GATE_LANES)).astype(bf16)
                    e1 = jnp.broadcast_to(e1_ref[hh, igrp, ln][ii:ii + 1], (PEER_KEYS, GATE_LANES)).astype(bf16)
                    g = g + jnp.where(r2_ref[hh, :, ln] < n1, e2_ref[hh, :, ln] * e1, jnp.zeros((), bf16))
                w_scr[rows, ln] = g * _gelu_tanh(a_scr[rows, ln])

    h = h_ref[...]
    a0_scr[...] = lax.dot_general(u_ref[:ec], h, _NT, preferred_element_type=f32).astype(bf16)
    acc_ref[...] += jnp.dot(vt_prev_ref[...], w1_scr[...], preferred_element_type=f32)
    gate_times_act(a0_scr, w0_scr, 2 * c)
    a1_scr[...] = lax.dot_general(u_ref[ec:], h, _NT, preferred_element_type=f32).astype(bf16)
    acc_ref[...] += jnp.dot(vt_even_ref[...], w0_scr[...], preferred_element_type=f32)
    gate_times_act(a1_scr, w1_scr, 2 * c + 1)

    @pl.when(c == pl.num_programs(1) - 1)
    def _():
        y = acc_ref[...] + jnp.dot(vt_last_ref[...], w1_scr[...], preferred_element_type=f32)
        o_ref[...] = x_ref[...] + gate_ref[0] * y.T


def peer_experts(h_bf16, u_bf16, vt_bf16, n1, e1, r2, e2, x, gate_t, tb=PEER_EXPERT_TOKENS, ec=SUBLANES * PEER_KEYS):
    T = h_bf16.shape[0]
    n_steps = PEER_N // (2 * ec)
    rblk = pl.BlockSpec((PEER_HEADS, PEER_KEYS, tb), lambda t, c: (0, 0, t))
    vt_blk = lambda chunk_of: pl.BlockSpec((D_MODEL, ec), lambda t, c: (0, chunk_of(c)))
    return pl.pallas_call(
        _peer_expert_kernel,
        grid=(T // tb, n_steps),
        in_specs=[pl.BlockSpec((tb, D_MODEL), lambda t, c: (t, 0)),
                  pl.BlockSpec((2 * ec, D_MODEL), lambda t, c: (c, 0)),
                  vt_blk(lambda c: 2 * c),
                  vt_blk(lambda c: jnp.maximum(2 * c - 1, 0)),
                  vt_blk(lambda c: 2 * n_steps - 1),
                  rblk, rblk, rblk, rblk,
                  pl.BlockSpec((tb, D_MODEL), lambda t, c: (t, 0)),
                  pl.BlockSpec((1, 1, D_MODEL), lambda t, c: (t, 0, 0))],
        out_specs=pl.BlockSpec((tb, D_MODEL), lambda t, c: (t, 0)),
        out_shape=jax.ShapeDtypeStruct((T, D_MODEL), jnp.float32),
        scratch_shapes=[pltpu.VMEM((D_MODEL, tb), jnp.float32)] + [pltpu.VMEM((ec, tb), jnp.bfloat16)] * 4,
        compiler_params=pltpu.CompilerParams(dimension_semantics=("arbitrary", "arbitrary"),
                                             vmem_limit_bytes=VMEM_LIMIT),
        name="peer_experts",
    )(h_bf16, u_bf16, vt_bf16, vt_bf16, vt_bf16, n1, e1, r2, e2, x, gate_t)


def peer_weights(w_q, sub_keys, u_tab, v_tab):
    return _split_bf16(w_q) + _split_bf16(sub_keys) + (u_tab.astype(jnp.bfloat16), v_tab.T.astype(jnp.bfloat16))


def peer_block(x, g, mods_route, mods_expert, weights):
    wq_hi, wq_lo, k_hi, k_lo, u_bf16, vt_bf16 = weights
    n1, e1, r2, e2, h_bf16 = peer_route(x, g[None], mods_route[0], mods_route[1], wq_hi, wq_lo, k_hi, k_lo)
    return peer_experts(h_bf16, u_bf16, vt_bf16, n1, e1, r2, e2, x, mods_expert)


def kernel(x_prompt, x_sample, state_rwkv_fwd, state_rwkv_bwd, cache_mla_ckv, cache_mla_kpe, c, c_ctx,
           norm_g, w_mod, b_mod, ab_w_in, rw_mu, rw_w0, rw_w2, rw_a0, rw_a2, rw_g2, rw_k_k, rw_k_a, rw_r_k,
           rw_lnx_g, rw_lnx_b, mla_q_norm, mla_q_up, mla_kv_norm, mla_kv_up, mla_qn, mla_kn, ab_w_out,
           hy_w_in, hy_b_in, hy_conv_w, hy_conv_b, hy_f_w1, hy_f_b1, hy_f_w2, hy_f_b2, hy_f_w3, hy_f_freq,
           hy_bias, hy_w_out, peer_w_q, peer_keys, peer_u, peer_v):
    rope = axial_rope(x_sample.shape[1])
    D = D_MODEL
    groups = [dict(x=x_prompt.reshape(-1, D), B=x_prompt.shape[0], L=x_prompt.shape[1]),
              dict(x=x_sample.reshape(-1, D), B=x_sample.shape[0], L=x_sample.shape[1])]
    st_f, st_b, st_ckv, st_kpe = [], [], [], []
    for li in range(DEPTH):
        j = li // 2
        groups[0]["mod"] = (_mm(jax.nn.silu(c_ctx)[None], w_mod[li]) + b_mod[li]).reshape(1, 6, D)
        groups[1]["mod"] = (_mm(jax.nn.silu(c), w_mod[li]) + b_mod[li]).reshape(-1, 6, D)
        peer_w = peer_weights(peer_w_q[li], peer_keys[li], peer_u[li], peer_v[li])
        for gi, g in enumerate(groups):
            x, B, L = g["x"], g["B"], g["L"]
            mod = lambda i, tm: tile_mod(g["mod"][:, i], B * L, tm)
            if li % 2 == 0:
                proj = norm_mod_matmul(x, norm_g[li, 0], mod(0, ROW_TILE), mod(1, ROW_TILE), ab_w_in[j],
                                       jnp.zeros((ab_w_in.shape[-1],), jnp.float32), ROW_TILE)
                prep = rwkv_prep(proj, L, rw_mu[j], rw_w0[j], rw_w2[j], rw_a0[j], rw_a2[j], rw_g2[j],
                                 rw_k_k[j], rw_k_a[j], rw_r_k[j], SEQ_TILE)
                s0s = None if gi == 0 else (state_rwkv_fwd[:, j], state_rwkv_bwd[:, j])
                ys, (sf, sb) = rwkv_scans(prep[:9], B, L, s0s)
                ctx = None if gi == 0 else (cache_mla_ckv[:, j], cache_mla_kpe[:, j], rope)
                att, ckv, kpe = mla_mixer(proj[:, RW_IN:].reshape(B, L, -1), ctx,
                                          (mla_q_norm[j], mla_q_up[j], mla_kv_norm[j], mla_kv_up[j], mla_qn[j], mla_kn[j]))
                if gi == 0:
                    st_f.append(sf)
                    st_b.append(sb)
                    st_ckv.append(ckv)
                    st_kpe.append(kpe)
                x = rwkv_out_residual(ys[0], ys[1], prep[10], prep[9], att.reshape(B * L, -1), rw_lnx_g[j], rw_lnx_b[j],
                                      ab_w_out[j], x, mod(2, ROW_TILE), ROW_TILE)
            else:
                proj = norm_mod_matmul(x, norm_g[li, 0], mod(0, ROW_TILE), mod(1, ROW_TILE), hy_w_in[j], hy_b_in[j],
                                       ROW_TILE)
                x0, zin = hyena_prep(proj, L, hy_conv_w[j], hy_conv_b[j], SEQ_TILE)
                circ = hyena_filters(L, hy_f_w1[j], hy_f_b1[j], hy_f_w2[j], hy_f_b2[j], hy_f_w3[j], hy_f_freq[j])
                conv = hyena_longconv(zin.reshape(B, L, D), circ).reshape(B * L, D)
                x = hyena_out_residual(x0, conv, zin, hy_bias[j], hy_w_out[j], x, mod(2, ROW_TILE), ROW_TILE)
            g["x"] = peer_block(x, norm_g[li, 1], (mod(3, PEER_ROUTE_TOKENS), mod(4, PEER_ROUTE_TOKENS)),
                                mod(5, PEER_EXPERT_TOKENS), peer_w)
    xp = groups[0]["x"].reshape(x_prompt.shape)
    xs = groups[1]["x"].reshape(x_sample.shape)
    new_state_rwkv_fwd = jnp.stack(st_f, axis=1).astype(x_prompt.dtype)
    new_state_rwkv_bwd = jnp.stack(st_b, axis=1).astype(x_prompt.dtype)
    new_cache_mla_ckv = jnp.stack(st_ckv, axis=1)
    new_cache_mla_kpe = jnp.stack(st_kpe, axis=1)
    return (xp, xs, new_state_rwkv_fwd, new_state_rwkv_bwd, new_cache_mla_ckv, new_cache_mla_kpe)
```

```python
import math
from functools import lru_cache, partial

import jax
import jax.numpy as jnp
import numpy as np
from jax import lax
from jax.experimental import pallas as pl
from jax.experimental.pallas import tpu as pltpu

D_MODEL = 1024
DEPTH = 2
GRID_W = 64
EPS = 1e-6
RW_HEADS = 8
RW_HEAD = 64
RW_DIM = RW_HEADS * RW_HEAD
W_LORA = 64
A_LORA = 64
G_LORA = 128
LNX_EPS = 64e-5
RW_IN = 3 * RW_DIM + W_LORA + A_LORA + G_LORA
RW_SPLITS = (RW_DIM, 2 * RW_DIM, 3 * RW_DIM, 3 * RW_DIM + W_LORA, 3 * RW_DIM + W_LORA + A_LORA)
MLA_HEADS = 4
QK_NOPE = 128
QK_ROPE = 64
QK_HEAD = QK_NOPE + QK_ROPE
V_HEAD = 128
Q_LORA = 256
KV_LORA = 128
MLA_DIM = MLA_HEADS * V_HEAD
ROPE_THETA = 10000.0
HY_BANDS = 16
HY_TARGET = 1e-2
HY_FAST = 0.3
HY_SLOW = 1.5
PEER_KEYS = 128
PEER_HEADS = 8
PEER_DKEY = 256
PEER_TOPK = 16

LANES = 128
SUBLANES = 8
VMEM_LIMIT = 56 * 1024 * 1024
_NT = (((1,), (1,)), ((), ()))


def _split_bf16(x):
    hi = x.astype(jnp.bfloat16)
    lo = (x - hi.astype(jnp.float32)).astype(jnp.bfloat16)
    return hi, lo


def _dot3(a_hi, a_lo, b_hi, b_lo):
    f32 = jnp.float32
    return (jnp.dot(a_hi, b_hi, preferred_element_type=f32) + jnp.dot(a_lo, b_hi, preferred_element_type=f32)
            + jnp.dot(a_hi, b_lo, preferred_element_type=f32))


def _mm_kernel(a_ref, b_ref, o_ref, *, split):
    if split:
        a_hi, a_lo = _split_bf16(a_ref[...])
        b_hi, b_lo = _split_bf16(b_ref[...])
        o_ref[...] = _dot3(a_hi, a_lo, b_hi, b_lo)
    else:
        o_ref[...] = jnp.dot(a_ref[...].astype(jnp.bfloat16), b_ref[...].astype(jnp.bfloat16),
                             preferred_element_type=jnp.float32)


def _mm(a, b, tm=512, tn=512, split=False):
    lead = a.shape[:-1]
    K = a.shape[-1]
    N = b.shape[-1]
    a2 = a.reshape(-1, K)
    if K % LANES:
        kp = -K % LANES
        a2 = jnp.pad(a2, ((0, 0), (0, kp)))
        b = jnp.pad(b, ((0, kp), (0, 0)))
        K += kp
    M = a2.shape[0]
    tm = min(tm, M)
    tn = min(tn, N)
    if N % tn:
        tn = N
    assert M % tm == 0 and N % tn == 0
    out = pl.pallas_call(
        partial(_mm_kernel, split=split),
        grid=(M // tm, N // tn),
        in_specs=[pl.BlockSpec((tm, K), lambda i, j: (i, 0)),
                  pl.BlockSpec((K, tn), lambda i, j: (0, j))],
        out_specs=pl.BlockSpec((tm, tn), lambda i, j: (i, j)),
        out_shape=jax.ShapeDtypeStruct((M, N), jnp.float32),
        name="matmul",
    )(a2, b)
    return out.reshape(*lead, N)


def rmsnorm(x, g):
    xf = x.astype(jnp.float32)
    y = xf * lax.rsqrt(jnp.mean(xf * xf, axis=-1, keepdims=True) + EPS)
    return (y * g.astype(jnp.float32)).astype(x.dtype)


def axial_rope(L):
    rows = L // GRID_W
    row = jnp.repeat(jnp.arange(rows, dtype=jnp.float32), GRID_W)
    col = jnp.tile(jnp.arange(GRID_W, dtype=jnp.float32), rows)
    n_freq = QK_ROPE // 4
    inv = ROPE_THETA ** (-jnp.arange(n_freq, dtype=jnp.float32) / n_freq)
    ang = jnp.concatenate([row[:, None] * inv, col[:, None] * inv], axis=-1)
    return jnp.cos(ang), jnp.sin(ang)


def rope_tail(x, cos, sin):
    xn, xr = x[..., :QK_NOPE], x[..., QK_NOPE:]
    x1, x2 = xr[..., 0::2], xr[..., 1::2]
    cs = cos[None, :, None, :].astype(x.dtype)
    sn = sin[None, :, None, :].astype(x.dtype)
    rot = jnp.stack([x1 * cs - x2 * sn, x1 * sn + x2 * cs], axis=-1).reshape(xr.shape)
    return jnp.concatenate([xn, rot], axis=-1)


def _mla_attn_kernel(q_ref, k_ref, v_ref, o_ref):
    scale = QK_HEAD ** -0.5
    for h in range(MLA_HEADS):
        q = q_ref[0, :, h * QK_HEAD:(h + 1) * QK_HEAD].astype(jnp.bfloat16)
        k = k_ref[0, :, h * QK_HEAD:(h + 1) * QK_HEAD].astype(jnp.bfloat16)
        s = lax.dot_general(q, k, _NT, preferred_element_type=jnp.float32) * scale
        m = jnp.max(s, axis=-1, keepdims=True)
        p = jnp.exp(s - m)
        l = jnp.sum(p, axis=-1, keepdims=True)
        v = v_ref[0, :, h * V_HEAD:(h + 1) * V_HEAD].astype(jnp.bfloat16)
        o = jnp.dot((p / l).astype(jnp.bfloat16), v, preferred_element_type=jnp.float32)
        o_ref[0, :, h * V_HEAD:(h + 1) * V_HEAD] = o


def attend(q, k, v, tq=256):
    B, Lq = q.shape[:2]
    Lk = k.shape[1]
    tq = min(tq, Lq)
    return pl.pallas_call(
        _mla_attn_kernel, grid=(B, Lq // tq),
        in_specs=[pl.BlockSpec((1, tq, MLA_HEADS * QK_HEAD), lambda b, i: (b, i, 0)),
                  pl.BlockSpec((1, Lk, MLA_HEADS * QK_HEAD), lambda b, i: (b, 0, 0)),
                  pl.BlockSpec((1, Lk, MLA_HEADS * V_HEAD), lambda b, i: (b, 0, 0))],
        out_specs=pl.BlockSpec((1, tq, MLA_HEADS * V_HEAD), lambda b, i: (b, i, 0)),
        out_shape=jax.ShapeDtypeStruct((B, Lq, MLA_HEADS * V_HEAD), jnp.float32),
        compiler_params=pltpu.CompilerParams(dimension_semantics=("arbitrary", "arbitrary"),
                                             vmem_limit_bytes=VMEM_LIMIT),
        name="mla_attend",
    )(q.reshape(B, Lq, -1), k.reshape(B, Lk, -1), v.reshape(B, Lk, -1))


def _first_row_block(fn):
    pl.when(pl.program_id(2) == 0)(fn)


def _dft_pair_kernel(c_hi, c_lo, s_hi, s_lo, x_ref, oc_ref, os_ref, x_hi, x_lo):
    def _():
        x_hi[...], x_lo[...] = _split_bf16(x_ref[0])
    _first_row_block(_)
    oc_ref[0] = _dot3(c_hi[...], c_lo[...], x_hi[...], x_lo[...])
    os_ref[0] = _dot3(s_hi[...], s_lo[...], x_hi[...], x_lo[...])


def _dft_spectral_kernel(c_tab, s_tab, x_ref, hr_ref, hi_ref, yr_ref, yi_ref, x16):
    def _():
        x16[...] = x_ref[0].astype(jnp.bfloat16)
    _first_row_block(_)
    rb, L = c_tab.shape
    zc = jnp.dot(c_tab[...], x16[...], preferred_element_type=jnp.float32)
    zs = jnp.dot(s_tab[...], x16[...], preferred_element_type=jnp.float32)
    hr = hr_ref[...]
    hi = hi_ref[...]
    f = pl.program_id(2) * rb + lax.broadcasted_iota(jnp.int32, zc.shape, 0)
    yr_ref[0] = jnp.where(f == 0, zc * hr * (0.5 / L), (zc * hr - zs * hi) * (1.0 / L))
    yi_ref[0] = jnp.where(f == 0, zs * hi * (0.5 / L), (zc * hi + zs * hr) * (1.0 / L))


def _dft_sum_kernel(c_tab, s_tab, x_ref, y_ref, o_ref, x16, y16):
    def _():
        x16[...] = x_ref[0].astype(jnp.bfloat16)
        y16[...] = y_ref[0].astype(jnp.bfloat16)
    _first_row_block(_)
    o_ref[0] = (jnp.dot(c_tab[...], x16[...], preferred_element_type=jnp.float32)
                + jnp.dot(s_tab[...], y16[...], preferred_element_type=jnp.float32))


@lru_cache(maxsize=None)
def dft_tables(L):
    f = np.arange(L, dtype=np.int64)
    ang = ((f[:, None] * f[None, :]) % (2 * L)).astype(np.float64) * (math.pi / L)
    c = np.cos(ang).astype(np.float32)
    s = (-np.sin(ang)).astype(np.float32)
    alt = np.where(f % 2 == 0, 1.0, -1.0).astype(np.float32)
    s_ana = np.where(f[:, None] == 0, alt[None, :], s)
    s_syn = np.where(f[None, :] == 0, alt[:, None], s)

    def split(x):
        hi = x.astype(jnp.bfloat16)
        return hi, (x - hi.astype(np.float32)).astype(jnp.bfloat16)

    cc = split(c)
    return cc + split(s_ana), cc + split(s_syn)


def _dft_call(body, tabs, xs, n_out, name, row_inputs=(), rb=256, nb=512):
    B, L, N = xs[0].shape
    rb = min(rb, L)
    nb = min(nb, N)
    tab = pl.BlockSpec((rb, L), lambda b, n, i: (i, 0))
    xin = pl.BlockSpec((1, L, nb), lambda b, n, i: (b, 0, n))
    rin = pl.BlockSpec((rb, nb), lambda b, n, i: (i, n))
    out = pl.BlockSpec((1, rb, nb), lambda b, n, i: (b, i, n))
    sd = jax.ShapeDtypeStruct((B, L, N), jnp.float32)
    return pl.pallas_call(
        body, grid=(B, N // nb, L // rb),
        in_specs=[tab] * len(tabs) + [xin] * len(xs) + [rin] * len(row_inputs),
        out_specs=[out] * n_out if n_out > 1 else out,
        out_shape=[sd] * n_out if n_out > 1 else sd,
        scratch_shapes=[pltpu.VMEM((L, nb), jnp.bfloat16)] * (len(tabs) // 2 * len(xs)),
        compiler_params=pltpu.CompilerParams(dimension_semantics=("arbitrary",) * 3,
                                             vmem_limit_bytes=VMEM_LIMIT),
        name=name,
    )(*tabs, *xs, *row_inputs)


def hyena_longconv(zin, circ):
    B, L, D = zin.shape
    tabs, tabs_syn = dft_tables(L)
    alt = jnp.where(jnp.arange(L) % 2 == 0, 1.0, -1.0).astype(jnp.float32)[:, None]
    fc, fs = _dft_call(_dft_pair_kernel, tabs, [jnp.concatenate([circ[:L], circ[L:]], axis=-1)[None]], 2, "dft_filter")
    hr = fc[0, :, :D] + alt * fc[0, :, D:]
    hi = fs[0, :, :D] + alt * fs[0, :, D:]
    yr, yi = _dft_call(_dft_spectral_kernel, tabs[0::2], [zin], 2, "dft_analysis", row_inputs=(hr, hi))
    return _dft_call(_dft_sum_kernel, tabs_syn[0::2], [yr, yi], 1, "dft_synthesis")


SCAN_ACCS = 4


def _rwkv_scan_kernel(rf_ref, kkf_ref, vf_ref, wf_ref, kdf_ref, bf_ref, rb_ref, kkb_ref, vb_ref, wb_ref, kdb_ref, bb_ref,
                      s0f_ref, s0b_ref, yf_ref, yb_ref, sff_ref, sfb_ref, sf_scr, sb_scr):
    c = pl.program_id(0)
    tc, nk, _ = rf_ref.shape
    nv = vf_ref.shape[1]
    groups = RW_HEAD // nk

    @pl.when(c == 0)
    def _():
        sf_scr[...] = s0f_ref[...]
        sb_scr[...] = s0b_ref[...]

    def row(ref, tt, k):
        return jnp.broadcast_to(ref[tt, pl.ds(k, 1), :], (nv, LANES))

    def all_groups(p):
        tot = p
        for q in range(1, groups):
            tot = tot + pltpu.roll(p, q * (LANES // groups), 1)
        return tot

    def one_step(tt, r_ref, kk_ref, v_ref, w_ref, kd_ref, b_ref, s_scr, y_ref):
        parts = [None] * SCAN_ACCS
        for k in range(nk):
            term = s_scr[k] * row(kk_ref, tt, k)
            parts[k % SCAN_ACCS] = term if parts[k % SCAN_ACCS] is None else parts[k % SCAN_ACCS] + term
        sa = -all_groups((parts[0] + parts[1]) + (parts[2] + parts[3]))
        vv = v_ref[tt]
        parts = [None] * SCAN_ACCS
        for k in range(nk):
            s_new = s_scr[k] * row(w_ref, tt, k) + sa * row(b_ref, tt, k) + vv * row(kd_ref, tt, k)
            s_scr[k] = s_new
            term = s_new * row(r_ref, tt, k)
            parts[k % SCAN_ACCS] = term if parts[k % SCAN_ACCS] is None else parts[k % SCAN_ACCS] + term
        y_ref[tt] = all_groups((parts[0] + parts[1]) + (parts[2] + parts[3]))

    def step(t, carry):
        one_step(t, rf_ref, kkf_ref, vf_ref, wf_ref, kdf_ref, bf_ref, sf_scr, yf_ref)
        one_step(tc - 1 - t, rb_ref, kkb_ref, vb_ref, wb_ref, kdb_ref, bb_ref, sb_scr, yb_ref)
        return carry

    lax.fori_loop(0, tc, step, 0)

    @pl.when(c == pl.num_programs(0) - 1)
    def _():
        sff_ref[...] = sf_scr[...]
        sfb_ref[...] = sb_scr[...]


def rwkv_scan(r, kk, v, wkb_fwd, wkb_bwd, s0_fwd, s0_bwd, tc=32):
    L, nk, _ = r.shape
    nv = v.shape[1]
    nc = L // tc
    fwd = lambda rows: pl.BlockSpec((tc, rows, LANES), lambda c: (c, 0, 0))
    bwd = lambda rows: pl.BlockSpec((tc, rows, LANES), lambda c: (nc - 1 - c, 0, 0))
    state = pl.BlockSpec((nk, nv, LANES), lambda c: (0, 0, 0))
    y_sd = jax.ShapeDtypeStruct((L, nv, LANES), jnp.float32)
    s_sd = jax.ShapeDtypeStruct((nk, nv, LANES), jnp.float32)
    return pl.pallas_call(
        _rwkv_scan_kernel,
        grid=(nc,),
        in_specs=[fwd(nk), fwd(nk), fwd(nv), fwd(nk), fwd(nk), fwd(nk),
                  bwd(nk), bwd(nk), bwd(nv), bwd(nk), bwd(nk), bwd(nk), state, state],
        out_specs=[fwd(nv), bwd(nv), state, state],
        out_shape=[y_sd, y_sd, s_sd, s_sd],
        scratch_shapes=[pltpu.VMEM((nk, nv, LANES), jnp.float32)] * 2,
        compiler_params=pltpu.CompilerParams(dimension_semantics=("arbitrary",), vmem_limit_bytes=VMEM_LIMIT),
        name="rwkv_scan",
    )(r, kk, v, *wkb_fwd, r, kk, v, *wkb_bwd, s0_fwd, s0_bwd)


def k_to_lanes(x, ksplit):
    B, L, H, N = x.shape
    nk = N // ksplit
    return x.reshape(B, L, H, ksplit, nk).transpose(1, 4, 3, 0, 2).reshape(L, nk, ksplit * B * H)


def v_to_lanes(x, ksplit):
    B, L, H, N = x.shape
    return jnp.tile(x.transpose(1, 3, 0, 2).reshape(L, N, B * H), (1, 1, ksplit))


def v_from_lanes(y, B, H):
    L, N, _ = y.shape
    return y[:, :, :B * H].reshape(L, N, B, H).transpose(2, 0, 3, 1)


def state_to_lanes(s, ksplit):
    B, H, N, K = s.shape
    nk = K // ksplit
    return s.reshape(B, H, N, ksplit, nk).transpose(4, 2, 3, 0, 1).reshape(nk, N, ksplit * B * H)


def state_from_lanes(s, B, H, ksplit):
    nk, N, _ = s.shape
    return s.reshape(nk, N, ksplit, B, H).transpose(3, 4, 1, 2, 0).reshape(B, H, N, ksplit * nk)


def rwkv_scan_both(rh, kk, vh, decay2, kd2, b2, s0_2, tc=32):
    B, L, H, N = rh.shape
    ksplit = LANES // (B * H)
    assert B * H * ksplit == LANES
    wkb = [[k_to_lanes(t[d], ksplit) for t in (decay2, kd2, b2)] for d in range(2)]
    y_f, y_b, sf_f, sf_b = rwkv_scan(k_to_lanes(rh, ksplit), k_to_lanes(kk, ksplit), v_to_lanes(vh, ksplit),
                                     wkb[0], wkb[1], state_to_lanes(s0_2[0], ksplit), state_to_lanes(s0_2[1], ksplit),
                                     tc=tc)
    return ([v_from_lanes(y_f, B, H), v_from_lanes(y_b, B, H)],
            [state_from_lanes(sf_f, B, H, ksplit), state_from_lanes(sf_b, B, H, ksplit)])


def _rms_mod(x, g, shift, scale):
    y = x * lax.rsqrt(jnp.mean(x * x, axis=-1, keepdims=True) + EPS)
    return (y * g) * (1.0 + scale) + shift


def tile_mod(m, n_rows, tm):
    return jnp.repeat(m, n_rows // m.shape[0] // tm, axis=0)[:, None, :]


def _norm_mod_matmul_kernel(x_ref, g_ref, sh_ref, sc_ref, w_ref, b_ref, o_ref):
    h = _rms_mod(x_ref[...], g_ref[...], sh_ref[0], sc_ref[0])
    o_ref[...] = jnp.dot(h.astype(jnp.bfloat16), w_ref[...], preferred_element_type=jnp.float32) + b_ref[...]


def norm_mod_matmul(x, g, shift_t, scale_t, w, b, tm):
    T, D = x.shape
    N = w.shape[1]
    mod = pl.BlockSpec((1, 1, D), lambda i: (i, 0, 0))
    return pl.pallas_call(
        _norm_mod_matmul_kernel, grid=(T // tm,),
        in_specs=[pl.BlockSpec((tm, D), lambda i: (i, 0)), pl.BlockSpec((1, D), lambda i: (0, 0)), mod, mod,
                  pl.BlockSpec((D, N), lambda i: (0, 0)), pl.BlockSpec((1, N), lambda i: (0, 0))],
        out_specs=pl.BlockSpec((tm, N), lambda i: (i, 0)),
        out_shape=jax.ShapeDtypeStruct((T, N), jnp.float32),
        compiler_params=pltpu.CompilerParams(dimension_semantics=("arbitrary",), vmem_limit_bytes=VMEM_LIMIT),
        name="norm_mod_matmul",
    )(x, g[None], shift_t, scale_t, w.astype(jnp.bfloat16), b[None])


def _head_sums(x, ones_bd):
    hi, lo = _split_bf16(x)
    return (jnp.dot(hi, ones_bd, preferred_element_type=jnp.float32)
            + jnp.dot(lo, ones_bd, preferred_element_type=jnp.float32))


def _with_neighbours(x_ref, prev_ref, next_ref, first_ref, last_ref, ncols):
    tm = x_ref.shape[0]
    x = x_ref[:, :ncols]
    row = lax.broadcasted_iota(jnp.int32, (tm, 1), 0)
    prev_row = prev_ref[SUBLANES - 1:SUBLANES, :ncols] * (1.0 - first_ref[0, :, :1])
    next_row = next_ref[0:1, :ncols] * (1.0 - last_ref[0, :, :1])
    prev = jnp.where(row == 0, prev_row, pltpu.roll(x, 1, 0))
    nxt = jnp.where(row == tm - 1, next_row, pltpu.roll(x, tm - 1, 0))
    return x, prev, nxt


def _neighbour_specs(T, C, L, tm):
    n_tiles = T // tm
    hb = tm // SUBLANES
    start = np.arange(n_tiles) * tm % L
    first = np.broadcast_to((start == 0).astype(np.float32)[:, None, None], (n_tiles, 1, LANES))
    last = np.broadcast_to((start + tm == L).astype(np.float32)[:, None, None], (n_tiles, 1, LANES))
    flag = pl.BlockSpec((1, 1, LANES), lambda i: (i, 0, 0))
    specs = [pl.BlockSpec((tm, C), lambda i: (i, 0)),
             pl.BlockSpec((SUBLANES, C), lambda i: (jnp.maximum(i * hb - 1, 0), 0)),
             pl.BlockSpec((SUBLANES, C), lambda i: (jnp.minimum((i + 1) * hb, T // SUBLANES - 1), 0)),
             flag, flag]
    return specs, (jnp.asarray(first), jnp.asarray(last))


def _rwkv_prep_kernel(proj_ref, prev_ref, next_ref, first_ref, last_ref, mu_ref, w0_ref, w2_ref, a0_ref, a2_ref,
                      g2_ref, kk_ref, ka_ref, rk_ref, ones_ref,
                      r_out, kk_out, v_out, w_out0, w_out1, kd_out0, kd_out1, b_out0, b_out1, gate_out, bonus_out):
    bf16 = jnp.bfloat16
    f32 = jnp.float32
    x, prev, nxt = _with_neighbours(proj_ref, prev_ref, next_ref, first_ref, last_ref, RW_IN)
    rw = x + mu_ref[...] * (0.5 * (prev + nxt) - x)
    r = rw[:, RW_SPLITS[0] - RW_DIM:RW_SPLITS[0]]
    k = rw[:, RW_SPLITS[0]:RW_SPLITS[1]]
    v = rw[:, RW_SPLITS[1]:RW_SPLITS[2]]
    wd = rw[:, RW_SPLITS[2]:RW_SPLITS[3]]
    ad = rw[:, RW_SPLITS[3]:RW_SPLITS[4]]
    gd = rw[:, RW_SPLITS[4]:]
    ones_bd = ones_ref[...]
    kk = k * kk_ref[...]
    kk = kk / jnp.maximum(jnp.sqrt(_head_sums(kk * kk, ones_bd)), 1e-12)
    tanh_wd = jnp.tanh(wd).astype(bf16)
    ad16 = ad.astype(bf16)
    rk = r * rk_ref[...]
    bonus = jnp.zeros_like(r)
    for d, (w_o, kd_o, b_o) in enumerate(((w_out0, kd_out0, b_out0), (w_out1, kd_out1, b_out1))):
        w_raw = w0_ref[d:d + 1, :] + jnp.dot(tanh_wd, w2_ref[d], preferred_element_type=f32)
        w_o[...] = jnp.exp(-jnp.exp(-jax.nn.softplus(-w_raw) - 0.5))
        lr = jax.nn.sigmoid(a0_ref[d:d + 1, :] + jnp.dot(ad16, a2_ref[d], preferred_element_type=f32))
        kd = k * (1.0 + (lr - 1.0) * ka_ref[...])
        kd_o[...] = kd
        b_o[...] = kk * lr
        bonus = bonus + _head_sums(rk * kd, ones_bd)
    r_out[...] = r
    kk_out[...] = kk
    v_out[...] = v
    gate_out[...] = jnp.dot(jax.nn.sigmoid(gd).astype(bf16), g2_ref[...], preferred_element_type=f32)
    bonus_out[...] = bonus * v


def _head_ones():
    h = np.arange(RW_DIM) // RW_HEAD
    return jnp.asarray(h[:, None] == h[None, :], jnp.bfloat16)


def rwkv_prep(proj, L, mu, w0, w2, a0, a2, g2, k_k, k_a, r_k, tm):
    T, C = proj.shape
    nb_specs, flags = _neighbour_specs(T, C, L, tm)
    row2 = lambda a: a.reshape(1, -1)
    whole = lambda a: pl.BlockSpec(a.shape, lambda i: (0,) * a.ndim)
    out = pl.BlockSpec((tm, RW_DIM), lambda i: (i, 0))
    consts = (row2(mu), w0, w2.astype(jnp.bfloat16), a0, a2.astype(jnp.bfloat16), g2.astype(jnp.bfloat16),
              row2(k_k), row2(k_a), row2(r_k), _head_ones())
    return pl.pallas_call(
        _rwkv_prep_kernel, grid=(T // tm,),
        in_specs=nb_specs + [whole(a) for a in consts],
        out_specs=[out] * 11,
        out_shape=[jax.ShapeDtypeStruct((T, RW_DIM), jnp.float32)] * 11,
        compiler_params=pltpu.CompilerParams(dimension_semantics=("arbitrary",), vmem_limit_bytes=VMEM_LIMIT),
        name="rwkv_prep",
    )(proj, proj, proj, *flags, *consts)


def _rwkv_out_kernel(y0_ref, y1_ref, bonus_ref, gate_ref, mla_ref, lng_ref, lnb_ref, ones_ref, w_ref, x_ref, mod_ref,
                     o_ref):
    bf16 = jnp.bfloat16
    f32 = jnp.float32
    ones_bd = ones_ref[...]
    y = y0_ref[...] + y1_ref[...]
    dlt = y - _head_sums(y, ones_bd) * (1.0 / RW_HEAD)
    var = _head_sums(dlt * dlt, ones_bd) * (1.0 / RW_HEAD)
    yn = dlt * lax.rsqrt(var + LNX_EPS) * lng_ref[...] + lnb_ref[...]
    rw_out = (yn + bonus_ref[...]) * gate_ref[...]
    upd = (jnp.dot(rw_out.astype(bf16), w_ref[:RW_DIM], preferred_element_type=f32)
           + jnp.dot(mla_ref[...].astype(bf16), w_ref[RW_DIM:], preferred_element_type=f32))
    o_ref[...] = x_ref[...] + mod_ref[0] * upd


def rwkv_out_residual(y0, y1, bonus, gate, mla_out, lnx_g, lnx_b, w_out, x, gate_t, tm):
    T, D = x.shape
    row = lambda n: pl.BlockSpec((tm, n), lambda i: (i, 0))
    whole = lambda a: pl.BlockSpec(a.shape, lambda i: (0,) * a.ndim)
    consts = (lnx_g.reshape(1, -1), lnx_b.reshape(1, -1), _head_ones(), w_out.astype(jnp.bfloat16))
    return pl.pallas_call(
        _rwkv_out_kernel, grid=(T // tm,),
        in_specs=[row(RW_DIM)] * 4 + [row(MLA_DIM)] + [whole(a) for a in consts]
                 + [row(D), pl.BlockSpec((1, 1, D), lambda i: (i, 0, 0))],
        out_specs=row(D),
        out_shape=jax.ShapeDtypeStruct((T, D), jnp.float32),
        compiler_params=pltpu.CompilerParams(dimension_semantics=("arbitrary",), vmem_limit_bytes=VMEM_LIMIT),
        name="rwkv_out_residual",
    )(y0, y1, bonus, gate, mla_out, *consts, x, gate_t)


def mla_keys(ckv, k_pe, kv_up, kn):
    B, L, _ = ckv.shape
    kv = _mm(ckv, kv_up).reshape(B, L, MLA_HEADS, QK_NOPE + V_HEAD)
    k_rope = jnp.broadcast_to(k_pe[:, :, None, :], (B, L, MLA_HEADS, QK_ROPE))
    k = rmsnorm(jnp.concatenate([kv[..., :QK_NOPE], k_rope], axis=-1), kn)
    return k, kv[..., QK_NOPE:]


def rwkv_scans(prep, B, L, s0s):
    heads = lambda t: t.reshape(B, L, RW_HEADS, RW_HEAD)
    r, kk, v, w_0, w_1, kd_0, kd_1, b_0, b_1 = [heads(t) for t in prep]
    if s0s is None:
        z = jnp.zeros((B, RW_HEADS, RW_HEAD, RW_HEAD), jnp.float32)
        s0s = (z, z)
    ys, finals = rwkv_scan_both(r, kk, v, [w_0, w_1], [kd_0, kd_1], [b_0, b_1], s0s)
    return [y.reshape(B * L, RW_DIM) for y in ys], finals


def mla_mixer(mla, ctx, p):
    q_norm, q_up, kv_norm, kv_up, qn, kn = p
    B, L, _ = mla.shape
    q_c, kv_c, k_pe = jnp.split(mla, (Q_LORA, Q_LORA + KV_LORA), axis=-1)
    q = rmsnorm(_mm(rmsnorm(q_c, q_norm), q_up).reshape(B, L, MLA_HEADS, QK_HEAD), qn)
    ckv = rmsnorm(kv_c, kv_norm)
    k_own, v_own = mla_keys(ckv, k_pe, kv_up, kn)
    if ctx is None:
        keys, vals = k_own, v_own
    else:
        cos, sin = ctx[2]
        q = rope_tail(q, cos, sin)
        k_own = rope_tail(k_own, cos, sin)
        k_ctx, v_ctx = mla_keys(ctx[0], ctx[1], kv_up, kn)
        keys = jnp.concatenate([k_ctx, k_own], axis=1)
        vals = jnp.concatenate([v_ctx, v_own], axis=1)
    return attend(q, keys, vals), ckv, k_pe


def _hyena_prep_kernel(proj_ref, prev_ref, next_ref, first_ref, last_ref, cw_ref, cb_ref, x0_ref, z_ref):
    D = x0_ref.shape[1]
    x, prev, nxt = _with_neighbours(proj_ref, prev_ref, next_ref, first_ref, last_ref, proj_ref.shape[1])
    u = prev * cw_ref[0:1, :] + x * cw_ref[1:2, :] + nxt * cw_ref[2:3, :] + cb_ref[...]
    x0_ref[...] = u[:, :D]
    z_ref[...] = u[:, D:2 * D] * u[:, 2 * D:]


def hyena_prep(proj, L, conv_w, conv_b, tm):
    T, C = proj.shape
    D = C // 3
    nb_specs, flags = _neighbour_specs(T, C, L, tm)
    out = pl.BlockSpec((tm, D), lambda i: (i, 0))
    return pl.pallas_call(
        _hyena_prep_kernel, grid=(T // tm,),
        in_specs=nb_specs + [pl.BlockSpec((3, C), lambda i: (0, 0)), pl.BlockSpec((1, C), lambda i: (0, 0))],
        out_specs=[out, out],
        out_shape=[jax.ShapeDtypeStruct((T, D), jnp.float32)] * 2,
        compiler_params=pltpu.CompilerParams(dimension_semantics=("arbitrary",), vmem_limit_bytes=VMEM_LIMIT),
        name="hyena_prep",
    )(proj, proj, proj, *flags, conv_w, conv_b[None])


def _hyena_out_kernel(x0_ref, conv_ref, z_ref, bias_ref, w_ref, x_ref, mod_ref, o_ref):
    a = x0_ref[...] * (conv_ref[...] + z_ref[...] * bias_ref[...])
    o_ref[...] = x_ref[...] + mod_ref[0] * jnp.dot(a.astype(jnp.bfloat16), w_ref[...],
                                                   preferred_element_type=jnp.float32)


def hyena_out_residual(x0, conv, zin, bias, w_out, x, gate_t, tm):
    T, D = x.shape
    row = pl.BlockSpec((tm, D), lambda i: (i, 0))
    return pl.pallas_call(
        _hyena_out_kernel, grid=(T // tm,),
        in_specs=[row, row, row, pl.BlockSpec((1, D), lambda i: (0, 0)), pl.BlockSpec((D, D), lambda i: (0, 0)),
                  row, pl.BlockSpec((1, 1, D), lambda i: (i, 0, 0))],
        out_specs=row,
        out_shape=jax.ShapeDtypeStruct((T, D), jnp.float32),
        compiler_params=pltpu.CompilerParams(dimension_semantics=("arbitrary",), vmem_limit_bytes=VMEM_LIMIT),
        name="hyena_out_residual",
    )(x0, conv, zin, bias[None], w_out.astype(jnp.bfloat16), x, gate_t)


def hyena_filters(L, w1, b1, w2, b2, w3, freq):
    f32 = jnp.float32
    u = jnp.arange(2 * L, dtype=jnp.int32)[:, None]
    t = jnp.where(u < L, u, 2 * L - u).astype(f32)
    t_unit = t / (L - 1)
    bands = jnp.linspace(1e-4, HY_BANDS - 1, HY_BANDS, dtype=f32)
    ang = 2.0 * math.pi * t * bands / L
    zpos = jnp.concatenate([t_unit, jnp.cos(ang), -jnp.sin(ang)], axis=-1)
    fr = freq.astype(f32)
    hid = jnp.sin(fr * (_mm(zpos, w1.astype(f32), split=True) + b1.astype(f32)))
    hid = jnp.sin(fr * (_mm(hid, w2.astype(f32), split=True) + b2.astype(f32)))
    filt = _mm(hid, w3.astype(f32), split=True)
    deltas = jnp.linspace(math.log(HY_TARGET) / HY_FAST, math.log(HY_TARGET) / HY_SLOW, D_MODEL, dtype=f32)
    window = jnp.exp(-t_unit * jnp.abs(deltas))
    circ = jnp.where(u < L, filt[:, :D_MODEL], filt[:, D_MODEL:]) * window
    circ = jnp.where(u == L, 0.0, circ)
    return circ / jnp.sum(jnp.abs(circ), axis=0, keepdims=True)


PEER_N = PEER_KEYS * PEER_KEYS
GATE_LANES = 2 * LANES
PEER_ROUTE_TOKENS = 256
PEER_EXPERT_TOKENS = 512
ROW_TILE = 512
SEQ_TILE = 256


def _top_vals(s, n, with_rank):
    vals = []
    rank = jnp.full(s.shape, float(n), jnp.float32) if with_rank else None
    for a in range(n):
        m = jnp.max(s, axis=0, keepdims=True)
        vals.append(m)
        hit = s == m
        if with_rank:
            rank = jnp.where(hit, float(a), rank)
        s = jnp.where(hit, -jnp.inf, s)
    return vals, rank


def _peer_route_kernel(x_ref, g_ref, sh_ref, sc_ref, wq_hi_ref, wq_lo_ref, k_hi_ref, k_lo_ref,
                       n1_ref, e1_ref, r2_ref, e2_ref, h_ref, s_scr):
    tb = x_ref.shape[0]
    half = PEER_DKEY // 2
    n_tiles = tb // LANES
    h_hi, h_lo = _split_bf16(_rms_mod(x_ref[...], g_ref[...], sh_ref[0], sc_ref[0]))
    h_ref[...] = h_hi
    q_hi, q_lo = _split_bf16(_dot3(h_hi, h_lo, wq_hi_ref[...], wq_lo_ref[...]))
    for hh in range(PEER_HEADS):
        for p in range(2):
            cols = slice((2 * hh + p) * half, (2 * hh + p + 1) * half)
            k_hi = k_hi_ref[hh, p]
            s_scr[hh, p] = (lax.dot_general(k_hi, q_hi[:, cols], _NT, preferred_element_type=jnp.float32)
                            + lax.dot_general(k_lo_ref[hh, p], q_hi[:, cols], _NT, preferred_element_type=jnp.float32)
                            + lax.dot_general(k_hi, q_lo[:, cols], _NT, preferred_element_type=jnp.float32))

    K = PEER_TOPK
    G = SUBLANES

    def tile(it, carry):
        hh = it // n_tiles
        ln = pl.ds(pl.multiple_of((it % n_tiles) * LANES, LANES), LANES)
        s1 = s_scr[hh, 0, :, ln]
        s2 = s_scr[hh, 1, :, ln]
        v1, _ = _top_vals(s1, K, False)
        v2l, r2 = _top_vals(s2, K, True)
        v2 = jnp.concatenate(v2l, axis=0)
        cand = [v1[0] + v2[:G], v1[0] + v2[G:]]
        cand += [v1[a] + v2[:G] for a in range(1, G)]
        cand += [jnp.concatenate(v1[G:], axis=0) + v2[0:1]]
        c = cand
        for k in range(K):
            m = c[0]
            for ci in c[1:]:
                m = jnp.maximum(m, ci)
            m = jnp.max(m, axis=0, keepdims=True)
            if k + 1 < K:
                c = [jnp.where(ci == m, -jnp.inf, ci) for ci in c]
        tau = m
        top = v1[0] + v2[0:1]
        keep = [ci >= tau for ci in cand]
        z = jnp.zeros_like(tau)
        for ci, ki in zip(cand, keep):
            z = z + jnp.sum(jnp.where(ki, jnp.exp(ci - top), 0.0), axis=0, keepdims=True)
        cnt = [jnp.sum(jnp.where(ki, 1.0, 0.0), axis=0, keepdims=True) for ki in keep[:G + 1]]
        tail = jnp.where(keep[G + 1], 1.0, 0.0)
        n_a = [cnt[0] + cnt[1]] + cnt[2:] + [tail[a:a + 1] for a in range(G)]
        n1 = jnp.zeros_like(s1)
        for a in range(K):
            n1 = jnp.where(s1 == v1[a], n_a[a], n1)
        n1_ref[hh, :, ln] = n1
        e1_ref[hh, :, ln] = jnp.exp(s1 - v1[0]) / z
        r2_ref[hh, :, ln] = r2.astype(jnp.bfloat16)
        e2_ref[hh, :, ln] = jnp.exp(s2 - v2[0:1]).astype(jnp.bfloat16)
        return carry

    lax.fori_loop(0, PEER_HEADS * n_tiles, tile, 0, unroll=2)


def peer_route(x, g, shift_t, scale_t, wq_hi, wq_lo, k_hi, k_lo, tb=PEER_ROUTE_TOKENS):
    T = x.shape[0]
    nh = PEER_HEADS
    blk = pl.BlockSpec((nh, PEER_KEYS, tb), lambda t: (0, 0, t))
    sd = lambda dt: jax.ShapeDtypeStruct((nh, PEER_KEYS, T), dt)
    whole = lambda a: pl.BlockSpec(a.shape, lambda t: (0,) * a.ndim)
    mod = pl.BlockSpec((1, 1, D_MODEL), lambda t: (t, 0, 0))
    return pl.pallas_call(
        _peer_route_kernel,
        grid=(T // tb,),
        in_specs=[pl.BlockSpec((tb, D_MODEL), lambda t: (t, 0)), whole(g), mod, mod,
                  whole(wq_hi), whole(wq_lo), whole(k_hi), whole(k_lo)],
        out_specs=[blk, blk, blk, blk, pl.BlockSpec((tb, D_MODEL), lambda t: (t, 0))],
        out_shape=[sd(jnp.float32), sd(jnp.float32), sd(jnp.bfloat16), sd(jnp.bfloat16),
                   jax.ShapeDtypeStruct((T, D_MODEL), jnp.bfloat16)],
        scratch_shapes=[pltpu.VMEM((nh, 2, PEER_KEYS, tb), jnp.float32)],
        compiler_params=pltpu.CompilerParams(dimension_semantics=("arbitrary",),
                                             vmem_limit_bytes=VMEM_LIMIT),
        name="peer_route",
    )(x, g, shift_t, scale_t, wq_hi, wq_lo, k_hi, k_lo)


def _gelu_tanh(x):
    hx = 0.5 * x
    return hx * jnp.tanh(x * (x * x * (0.7978845608028654 * 0.044715) + 0.7978845608028654)) + hx


def _peer_expert_kernel(h_ref, u_ref, vt_even_ref, vt_prev_ref, vt_last_ref, n1_ref, e1_ref, r2_ref, e2_ref,
                        x_ref, gate_ref, o_ref, acc_ref, a0_scr, a1_scr, w0_scr, w1_scr):
    c = pl.program_id(1)
    ec, tb = a0_scr.shape
    n_i = ec // PEER_KEYS
    bf16 = jnp.bfloat16
    f32 = jnp.float32

    @pl.when(c == 0)
    def _():
        acc_ref[...] = jnp.zeros_like(acc_ref)
        w1_scr[...] = jnp.zeros_like(w1_scr)

    def gate_times_act(a_scr, w_scr, chunk):
        igrp = pl.ds(pl.multiple_of(chunk * n_i, SUBLANES), SUBLANES)
        for lt in range(tb // GATE_LANES):
            ln = slice(lt * GATE_LANES, (lt + 1) * GATE_LANES)
            for ii in range(n_i):
                rows = slice(ii * PEER_KEYS, (ii + 1) * PEER_KEYS)
                g = jnp.zeros((PEER_KEYS, GATE_LANES), bf16)
                for hh in range(PEER_HEADS):
                    n1 = jnp.broadcast_to(n1_ref[hh, igrp, ln][ii:ii + 1], (PEER_KEYS, GATE_LANES)).astype(bf16)
                    e1 = jnp.broadcast_to(e1_ref[hh, igrp, ln][ii:ii + 1], (PEER_KEYS, GATE_LANES)).astype(bf16)
                    g = g + jnp.where(r2_ref[hh, :, ln] < n1, e2_ref[hh, :, ln] * e1, jnp.zeros((), bf16))
                w_scr[rows, ln] = g * _gelu_tanh(a_scr[rows, ln])

    h = h_ref[...]
    a0_scr[...] = lax.dot_general(u_ref[:ec], h, _NT, preferred_element_type=f32).astype(bf16)
    acc_ref[...] += jnp.dot(vt_prev_ref[...], w1_scr[...], preferred_element_type=f32)
    gate_times_act(a0_scr, w0_scr, 2 * c)
    a1_scr[...] = lax.dot_general(u_ref[ec:], h, _NT, preferred_element_type=f32).astype(bf16)
    acc_ref[...] += jnp.dot(vt_even_ref[...], w0_scr[...], preferred_element_type=f32)
    gate_times_act(a1_scr, w1_scr, 2 * c + 1)

    @pl.when(c == pl.num_programs(1) - 1)
    def _():
        y = acc_ref[...] + jnp.dot(vt_last_ref[...], w1_scr[...], preferred_element_type=f32)
        o_ref[...] = x_ref[...] + gate_ref[0] * y.T


def peer_experts(h_bf16, u_bf16, vt_bf16, n1, e1, r2, e2, x, gate_t, tb=PEER_EXPERT_TOKENS, ec=SUBLANES * PEER_KEYS):
    T = h_bf16.shape[0]
    n_steps = PEER_N // (2 * ec)
    rblk = pl.BlockSpec((PEER_HEADS, PEER_KEYS, tb), lambda t, c: (0, 0, t))
    vt_blk = lambda chunk_of: pl.BlockSpec((D_MODEL, ec), lambda t, c: (0, chunk_of(c)))
    return pl.pallas_call(
        _peer_expert_kernel,
        grid=(T // tb, n_steps),
        in_specs=[pl.BlockSpec((tb, D_MODEL), lambda t, c: (t, 0)),
                  pl.BlockSpec((2 * ec, D_MODEL), lambda t, c: (c, 0)),
                  vt_blk(lambda c: 2 * c),
                  vt_blk(lambda c: jnp.maximum(2 * c - 1, 0)),
                  vt_blk(lambda c: 2 * n_steps - 1),
                  rblk, rblk, rblk, rblk,
                  pl.BlockSpec((tb, D_MODEL), lambda t, c: (t, 0)),
                  pl.BlockSpec((1, 1, D_MODEL), lambda t, c: (t, 0, 0))],
        out_specs=pl.BlockSpec((tb, D_MODEL), lambda t, c: (t, 0)),
        out_shape=jax.ShapeDtypeStruct((T, D_MODEL), jnp.float32),
        scratch_shapes=[pltpu.VMEM((D_MODEL, tb), jnp.float32)] + [pltpu.VMEM((ec, tb), jnp.bfloat16)] * 4,
        compiler_params=pltpu.CompilerParams(dimension_semantics=("arbitrary", "arbitrary"),
                                             vmem_limit_bytes=VMEM_LIMIT),
        name="peer_experts",
    )(h_bf16, u_bf16, vt_bf16, vt_bf16, vt_bf16, n1, e1, r2, e2, x, gate_t)


def peer_weights(w_q, sub_keys, u_tab, v_tab):
    return _split_bf16(w_q) + _split_bf16(sub_keys) + (u_tab.astype(jnp.bfloat16), v_tab.T.astype(jnp.bfloat16))


def peer_block(x, g, mods_route, mods_expert, weights):
    wq_hi, wq_lo, k_hi, k_lo, u_bf16, vt_bf16 = weights
    n1, e1, r2, e2, h_bf16 = peer_route(x, g[None], mods_route[0], mods_route[1], wq_hi, wq_lo, k_hi, k_lo)
    return peer_experts(h_bf16, u_bf16, vt_bf16, n1, e1, r2, e2, x, mods_expert)


def kernel(x_prompt, x_sample, state_rwkv_fwd, state_rwkv_bwd, cache_mla_ckv, cache_mla_kpe, c, c_ctx,
           norm_g, w_mod, b_mod, ab_w_in, rw_mu, rw_w0, rw_w2, rw_a0, rw_a2, rw_g2, rw_k_k, rw_k_a, rw_r_k,
           rw_lnx_g, rw_lnx_b, mla_q_norm, mla_q_up, mla_kv_norm, mla_kv_up, mla_qn, mla_kn, ab_w_out,
           hy_w_in, hy_b_in, hy_conv_w, hy_conv_b, hy_f_w1, hy_f_b1, hy_f_w2, hy_f_b2, hy_f_w3, hy_f_freq,
           hy_bias, hy_w_out, peer_w_q, peer_keys, peer_u, peer_v):
    rope = axial_rope(x_sample.shape[1])
    D = D_MODEL
    groups = [dict(x=x_prompt.reshape(-1, D), B=x_prompt.shape[0], L=x_prompt.shape[1]),
              dict(x=x_sample.reshape(-1, D), B=x_sample.shape[0], L=x_sample.shape[1])]
    st_f, st_b, st_ckv, st_kpe = [], [], [], []
    for li in range(DEPTH):
        j = li // 2
        groups[0]["mod"] = (_mm(jax.nn.silu(c_ctx)[None], w_mod[li]) + b_mod[li]).reshape(1, 6, D)
        groups[1]["mod"] = (_mm(jax.nn.silu(c), w_mod[li]) + b_mod[li]).reshape(-1, 6, D)
        peer_w = peer_weights(peer_w_q[li], peer_keys[li], peer_u[li], peer_v[li])
        for gi, g in enumerate(groups):
            x, B, L = g["x"], g["B"], g["L"]
            mod = lambda i, tm: tile_mod(g["mod"][:, i], B * L, tm)
            if li % 2 == 0:
                proj = norm_mod_matmul(x, norm_g[li, 0], mod(0, ROW_TILE), mod(1, ROW_TILE), ab_w_in[j],
                                       jnp.zeros((ab_w_in.shape[-1],), jnp.float32), ROW_TILE)
                prep = rwkv_prep(proj, L, rw_mu[j], rw_w0[j], rw_w2[j], rw_a0[j], rw_a2[j], rw_g2[j],
                                 rw_k_k[j], rw_k_a[j], rw_r_k[j], SEQ_TILE)
                s0s = None if gi == 0 else (state_rwkv_fwd[:, j], state_rwkv_bwd[:, j])
                ys, (sf, sb) = rwkv_scans(prep[:9], B, L, s0s)
                ctx = None if gi == 0 else (cache_mla_ckv[:, j], cache_mla_kpe[:, j], rope)
                att, ckv, kpe = mla_mixer(proj[:, RW_IN:].reshape(B, L, -1), ctx,
                                          (mla_q_norm[j], mla_q_up[j], mla_kv_norm[j], mla_kv_up[j], mla_qn[j], mla_kn[j]))
                if gi == 0:
                    st_f.append(sf)
                    st_b.append(sb)
                    st_ckv.append(ckv)
                    st_kpe.append(kpe)
                x = rwkv_out_residual(ys[0], ys[1], prep[10], prep[9], att.reshape(B * L, -1), rw_lnx_g[j], rw_lnx_b[j],
                                      ab_w_out[j], x, mod(2, ROW_TILE), ROW_TILE)
            else:
                proj = norm_mod_matmul(x, norm_g[li, 0], mod(0, ROW_TILE), mod(1, ROW_TILE), hy_w_in[j], hy_b_in[j],
                                       ROW_TILE)
                x0, zin = hyena_prep(proj, L, hy_conv_w[j], hy_conv_b[j], SEQ_TILE)
                circ = hyena_filters(L, hy_f_w1[j], hy_f_b1[j], hy_f_w2[j], hy_f_b2[j], hy_f_w3[j], hy_f_freq[j])
                conv = hyena_longconv(zin.reshape(B, L, D), circ).reshape(B * L, D)
                x = hyena_out_residual(x0, conv, zin, hy_bias[j], hy_w_out[j], x, mod(2, ROW_TILE), ROW_TILE)
            g["x"] = peer_block(x, norm_g[li, 1], (mod(3, PEER_ROUTE_TOKENS), mod(4, PEER_ROUTE_TOKENS)),
                                mod(5, PEER_EXPERT_TOKENS), peer_w)
    xp = groups[0]["x"].reshape(x_prompt.shape)
    xs = groups[1]["x"].reshape(x_sample.shape)
    new_state_rwkv_fwd = jnp.stack(st_f, axis=1).astype(x_prompt.dtype)
    new_state_rwkv_bwd = jnp.stack(st_b, axis=1).astype(x_prompt.dtype)
    new_cache_mla_ckv = jnp.stack(st_ckv, axis=1)
    new_cache_mla_kpe = jnp.stack(st_kpe, axis=1)
    return (xp, xs, new_state_rwkv_fwd, new_state_rwkv_bwd, new_cache_mla_ckv, new_cache_mla_kpe)
```

```python
import math
from functools import lru_cache, partial

import jax
import jax.numpy as jnp
import numpy as np
from jax import lax
from jax.experimental import pallas as pl
from jax.experimental.pallas import tpu as pltpu

D_MODEL = 1024
DEPTH = 2
GRID_W = 64
EPS = 1e-6
RW_HEADS = 8
RW_HEAD = 64
RW_DIM = RW_HEADS * RW_HEAD
W_LORA = 64
A_LORA = 64
G_LORA = 128
LNX_EPS = 64e-5
RW_IN = 3 * RW_DIM + W_LORA + A_LORA + G_LORA
RW_SPLITS = (RW_DIM, 2 * RW_DIM, 3 * RW_DIM, 3 * RW_DIM + W_LORA, 3 * RW_DIM + W_LORA + A_LORA)
MLA_HEADS = 4
QK_NOPE = 128
QK_ROPE = 64
QK_HEAD = QK_NOPE + QK_ROPE
V_HEAD = 128
Q_LORA = 256
KV_LORA = 128
MLA_DIM = MLA_HEADS * V_HEAD
ROPE_THETA = 10000.0
HY_BANDS = 16
HY_TARGET = 1e-2
HY_FAST = 0.3
HY_SLOW = 1.5
PEER_KEYS = 128
PEER_HEADS = 8
PEER_DKEY = 256
PEER_TOPK = 16

LANES = 128
SUBLANES = 8
V7X_VMEM_BYTES = 64 * 1024 * 1024
VMEM_LIMIT = V7X_VMEM_BYTES * 7 // 8
_NT = (((1,), (1,)), ((), ()))


def _split_bf16(x):
    hi = x.astype(jnp.bfloat16)
    lo = (x - hi.astype(jnp.float32)).astype(jnp.bfloat16)
    return hi, lo


def _dot3(a_hi, a_lo, b_hi, b_lo):
    f32 = jnp.float32
    return (jnp.dot(a_hi, b_hi, preferred_element_type=f32) + jnp.dot(a_lo, b_hi, preferred_element_type=f32)
            + jnp.dot(a_hi, b_lo, preferred_element_type=f32))


def _mm_kernel(a_ref, b_ref, o_ref, *, split):
    if split:
        a_hi, a_lo = _split_bf16(a_ref[...])
        b_hi, b_lo = _split_bf16(b_ref[...])
        o_ref[...] = _dot3(a_hi, a_lo, b_hi, b_lo)
    else:
        o_ref[...] = jnp.dot(a_ref[...].astype(jnp.bfloat16), b_ref[...].astype(jnp.bfloat16),
                             preferred_element_type=jnp.float32)


def _mm(a, b, tm=512, tn=512, split=False):
    lead = a.shape[:-1]
    K = a.shape[-1]
    N = b.shape[-1]
    a2 = a.reshape(-1, K)
    if K % LANES:
        kp = -K % LANES
        a2 = jnp.pad(a2, ((0, 0), (0, kp)))
        b = jnp.pad(b, ((0, kp), (0, 0)))
        K += kp
    M = a2.shape[0]
    tm = min(tm, M)
    tn = min(tn, N)
    if N % tn:
        tn = N
    assert M % tm == 0 and N % tn == 0
    out = pl.pallas_call(
        partial(_mm_kernel, split=split),
        grid=(M // tm, N // tn),
        in_specs=[pl.BlockSpec((tm, K), lambda i, j: (i, 0)),
                  pl.BlockSpec((K, tn), lambda i, j: (0, j))],
        out_specs=pl.BlockSpec((tm, tn), lambda i, j: (i, j)),
        out_shape=jax.ShapeDtypeStruct((M, N), jnp.float32),
        name="matmul",
    )(a2, b)
    return out.reshape(*lead, N)


def axial_rope(L):
    rows = L // GRID_W
    row = jnp.repeat(jnp.arange(rows, dtype=jnp.float32), GRID_W)
    col = jnp.tile(jnp.arange(GRID_W, dtype=jnp.float32), rows)
    n_freq = QK_ROPE // 4
    inv = ROPE_THETA ** (-jnp.arange(n_freq, dtype=jnp.float32) / n_freq)
    ang = jnp.concatenate([row[:, None] * inv, col[:, None] * inv], axis=-1)
    return jnp.cos(ang), jnp.sin(ang)


def _mla_attn_kernel(q_ref, k_ref, v_ref, *rest, with_ctx):
    if with_ctx:
        kc_ref, vc_ref, o_ref = rest
    else:
        (o_ref,) = rest
    bf16 = jnp.bfloat16
    f32 = jnp.float32
    scale = QK_HEAD ** -0.5
    for h in range(MLA_HEADS):
        qs = slice(h * QK_HEAD, (h + 1) * QK_HEAD)
        vs = slice(h * V_HEAD, (h + 1) * V_HEAD)
        q = q_ref[0, :, qs].astype(bf16)
        s = lax.dot_general(q, k_ref[0, :, qs].astype(bf16), _NT, preferred_element_type=f32) * scale
        m = jnp.max(s, axis=-1, keepdims=True)
        if with_ctx:
            sc = lax.dot_general(q, kc_ref[0, :, qs].astype(bf16), _NT, preferred_element_type=f32) * scale
            m = jnp.maximum(m, jnp.max(sc, axis=-1, keepdims=True))
            pc = jnp.exp(sc - m)
        p = jnp.exp(s - m)
        l = jnp.sum(p, axis=-1, keepdims=True)
        if with_ctx:
            l = l + jnp.sum(pc, axis=-1, keepdims=True)
        o = jnp.dot((p / l).astype(bf16), v_ref[0, :, vs].astype(bf16), preferred_element_type=f32)
        if with_ctx:
            o = o + jnp.dot((pc / l).astype(bf16), vc_ref[0, :, vs].astype(bf16), preferred_element_type=f32)
        o_ref[0, :, vs] = o


def attend(q, k, v, ctx=None, tq=256):
    B, Lq, _ = q.shape
    tq = min(tq, Lq)
    kv = [k, v] + (list(ctx) if ctx is not None else [])
    full = lambda a: pl.BlockSpec((1,) + a.shape[1:], lambda b, i: (b, 0, 0))
    return pl.pallas_call(
        partial(_mla_attn_kernel, with_ctx=ctx is not None), grid=(B, Lq // tq),
        in_specs=[pl.BlockSpec((1, tq, MLA_HEADS * QK_HEAD), lambda b, i: (b, i, 0))] + [full(a) for a in kv],
        out_specs=pl.BlockSpec((1, tq, MLA_HEADS * V_HEAD), lambda b, i: (b, i, 0)),
        out_shape=jax.ShapeDtypeStruct((B, Lq, MLA_HEADS * V_HEAD), jnp.float32),
        compiler_params=pltpu.CompilerParams(dimension_semantics=("arbitrary", "arbitrary"),
                                             vmem_limit_bytes=VMEM_LIMIT),
        name="mla_attend",
    )(q, *kv)


def _first_row_block(fn):
    pl.when(pl.program_id(2) == 0)(fn)


def _dft_pair_kernel(c_hi, c_lo, s_hi, s_lo, x_ref, oc_ref, os_ref, x_hi, x_lo):
    def _():
        x_hi[...], x_lo[...] = _split_bf16(x_ref[0])
    _first_row_block(_)
    oc_ref[0] = _dot3(c_hi[...], c_lo[...], x_hi[...], x_lo[...])
    os_ref[0] = _dot3(s_hi[...], s_lo[...], x_hi[...], x_lo[...])


def _dft_spectral_kernel(c_tab, s_tab, x_ref, hr_ref, hi_ref, yr_ref, yi_ref, x16):
    def _():
        x16[...] = x_ref[0].astype(jnp.bfloat16)
    _first_row_block(_)
    rb, L = c_tab.shape
    zc = jnp.dot(c_tab[...], x16[...], preferred_element_type=jnp.float32)
    zs = jnp.dot(s_tab[...], x16[...], preferred_element_type=jnp.float32)
    hr = hr_ref[...]
    hi = hi_ref[...]
    f = pl.program_id(2) * rb + lax.broadcasted_iota(jnp.int32, zc.shape, 0)
    yr_ref[0] = jnp.where(f == 0, zc * hr * (0.5 / L), (zc * hr - zs * hi) * (1.0 / L))
    yi_ref[0] = jnp.where(f == 0, zs * hi * (0.5 / L), (zc * hi + zs * hr) * (1.0 / L))


def _dft_sum_kernel(c_tab, s_tab, x_ref, y_ref, o_ref, x16, y16):
    def _():
        x16[...] = x_ref[0].astype(jnp.bfloat16)
        y16[...] = y_ref[0].astype(jnp.bfloat16)
    _first_row_block(_)
    o_ref[0] = (jnp.dot(c_tab[...], x16[...], preferred_element_type=jnp.float32)
                + jnp.dot(s_tab[...], y16[...], preferred_element_type=jnp.float32))


@lru_cache(maxsize=None)
def dft_tables(L):
    f = np.arange(L, dtype=np.int64)
    ang = ((f[:, None] * f[None, :]) % (2 * L)).astype(np.float64) * (math.pi / L)
    c = np.cos(ang).astype(np.float32)
    s = (-np.sin(ang)).astype(np.float32)
    alt = np.where(f % 2 == 0, 1.0, -1.0).astype(np.float32)
    s_ana = np.where(f[:, None] == 0, alt[None, :], s)
    s_syn = np.where(f[None, :] == 0, alt[:, None], s)

    def split(x):
        hi = x.astype(jnp.bfloat16)
        return hi, (x - hi.astype(np.float32)).astype(jnp.bfloat16)

    cc = split(c)
    return cc + split(s_ana), cc + split(s_syn)


def _dft_call(body, tabs, xs, n_out, name, row_inputs=(), rb=256, nb=512):
    B, L, N = xs[0].shape
    rb = min(rb, L)
    nb = min(nb, N)
    tab = pl.BlockSpec((rb, L), lambda b, n, i: (i, 0))
    xin = pl.BlockSpec((1, L, nb), lambda b, n, i: (b, 0, n))
    rin = pl.BlockSpec((rb, nb), lambda b, n, i: (i, n))
    out = pl.BlockSpec((1, rb, nb), lambda b, n, i: (b, i, n))
    sd = jax.ShapeDtypeStruct((B, L, N), jnp.float32)
    return pl.pallas_call(
        body, grid=(B, N // nb, L // rb),
        in_specs=[tab] * len(tabs) + [xin] * len(xs) + [rin] * len(row_inputs),
        out_specs=[out] * n_out if n_out > 1 else out,
        out_shape=[sd] * n_out if n_out > 1 else sd,
        scratch_shapes=[pltpu.VMEM((L, nb), jnp.bfloat16)] * (len(tabs) // 2 * len(xs)),
        compiler_params=pltpu.CompilerParams(dimension_semantics=("arbitrary",) * 3,
                                             vmem_limit_bytes=VMEM_LIMIT),
        name=name,
    )(*tabs, *xs, *row_inputs)


def hyena_longconv(zin, circ):
    B, L, D = zin.shape
    tabs, tabs_syn = dft_tables(L)
    alt = jnp.where(jnp.arange(L) % 2 == 0, 1.0, -1.0).astype(jnp.float32)[:, None]
    fc, fs = _dft_call(_dft_pair_kernel, tabs, [jnp.concatenate([circ[:L], circ[L:]], axis=-1)[None]], 2, "dft_filter")
    hr = fc[0, :, :D] + alt * fc[0, :, D:]
    hi = fs[0, :, :D] + alt * fs[0, :, D:]
    yr, yi = _dft_call(_dft_spectral_kernel, tabs[0::2], [zin], 2, "dft_analysis", row_inputs=(hr, hi))
    return _dft_call(_dft_sum_kernel, tabs_syn[0::2], [yr, yi], 1, "dft_synthesis")


SCAN_ACCS = 4


def _rwkv_scan_kernel(rf_ref, kkf_ref, vf_ref, wf_ref, kdf_ref, bf_ref, rb_ref, kkb_ref, vb_ref, wb_ref, kdb_ref, bb_ref,
                      s0f_ref, s0b_ref, yf_ref, yb_ref, sff_ref, sfb_ref, sf_scr, sb_scr):
    c = pl.program_id(0)
    tc, nk, _ = rf_ref.shape
    nv = vf_ref.shape[1]
    groups = RW_HEAD // nk

    @pl.when(c == 0)
    def _():
        sf_scr[...] = s0f_ref[...]
        sb_scr[...] = s0b_ref[...]

    def row(ref, tt, k):
        return jnp.broadcast_to(ref[tt, pl.ds(k, 1), :], (nv, LANES))

    def all_groups(p):
        tot = p
        for q in range(1, groups):
            tot = tot + pltpu.roll(p, q * (LANES // groups), 1)
        return tot

    def one_step(tt, r_ref, kk_ref, v_ref, w_ref, kd_ref, b_ref, s_scr, y_ref):
        parts = [None] * SCAN_ACCS
        for k in range(nk):
            term = s_scr[k] * row(kk_ref, tt, k)
            parts[k % SCAN_ACCS] = term if parts[k % SCAN_ACCS] is None else parts[k % SCAN_ACCS] + term
        sa = -all_groups((parts[0] + parts[1]) + (parts[2] + parts[3]))
        vv = v_ref[tt]
        parts = [None] * SCAN_ACCS
        for k in range(nk):
            s_new = s_scr[k] * row(w_ref, tt, k) + sa * row(b_ref, tt, k) + vv * row(kd_ref, tt, k)
            s_scr[k] = s_new
            term = s_new * row(r_ref, tt, k)
            parts[k % SCAN_ACCS] = term if parts[k % SCAN_ACCS] is None else parts[k % SCAN_ACCS] + term
        y_ref[tt] = all_groups((parts[0] + parts[1]) + (parts[2] + parts[3]))

    def step(t, carry):
        one_step(t, rf_ref, kkf_ref, vf_ref, wf_ref, kdf_ref, bf_ref, sf_scr, yf_ref)
        one_step(tc - 1 - t, rb_ref, kkb_ref, vb_ref, wb_ref, kdb_ref, bb_ref, sb_scr, yb_ref)
        return carry

    lax.fori_loop(0, tc, step, 0, unroll=2)

    @pl.when(c == pl.num_programs(0) - 1)
    def _():
        sff_ref[...] = sf_scr[...]
        sfb_ref[...] = sb_scr[...]


def rwkv_scan(r, kk, v, wkb_fwd, wkb_bwd, s0_fwd, s0_bwd, tc=32):
    L, nk, _ = r.shape
    nv = v.shape[1]
    nc = L // tc
    fwd = lambda rows: pl.BlockSpec((tc, rows, LANES), lambda c: (c, 0, 0))
    bwd = lambda rows: pl.BlockSpec((tc, rows, LANES), lambda c: (nc - 1 - c, 0, 0))
    state = pl.BlockSpec((nk, nv, LANES), lambda c: (0, 0, 0))
    y_sd = jax.ShapeDtypeStruct((L, nv, LANES), jnp.float32)
    s_sd = jax.ShapeDtypeStruct((nk, nv, LANES), jnp.float32)
    return pl.pallas_call(
        _rwkv_scan_kernel,
        grid=(nc,),
        in_specs=[fwd(nk), fwd(nk), fwd(nv), fwd(nk), fwd(nk), fwd(nk),
                  bwd(nk), bwd(nk), bwd(nv), bwd(nk), bwd(nk), bwd(nk), state, state],
        out_specs=[fwd(nv), bwd(nv), state, state],
        out_shape=[y_sd, y_sd, s_sd, s_sd],
        scratch_shapes=[pltpu.VMEM((nk, nv, LANES), jnp.float32)] * 2,
        compiler_params=pltpu.CompilerParams(dimension_semantics=("arbitrary",), vmem_limit_bytes=VMEM_LIMIT),
        name="rwkv_scan",
    )(r, kk, v, *wkb_fwd, r, kk, v, *wkb_bwd, s0_fwd, s0_bwd)


def k_to_lanes(x, ksplit):
    B, L, H, N = x.shape
    nk = N // ksplit
    return x.reshape(B, L, H, ksplit, nk).transpose(1, 4, 3, 0, 2).reshape(L, nk, ksplit * B * H)


def v_to_lanes(x, ksplit):
    B, L, H, N = x.shape
    return jnp.tile(x.transpose(1, 3, 0, 2).reshape(L, N, B * H), (1, 1, ksplit))


def v_from_lanes(y, B, H):
    L, N, _ = y.shape
    return y[:, :, :B * H].reshape(L, N, B, H).transpose(2, 0, 3, 1)


def state_to_lanes(s, ksplit):
    B, H, N, K = s.shape
    nk = K // ksplit
    return s.reshape(B, H, N, ksplit, nk).transpose(4, 2, 3, 0, 1).reshape(nk, N, ksplit * B * H)


def state_from_lanes(s, B, H, ksplit):
    nk, N, _ = s.shape
    return s.reshape(nk, N, ksplit, B, H).transpose(3, 4, 1, 2, 0).reshape(B, H, N, ksplit * nk)


def rwkv_scan_both(rh, kk, vh, decay2, kd2, b2, s0_2, tc=32):
    B, L, H, N = rh.shape
    ksplit = LANES // (B * H)
    assert B * H * ksplit == LANES
    wkb = [[k_to_lanes(t[d], ksplit) for t in (decay2, kd2, b2)] for d in range(2)]
    y_f, y_b, sf_f, sf_b = rwkv_scan(k_to_lanes(rh, ksplit), k_to_lanes(kk, ksplit), v_to_lanes(vh, ksplit),
                                     wkb[0], wkb[1], state_to_lanes(s0_2[0], ksplit), state_to_lanes(s0_2[1], ksplit),
                                     tc=tc)
    return ([v_from_lanes(y_f, B, H), v_from_lanes(y_b, B, H)],
            [state_from_lanes(sf_f, B, H, ksplit), state_from_lanes(sf_b, B, H, ksplit)])


def _rms_mod(x, g, shift, scale):
    y = x * lax.rsqrt(jnp.mean(x * x, axis=-1, keepdims=True) + EPS)
    return (y * g) * (1.0 + scale) + shift


def tile_mod(m, n_rows, tm):
    return jnp.repeat(m, n_rows // m.shape[0] // tm, axis=0)[:, None, :]


def _norm_mod_matmul_kernel(x_ref, g_ref, sh_ref, sc_ref, w_ref, b_ref, o_ref):
    h = _rms_mod(x_ref[...], g_ref[...], sh_ref[0], sc_ref[0])
    o_ref[...] = jnp.dot(h.astype(jnp.bfloat16), w_ref[...], preferred_element_type=jnp.float32) + b_ref[...]


def norm_mod_matmul(x, g, shift_t, scale_t, w, b, tm):
    T, D = x.shape
    N = w.shape[1]
    mod = pl.BlockSpec((1, 1, D), lambda i: (i, 0, 0))
    return pl.pallas_call(
        _norm_mod_matmul_kernel, grid=(T // tm,),
        in_specs=[pl.BlockSpec((tm, D), lambda i: (i, 0)), pl.BlockSpec((1, D), lambda i: (0, 0)), mod, mod,
                  pl.BlockSpec((D, N), lambda i: (0, 0)), pl.BlockSpec((1, N), lambda i: (0, 0))],
        out_specs=pl.BlockSpec((tm, N), lambda i: (i, 0)),
        out_shape=jax.ShapeDtypeStruct((T, N), jnp.float32),
        compiler_params=pltpu.CompilerParams(dimension_semantics=("arbitrary",), vmem_limit_bytes=VMEM_LIMIT),
        name="norm_mod_matmul",
    )(x, g[None], shift_t, scale_t, w.astype(jnp.bfloat16), b[None])


def _head_sums(x, ones_bd):
    hi, lo = _split_bf16(x)
    return (jnp.dot(hi, ones_bd, preferred_element_type=jnp.float32)
            + jnp.dot(lo, ones_bd, preferred_element_type=jnp.float32))


def _with_neighbours(x_ref, prev_ref, next_ref, first_ref, last_ref, ncols):
    tm = x_ref.shape[0]
    x = x_ref[:, :ncols]
    row = lax.broadcasted_iota(jnp.int32, (tm, 1), 0)
    prev_row = prev_ref[SUBLANES - 1:SUBLANES, :ncols] * (1.0 - first_ref[0, :, :1])
    next_row = next_ref[0:1, :ncols] * (1.0 - last_ref[0, :, :1])
    prev = jnp.where(row == 0, prev_row, pltpu.roll(x, 1, 0))
    nxt = jnp.where(row == tm - 1, next_row, pltpu.roll(x, tm - 1, 0))
    return x, prev, nxt


def _neighbour_specs(T, C, L, tm):
    n_tiles = T // tm
    hb = tm // SUBLANES
    start = np.arange(n_tiles) * tm % L
    first = np.broadcast_to((start == 0).astype(np.float32)[:, None, None], (n_tiles, 1, LANES))
    last = np.broadcast_to((start + tm == L).astype(np.float32)[:, None, None], (n_tiles, 1, LANES))
    flag = pl.BlockSpec((1, 1, LANES), lambda i: (i, 0, 0))
    specs = [pl.BlockSpec((tm, C), lambda i: (i, 0)),
             pl.BlockSpec((SUBLANES, C), lambda i: (jnp.maximum(i * hb - 1, 0), 0)),
             pl.BlockSpec((SUBLANES, C), lambda i: (jnp.minimum((i + 1) * hb, T // SUBLANES - 1), 0)),
             flag, flag]
    return specs, (jnp.asarray(first), jnp.asarray(last))


def _rwkv_prep_kernel(proj_ref, prev_ref, next_ref, first_ref, last_ref, mu_ref, w0_ref, w2_ref, a0_ref, a2_ref,
                      g2_ref, kk_ref, ka_ref, rk_ref, ones_ref,
                      r_out, kk_out, v_out, w_out0, w_out1, kd_out0, kd_out1, b_out0, b_out1, gate_out, bonus_out):
    bf16 = jnp.bfloat16
    f32 = jnp.float32
    x, prev, nxt = _with_neighbours(proj_ref, prev_ref, next_ref, first_ref, last_ref, RW_IN)
    rw = x + mu_ref[...] * (0.5 * (prev + nxt) - x)
    r = rw[:, RW_SPLITS[0] - RW_DIM:RW_SPLITS[0]]
    k = rw[:, RW_SPLITS[0]:RW_SPLITS[1]]
    v = rw[:, RW_SPLITS[1]:RW_SPLITS[2]]
    wd = rw[:, RW_SPLITS[2]:RW_SPLITS[3]]
    ad = rw[:, RW_SPLITS[3]:RW_SPLITS[4]]
    gd = rw[:, RW_SPLITS[4]:]
    ones_bd = ones_ref[...]
    kk = k * kk_ref[...]
    kk = kk / jnp.maximum(jnp.sqrt(_head_sums(kk * kk, ones_bd)), 1e-12)
    tanh_wd = jnp.tanh(wd).astype(bf16)
    ad16 = ad.astype(bf16)
    rk = r * rk_ref[...]
    bonus = jnp.zeros_like(r)
    for d, (w_o, kd_o, b_o) in enumerate(((w_out0, kd_out0, b_out0), (w_out1, kd_out1, b_out1))):
        w_raw = w0_ref[d:d + 1, :] + jnp.dot(tanh_wd, w2_ref[d], preferred_element_type=f32)
        w_o[...] = jnp.exp(-jnp.exp(-jax.nn.softplus(-w_raw) - 0.5))
        lr = jax.nn.sigmoid(a0_ref[d:d + 1, :] + jnp.dot(ad16, a2_ref[d], preferred_element_type=f32))
        kd = k * (1.0 + (lr - 1.0) * ka_ref[...])
        kd_o[...] = kd
        b_o[...] = kk * lr
        bonus = bonus + _head_sums(rk * kd, ones_bd)
    r_out[...] = r
    kk_out[...] = kk
    v_out[...] = v
    gate_out[...] = jnp.dot(jax.nn.sigmoid(gd).astype(bf16), g2_ref[...], preferred_element_type=f32)
    bonus_out[...] = bonus * v


def rwkv_prep(proj, L, mu, w0, w2, a0, a2, g2, k_k, k_a, r_k, tm):
    T, C = proj.shape
    nb_specs, flags = _neighbour_specs(T, C, L, tm)
    row2 = lambda a: a.reshape(1, -1)
    whole = lambda a: pl.BlockSpec(a.shape, lambda i: (0,) * a.ndim)
    out = pl.BlockSpec((tm, RW_DIM), lambda i: (i, 0))
    consts = (row2(mu), w0, w2.astype(jnp.bfloat16), a0, a2.astype(jnp.bfloat16), g2.astype(jnp.bfloat16),
              row2(k_k), row2(k_a), row2(r_k), _segment_ones(RW_DIM, RW_HEAD))
    return pl.pallas_call(
        _rwkv_prep_kernel, grid=(T // tm,),
        in_specs=nb_specs + [whole(a) for a in consts],
        out_specs=[out] * 11,
        out_shape=[jax.ShapeDtypeStruct((T, RW_DIM), jnp.float32)] * 11,
        compiler_params=pltpu.CompilerParams(dimension_semantics=("arbitrary",), vmem_limit_bytes=VMEM_LIMIT),
        name="rwkv_prep",
    )(proj, proj, proj, *flags, *consts)


def _rwkv_out_kernel(y0_ref, y1_ref, bonus_ref, gate_ref, mla_ref, lng_ref, lnb_ref, ones_ref, w_ref, x_ref, mod_ref,
                     o_ref):
    bf16 = jnp.bfloat16
    f32 = jnp.float32
    ones_bd = ones_ref[...]
    y = y0_ref[...] + y1_ref[...]
    dlt = y - _head_sums(y, ones_bd) * (1.0 / RW_HEAD)
    var = _head_sums(dlt * dlt, ones_bd) * (1.0 / RW_HEAD)
    yn = dlt * lax.rsqrt(var + LNX_EPS) * lng_ref[...] + lnb_ref[...]
    rw_out = (yn + bonus_ref[...]) * gate_ref[...]
    upd = (jnp.dot(rw_out.astype(bf16), w_ref[:RW_DIM], preferred_element_type=f32)
           + jnp.dot(mla_ref[...].astype(bf16), w_ref[RW_DIM:], preferred_element_type=f32))
    o_ref[...] = x_ref[...] + mod_ref[0] * upd


def rwkv_out_residual(y0, y1, bonus, gate, mla_out, lnx_g, lnx_b, w_out, x, gate_t, tm):
    T, D = x.shape
    row = lambda n: pl.BlockSpec((tm, n), lambda i: (i, 0))
    whole = lambda a: pl.BlockSpec(a.shape, lambda i: (0,) * a.ndim)
    consts = (lnx_g.reshape(1, -1), lnx_b.reshape(1, -1), _segment_ones(RW_DIM, RW_HEAD), w_out.astype(jnp.bfloat16))
    return pl.pallas_call(
        _rwkv_out_kernel, grid=(T // tm,),
        in_specs=[row(RW_DIM)] * 4 + [row(MLA_DIM)] + [whole(a) for a in consts]
                 + [row(D), pl.BlockSpec((1, 1, D), lambda i: (i, 0, 0))],
        out_specs=row(D),
        out_shape=jax.ShapeDtypeStruct((T, D), jnp.float32),
        compiler_params=pltpu.CompilerParams(dimension_semantics=("arbitrary",), vmem_limit_bytes=VMEM_LIMIT),
        name="rwkv_out_residual",
    )(y0, y1, bonus, gate, mla_out, *consts, x, gate_t)


def rwkv_scans(prep, B, L, s0s):
    heads = lambda t: t.reshape(B, L, RW_HEADS, RW_HEAD)
    r, kk, v, w_0, w_1, kd_0, kd_1, b_0, b_1 = [heads(t) for t in prep]
    if s0s is None:
        z = jnp.zeros((B, RW_HEADS, RW_HEAD, RW_HEAD), jnp.float32)
        s0s = (z, z)
    ys, finals = rwkv_scan_both(r, kk, v, [w_0, w_1], [kd_0, kd_1], [b_0, b_1], s0s)
    return [y.reshape(B * L, RW_DIM) for y in ys], finals


def _segment_ones(n, seg):
    h = np.arange(n) // seg
    return jnp.asarray(h[:, None] == h[None, :], jnp.bfloat16)


def _mla_prep_kernel(x_ref, cos_ref, sin_ref, qnorm_ref, qup_ref, kvnorm_ref, wk_ref, wv_ref, place_ref, qn_ref, kn_ref,
                     ones_ref, q_out, k_out, v_out, ckv_out, *, rope, norm_kv):
    bf16 = jnp.bfloat16
    f32 = jnp.float32
    x = x_ref[...]
    qc = x[:, :Q_LORA]
    kvc = x[:, Q_LORA:Q_LORA + KV_LORA]
    kpe = x[:, Q_LORA + KV_LORA:]
    rms = lambda t, g: t * lax.rsqrt(jnp.mean(t * t, axis=-1, keepdims=True) + EPS) * g
    ones_seg = ones_ref[...]
    head_norm = lambda t, g: t * lax.rsqrt(_head_sums(t * t, ones_seg) * (1.0 / QK_HEAD) + EPS) * g
    q = jnp.dot(rms(qc, qnorm_ref[...]).astype(bf16), qup_ref[...], preferred_element_type=f32)
    ckv = rms(kvc, kvnorm_ref[...]) if norm_kv else kvc
    ckv_out[...] = ckv
    c16 = ckv.astype(bf16)
    kpe_hi, kpe_lo = _split_bf16(kpe)
    k = (jnp.dot(c16, wk_ref[...], preferred_element_type=f32)
         + jnp.dot(kpe_hi, place_ref[...], preferred_element_type=f32)
         + jnp.dot(kpe_lo, place_ref[...], preferred_element_type=f32))
    v_out[...] = jnp.dot(c16, wv_ref[...], preferred_element_type=f32)
    q = head_norm(q, qn_ref[...])
    k = head_norm(k, kn_ref[...])
    if rope:
        n = q.shape[1]
        even = lax.broadcasted_iota(jnp.int32, (1, n), 1) % 2 == 0
        swap = lambda t: jnp.where(even, pltpu.roll(t, n - 1, 1), pltpu.roll(t, 1, 1))
        cos = cos_ref[...]
        sin = sin_ref[...]
        q = q * cos + swap(q) * sin
        k = k * cos + swap(k) * sin
    q_out[...] = q
    k_out[...] = k


def mla_prep(x, L, rope, weights, norm_kv=True):
    q_norm, q_up, kv_norm, kv_up, qn, kn = weights
    T, C = x.shape
    tm = min(ROW_TILE, L)
    nq = MLA_HEADS * QK_HEAD
    kv4 = kv_up.reshape(KV_LORA, MLA_HEADS, QK_NOPE + V_HEAD)
    wk = jnp.pad(kv4[:, :, :QK_NOPE], ((0, 0), (0, 0), (0, QK_ROPE))).reshape(KV_LORA, nq).astype(jnp.bfloat16)
    wv = kv4[:, :, QK_NOPE:].reshape(KV_LORA, MLA_HEADS * V_HEAD).astype(jnp.bfloat16)
    slot = np.arange(nq) % QK_HEAD - QK_NOPE
    place = jnp.asarray(np.arange(QK_ROPE)[:, None] == slot[None, :], jnp.bfloat16)
    if rope is not None:
        cos, sin = rope
        pair = np.maximum(slot, 0) // 2
        sign = np.where(slot % 2 == 0, -1.0, 1.0).astype(np.float32)
        cos_f = jnp.where(slot >= 0, cos[:, pair], 1.0)
        sin_f = jnp.where(slot >= 0, sin[:, pair] * sign, 0.0)
    else:
        cos_f = sin_f = jnp.zeros((tm, nq), jnp.float32)
    tiles_per_seq = max(L // tm, 1)
    rope_blk = pl.BlockSpec((tm, nq), (lambda i: (i % tiles_per_seq, 0)) if rope is not None else (lambda i: (0, 0)))
    row2 = lambda a: a.reshape(1, -1)
    whole = lambda a: pl.BlockSpec(a.shape, lambda i: (0,) * a.ndim)
    consts = (row2(q_norm), q_up.astype(jnp.bfloat16), row2(kv_norm), wk, wv, place,
              row2(jnp.tile(qn, MLA_HEADS)), row2(jnp.tile(kn, MLA_HEADS)), _segment_ones(nq, QK_HEAD))
    out = lambda n: pl.BlockSpec((tm, n), lambda i: (i, 0))
    sd = lambda n: jax.ShapeDtypeStruct((T, n), jnp.float32)
    return pl.pallas_call(
        partial(_mla_prep_kernel, rope=rope is not None, norm_kv=norm_kv), grid=(T // tm,),
        in_specs=[pl.BlockSpec((tm, C), lambda i: (i, 0)), rope_blk, rope_blk] + [whole(a) for a in consts],
        out_specs=[out(nq), out(nq), out(MLA_HEADS * V_HEAD), out(KV_LORA)],
        out_shape=[sd(nq), sd(nq), sd(MLA_HEADS * V_HEAD), sd(KV_LORA)],
        compiler_params=pltpu.CompilerParams(dimension_semantics=("arbitrary",), vmem_limit_bytes=VMEM_LIMIT),
        name="mla_prep",
    )(x, cos_f, sin_f, *consts)


def mla_mixer(mla, B, L, ctx, weights):
    q, k, v, ckv = mla_prep(mla, L, None if ctx is None else ctx[2], weights)
    seq = lambda t: t.reshape(B, L, -1)
    kv_ctx = None
    if ctx is not None:
        c_ckv, c_kpe = ctx[0], ctx[1]
        Lc = c_ckv.shape[1]
        xc = jnp.concatenate([jnp.zeros((B, Lc, Q_LORA), jnp.float32), c_ckv, c_kpe], axis=-1).reshape(B * Lc, -1)
        _, kc, vc, _ = mla_prep(xc, Lc, None, weights, norm_kv=False)
        kv_ctx = (kc.reshape(B, Lc, -1), vc.reshape(B, Lc, -1))
    att = attend(seq(q), seq(k), seq(v), kv_ctx)
    return att.reshape(B * L, -1), ckv.reshape(B, L, -1), mla[:, Q_LORA + KV_LORA:].reshape(B, L, -1)


def _hyena_prep_kernel(proj_ref, prev_ref, next_ref, first_ref, last_ref, cw_ref, cb_ref, x0_ref, z_ref):
    D = x0_ref.shape[1]
    x, prev, nxt = _with_neighbours(proj_ref, prev_ref, next_ref, first_ref, last_ref, proj_ref.shape[1])
    u = prev * cw_ref[0:1, :] + x * cw_ref[1:2, :] + nxt * cw_ref[2:3, :] + cb_ref[...]
    x0_ref[...] = u[:, :D]
    z_ref[...] = u[:, D:2 * D] * u[:, 2 * D:]


def hyena_prep(proj, L, conv_w, conv_b, tm):
    T, C = proj.shape
    D = C // 3
    nb_specs, flags = _neighbour_specs(T, C, L, tm)
    out = pl.BlockSpec((tm, D), lambda i: (i, 0))
    return pl.pallas_call(
        _hyena_prep_kernel, grid=(T // tm,),
        in_specs=nb_specs + [pl.BlockSpec((3, C), lambda i: (0, 0)), pl.BlockSpec((1, C), lambda i: (0, 0))],
        out_specs=[out, out],
        out_shape=[jax.ShapeDtypeStruct((T, D), jnp.float32)] * 2,
        compiler_params=pltpu.CompilerParams(dimension_semantics=("arbitrary",), vmem_limit_bytes=VMEM_LIMIT),
        name="hyena_prep",
    )(proj, proj, proj, *flags, conv_w, conv_b[None])


def _hyena_out_kernel(x0_ref, conv_ref, z_ref, bias_ref, w_ref, x_ref, mod_ref, o_ref):
    a = x0_ref[...] * (conv_ref[...] + z_ref[...] * bias_ref[...])
    o_ref[...] = x_ref[...] + mod_ref[0] * jnp.dot(a.astype(jnp.bfloat16), w_ref[...],
                                                   preferred_element_type=jnp.float32)


def hyena_out_residual(x0, conv, zin, bias, w_out, x, gate_t, tm):
    T, D = x.shape
    row = pl.BlockSpec((tm, D), lambda i: (i, 0))
    return pl.pallas_call(
        _hyena_out_kernel, grid=(T // tm,),
        in_specs=[row, row, row, pl.BlockSpec((1, D), lambda i: (0, 0)), pl.BlockSpec((D, D), lambda i: (0, 0)),
                  row, pl.BlockSpec((1, 1, D), lambda i: (i, 0, 0))],
        out_specs=row,
        out_shape=jax.ShapeDtypeStruct((T, D), jnp.float32),
        compiler_params=pltpu.CompilerParams(dimension_semantics=("arbitrary",), vmem_limit_bytes=VMEM_LIMIT),
        name="hyena_out_residual",
    )(x0, conv, zin, bias[None], w_out.astype(jnp.bfloat16), x, gate_t)


def hyena_filters(L, w1, b1, w2, b2, w3, freq):
    f32 = jnp.float32
    u = jnp.arange(2 * L, dtype=jnp.int32)[:, None]
    t = jnp.where(u < L, u, 2 * L - u).astype(f32)
    t_unit = t / (L - 1)
    bands = jnp.linspace(1e-4, HY_BANDS - 1, HY_BANDS, dtype=f32)
    ang = 2.0 * math.pi * t * bands / L
    zpos = jnp.concatenate([t_unit, jnp.cos(ang), -jnp.sin(ang)], axis=-1)
    fr = freq.astype(f32)
    hid = jnp.sin(fr * (_mm(zpos, w1.astype(f32), split=True) + b1.astype(f32)))
    hid = jnp.sin(fr * (_mm(hid, w2.astype(f32), split=True) + b2.astype(f32)))
    filt = _mm(hid, w3.astype(f32), split=True)
    deltas = jnp.linspace(math.log(HY_TARGET) / HY_FAST, math.log(HY_TARGET) / HY_SLOW, D_MODEL, dtype=f32)
    window = jnp.exp(-t_unit * jnp.abs(deltas))
    circ = jnp.where(u < L, filt[:, :D_MODEL], filt[:, D_MODEL:]) * window
    circ = jnp.where(u == L, 0.0, circ)
    return circ / jnp.sum(jnp.abs(circ), axis=0, keepdims=True)


PEER_N = PEER_KEYS * PEER_KEYS
GATE_LANES = 2 * LANES
PEER_ROUTE_TOKENS = 256
PEER_EXPERT_TOKENS = 512
ROW_TILE = 512
SEQ_TILE = 256


def _top_vals(s, n, with_rank):
    vals = []
    rank = jnp.full(s.shape, float(n), jnp.float32) if with_rank else None
    for a in range(n):
        m = jnp.max(s, axis=0, keepdims=True)
        vals.append(m)
        hit = s == m
        if with_rank:
            rank = jnp.where(hit, float(a), rank)
        s = jnp.where(hit, -jnp.inf, s)
    return vals, rank


def _peer_route_kernel(x_ref, g_ref, sh_ref, sc_ref, wq_hi_ref, wq_lo_ref, k_hi_ref, k_lo_ref,
                       n1_ref, e1_ref, r2_ref, e2_ref, h_ref, s_scr):
    tb = x_ref.shape[0]
    half = PEER_DKEY // 2
    n_tiles = tb // LANES
    h_hi, h_lo = _split_bf16(_rms_mod(x_ref[...], g_ref[...], sh_ref[0], sc_ref[0]))
    h_ref[...] = h_hi
    q_hi, q_lo = _split_bf16(_dot3(h_hi, h_lo, wq_hi_ref[...], wq_lo_ref[...]))
    for hh in range(PEER_HEADS):
        for p in range(2):
            cols = slice((2 * hh + p) * half, (2 * hh + p + 1) * half)
            k_hi = k_hi_ref[hh, p]
            s_scr[hh, p] = (lax.dot_general(k_hi, q_hi[:, cols], _NT, preferred_element_type=jnp.float32)
                            + lax.dot_general(k_lo_ref[hh, p], q_hi[:, cols], _NT, preferred_element_type=jnp.float32)
                            + lax.dot_general(k_hi, q_lo[:, cols], _NT, preferred_element_type=jnp.float32))

    K = PEER_TOPK
    G = SUBLANES

    def tile(it, carry):
        hh = it // n_tiles
        ln = pl.ds(pl.multiple_of((it % n_tiles) * LANES, LANES), LANES)
        s1 = s_scr[hh, 0, :, ln]
        s2 = s_scr[hh, 1, :, ln]
        v1, _ = _top_vals(s1, K, False)
        v2l, r2 = _top_vals(s2, K, True)
        v2 = jnp.concatenate(v2l, axis=0)
        cand = [v1[0] + v2[:G], v1[0] + v2[G:]]
        cand += [v1[a] + v2[:G] for a in range(1, G)]
        cand += [jnp.concatenate(v1[G:], axis=0) + v2[0:1]]
        c = cand
        for k in range(K):
            m = c[0]
            for ci in c[1:]:
                m = jnp.maximum(m, ci)
            m = jnp.max(m, axis=0, keepdims=True)
            if k + 1 < K:
                c = [jnp.where(ci == m, -jnp.inf, ci) for ci in c]
        tau = m
        top = v1[0] + v2[0:1]
        keep = [ci >= tau for ci in cand]
        z = jnp.zeros_like(tau)
        for ci, ki in zip(cand, keep):
            z = z + jnp.sum(jnp.where(ki, jnp.exp(ci - top), 0.0), axis=0, keepdims=True)
        cnt = [jnp.sum(jnp.where(ki, 1.0, 0.0), axis=0, keepdims=True) for ki in keep[:G + 1]]
        tail = jnp.where(keep[G + 1], 1.0, 0.0)
        n_a = [cnt[0] + cnt[1]] + cnt[2:] + [tail[a:a + 1] for a in range(G)]
        n1 = jnp.zeros_like(s1)
        for a in range(K):
            n1 = jnp.where(s1 == v1[a], n_a[a], n1)
        n1_ref[hh, :, ln] = n1
        e1_ref[hh, :, ln] = jnp.exp(s1 - v1[0]) / z
        r2_ref[hh, :, ln] = r2.astype(jnp.bfloat16)
        e2_ref[hh, :, ln] = jnp.exp(s2 - v2[0:1]).astype(jnp.bfloat16)
        return carry

    lax.fori_loop(0, PEER_HEADS * n_tiles, tile, 0, unroll=2)


def peer_route(x, g, shift_t, scale_t, wq_hi, wq_lo, k_hi, k_lo, tb=PEER_ROUTE_TOKENS):
    T = x.shape[0]
    nh = PEER_HEADS
    blk = pl.BlockSpec((nh, PEER_KEYS, tb), lambda t: (0, 0, t))
    sd = lambda dt: jax.ShapeDtypeStruct((nh, PEER_KEYS, T), dt)
    whole = lambda a: pl.BlockSpec(a.shape, lambda t: (0,) * a.ndim)
    mod = pl.BlockSpec((1, 1, D_MODEL), lambda t: (t, 0, 0))
    return pl.pallas_call(
        _peer_route_kernel,
        grid=(T // tb,),
        in_specs=[pl.BlockSpec((tb, D_MODEL), lambda t: (t, 0)), whole(g), mod, mod,
                  whole(wq_hi), whole(wq_lo), whole(k_hi), whole(k_lo)],
        out_specs=[blk, blk, blk, blk, pl.BlockSpec((tb, D_MODEL), lambda t: (t, 0))],
        out_shape=[sd(jnp.float32), sd(jnp.float32), sd(jnp.bfloat16), sd(jnp.bfloat16),
                   jax.ShapeDtypeStruct((T, D_MODEL), jnp.bfloat16)],
        scratch_shapes=[pltpu.VMEM((nh, 2, PEER_KEYS, tb), jnp.float32)],
        compiler_params=pltpu.CompilerParams(dimension_semantics=("arbitrary",),
                                             vmem_limit_bytes=VMEM_LIMIT),
        name="peer_route",
    )(x, g, shift_t, scale_t, wq_hi, wq_lo, k_hi, k_lo)


def _gelu_tanh(x):
    hx = 0.5 * x
    return hx * jnp.tanh(x * (x * x * (0.7978845608028654 * 0.044715) + 0.7978845608028654)) + hx


def _peer_expert_kernel(h_ref, u_ref, vt_even_ref, vt_prev_ref, vt_last_ref, n1_ref, e1_ref, r2_ref, e2_ref,
                        x_ref, gate_ref, o_ref, acc_ref, a0_scr, a1_scr, w0_scr, w1_scr):
    c = pl.program_id(1)
    ec, tb = a0_scr.shape
    n_i = ec // PEER_KEYS
    bf16 = jnp.bfloat16
    f32 = jnp.float32

    @pl.when(c == 0)
    def _():
        acc_ref[...] = jnp.zeros_like(acc_ref)
        w1_scr[...] = jnp.zeros_like(w1_scr)

    def gate_times_act(a_scr, w_scr, chunk):
        igrp = pl.ds(pl.multiple_of(chunk * n_i, SUBLANES), SUBLANES)
        for lt in range(tb // GATE_LANES):
            ln = slice(lt * GATE_LANES, (lt + 1) * GATE_LANES)
            for ii in range(n_i):
                rows = slice(ii * PEER_KEYS, (ii + 1) * PEER_KEYS)
                g = jnp.zeros((PEER_KEYS, GATE_LANES), bf16)
                for hh in range(PEER_HEADS):
                    n1 = jnp.broadcast_to(n1_ref[hh, igrp, ln][ii:ii + 1], (PEER_KEYS, GATE_LANES)).astype(bf16)
                    e1 = jnp.broadcast_to(e1_ref[hh, igrp, ln][ii:ii + 1], (PEER_KEYS, GATE_LANES)).astype(bf16)
                    g = g + jnp.where(r2_ref[hh, :, ln] < n1, e2_ref[hh, :, ln] * e1, jnp.zeros((), bf16))
                w_scr[rows, ln] = g * _gelu_tanh(a_scr[rows, ln])

    h = h_ref[...]
    a0_scr[...] = lax.dot_general(u_ref[:ec], h, _NT, preferred_element_type=f32).astype(bf16)
    acc_ref[...] += jnp.dot(vt_prev_ref[...], w1_scr[...], preferred_element_type=f32)
    gate_times_act(a0_scr, w0_scr, 2 * c)
    a1_scr[...] = lax.dot_general(u_ref[ec:], h, _NT, preferred_element_type=f32).astype(bf16)
    acc_ref[...] += jnp.dot(vt_even_ref[...], w0_scr[...], preferred_element_type=f32)
    gate_times_act(a1_scr, w1_scr, 2 * c + 1)

    @pl.when(c == pl.num_programs(1) - 1)
    def _():
        y = acc_ref[...] + jnp.dot(vt_last_ref[...], w1_scr[...], preferred_element_type=f32)
        o_ref[...] = x_ref[...] + gate_ref[0] * y.T


def peer_experts(h_bf16, u_bf16, vt_bf16, n1, e1, r2, e2, x, gate_t, tb=PEER_EXPERT_TOKENS, ec=SUBLANES * PEER_KEYS):
    T = h_bf16.shape[0]
    n_steps = PEER_N // (2 * ec)
    rblk = pl.BlockSpec((PEER_HEADS, PEER_KEYS, tb), lambda t, c: (0, 0, t))
    vt_blk = lambda chunk_of: pl.BlockSpec((D_MODEL, ec), lambda t, c: (0, chunk_of(c)))
    return pl.pallas_call(
        _peer_expert_kernel,
        grid=(T // tb, n_steps),
        in_specs=[pl.BlockSpec((tb, D_MODEL), lambda t, c: (t, 0)),
                  pl.BlockSpec((2 * ec, D_MODEL), lambda t, c: (c, 0)),
                  vt_blk(lambda c: 2 * c),
                  vt_blk(lambda c: jnp.maximum(2 * c - 1, 0)),
                  vt_blk(lambda c: 2 * n_steps - 1),
                  rblk, rblk, rblk, rblk,
                  pl.BlockSpec((tb, D_MODEL), lambda t, c: (t, 0)),
                  pl.BlockSpec((1, 1, D_MODEL), lambda t, c: (t, 0, 0))],
        out_specs=pl.BlockSpec((tb, D_MODEL), lambda t, c: (t, 0)),
        out_shape=jax.ShapeDtypeStruct((T, D_MODEL), jnp.float32),
        scratch_shapes=[pltpu.VMEM((D_MODEL, tb), jnp.float32)] + [pltpu.VMEM((ec, tb), jnp.bfloat16)] * 4,
        compiler_params=pltpu.CompilerParams(dimension_semantics=("arbitrary", "arbitrary"),
                                             vmem_limit_bytes=VMEM_LIMIT),
        name="peer_experts",
    )(h_bf16, u_bf16, vt_bf16, vt_bf16, vt_bf16, n1, e1, r2, e2, x, gate_t)


def peer_weights(w_q, sub_keys, u_tab, v_tab):
    return _split_bf16(w_q) + _split_bf16(sub_keys) + (u_tab.astype(jnp.bfloat16), v_tab.T.astype(jnp.bfloat16))


def peer_block(x, g, mods_route, mods_expert, weights):
    wq_hi, wq_lo, k_hi, k_lo, u_bf16, vt_bf16 = weights
    n1, e1, r2, e2, h_bf16 = peer_route(x, g[None], mods_route[0], mods_route[1], wq_hi, wq_lo, k_hi, k_lo)
    return peer_experts(h_bf16, u_bf16, vt_bf16, n1, e1, r2, e2, x, mods_expert)


def kernel(x_prompt, x_sample, state_rwkv_fwd, state_rwkv_bwd, cache_mla_ckv, cache_mla_kpe, c, c_ctx,
           norm_g, w_mod, b_mod, ab_w_in, rw_mu, rw_w0, rw_w2, rw_a0, rw_a2, rw_g2, rw_k_k, rw_k_a, rw_r_k,
           rw_lnx_g, rw_lnx_b, mla_q_norm, mla_q_up, mla_kv_norm, mla_kv_up, mla_qn, mla_kn, ab_w_out,
           hy_w_in, hy_b_in, hy_conv_w, hy_conv_b, hy_f_w1, hy_f_b1, hy_f_w2, hy_f_b2, hy_f_w3, hy_f_freq,
           hy_bias, hy_w_out, peer_w_q, peer_keys, peer_u, peer_v):
    rope = axial_rope(x_sample.shape[1])
    D = D_MODEL
    groups = [dict(x=x_prompt.reshape(-1, D), B=x_prompt.shape[0], L=x_prompt.shape[1]),
              dict(x=x_sample.reshape(-1, D), B=x_sample.shape[0], L=x_sample.shape[1])]
    st_f, st_b, st_ckv, st_kpe = [], [], [], []
    for li in range(DEPTH):
        j = li // 2
        groups[0]["mod"] = (_mm(jax.nn.silu(c_ctx)[None], w_mod[li]) + b_mod[li]).reshape(1, 6, D)
        groups[1]["mod"] = (_mm(jax.nn.silu(c), w_mod[li]) + b_mod[li]).reshape(-1, 6, D)
        peer_w = peer_weights(peer_w_q[li], peer_keys[li], peer_u[li], peer_v[li])
        for gi, g in enumerate(groups):
            x, B, L = g["x"], g["B"], g["L"]
            mod = lambda i, tm: tile_mod(g["mod"][:, i], B * L, tm)
            if li % 2 == 0:
                proj = norm_mod_matmul(x, norm_g[li, 0], mod(0, ROW_TILE), mod(1, ROW_TILE), ab_w_in[j],
                                       jnp.zeros((ab_w_in.shape[-1],), jnp.float32), ROW_TILE)
                prep = rwkv_prep(proj, L, rw_mu[j], rw_w0[j], rw_w2[j], rw_a0[j], rw_a2[j], rw_g2[j],
                                 rw_k_k[j], rw_k_a[j], rw_r_k[j], SEQ_TILE)
                s0s = None if gi == 0 else (state_rwkv_fwd[:, j], state_rwkv_bwd[:, j])
                ys, (sf, sb) = rwkv_scans(prep[:9], B, L, s0s)
                ctx = None if gi == 0 else (cache_mla_ckv[:, j], cache_mla_kpe[:, j], rope)
                att, ckv, kpe = mla_mixer(proj[:, RW_IN:], B, L, ctx,
                                          (mla_q_norm[j], mla_q_up[j], mla_kv_norm[j], mla_kv_up[j], mla_qn[j], mla_kn[j]))
                if gi == 0:
                    st_f.append(sf)
                    st_b.append(sb)
                    st_ckv.append(ckv)
                    st_kpe.append(kpe)
                x = rwkv_out_residual(ys[0], ys[1], prep[10], prep[9], att, rw_lnx_g[j], rw_lnx_b[j],
                                      ab_w_out[j], x, mod(2, ROW_TILE), ROW_TILE)
            else:
                proj = norm_mod_matmul(x, norm_g[li, 0], mod(0, ROW_TILE), mod(1, ROW_TILE), hy_w_in[j], hy_b_in[j],
                                       ROW_TILE)
                x0, zin = hyena_prep(proj, L, hy_conv_w[j], hy_conv_b[j], SEQ_TILE)
                circ = hyena_filters(L, hy_f_w1[j], hy_f_b1[j], hy_f_w2[j], hy_f_b2[j], hy_f_w3[j], hy_f_freq[j])
                conv = hyena_longconv(zin.reshape(B, L, D), circ).reshape(B * L, D)
                x = hyena_out_residual(x0, conv, zin, hy_bias[j], hy_w_out[j], x, mod(2, ROW_TILE), ROW_TILE)
            g["x"] = peer_block(x, norm_g[li, 1], (mod(3, PEER_ROUTE_TOKENS), mod(4, PEER_ROUTE_TOKENS)),
                                mod(5, PEER_EXPERT_TOKENS), peer_w)
    xp = groups[0]["x"].reshape(x_prompt.shape)
    xs = groups[1]["x"].reshape(x_sample.shape)
    new_state_rwkv_fwd = jnp.stack(st_f, axis=1).astype(x_prompt.dtype)
    new_state_rwkv_bwd = jnp.stack(st_b, axis=1).astype(x_prompt.dtype)
    new_cache_mla_ckv = jnp.stack(st_ckv, axis=1)
    new_cache_mla_kpe = jnp.stack(st_kpe, axis=1)
    return (xp, xs, new_state_rwkv_fwd, new_state_rwkv_bwd, new_cache_mla_ckv, new_cache_mla_kpe)
```

```python
import math
from functools import lru_cache, partial

import jax
import jax.numpy as jnp
import numpy as np
from jax import lax
from jax.experimental import pallas as pl
from jax.experimental.pallas import tpu as pltpu

D_MODEL = 1024
DEPTH = 2
GRID_W = 64
EPS = 1e-6
RW_HEADS = 8
RW_HEAD = 64
RW_DIM = RW_HEADS * RW_HEAD
W_LORA = 64
A_LORA = 64
G_LORA = 128
LNX_EPS = 64e-5
RW_IN = 3 * RW_DIM + W_LORA + A_LORA + G_LORA
RW_SPLITS = (RW_DIM, 2 * RW_DIM, 3 * RW_DIM, 3 * RW_DIM + W_LORA, 3 * RW_DIM + W_LORA + A_LORA)
MLA_HEADS = 4
QK_NOPE = 128
QK_ROPE = 64
QK_HEAD = QK_NOPE + QK_ROPE
V_HEAD = 128
Q_LORA = 256
KV_LORA = 128
MLA_DIM = MLA_HEADS * V_HEAD
ROPE_THETA = 10000.0
HY_BANDS = 16
HY_TARGET = 1e-2
HY_FAST = 0.3
HY_SLOW = 1.5
PEER_KEYS = 128
PEER_HEADS = 8
PEER_DKEY = 256
PEER_TOPK = 16

LANES = 128
SUBLANES = 8
V7X_VMEM_BYTES = 64 * 1024 * 1024
VMEM_LIMIT = V7X_VMEM_BYTES * 7 // 8
_NT = (((1,), (1,)), ((), ()))


def _split_bf16(x):
    hi = x.astype(jnp.bfloat16)
    lo = (x - hi.astype(jnp.float32)).astype(jnp.bfloat16)
    return hi, lo


def _dot3(a_hi, a_lo, b_hi, b_lo):
    f32 = jnp.float32
    return (jnp.dot(a_hi, b_hi, preferred_element_type=f32) + jnp.dot(a_lo, b_hi, preferred_element_type=f32)
            + jnp.dot(a_hi, b_lo, preferred_element_type=f32))


def _mm_kernel(a_ref, b_ref, o_ref, *, split):
    if split:
        a_hi, a_lo = _split_bf16(a_ref[...])
        b_hi, b_lo = _split_bf16(b_ref[...])
        o_ref[...] = _dot3(a_hi, a_lo, b_hi, b_lo)
    else:
        o_ref[...] = jnp.dot(a_ref[...].astype(jnp.bfloat16), b_ref[...].astype(jnp.bfloat16),
                             preferred_element_type=jnp.float32)


def _mm(a, b, tm=512, tn=512, split=False):
    lead = a.shape[:-1]
    K = a.shape[-1]
    N = b.shape[-1]
    a2 = a.reshape(-1, K)
    if K % LANES:
        kp = -K % LANES
        a2 = jnp.pad(a2, ((0, 0), (0, kp)))
        b = jnp.pad(b, ((0, kp), (0, 0)))
        K += kp
    M = a2.shape[0]
    tm = min(tm, M)
    tn = min(tn, N)
    if N % tn:
        tn = N
    assert M % tm == 0 and N % tn == 0
    out = pl.pallas_call(
        partial(_mm_kernel, split=split),
        grid=(M // tm, N // tn),
        in_specs=[pl.BlockSpec((tm, K), lambda i, j: (i, 0)),
                  pl.BlockSpec((K, tn), lambda i, j: (0, j))],
        out_specs=pl.BlockSpec((tm, tn), lambda i, j: (i, j)),
        out_shape=jax.ShapeDtypeStruct((M, N), jnp.float32),
        name="matmul",
    )(a2, b)
    return out.reshape(*lead, N)


def axial_rope(L):
    rows = L // GRID_W
    row = jnp.repeat(jnp.arange(rows, dtype=jnp.float32), GRID_W)
    col = jnp.tile(jnp.arange(GRID_W, dtype=jnp.float32), rows)
    n_freq = QK_ROPE // 4
    inv = ROPE_THETA ** (-jnp.arange(n_freq, dtype=jnp.float32) / n_freq)
    ang = jnp.concatenate([row[:, None] * inv, col[:, None] * inv], axis=-1)
    return jnp.cos(ang), jnp.sin(ang)


def _mla_attn_kernel(q_ref, k_ref, v_ref, *rest, with_ctx):
    if with_ctx:
        kc_ref, vc_ref, o_ref = rest
    else:
        (o_ref,) = rest
    bf16 = jnp.bfloat16
    f32 = jnp.float32
    scale = QK_HEAD ** -0.5
    for h in range(MLA_HEADS):
        qs = slice(h * QK_HEAD, (h + 1) * QK_HEAD)
        vs = slice(h * V_HEAD, (h + 1) * V_HEAD)
        q = q_ref[0, :, qs].astype(bf16)
        s = lax.dot_general(q, k_ref[0, :, qs].astype(bf16), _NT, preferred_element_type=f32) * scale
        m = jnp.max(s, axis=-1, keepdims=True)
        if with_ctx:
            sc = lax.dot_general(q, kc_ref[0, :, qs].astype(bf16), _NT, preferred_element_type=f32) * scale
            m = jnp.maximum(m, jnp.max(sc, axis=-1, keepdims=True))
            pc = jnp.exp(sc - m)
        p = jnp.exp(s - m)
        l = jnp.sum(p, axis=-1, keepdims=True)
        if with_ctx:
            l = l + jnp.sum(pc, axis=-1, keepdims=True)
        o = jnp.dot((p / l).astype(bf16), v_ref[0, :, vs].astype(bf16), preferred_element_type=f32)
        if with_ctx:
            o = o + jnp.dot((pc / l).astype(bf16), vc_ref[0, :, vs].astype(bf16), preferred_element_type=f32)
        o_ref[0, :, vs] = o


def attend(q, k, v, ctx=None, tq=256):
    B, Lq, _ = q.shape
    tq = min(tq, Lq)
    kv = [k, v] + (list(ctx) if ctx is not None else [])
    full = lambda a: pl.BlockSpec((1,) + a.shape[1:], lambda b, i: (b, 0, 0))
    return pl.pallas_call(
        partial(_mla_attn_kernel, with_ctx=ctx is not None), grid=(B, Lq // tq),
        in_specs=[pl.BlockSpec((1, tq, MLA_HEADS * QK_HEAD), lambda b, i: (b, i, 0))] + [full(a) for a in kv],
        out_specs=pl.BlockSpec((1, tq, MLA_HEADS * V_HEAD), lambda b, i: (b, i, 0)),
        out_shape=jax.ShapeDtypeStruct((B, Lq, MLA_HEADS * V_HEAD), jnp.float32),
        compiler_params=pltpu.CompilerParams(dimension_semantics=("arbitrary", "arbitrary"),
                                             vmem_limit_bytes=VMEM_LIMIT),
        name="mla_attend",
    )(q, *kv)


def _first_row_block(fn):
    pl.when(pl.program_id(2) == 0)(fn)


def _dft_pair_kernel(c_tab, s_tab, x_ref, oc_ref, os_ref, x16):
    def _():
        x16[...] = x_ref[0].astype(jnp.bfloat16)
    _first_row_block(_)
    oc_ref[0] = jnp.dot(c_tab[...], x16[...], preferred_element_type=jnp.float32)
    os_ref[0] = jnp.dot(s_tab[...], x16[...], preferred_element_type=jnp.float32)


def _dft_spectral_kernel(c_tab, s_tab, x_ref, hr_ref, hi_ref, yr_ref, yi_ref, x16):
    def _():
        x16[...] = x_ref[0].astype(jnp.bfloat16)
    _first_row_block(_)
    rb, L = c_tab.shape
    zc = jnp.dot(c_tab[...], x16[...], preferred_element_type=jnp.float32)
    zs = jnp.dot(s_tab[...], x16[...], preferred_element_type=jnp.float32)
    hr = hr_ref[...]
    hi = hi_ref[...]
    f = pl.program_id(2) * rb + lax.broadcasted_iota(jnp.int32, zc.shape, 0)
    yr_ref[0] = jnp.where(f == 0, zc * hr * (0.5 / L), (zc * hr - zs * hi) * (1.0 / L))
    yi_ref[0] = jnp.where(f == 0, zs * hi * (0.5 / L), (zc * hi + zs * hr) * (1.0 / L))


def _dft_sum_kernel(c_tab, s_tab, x_ref, y_ref, o_ref, x16, y16):
    def _():
        x16[...] = x_ref[0].astype(jnp.bfloat16)
        y16[...] = y_ref[0].astype(jnp.bfloat16)
    _first_row_block(_)
    o_ref[0] = (jnp.dot(c_tab[...], x16[...], preferred_element_type=jnp.float32)
                + jnp.dot(s_tab[...], y16[...], preferred_element_type=jnp.float32))


@lru_cache(maxsize=None)
def dft_tables(L):
    f = np.arange(L, dtype=np.int64)
    ang = ((f[:, None] * f[None, :]) % (2 * L)).astype(np.float64) * (math.pi / L)
    c = np.cos(ang).astype(jnp.bfloat16)
    s = -np.sin(ang)
    alt = np.where(f % 2 == 0, 1.0, -1.0)
    s_ana = np.where(f[:, None] == 0, alt[None, :], s).astype(jnp.bfloat16)
    s_syn = np.where(f[None, :] == 0, alt[:, None], s).astype(jnp.bfloat16)
    return (c, s_ana), (c, s_syn)


def _dft_call(body, tabs, xs, n_out, name, row_inputs=(), rb=256, nb=512):
    B, L, N = xs[0].shape
    rb = min(rb, L)
    nb = min(nb, N)
    tab = pl.BlockSpec((rb, L), lambda b, n, i: (i, 0))
    xin = pl.BlockSpec((1, L, nb), lambda b, n, i: (b, 0, n))
    rin = pl.BlockSpec((rb, nb), lambda b, n, i: (i, n))
    out = pl.BlockSpec((1, rb, nb), lambda b, n, i: (b, i, n))
    sd = jax.ShapeDtypeStruct((B, L, N), jnp.float32)
    return pl.pallas_call(
        body, grid=(B, N // nb, L // rb),
        in_specs=[tab] * len(tabs) + [xin] * len(xs) + [rin] * len(row_inputs),
        out_specs=[out] * n_out if n_out > 1 else out,
        out_shape=[sd] * n_out if n_out > 1 else sd,
        scratch_shapes=[pltpu.VMEM((L, nb), jnp.bfloat16)] * len(xs),
        compiler_params=pltpu.CompilerParams(dimension_semantics=("arbitrary",) * 3,
                                             vmem_limit_bytes=VMEM_LIMIT),
        name=name,
    )(*tabs, *xs, *row_inputs)


def hyena_longconv(zin, circ):
    B, L, D = zin.shape
    tabs, tabs_syn = dft_tables(L)
    alt = jnp.where(jnp.arange(L) % 2 == 0, 1.0, -1.0).astype(jnp.float32)[:, None]
    fc, fs = _dft_call(_dft_pair_kernel, tabs, [jnp.concatenate([circ[:L], circ[L:]], axis=-1)[None]], 2, "dft_filter")
    hr = fc[0, :, :D] + alt * fc[0, :, D:]
    hi = fs[0, :, :D] + alt * fs[0, :, D:]
    yr, yi = _dft_call(_dft_spectral_kernel, tabs, [zin], 2, "dft_analysis", row_inputs=(hr, hi))
    return _dft_call(_dft_sum_kernel, tabs_syn, [yr, yi], 1, "dft_synthesis")


SCAN_ACCS = 4


def _rwkv_scan_kernel(rf_ref, kkf_ref, vf_ref, wf_ref, kdf_ref, bf_ref, rb_ref, kkb_ref, vb_ref, wb_ref, kdb_ref, bb_ref,
                      s0f_ref, s0b_ref, yf_ref, yb_ref, sff_ref, sfb_ref, sf_scr, sb_scr):
    c = pl.program_id(0)
    tc, nk, _ = rf_ref.shape
    nv = vf_ref.shape[1]
    groups = RW_HEAD // nk

    @pl.when(c == 0)
    def _():
        sf_scr[...] = s0f_ref[...]
        sb_scr[...] = s0b_ref[...]

    def row(ref, tt, k):
        return jnp.broadcast_to(ref[tt, pl.ds(k, 1), :], (nv, LANES))

    def all_groups(p):
        tot = p
        for q in range(1, groups):
            tot = tot + pltpu.roll(p, q * (LANES // groups), 1)
        return tot

    def one_step(tt, r_ref, kk_ref, v_ref, w_ref, kd_ref, b_ref, s_scr, y_ref):
        parts = [None] * SCAN_ACCS
        for k in range(nk):
            term = s_scr[k] * row(kk_ref, tt, k)
            parts[k % SCAN_ACCS] = term if parts[k % SCAN_ACCS] is None else parts[k % SCAN_ACCS] + term
        sa = -all_groups((parts[0] + parts[1]) + (parts[2] + parts[3]))
        vv = v_ref[tt]
        parts = [None] * SCAN_ACCS
        for k in range(nk):
            s_new = s_scr[k] * row(w_ref, tt, k) + sa * row(b_ref, tt, k) + vv * row(kd_ref, tt, k)
            s_scr[k] = s_new
            term = s_new * row(r_ref, tt, k)
            parts[k % SCAN_ACCS] = term if parts[k % SCAN_ACCS] is None else parts[k % SCAN_ACCS] + term
        y_ref[tt] = all_groups((parts[0] + parts[1]) + (parts[2] + parts[3]))

    def step(t, carry):
        one_step(t, rf_ref, kkf_ref, vf_ref, wf_ref, kdf_ref, bf_ref, sf_scr, yf_ref)
        one_step(tc - 1 - t, rb_ref, kkb_ref, vb_ref, wb_ref, kdb_ref, bb_ref, sb_scr, yb_ref)
        return carry

    lax.fori_loop(0, tc, step, 0, unroll=2)

    @pl.when(c == pl.num_programs(0) - 1)
    def _():
        sff_ref[...] = sf_scr[...]
        sfb_ref[...] = sb_scr[...]


def rwkv_scan(r, kk, v, wkb_fwd, wkb_bwd, s0_fwd, s0_bwd, tc=32):
    L, nk, _ = r.shape
    nv = v.shape[1]
    nc = L // tc
    fwd = lambda rows: pl.BlockSpec((tc, rows, LANES), lambda c: (c, 0, 0))
    bwd = lambda rows: pl.BlockSpec((tc, rows, LANES), lambda c: (nc - 1 - c, 0, 0))
    state = pl.BlockSpec((nk, nv, LANES), lambda c: (0, 0, 0))
    y_sd = jax.ShapeDtypeStruct((L, nv, LANES), jnp.float32)
    s_sd = jax.ShapeDtypeStruct((nk, nv, LANES), jnp.float32)
    return pl.pallas_call(
        _rwkv_scan_kernel,
        grid=(nc,),
        in_specs=[fwd(nk), fwd(nk), fwd(nv), fwd(nk), fwd(nk), fwd(nk),
                  bwd(nk), bwd(nk), bwd(nv), bwd(nk), bwd(nk), bwd(nk), state, state],
        out_specs=[fwd(nv), bwd(nv), state, state],
        out_shape=[y_sd, y_sd, s_sd, s_sd],
        scratch_shapes=[pltpu.VMEM((nk, nv, LANES), jnp.float32)] * 2,
        compiler_params=pltpu.CompilerParams(dimension_semantics=("arbitrary",), vmem_limit_bytes=VMEM_LIMIT),
        name="rwkv_scan",
    )(r, kk, v, *wkb_fwd, r, kk, v, *wkb_bwd, s0_fwd, s0_bwd)


def k_to_lanes(x, ksplit):
    B, L, H, N = x.shape
    nk = N // ksplit
    return x.reshape(B, L, H, ksplit, nk).transpose(1, 4, 3, 0, 2).reshape(L, nk, ksplit * B * H)


def v_to_lanes(x, ksplit):
    B, L, H, N = x.shape
    return jnp.tile(x.transpose(1, 3, 0, 2).reshape(L, N, B * H), (1, 1, ksplit))


def v_from_lanes(y, B, H):
    L, N, _ = y.shape
    return y[:, :, :B * H].reshape(L, N, B, H).transpose(2, 0, 3, 1)


def state_to_lanes(s, ksplit):
    B, H, N, K = s.shape
    nk = K // ksplit
    return s.reshape(B, H, N, ksplit, nk).transpose(4, 2, 3, 0, 1).reshape(nk, N, ksplit * B * H)


def state_from_lanes(s, B, H, ksplit):
    nk, N, _ = s.shape
    return s.reshape(nk, N, ksplit, B, H).transpose(3, 4, 1, 2, 0).reshape(B, H, N, ksplit * nk)


def rwkv_scan_both(rh, kk, vh, decay2, kd2, b2, s0_2, tc=32):
    B, L, H, N = rh.shape
    ksplit = LANES // (B * H)
    assert B * H * ksplit == LANES
    wkb = [[k_to_lanes(t[d], ksplit) for t in (decay2, kd2, b2)] for d in range(2)]
    y_f, y_b, sf_f, sf_b = rwkv_scan(k_to_lanes(rh, ksplit), k_to_lanes(kk, ksplit), v_to_lanes(vh, ksplit),
                                     wkb[0], wkb[1], state_to_lanes(s0_2[0], ksplit), state_to_lanes(s0_2[1], ksplit),
                                     tc=tc)
    return ([v_from_lanes(y_f, B, H), v_from_lanes(y_b, B, H)],
            [state_from_lanes(sf_f, B, H, ksplit), state_from_lanes(sf_b, B, H, ksplit)])


def _rms_mod(x, g, shift, scale):
    y = x * lax.rsqrt(jnp.mean(x * x, axis=-1, keepdims=True) + EPS)
    return (y * g) * (1.0 + scale) + shift


def tile_mod(m, n_rows, tm):
    return jnp.repeat(m, n_rows // m.shape[0] // tm, axis=0)[:, None, :]


def _norm_mod_matmul_kernel(x_ref, g_ref, sh_ref, sc_ref, w_ref, b_ref, o_ref):
    h = _rms_mod(x_ref[...], g_ref[...], sh_ref[0], sc_ref[0])
    o_ref[...] = jnp.dot(h.astype(jnp.bfloat16), w_ref[...], preferred_element_type=jnp.float32) + b_ref[...]


def norm_mod_matmul(x, g, shift_t, scale_t, w, b, tm):
    T, D = x.shape
    N = w.shape[1]
    mod = pl.BlockSpec((1, 1, D), lambda i: (i, 0, 0))
    return pl.pallas_call(
        _norm_mod_matmul_kernel, grid=(T // tm,),
        in_specs=[pl.BlockSpec((tm, D), lambda i: (i, 0)), pl.BlockSpec((1, D), lambda i: (0, 0)), mod, mod,
                  pl.BlockSpec((D, N), lambda i: (0, 0)), pl.BlockSpec((1, N), lambda i: (0, 0))],
        out_specs=pl.BlockSpec((tm, N), lambda i: (i, 0)),
        out_shape=jax.ShapeDtypeStruct((T, N), jnp.float32),
        compiler_params=pltpu.CompilerParams(dimension_semantics=("arbitrary",), vmem_limit_bytes=VMEM_LIMIT),
        name="norm_mod_matmul",
    )(x, g[None], shift_t, scale_t, w.astype(jnp.bfloat16), b[None])


def _head_sums(x, ones_bd):
    hi, lo = _split_bf16(x)
    return (jnp.dot(hi, ones_bd, preferred_element_type=jnp.float32)
            + jnp.dot(lo, ones_bd, preferred_element_type=jnp.float32))


def _with_neighbours(x_ref, prev_ref, next_ref, first_ref, last_ref, ncols):
    tm = x_ref.shape[0]
    x = x_ref[:, :ncols]
    row = lax.broadcasted_iota(jnp.int32, (tm, 1), 0)
    prev_row = prev_ref[SUBLANES - 1:SUBLANES, :ncols] * (1.0 - first_ref[0, :, :1])
    next_row = next_ref[0:1, :ncols] * (1.0 - last_ref[0, :, :1])
    prev = jnp.where(row == 0, prev_row, pltpu.roll(x, 1, 0))
    nxt = jnp.where(row == tm - 1, next_row, pltpu.roll(x, tm - 1, 0))
    return x, prev, nxt


def _neighbour_specs(T, C, L, tm):
    n_tiles = T // tm
    hb = tm // SUBLANES
    start = np.arange(n_tiles) * tm % L
    first = np.broadcast_to((start == 0).astype(np.float32)[:, None, None], (n_tiles, 1, LANES))
    last = np.broadcast_to((start + tm == L).astype(np.float32)[:, None, None], (n_tiles, 1, LANES))
    flag = pl.BlockSpec((1, 1, LANES), lambda i: (i, 0, 0))
    specs = [pl.BlockSpec((tm, C), lambda i: (i, 0)),
             pl.BlockSpec((SUBLANES, C), lambda i: (jnp.maximum(i * hb - 1, 0), 0)),
             pl.BlockSpec((SUBLANES, C), lambda i: (jnp.minimum((i + 1) * hb, T // SUBLANES - 1), 0)),
             flag, flag]
    return specs, (jnp.asarray(first), jnp.asarray(last))


def _rwkv_prep_kernel(proj_ref, prev_ref, next_ref, first_ref, last_ref, mu_ref, w0_ref, w2_ref, a0_ref, a2_ref,
                      g2_ref, kk_ref, ka_ref, rk_ref, ones_ref,
                      r_out, kk_out, v_out, w_out0, w_out1, kd_out0, kd_out1, b_out0, b_out1, gate_out, bonus_out):
    bf16 = jnp.bfloat16
    f32 = jnp.float32
    x, prev, nxt = _with_neighbours(proj_ref, prev_ref, next_ref, first_ref, last_ref, RW_IN)
    rw = x + mu_ref[...] * (0.5 * (prev + nxt) - x)
    r = rw[:, RW_SPLITS[0] - RW_DIM:RW_SPLITS[0]]
    k = rw[:, RW_SPLITS[0]:RW_SPLITS[1]]
    v = rw[:, RW_SPLITS[1]:RW_SPLITS[2]]
    wd = rw[:, RW_SPLITS[2]:RW_SPLITS[3]]
    ad = rw[:, RW_SPLITS[3]:RW_SPLITS[4]]
    gd = rw[:, RW_SPLITS[4]:]
    ones_bd = ones_ref[...]
    kk = k * kk_ref[...]
    kk = kk / jnp.maximum(jnp.sqrt(_head_sums(kk * kk, ones_bd)), 1e-12)
    tanh_wd = jnp.tanh(wd).astype(bf16)
    ad16 = ad.astype(bf16)
    rk = r * rk_ref[...]
    bonus = jnp.zeros_like(r)
    for d, (w_o, kd_o, b_o) in enumerate(((w_out0, kd_out0, b_out0), (w_out1, kd_out1, b_out1))):
        w_raw = w0_ref[d:d + 1, :] + jnp.dot(tanh_wd, w2_ref[d], preferred_element_type=f32)
        w_o[...] = jnp.exp(-jnp.exp(-jax.nn.softplus(-w_raw) - 0.5))
        lr = jax.nn.sigmoid(a0_ref[d:d + 1, :] + jnp.dot(ad16, a2_ref[d], preferred_element_type=f32))
        kd = k * (1.0 + (lr - 1.0) * ka_ref[...])
        kd_o[...] = kd
        b_o[...] = kk * lr
        bonus = bonus + _head_sums(rk * kd, ones_bd)
    r_out[...] = r
    kk_out[...] = kk
    v_out[...] = v
    gate_out[...] = jnp.dot(jax.nn.sigmoid(gd).astype(bf16), g2_ref[...], preferred_element_type=f32)
    bonus_out[...] = bonus * v


def rwkv_prep(proj, L, mu, w0, w2, a0, a2, g2, k_k, k_a, r_k, tm):
    T, C = proj.shape
    nb_specs, flags = _neighbour_specs(T, C, L, tm)
    row2 = lambda a: a.reshape(1, -1)
    whole = lambda a: pl.BlockSpec(a.shape, lambda i: (0,) * a.ndim)
    out = pl.BlockSpec((tm, RW_DIM), lambda i: (i, 0))
    consts = (row2(mu), w0, w2.astype(jnp.bfloat16), a0, a2.astype(jnp.bfloat16), g2.astype(jnp.bfloat16),
              row2(k_k), row2(k_a), row2(r_k), _segment_ones(RW_DIM, RW_HEAD))
    return pl.pallas_call(
        _rwkv_prep_kernel, grid=(T // tm,),
        in_specs=nb_specs + [whole(a) for a in consts],
        out_specs=[out] * 11,
        out_shape=[jax.ShapeDtypeStruct((T, RW_DIM), jnp.float32)] * 11,
        compiler_params=pltpu.CompilerParams(dimension_semantics=("arbitrary",), vmem_limit_bytes=VMEM_LIMIT),
        name="rwkv_prep",
    )(proj, proj, proj, *flags, *consts)


def _rwkv_out_kernel(y0_ref, y1_ref, bonus_ref, gate_ref, mla_ref, lng_ref, lnb_ref, ones_ref, w_ref, x_ref, mod_ref,
                     o_ref):
    bf16 = jnp.bfloat16
    f32 = jnp.float32
    ones_bd = ones_ref[...]
    y = y0_ref[...] + y1_ref[...]
    dlt = y - _head_sums(y, ones_bd) * (1.0 / RW_HEAD)
    var = _head_sums(dlt * dlt, ones_bd) * (1.0 / RW_HEAD)
    yn = dlt * lax.rsqrt(var + LNX_EPS) * lng_ref[...] + lnb_ref[...]
    rw_out = (yn + bonus_ref[...]) * gate_ref[...]
    upd = (jnp.dot(rw_out.astype(bf16), w_ref[:RW_DIM], preferred_element_type=f32)
           + jnp.dot(mla_ref[...].astype(bf16), w_ref[RW_DIM:], preferred_element_type=f32))
    o_ref[...] = x_ref[...] + mod_ref[0] * upd


def rwkv_out_residual(y0, y1, bonus, gate, mla_out, lnx_g, lnx_b, w_out, x, gate_t, tm):
    T, D = x.shape
    row = lambda n: pl.BlockSpec((tm, n), lambda i: (i, 0))
    whole = lambda a: pl.BlockSpec(a.shape, lambda i: (0,) * a.ndim)
    consts = (lnx_g.reshape(1, -1), lnx_b.reshape(1, -1), _segment_ones(RW_DIM, RW_HEAD), w_out.astype(jnp.bfloat16))
    return pl.pallas_call(
        _rwkv_out_kernel, grid=(T // tm,),
        in_specs=[row(RW_DIM)] * 4 + [row(MLA_DIM)] + [whole(a) for a in consts]
                 + [row(D), pl.BlockSpec((1, 1, D), lambda i: (i, 0, 0))],
        out_specs=row(D),
        out_shape=jax.ShapeDtypeStruct((T, D), jnp.float32),
        compiler_params=pltpu.CompilerParams(dimension_semantics=("arbitrary",), vmem_limit_bytes=VMEM_LIMIT),
        name="rwkv_out_residual",
    )(y0, y1, bonus, gate, mla_out, *consts, x, gate_t)


def rwkv_scans(prep, B, L, s0s):
    heads = lambda t: t.reshape(B, L, RW_HEADS, RW_HEAD)
    r, kk, v, w_0, w_1, kd_0, kd_1, b_0, b_1 = [heads(t) for t in prep]
    if s0s is None:
        z = jnp.zeros((B, RW_HEADS, RW_HEAD, RW_HEAD), jnp.float32)
        s0s = (z, z)
    ys, finals = rwkv_scan_both(r, kk, v, [w_0, w_1], [kd_0, kd_1], [b_0, b_1], s0s)
    return [y.reshape(B * L, RW_DIM) for y in ys], finals


def _segment_ones(n, seg):
    h = np.arange(n) // seg
    return jnp.asarray(h[:, None] == h[None, :], jnp.bfloat16)


def _mla_prep_kernel(x_ref, cos_ref, sin_ref, qnorm_ref, qup_ref, kvnorm_ref, wk_ref, wv_ref, place_ref, qn_ref, kn_ref,
                     ones_ref, q_out, k_out, v_out, ckv_out, *, rope, norm_kv):
    bf16 = jnp.bfloat16
    f32 = jnp.float32
    x = x_ref[...]
    qc = x[:, :Q_LORA]
    kvc = x[:, Q_LORA:Q_LORA + KV_LORA]
    kpe = x[:, Q_LORA + KV_LORA:]
    rms = lambda t, g: t * lax.rsqrt(jnp.mean(t * t, axis=-1, keepdims=True) + EPS) * g
    ones_seg = ones_ref[...]
    head_norm = lambda t, g: t * lax.rsqrt(_head_sums(t * t, ones_seg) * (1.0 / QK_HEAD) + EPS) * g
    q = jnp.dot(rms(qc, qnorm_ref[...]).astype(bf16), qup_ref[...], preferred_element_type=f32)
    ckv = rms(kvc, kvnorm_ref[...]) if norm_kv else kvc
    ckv_out[...] = ckv
    c16 = ckv.astype(bf16)
    kpe_hi, kpe_lo = _split_bf16(kpe)
    k = (jnp.dot(c16, wk_ref[...], preferred_element_type=f32)
         + jnp.dot(kpe_hi, place_ref[...], preferred_element_type=f32)
         + jnp.dot(kpe_lo, place_ref[...], preferred_element_type=f32))
    v_out[...] = jnp.dot(c16, wv_ref[...], preferred_element_type=f32)
    q = head_norm(q, qn_ref[...])
    k = head_norm(k, kn_ref[...])
    if rope:
        n = q.shape[1]
        even = lax.broadcasted_iota(jnp.int32, (1, n), 1) % 2 == 0
        swap = lambda t: jnp.where(even, pltpu.roll(t, n - 1, 1), pltpu.roll(t, 1, 1))
        cos = cos_ref[...]
        sin = sin_ref[...]
        q = q * cos + swap(q) * sin
        k = k * cos + swap(k) * sin
    q_out[...] = q
    k_out[...] = k


def mla_prep(x, L, rope, weights, norm_kv=True):
    q_norm, q_up, kv_norm, kv_up, qn, kn = weights
    T, C = x.shape
    tm = min(ROW_TILE, L)
    nq = MLA_HEADS * QK_HEAD
    kv4 = kv_up.reshape(KV_LORA, MLA_HEADS, QK_NOPE + V_HEAD)
    wk = jnp.pad(kv4[:, :, :QK_NOPE], ((0, 0), (0, 0), (0, QK_ROPE))).reshape(KV_LORA, nq).astype(jnp.bfloat16)
    wv = kv4[:, :, QK_NOPE:].reshape(KV_LORA, MLA_HEADS * V_HEAD).astype(jnp.bfloat16)
    slot = np.arange(nq) % QK_HEAD - QK_NOPE
    place = jnp.asarray(np.arange(QK_ROPE)[:, None] == slot[None, :], jnp.bfloat16)
    if rope is not None:
        cos, sin = rope
        pair = np.maximum(slot, 0) // 2
        sign = np.where(slot % 2 == 0, -1.0, 1.0).astype(np.float32)
        cos_f = jnp.where(slot >= 0, cos[:, pair], 1.0)
        sin_f = jnp.where(slot >= 0, sin[:, pair] * sign, 0.0)
    else:
        cos_f = sin_f = jnp.zeros((tm, nq), jnp.float32)
    tiles_per_seq = max(L // tm, 1)
    rope_blk = pl.BlockSpec((tm, nq), (lambda i: (i % tiles_per_seq, 0)) if rope is not None else (lambda i: (0, 0)))
    row2 = lambda a: a.reshape(1, -1)
    whole = lambda a: pl.BlockSpec(a.shape, lambda i: (0,) * a.ndim)
    consts = (row2(q_norm), q_up.astype(jnp.bfloat16), row2(kv_norm), wk, wv, place,
              row2(jnp.tile(qn, MLA_HEADS)), row2(jnp.tile(kn, MLA_HEADS)), _segment_ones(nq, QK_HEAD))
    out = lambda n: pl.BlockSpec((tm, n), lambda i: (i, 0))
    sd = lambda n: jax.ShapeDtypeStruct((T, n), jnp.float32)
    return pl.pallas_call(
        partial(_mla_prep_kernel, rope=rope is not None, norm_kv=norm_kv), grid=(T // tm,),
        in_specs=[pl.BlockSpec((tm, C), lambda i: (i, 0)), rope_blk, rope_blk] + [whole(a) for a in consts],
        out_specs=[out(nq), out(nq), out(MLA_HEADS * V_HEAD), out(KV_LORA)],
        out_shape=[sd(nq), sd(nq), sd(MLA_HEADS * V_HEAD), sd(KV_LORA)],
        compiler_params=pltpu.CompilerParams(dimension_semantics=("arbitrary",), vmem_limit_bytes=VMEM_LIMIT),
        name="mla_prep",
    )(x, cos_f, sin_f, *consts)


def mla_mixer(mla, B, L, ctx, weights):
    q, k, v, ckv = mla_prep(mla, L, None if ctx is None else ctx[2], weights)
    seq = lambda t: t.reshape(B, L, -1)
    kv_ctx = None
    if ctx is not None:
        c_ckv, c_kpe = ctx[0], ctx[1]
        Lc = c_ckv.shape[1]
        xc = jnp.concatenate([jnp.zeros((B, Lc, Q_LORA), jnp.float32), c_ckv, c_kpe], axis=-1).reshape(B * Lc, -1)
        _, kc, vc, _ = mla_prep(xc, Lc, None, weights, norm_kv=False)
        kv_ctx = (kc.reshape(B, Lc, -1), vc.reshape(B, Lc, -1))
    att = attend(seq(q), seq(k), seq(v), kv_ctx)
    return att.reshape(B * L, -1), ckv.reshape(B, L, -1), mla[:, Q_LORA + KV_LORA:].reshape(B, L, -1)


def _hyena_prep_kernel(proj_ref, prev_ref, next_ref, first_ref, last_ref, cw_ref, cb_ref, x0_ref, z_ref):
    D = x0_ref.shape[1]
    x, prev, nxt = _with_neighbours(proj_ref, prev_ref, next_ref, first_ref, last_ref, proj_ref.shape[1])
    u = prev * cw_ref[0:1, :] + x * cw_ref[1:2, :] + nxt * cw_ref[2:3, :] + cb_ref[...]
    x0_ref[...] = u[:, :D]
    z_ref[...] = u[:, D:2 * D] * u[:, 2 * D:]


def hyena_prep(proj, L, conv_w, conv_b, tm):
    T, C = proj.shape
    D = C // 3
    nb_specs, flags = _neighbour_specs(T, C, L, tm)
    out = pl.BlockSpec((tm, D), lambda i: (i, 0))
    return pl.pallas_call(
        _hyena_prep_kernel, grid=(T // tm,),
        in_specs=nb_specs + [pl.BlockSpec((3, C), lambda i: (0, 0)), pl.BlockSpec((1, C), lambda i: (0, 0))],
        out_specs=[out, out],
        out_shape=[jax.ShapeDtypeStruct((T, D), jnp.float32)] * 2,
        compiler_params=pltpu.CompilerParams(dimension_semantics=("arbitrary",), vmem_limit_bytes=VMEM_LIMIT),
        name="hyena_prep",
    )(proj, proj, proj, *flags, conv_w, conv_b[None])


def _hyena_out_kernel(x0_ref, conv_ref, z_ref, bias_ref, w_ref, x_ref, mod_ref, o_ref):
    a = x0_ref[...] * (conv_ref[...] + z_ref[...] * bias_ref[...])
    o_ref[...] = x_ref[...] + mod_ref[0] * jnp.dot(a.astype(jnp.bfloat16), w_ref[...],
                                                   preferred_element_type=jnp.float32)


def hyena_out_residual(x0, conv, zin, bias, w_out, x, gate_t, tm):
    T, D = x.shape
    row = pl.BlockSpec((tm, D), lambda i: (i, 0))
    return pl.pallas_call(
        _hyena_out_kernel, grid=(T // tm,),
        in_specs=[row, row, row, pl.BlockSpec((1, D), lambda i: (0, 0)), pl.BlockSpec((D, D), lambda i: (0, 0)),
                  row, pl.BlockSpec((1, 1, D), lambda i: (i, 0, 0))],
        out_specs=row,
        out_shape=jax.ShapeDtypeStruct((T, D), jnp.float32),
        compiler_params=pltpu.CompilerParams(dimension_semantics=("arbitrary",), vmem_limit_bytes=VMEM_LIMIT),
        name="hyena_out_residual",
    )(x0, conv, zin, bias[None], w_out.astype(jnp.bfloat16), x, gate_t)


def hyena_filters(L, w1, b1, w2, b2, w3, freq):
    f32 = jnp.float32
    u = jnp.arange(2 * L, dtype=jnp.int32)[:, None]
    t = jnp.where(u < L, u, 2 * L - u).astype(f32)
    t_unit = t / (L - 1)
    bands = jnp.linspace(1e-4, HY_BANDS - 1, HY_BANDS, dtype=f32)
    ang = 2.0 * math.pi * t * bands / L
    zpos = jnp.concatenate([t_unit, jnp.cos(ang), -jnp.sin(ang)], axis=-1)
    fr = freq.astype(f32)
    hid = jnp.sin(fr * (_mm(zpos, w1.astype(f32), split=True) + b1.astype(f32)))
    hid = jnp.sin(fr * (_mm(hid, w2.astype(f32), split=True) + b2.astype(f32)))
    filt = _mm(hid, w3.astype(f32), split=True)
    deltas = jnp.linspace(math.log(HY_TARGET) / HY_FAST, math.log(HY_TARGET) / HY_SLOW, D_MODEL, dtype=f32)
    window = jnp.exp(-t_unit * jnp.abs(deltas))
    circ = jnp.where(u < L, filt[:, :D_MODEL], filt[:, D_MODEL:]) * window
    circ = jnp.where(u == L, 0.0, circ)
    return circ / jnp.sum(jnp.abs(circ), axis=0, keepdims=True)


PEER_N = PEER_KEYS * PEER_KEYS
GATE_LANES = 2 * LANES
PEER_ROUTE_TOKENS = 256
PEER_EXPERT_TOKENS = 512
ROW_TILE = 512
SEQ_TILE = 256


def _top_vals(s, n, with_rank):
    vals = []
    rank = jnp.full(s.shape, float(n), jnp.float32) if with_rank else None
    for a in range(n):
        m = jnp.max(s, axis=0, keepdims=True)
        vals.append(m)
        hit = s == m
        if with_rank:
            rank = jnp.where(hit, float(a), rank)
        s = jnp.where(hit, -jnp.inf, s)
    return vals, rank


def _peer_route_kernel(x_ref, g_ref, sh_ref, sc_ref, wq_hi_ref, wq_lo_ref, k_hi_ref, k_lo_ref,
                       n1_ref, e1_ref, r2_ref, e2_ref, h_ref, s_scr):
    tb = x_ref.shape[0]
    half = PEER_DKEY // 2
    n_tiles = tb // LANES
    h_hi, h_lo = _split_bf16(_rms_mod(x_ref[...], g_ref[...], sh_ref[0], sc_ref[0]))
    h_ref[...] = h_hi
    q_hi, q_lo = _split_bf16(_dot3(h_hi, h_lo, wq_hi_ref[...], wq_lo_ref[...]))
    for hh in range(PEER_HEADS):
        for p in range(2):
            cols = slice((2 * hh + p) * half, (2 * hh + p + 1) * half)
            k_hi = k_hi_ref[hh, p]
            s_scr[hh, p] = (lax.dot_general(k_hi, q_hi[:, cols], _NT, preferred_element_type=jnp.float32)
                            + lax.dot_general(k_lo_ref[hh, p], q_hi[:, cols], _NT, preferred_element_type=jnp.float32)
                            + lax.dot_general(k_hi, q_lo[:, cols], _NT, preferred_element_type=jnp.float32))

    K = PEER_TOPK
    G = SUBLANES

    def tile(it, carry):
        hh = it // n_tiles
        ln = pl.ds(pl.multiple_of((it % n_tiles) * LANES, LANES), LANES)
        s1 = s_scr[hh, 0, :, ln]
        s2 = s_scr[hh, 1, :, ln]
        v1, _ = _top_vals(s1, K, False)
        v2l, r2 = _top_vals(s2, K, True)
        v2 = jnp.concatenate(v2l, axis=0)
        cand = [v1[0] + v2[:G], v1[0] + v2[G:]]
        cand += [v1[a] + v2[:G] for a in range(1, G)]
        cand += [jnp.concatenate(v1[G:], axis=0) + v2[0:1]]
        c = cand
        for k in range(K):
            m = c[0]
            for ci in c[1:]:
                m = jnp.maximum(m, ci)
            m = jnp.max(m, axis=0, keepdims=True)
            if k + 1 < K:
                c = [jnp.where(ci == m, -jnp.inf, ci) for ci in c]
        tau = m
        top = v1[0] + v2[0:1]
        keep = [ci >= tau for ci in cand]
        z = jnp.zeros_like(tau)
        for ci, ki in zip(cand, keep):
            z = z + jnp.sum(jnp.where(ki, jnp.exp(ci - top), 0.0), axis=0, keepdims=True)
        cnt = [jnp.sum(jnp.where(ki, 1.0, 0.0), axis=0, keepdims=True) for ki in keep[:G + 1]]
        tail = jnp.where(keep[G + 1], 1.0, 0.0)
        n_a = [cnt[0] + cnt[1]] + cnt[2:] + [tail[a:a + 1] for a in range(G)]
        n1 = jnp.zeros_like(s1)
        for a in range(K):
            n1 = jnp.where(s1 == v1[a], n_a[a], n1)
        n1_ref[hh, :, ln] = n1
        e1_ref[hh, :, ln] = jnp.exp(s1 - v1[0]) / z
        r2_ref[hh, :, ln] = r2.astype(jnp.bfloat16)
        e2_ref[hh, :, ln] = jnp.exp(s2 - v2[0:1]).astype(jnp.bfloat16)
        return carry

    lax.fori_loop(0, PEER_HEADS * n_tiles, tile, 0, unroll=2)


def peer_route(x, g, shift_t, scale_t, wq_hi, wq_lo, k_hi, k_lo, tb=PEER_ROUTE_TOKENS):
    T = x.shape[0]
    nh = PEER_HEADS
    blk = pl.BlockSpec((nh, PEER_KEYS, tb), lambda t: (0, 0, t))
    sd = lambda dt: jax.ShapeDtypeStruct((nh, PEER_KEYS, T), dt)
    whole = lambda a: pl.BlockSpec(a.shape, lambda t: (0,) * a.ndim)
    mod = pl.BlockSpec((1, 1, D_MODEL), lambda t: (t, 0, 0))
    return pl.pallas_call(
        _peer_route_kernel,
        grid=(T // tb,),
        in_specs=[pl.BlockSpec((tb, D_MODEL), lambda t: (t, 0)), whole(g), mod, mod,
                  whole(wq_hi), whole(wq_lo), whole(k_hi), whole(k_lo)],
        out_specs=[blk, blk, blk, blk, pl.BlockSpec((tb, D_MODEL), lambda t: (t, 0))],
        out_shape=[sd(jnp.float32), sd(jnp.float32), sd(jnp.bfloat16), sd(jnp.bfloat16),
                   jax.ShapeDtypeStruct((T, D_MODEL), jnp.bfloat16)],
        scratch_shapes=[pltpu.VMEM((nh, 2, PEER_KEYS, tb), jnp.float32)],
        compiler_params=pltpu.CompilerParams(dimension_semantics=("arbitrary",),
                                             vmem_limit_bytes=VMEM_LIMIT),
        name="peer_route",
    )(x, g, shift_t, scale_t, wq_hi, wq_lo, k_hi, k_lo)


def _gelu_tanh(x):
    hx = 0.5 * x
    return hx * jnp.tanh(x * (x * x * (0.7978845608028654 * 0.044715) + 0.7978845608028654)) + hx


def _peer_expert_kernel(h_ref, u_ref, vt_even_ref, vt_prev_ref, vt_last_ref, n1_ref, e1_ref, r2_ref, e2_ref,
                        x_ref, gate_ref, o_ref, acc_ref, a0_scr, a1_scr, w0_scr, w1_scr):
    c = pl.program_id(1)
    ec, tb = a0_scr.shape
    n_i = ec // PEER_KEYS
    bf16 = jnp.bfloat16
    f32 = jnp.float32

    @pl.when(c == 0)
    def _():
        acc_ref[...] = jnp.zeros_like(acc_ref)
        w1_scr[...] = jnp.zeros_like(w1_scr)

    def gate_times_act(a_scr, w_scr, chunk):
        igrp = pl.ds(pl.multiple_of(chunk * n_i, SUBLANES), SUBLANES)
        for lt in range(tb // GATE_LANES):
            ln = slice(lt * GATE_LANES, (lt + 1) * GATE_LANES)
            for ii in range(n_i):
                rows = slice(ii * PEER_KEYS, (ii + 1) * PEER_KEYS)
                g = jnp.zeros((PEER_KEYS, GATE_LANES), bf16)
                for hh in range(PEER_HEADS):
                    n1 = jnp.broadcast_to(n1_ref[hh, igrp, ln][ii:ii + 1], (PEER_KEYS, GATE_LANES)).astype(bf16)
                    e1 = jnp.broadcast_to(e1_ref[hh, igrp, ln][ii:ii + 1], (PEER_KEYS, GATE_LANES)).astype(bf16)
                    g = g + jnp.where(r2_ref[hh, :, ln] < n1, e2_ref[hh, :, ln] * e1, jnp.zeros((), bf16))
                w_scr[rows, ln] = g * _gelu_tanh(a_scr[rows, ln])

    h = h_ref[...]
    a0_scr[...] = lax.dot_general(u_ref[:ec], h, _NT, preferred_element_type=f32).astype(bf16)
    acc_ref[...] += jnp.dot(vt_prev_ref[...], w1_scr[...], preferred_element_type=f32)
    gate_times_act(a0_scr, w0_scr, 2 * c)
    a1_scr[...] = lax.dot_general(u_ref[ec:], h, _NT, preferred_element_type=f32).astype(bf16)
    acc_ref[...] += jnp.dot(vt_even_ref[...], w0_scr[...], preferred_element_type=f32)
    gate_times_act(a1_scr, w1_scr, 2 * c + 1)

    @pl.when(c == pl.num_programs(1) - 1)
    def _():
        y = acc_ref[...] + jnp.dot(vt_last_ref[...], w1_scr[...], preferred_element_type=f32)
        o_ref[...] = x_ref[...] + gate_ref[0] * y.T


def peer_experts(h_bf16, u_bf16, vt_bf16, n1, e1, r2, e2, x, gate_t, tb=PEER_EXPERT_TOKENS, ec=SUBLANES * PEER_KEYS):
    T = h_bf16.shape[0]
    n_steps = PEER_N // (2 * ec)
    rblk = pl.BlockSpec((PEER_HEADS, PEER_KEYS, tb), lambda t, c: (0, 0, t))
    vt_blk = lambda chunk_of: pl.BlockSpec((D_MODEL, ec), lambda t, c: (0, chunk_of(c)))
    return pl.pallas_call(
        _peer_expert_kernel,
        grid=(T // tb, n_steps),
        in_specs=[pl.BlockSpec((tb, D_MODEL), lambda t, c: (t, 0)),
                  pl.BlockSpec((2 * ec, D_MODEL), lambda t, c: (c, 0)),
                  vt_blk(lambda c: 2 * c),
                  vt_blk(lambda c: jnp.maximum(2 * c - 1, 0)),
                  vt_blk(lambda c: 2 * n_steps - 1),
                  rblk, rblk, rblk, rblk,
                  pl.BlockSpec((tb, D_MODEL), lambda t, c: (t, 0)),
                  pl.BlockSpec((1, 1, D_MODEL), lambda t, c: (t, 0, 0))],
        out_specs=pl.BlockSpec((tb, D_MODEL), lambda t, c: (t, 0)),
        out_shape=jax.ShapeDtypeStruct((T, D_MODEL), jnp.float32),
        scratch_shapes=[pltpu.VMEM((D_MODEL, tb), jnp.float32)] + [pltpu.VMEM((ec, tb), jnp.bfloat16)] * 4,
        compiler_params=pltpu.CompilerParams(dimension_semantics=("arbitrary", "arbitrary"),
                                             vmem_limit_bytes=VMEM_LIMIT),
        name="peer_experts",
    )(h_bf16, u_bf16, vt_bf16, vt_bf16, vt_bf16, n1, e1, r2, e2, x, gate_t)


def _cast_table_kernel(x_ref, o_ref, *, transpose):
    x = x_ref[0]
    o_ref[...] = (x.T if transpose else x).astype(o_ref.dtype)


def cast_table(tabs, li, transpose, rows=1024):
    _, N, D = tabs.shape
    out_spec = pl.BlockSpec((D, rows), lambda i: (0, i)) if transpose else pl.BlockSpec((rows, D), lambda i: (i, 0))
    return pl.pallas_call(
        partial(_cast_table_kernel, transpose=transpose), grid=(N // rows,),
        in_specs=[pl.BlockSpec((1, rows, D), lambda i: (li, i, 0))],
        out_specs=out_spec,
        out_shape=jax.ShapeDtypeStruct((D, N) if transpose else (N, D), jnp.bfloat16),
        compiler_params=pltpu.CompilerParams(dimension_semantics=("arbitrary",), vmem_limit_bytes=VMEM_LIMIT),
        name="cast_table",
    )(tabs)


def peer_weights(w_q, sub_keys, u_tabs, v_tabs, li):
    return (_split_bf16(w_q[li]) + _split_bf16(sub_keys[li])
            + (cast_table(u_tabs, li, transpose=False), cast_table(v_tabs, li, transpose=True)))


def peer_block(x, g, mods_route, mods_expert, weights):
    wq_hi, wq_lo, k_hi, k_lo, u_bf16, vt_bf16 = weights
    n1, e1, r2, e2, h_bf16 = peer_route(x, g[None], mods_route[0], mods_route[1], wq_hi, wq_lo, k_hi, k_lo)
    return peer_experts(h_bf16, u_bf16, vt_bf16, n1, e1, r2, e2, x, mods_expert)


def kernel(x_prompt, x_sample, state_rwkv_fwd, state_rwkv_bwd, cache_mla_ckv, cache_mla_kpe, c, c_ctx,
           norm_g, w_mod, b_mod, ab_w_in, rw_mu, rw_w0, rw_w2, rw_a0, rw_a2, rw_g2, rw_k_k, rw_k_a, rw_r_k,
           rw_lnx_g, rw_lnx_b, mla_q_norm, mla_q_up, mla_kv_norm, mla_kv_up, mla_qn, mla_kn, ab_w_out,
           hy_w_in, hy_b_in, hy_conv_w, hy_conv_b, hy_f_w1, hy_f_b1, hy_f_w2, hy_f_b2, hy_f_w3, hy_f_freq,
           hy_bias, hy_w_out, peer_w_q, peer_keys, peer_u, peer_v):
    rope = axial_rope(x_sample.shape[1])
    D = D_MODEL
    groups = [dict(x=x_prompt.reshape(-1, D), B=x_prompt.shape[0], L=x_prompt.shape[1]),
              dict(x=x_sample.reshape(-1, D), B=x_sample.shape[0], L=x_sample.shape[1])]
    st_f, st_b, st_ckv, st_kpe = [], [], [], []
    for li in range(DEPTH):
        j = li // 2
        groups[0]["mod"] = (_mm(jax.nn.silu(c_ctx)[None], w_mod[li]) + b_mod[li]).reshape(1, 6, D)
        groups[1]["mod"] = (_mm(jax.nn.silu(c), w_mod[li]) + b_mod[li]).reshape(-1, 6, D)
        peer_w = peer_weights(peer_w_q, peer_keys, peer_u, peer_v, li)
        for gi, g in enumerate(groups):
            x, B, L = g["x"], g["B"], g["L"]
            mod = lambda i, tm: tile_mod(g["mod"][:, i], B * L, tm)
            if li % 2 == 0:
                proj = norm_mod_matmul(x, norm_g[li, 0], mod(0, ROW_TILE), mod(1, ROW_TILE), ab_w_in[j],
                                       jnp.zeros((ab_w_in.shape[-1],), jnp.float32), ROW_TILE)
                prep = rwkv_prep(proj, L, rw_mu[j], rw_w0[j], rw_w2[j], rw_a0[j], rw_a2[j], rw_g2[j],
                                 rw_k_k[j], rw_k_a[j], rw_r_k[j], SEQ_TILE)
                s0s = None if gi == 0 else (state_rwkv_fwd[:, j], state_rwkv_bwd[:, j])
                ys, (sf, sb) = rwkv_scans(prep[:9], B, L, s0s)
                ctx = None if gi == 0 else (cache_mla_ckv[:, j], cache_mla_kpe[:, j], rope)
                att, ckv, kpe = mla_mixer(proj[:, RW_IN:], B, L, ctx,
                                          (mla_q_norm[j], mla_q_up[j], mla_kv_norm[j], mla_kv_up[j], mla_qn[j], mla_kn[j]))
                if gi == 0:
                    st_f.append(sf)
                    st_b.append(sb)
                    st_ckv.append(ckv)
                    st_kpe.append(kpe)
                x = rwkv_out_residual(ys[0], ys[1], prep[10], prep[9], att, rw_lnx_g[j], rw_lnx_b[j],
                                      ab_w_out[j], x, mod(2, ROW_TILE), ROW_TILE)
            else:
                proj = norm_mod_matmul(x, norm_g[li, 0], mod(0, ROW_TILE), mod(1, ROW_TILE), hy_w_in[j], hy_b_in[j],
                                       ROW_TILE)
                x0, zin = hyena_prep(proj, L, hy_conv_w[j], hy_conv_b[j], SEQ_TILE)
                circ = hyena_filters(L, hy_f_w1[j], hy_f_b1[j], hy_f_w2[j], hy_f_b2[j], hy_f_w3[j], hy_f_freq[j])
                conv = hyena_longconv(zin.reshape(B, L, D), circ).reshape(B * L, D)
                x = hyena_out_residual(x0, conv, zin, hy_bias[j], hy_w_out[j], x, mod(2, ROW_TILE), ROW_TILE)
            g["x"] = peer_block(x, norm_g[li, 1], (mod(3, PEER_ROUTE_TOKENS), mod(4, PEER_ROUTE_TOKENS)),
                                mod(5, PEER_EXPERT_TOKENS), peer_w)
    xp = groups[0]["x"].reshape(x_prompt.shape)
    xs = groups[1]["x"].reshape(x_sample.shape)
    new_state_rwkv_fwd = jnp.stack(st_f, axis=1).astype(x_prompt.dtype)
    new_state_rwkv_bwd = jnp.stack(st_b, axis=1).astype(x_prompt.dtype)
    new_cache_mla_ckv = jnp.stack(st_ckv, axis=1)
    new_cache_mla_kpe = jnp.stack(st_kpe, axis=1)
    return (xp, xs, new_state_rwkv_fwd, new_state_rwkv_bwd, new_cache_mla_ckv, new_cache_mla_kpe)
```

```python
import math
from functools import lru_cache, partial

import jax
import jax.numpy as jnp
import numpy as np
from jax import lax
from jax.experimental import pallas as pl
from jax.experimental.pallas import tpu as pltpu

D_MODEL = 1024
DEPTH = 2
GRID_W = 64
EPS = 1e-6
RW_HEADS = 8
RW_HEAD = 64
RW_DIM = RW_HEADS * RW_HEAD
W_LORA = 64
A_LORA = 64
G_LORA = 128
LNX_EPS = 64e-5
RW_IN = 3 * RW_DIM + W_LORA + A_LORA + G_LORA
RW_SPLITS = (RW_DIM, 2 * RW_DIM, 3 * RW_DIM, 3 * RW_DIM + W_LORA, 3 * RW_DIM + W_LORA + A_LORA)
MLA_HEADS = 4
QK_NOPE = 128
QK_ROPE = 64
QK_HEAD = QK_NOPE + QK_ROPE
V_HEAD = 128
Q_LORA = 256
KV_LORA = 128
MLA_DIM = MLA_HEADS * V_HEAD
ROPE_THETA = 10000.0
HY_BANDS = 16
HY_TARGET = 1e-2
HY_FAST = 0.3
HY_SLOW = 1.5
PEER_KEYS = 128
PEER_HEADS = 8
PEER_DKEY = 256
PEER_TOPK = 16

LANES = 128
SUBLANES = 8
V7X_VMEM_BYTES = 64 * 1024 * 1024
VMEM_LIMIT = V7X_VMEM_BYTES * 7 // 8
_NT = (((1,), (1,)), ((), ()))


def _split_bf16(x):
    hi = x.astype(jnp.bfloat16)
    lo = (x - hi.astype(jnp.float32)).astype(jnp.bfloat16)
    return hi, lo


def _dot3(a_hi, a_lo, b_hi, b_lo):
    f32 = jnp.float32
    return (jnp.dot(a_hi, b_hi, preferred_element_type=f32) + jnp.dot(a_lo, b_hi, preferred_element_type=f32)
            + jnp.dot(a_hi, b_lo, preferred_element_type=f32))


def _mm_kernel(a_ref, b_ref, o_ref, *, split):
    if split:
        a_hi, a_lo = _split_bf16(a_ref[...])
        b_hi, b_lo = _split_bf16(b_ref[...])
        o_ref[...] = _dot3(a_hi, a_lo, b_hi, b_lo)
    else:
        o_ref[...] = jnp.dot(a_ref[...].astype(jnp.bfloat16), b_ref[...].astype(jnp.bfloat16),
                             preferred_element_type=jnp.float32)


def _mm(a, b, tm=512, tn=512, split=False):
    lead = a.shape[:-1]
    K = a.shape[-1]
    N = b.shape[-1]
    a2 = a.reshape(-1, K)
    if K % LANES:
        kp = -K % LANES
        a2 = jnp.pad(a2, ((0, 0), (0, kp)))
        b = jnp.pad(b, ((0, kp), (0, 0)))
        K += kp
    M = a2.shape[0]
    tm = min(tm, M)
    tn = min(tn, N)
    if N % tn:
        tn = N
    assert M % tm == 0 and N % tn == 0
    out = pl.pallas_call(
        partial(_mm_kernel, split=split),
        grid=(M // tm, N // tn),
        in_specs=[pl.BlockSpec((tm, K), lambda i, j: (i, 0)),
                  pl.BlockSpec((K, tn), lambda i, j: (0, j))],
        out_specs=pl.BlockSpec((tm, tn), lambda i, j: (i, j)),
        out_shape=jax.ShapeDtypeStruct((M, N), jnp.float32),
        name="matmul",
    )(a2, b)
    return out.reshape(*lead, N)


def axial_rope(L):
    rows = L // GRID_W
    row = jnp.repeat(jnp.arange(rows, dtype=jnp.float32), GRID_W)
    col = jnp.tile(jnp.arange(GRID_W, dtype=jnp.float32), rows)
    n_freq = QK_ROPE // 4
    inv = ROPE_THETA ** (-jnp.arange(n_freq, dtype=jnp.float32) / n_freq)
    ang = jnp.concatenate([row[:, None] * inv, col[:, None] * inv], axis=-1)
    return jnp.cos(ang), jnp.sin(ang)


def _mla_attn_kernel(q_ref, k_ref, v_ref, *rest, with_ctx):
    if with_ctx:
        kc_ref, vc_ref, o_ref = rest
    else:
        (o_ref,) = rest
    bf16 = jnp.bfloat16
    f32 = jnp.float32
    scale = QK_HEAD ** -0.5
    for h in range(MLA_HEADS):
        qs = slice(h * QK_HEAD, (h + 1) * QK_HEAD)
        vs = slice(h * V_HEAD, (h + 1) * V_HEAD)
        q = q_ref[0, :, qs].astype(bf16)
        s = lax.dot_general(q, k_ref[0, :, qs].astype(bf16), _NT, preferred_element_type=f32) * scale
        m = jnp.max(s, axis=-1, keepdims=True)
        if with_ctx:
            sc = lax.dot_general(q, kc_ref[0, :, qs].astype(bf16), _NT, preferred_element_type=f32) * scale
            m = jnp.maximum(m, jnp.max(sc, axis=-1, keepdims=True))
            pc = jnp.exp(sc - m)
        p = jnp.exp(s - m)
        l = jnp.sum(p, axis=-1, keepdims=True)
        if with_ctx:
            l = l + jnp.sum(pc, axis=-1, keepdims=True)
        o = jnp.dot((p / l).astype(bf16), v_ref[0, :, vs].astype(bf16), preferred_element_type=f32)
        if with_ctx:
            o = o + jnp.dot((pc / l).astype(bf16), vc_ref[0, :, vs].astype(bf16), preferred_element_type=f32)
        o_ref[0, :, vs] = o


def attend(q, k, v, ctx=None, tq=256):
    B, Lq, _ = q.shape
    tq = min(tq, Lq)
    kv = [k, v] + (list(ctx) if ctx is not None else [])
    full = lambda a: pl.BlockSpec((1,) + a.shape[1:], lambda b, i: (b, 0, 0))
    return pl.pallas_call(
        partial(_mla_attn_kernel, with_ctx=ctx is not None), grid=(B, Lq // tq),
        in_specs=[pl.BlockSpec((1, tq, MLA_HEADS * QK_HEAD), lambda b, i: (b, i, 0))] + [full(a) for a in kv],
        out_specs=pl.BlockSpec((1, tq, MLA_HEADS * V_HEAD), lambda b, i: (b, i, 0)),
        out_shape=jax.ShapeDtypeStruct((B, Lq, MLA_HEADS * V_HEAD), jnp.float32),
        compiler_params=pltpu.CompilerParams(dimension_semantics=("arbitrary", "arbitrary"),
                                             vmem_limit_bytes=VMEM_LIMIT),
        name="mla_attend",
    )(q, *kv)


def _first_row_block(fn):
    pl.when(pl.program_id(2) == 0)(fn)


def _dft_pair_kernel(c_tab, s_tab, x_ref, oc_ref, os_ref, x16):
    def _():
        x16[...] = x_ref[0].astype(jnp.bfloat16)
    _first_row_block(_)
    oc_ref[0] = jnp.dot(c_tab[...], x16[...], preferred_element_type=jnp.float32)
    os_ref[0] = jnp.dot(s_tab[...], x16[...], preferred_element_type=jnp.float32)


def _dft_spectral_kernel(c_tab, s_tab, x_ref, hr_ref, hi_ref, yr_ref, yi_ref, x16):
    def _():
        x16[...] = x_ref[0].astype(jnp.bfloat16)
    _first_row_block(_)
    rb, L = c_tab.shape
    zc = jnp.dot(c_tab[...], x16[...], preferred_element_type=jnp.float32)
    zs = jnp.dot(s_tab[...], x16[...], preferred_element_type=jnp.float32)
    hr = hr_ref[...]
    hi = hi_ref[...]
    f = pl.program_id(2) * rb + lax.broadcasted_iota(jnp.int32, zc.shape, 0)
    yr_ref[0] = jnp.where(f == 0, zc * hr * (0.5 / L), (zc * hr - zs * hi) * (1.0 / L))
    yi_ref[0] = jnp.where(f == 0, zs * hi * (0.5 / L), (zc * hi + zs * hr) * (1.0 / L))


def _dft_sum_kernel(c_tab, s_tab, x_ref, y_ref, o_ref, x16, y16):
    def _():
        x16[...] = x_ref[0].astype(jnp.bfloat16)
        y16[...] = y_ref[0].astype(jnp.bfloat16)
    _first_row_block(_)
    o_ref[0] = (jnp.dot(c_tab[...], x16[...], preferred_element_type=jnp.float32)
                + jnp.dot(s_tab[...], y16[...], preferred_element_type=jnp.float32))


@lru_cache(maxsize=None)
def dft_tables(L):
    f = np.arange(L, dtype=np.int64)
    ang = ((f[:, None] * f[None, :]) % (2 * L)).astype(np.float64) * (math.pi / L)
    c = np.cos(ang).astype(jnp.bfloat16)
    s = -np.sin(ang)
    alt = np.where(f % 2 == 0, 1.0, -1.0)
    s_ana = np.where(f[:, None] == 0, alt[None, :], s).astype(jnp.bfloat16)
    s_syn = np.where(f[None, :] == 0, alt[:, None], s).astype(jnp.bfloat16)
    return (c, s_ana), (c, s_syn)


def _dft_call(body, tabs, xs, n_out, name, row_inputs=(), rb=256, nb=512):
    B, L, N = xs[0].shape
    rb = min(rb, L)
    nb = min(nb, N)
    tab = pl.BlockSpec((rb, L), lambda b, n, i: (i, 0))
    xin = pl.BlockSpec((1, L, nb), lambda b, n, i: (b, 0, n))
    rin = pl.BlockSpec((rb, nb), lambda b, n, i: (i, n))
    out = pl.BlockSpec((1, rb, nb), lambda b, n, i: (b, i, n))
    sd = jax.ShapeDtypeStruct((B, L, N), jnp.float32)
    return pl.pallas_call(
        body, grid=(B, N // nb, L // rb),
        in_specs=[tab] * len(tabs) + [xin] * len(xs) + [rin] * len(row_inputs),
        out_specs=[out] * n_out if n_out > 1 else out,
        out_shape=[sd] * n_out if n_out > 1 else sd,
        scratch_shapes=[pltpu.VMEM((L, nb), jnp.bfloat16)] * len(xs),
        compiler_params=pltpu.CompilerParams(dimension_semantics=("arbitrary",) * 3,
                                             vmem_limit_bytes=VMEM_LIMIT),
        name=name,
    )(*tabs, *xs, *row_inputs)


def hyena_longconv(zin, circ):
    B, L, D = zin.shape
    tabs, tabs_syn = dft_tables(L)
    alt = jnp.where(jnp.arange(L) % 2 == 0, 1.0, -1.0).astype(jnp.float32)[:, None]
    fc, fs = _dft_call(_dft_pair_kernel, tabs, [jnp.concatenate([circ[:L], circ[L:]], axis=-1)[None]], 2, "dft_filter")
    hr = fc[0, :, :D] + alt * fc[0, :, D:]
    hi = fs[0, :, :D] + alt * fs[0, :, D:]
    yr, yi = _dft_call(_dft_spectral_kernel, tabs, [zin], 2, "dft_analysis", row_inputs=(hr, hi))
    return _dft_call(_dft_sum_kernel, tabs_syn, [yr, yi], 1, "dft_synthesis")


SCAN_ACCS = 4


def _rwkv_scan_kernel(rf_ref, kkf_ref, vf_ref, wf_ref, kdf_ref, bf_ref, rb_ref, kkb_ref, vb_ref, wb_ref, kdb_ref, bb_ref,
                      s0f_ref, s0b_ref, yf_ref, yb_ref, sff_ref, sfb_ref, sf_scr, sb_scr):
    c = pl.program_id(0)
    tc, nk, _ = rf_ref.shape
    nv = vf_ref.shape[1]
    groups = RW_HEAD // nk

    @pl.when(c == 0)
    def _():
        sf_scr[...] = s0f_ref[...]
        sb_scr[...] = s0b_ref[...]

    def row(ref, tt, k):
        return jnp.broadcast_to(ref[tt, pl.ds(k, 1), :], (nv, LANES))

    def all_groups(p):
        tot = p
        for q in range(1, groups):
            tot = tot + pltpu.roll(p, q * (LANES // groups), 1)
        return tot

    def one_step(tt, r_ref, kk_ref, v_ref, w_ref, kd_ref, b_ref, s_scr, y_ref):
        parts = [None] * SCAN_ACCS
        for k in range(nk):
            term = s_scr[k] * row(kk_ref, tt, k)
            parts[k % SCAN_ACCS] = term if parts[k % SCAN_ACCS] is None else parts[k % SCAN_ACCS] + term
        sa = -all_groups((parts[0] + parts[1]) + (parts[2] + parts[3]))
        vv = v_ref[tt]
        parts = [None] * SCAN_ACCS
        for k in range(nk):
            s_new = s_scr[k] * row(w_ref, tt, k) + sa * row(b_ref, tt, k) + vv * row(kd_ref, tt, k)
            s_scr[k] = s_new
            term = s_new * row(r_ref, tt, k)
            parts[k % SCAN_ACCS] = term if parts[k % SCAN_ACCS] is None else parts[k % SCAN_ACCS] + term
        y_ref[tt] = all_groups((parts[0] + parts[1]) + (parts[2] + parts[3]))

    def step(t, carry):
        one_step(t, rf_ref, kkf_ref, vf_ref, wf_ref, kdf_ref, bf_ref, sf_scr, yf_ref)
        one_step(tc - 1 - t, rb_ref, kkb_ref, vb_ref, wb_ref, kdb_ref, bb_ref, sb_scr, yb_ref)
        return carry

    lax.fori_loop(0, tc, step, 0, unroll=4 if groups > 1 else 2)

    @pl.when(c == pl.num_programs(0) - 1)
    def _():
        sff_ref[...] = sf_scr[...]
        sfb_ref[...] = sb_scr[...]


def rwkv_scan(r, kk, v, wkb_fwd, wkb_bwd, s0_fwd, s0_bwd, tc=32):
    L, nk, _ = r.shape
    nv = v.shape[1]
    nc = L // tc
    fwd = lambda rows: pl.BlockSpec((tc, rows, LANES), lambda c: (c, 0, 0))
    bwd = lambda rows: pl.BlockSpec((tc, rows, LANES), lambda c: (nc - 1 - c, 0, 0))
    state = pl.BlockSpec((nk, nv, LANES), lambda c: (0, 0, 0))
    y_sd = jax.ShapeDtypeStruct((L, nv, LANES), jnp.float32)
    s_sd = jax.ShapeDtypeStruct((nk, nv, LANES), jnp.float32)
    return pl.pallas_call(
        _rwkv_scan_kernel,
        grid=(nc,),
        in_specs=[fwd(nk), fwd(nk), fwd(nv), fwd(nk), fwd(nk), fwd(nk),
                  bwd(nk), bwd(nk), bwd(nv), bwd(nk), bwd(nk), bwd(nk), state, state],
        out_specs=[fwd(nv), bwd(nv), state, state],
        out_shape=[y_sd, y_sd, s_sd, s_sd],
        scratch_shapes=[pltpu.VMEM((nk, nv, LANES), jnp.float32)] * 2,
        compiler_params=pltpu.CompilerParams(dimension_semantics=("arbitrary",), vmem_limit_bytes=VMEM_LIMIT),
        name="rwkv_scan",
    )(r, kk, v, *wkb_fwd, r, kk, v, *wkb_bwd, s0_fwd, s0_bwd)


def k_to_lanes(x, ksplit):
    B, L, H, N = x.shape
    nk = N // ksplit
    return x.reshape(B, L, H, ksplit, nk).transpose(1, 4, 3, 0, 2).reshape(L, nk, ksplit * B * H)


def v_to_lanes(x, ksplit):
    B, L, H, N = x.shape
    return jnp.tile(x.transpose(1, 3, 0, 2).reshape(L, N, B * H), (1, 1, ksplit))


def v_from_lanes(y, B, H):
    L, N, _ = y.shape
    return y[:, :, :B * H].reshape(L, N, B, H).transpose(2, 0, 3, 1)


def state_to_lanes(s, ksplit):
    B, H, N, K = s.shape
    nk = K // ksplit
    return s.reshape(B, H, N, ksplit, nk).transpose(4, 2, 3, 0, 1).reshape(nk, N, ksplit * B * H)


def state_from_lanes(s, B, H, ksplit):
    nk, N, _ = s.shape
    return s.reshape(nk, N, ksplit, B, H).transpose(3, 4, 1, 2, 0).reshape(B, H, N, ksplit * nk)


def rwkv_scan_both(rh, kk, vh, decay2, kd2, b2, s0_2, tc=32):
    B, L, H, N = rh.shape
    ksplit = LANES // (B * H)
    assert B * H * ksplit == LANES
    wkb = [[k_to_lanes(t[d], ksplit) for t in (decay2, kd2, b2)] for d in range(2)]
    y_f, y_b, sf_f, sf_b = rwkv_scan(k_to_lanes(rh, ksplit), k_to_lanes(kk, ksplit), v_to_lanes(vh, ksplit),
                                     wkb[0], wkb[1], state_to_lanes(s0_2[0], ksplit), state_to_lanes(s0_2[1], ksplit),
                                     tc=tc)
    return ([v_from_lanes(y_f, B, H), v_from_lanes(y_b, B, H)],
            [state_from_lanes(sf_f, B, H, ksplit), state_from_lanes(sf_b, B, H, ksplit)])


def _rms_mod(x, g, shift, scale):
    y = x * lax.rsqrt(jnp.mean(x * x, axis=-1, keepdims=True) + EPS)
    return (y * g) * (1.0 + scale) + shift


def tile_mod(m, n_rows, tm):
    return jnp.repeat(m, n_rows // m.shape[0] // tm, axis=0)[:, None, :]


def _norm_mod_matmul_kernel(x_ref, g_ref, sh_ref, sc_ref, w_ref, b_ref, o_ref):
    h = _rms_mod(x_ref[...], g_ref[...], sh_ref[0], sc_ref[0])
    o_ref[...] = jnp.dot(h.astype(jnp.bfloat16), w_ref[...], preferred_element_type=jnp.float32) + b_ref[...]


def norm_mod_matmul(x, g, shift_t, scale_t, w, b, tm):
    T, D = x.shape
    N = w.shape[1]
    mod = pl.BlockSpec((1, 1, D), lambda i: (i, 0, 0))
    return pl.pallas_call(
        _norm_mod_matmul_kernel, grid=(T // tm,),
        in_specs=[pl.BlockSpec((tm, D), lambda i: (i, 0)), pl.BlockSpec((1, D), lambda i: (0, 0)), mod, mod,
                  pl.BlockSpec((D, N), lambda i: (0, 0)), pl.BlockSpec((1, N), lambda i: (0, 0))],
        out_specs=pl.BlockSpec((tm, N), lambda i: (i, 0)),
        out_shape=jax.ShapeDtypeStruct((T, N), jnp.float32),
        compiler_params=pltpu.CompilerParams(dimension_semantics=("arbitrary",), vmem_limit_bytes=VMEM_LIMIT),
        name="norm_mod_matmul",
    )(x, g[None], shift_t, scale_t, w.astype(jnp.bfloat16), b[None])


def _head_sums(x, ones_bd):
    hi, lo = _split_bf16(x)
    return (jnp.dot(hi, ones_bd, preferred_element_type=jnp.float32)
            + jnp.dot(lo, ones_bd, preferred_element_type=jnp.float32))


def _with_neighbours(x_ref, prev_ref, next_ref, first_ref, last_ref, ncols):
    tm = x_ref.shape[0]
    x = x_ref[:, :ncols]
    row = lax.broadcasted_iota(jnp.int32, (tm, 1), 0)
    prev_row = prev_ref[SUBLANES - 1:SUBLANES, :ncols] * (1.0 - first_ref[0, :, :1])
    next_row = next_ref[0:1, :ncols] * (1.0 - last_ref[0, :, :1])
    prev = jnp.where(row == 0, prev_row, pltpu.roll(x, 1, 0))
    nxt = jnp.where(row == tm - 1, next_row, pltpu.roll(x, tm - 1, 0))
    return x, prev, nxt


def _neighbour_specs(T, C, L, tm):
    n_tiles = T // tm
    hb = tm // SUBLANES
    start = np.arange(n_tiles) * tm % L
    first = np.broadcast_to((start == 0).astype(np.float32)[:, None, None], (n_tiles, 1, LANES))
    last = np.broadcast_to((start + tm == L).astype(np.float32)[:, None, None], (n_tiles, 1, LANES))
    flag = pl.BlockSpec((1, 1, LANES), lambda i: (i, 0, 0))
    specs = [pl.BlockSpec((tm, C), lambda i: (i, 0)),
             pl.BlockSpec((SUBLANES, C), lambda i: (jnp.maximum(i * hb - 1, 0), 0)),
             pl.BlockSpec((SUBLANES, C), lambda i: (jnp.minimum((i + 1) * hb, T // SUBLANES - 1), 0)),
             flag, flag]
    return specs, (jnp.asarray(first), jnp.asarray(last))


def _rwkv_prep_kernel(proj_ref, prev_ref, next_ref, first_ref, last_ref, mu_ref, w0_ref, w2_ref, a0_ref, a2_ref,
                      g2_ref, kk_ref, ka_ref, rk_ref, ones_ref,
                      r_out, kk_out, v_out, w_out0, w_out1, kd_out0, kd_out1, b_out0, b_out1, gate_out, bonus_out):
    bf16 = jnp.bfloat16
    f32 = jnp.float32
    x, prev, nxt = _with_neighbours(proj_ref, prev_ref, next_ref, first_ref, last_ref, RW_IN)
    rw = x + mu_ref[...] * (0.5 * (prev + nxt) - x)
    r = rw[:, RW_SPLITS[0] - RW_DIM:RW_SPLITS[0]]
    k = rw[:, RW_SPLITS[0]:RW_SPLITS[1]]
    v = rw[:, RW_SPLITS[1]:RW_SPLITS[2]]
    wd = rw[:, RW_SPLITS[2]:RW_SPLITS[3]]
    ad = rw[:, RW_SPLITS[3]:RW_SPLITS[4]]
    gd = rw[:, RW_SPLITS[4]:]
    ones_bd = ones_ref[...]
    kk = k * kk_ref[...]
    kk = kk / jnp.maximum(jnp.sqrt(_head_sums(kk * kk, ones_bd)), 1e-12)
    tanh_wd = jnp.tanh(wd).astype(bf16)
    ad16 = ad.astype(bf16)
    rk = r * rk_ref[...]
    bonus = jnp.zeros_like(r)
    for d, (w_o, kd_o, b_o) in enumerate(((w_out0, kd_out0, b_out0), (w_out1, kd_out1, b_out1))):
        w_raw = w0_ref[d:d + 1, :] + jnp.dot(tanh_wd, w2_ref[d], preferred_element_type=f32)
        w_o[...] = jnp.exp(-jnp.exp(-jax.nn.softplus(-w_raw) - 0.5))
        lr = jax.nn.sigmoid(a0_ref[d:d + 1, :] + jnp.dot(ad16, a2_ref[d], preferred_element_type=f32))
        kd = k * (1.0 + (lr - 1.0) * ka_ref[...])
        kd_o[...] = kd
        b_o[...] = kk * lr
        bonus = bonus + _head_sums(rk * kd, ones_bd)
    r_out[...] = r
    kk_out[...] = kk
    v_out[...] = v
    gate_out[...] = jnp.dot(jax.nn.sigmoid(gd).astype(bf16), g2_ref[...], preferred_element_type=f32)
    bonus_out[...] = bonus * v


def rwkv_prep(proj, L, mu, w0, w2, a0, a2, g2, k_k, k_a, r_k, tm):
    T, C = proj.shape
    nb_specs, flags = _neighbour_specs(T, C, L, tm)
    row2 = lambda a: a.reshape(1, -1)
    whole = lambda a: pl.BlockSpec(a.shape, lambda i: (0,) * a.ndim)
    out = pl.BlockSpec((tm, RW_DIM), lambda i: (i, 0))
    consts = (row2(mu), w0, w2.astype(jnp.bfloat16), a0, a2.astype(jnp.bfloat16), g2.astype(jnp.bfloat16),
              row2(k_k), row2(k_a), row2(r_k), _segment_ones(RW_DIM, RW_HEAD))
    return pl.pallas_call(
        _rwkv_prep_kernel, grid=(T // tm,),
        in_specs=nb_specs + [whole(a) for a in consts],
        out_specs=[out] * 11,
        out_shape=[jax.ShapeDtypeStruct((T, RW_DIM), jnp.float32)] * 11,
        compiler_params=pltpu.CompilerParams(dimension_semantics=("arbitrary",), vmem_limit_bytes=VMEM_LIMIT),
        name="rwkv_prep",
    )(proj, proj, proj, *flags, *consts)


def _rwkv_out_kernel(y0_ref, y1_ref, bonus_ref, gate_ref, mla_ref, lng_ref, lnb_ref, ones_ref, w_ref, x_ref, mod_ref,
                     o_ref):
    bf16 = jnp.bfloat16
    f32 = jnp.float32
    ones_bd = ones_ref[...]
    y = y0_ref[...] + y1_ref[...]
    dlt = y - _head_sums(y, ones_bd) * (1.0 / RW_HEAD)
    var = _head_sums(dlt * dlt, ones_bd) * (1.0 / RW_HEAD)
    yn = dlt * lax.rsqrt(var + LNX_EPS) * lng_ref[...] + lnb_ref[...]
    rw_out = (yn + bonus_ref[...]) * gate_ref[...]
    upd = (jnp.dot(rw_out.astype(bf16), w_ref[:RW_DIM], preferred_element_type=f32)
           + jnp.dot(mla_ref[...].astype(bf16), w_ref[RW_DIM:], preferred_element_type=f32))
    o_ref[...] = x_ref[...] + mod_ref[0] * upd


def rwkv_out_residual(y0, y1, bonus, gate, mla_out, lnx_g, lnx_b, w_out, x, gate_t, tm):
    T, D = x.shape
    row = lambda n: pl.BlockSpec((tm, n), lambda i: (i, 0))
    whole = lambda a: pl.BlockSpec(a.shape, lambda i: (0,) * a.ndim)
    consts = (lnx_g.reshape(1, -1), lnx_b.reshape(1, -1), _segment_ones(RW_DIM, RW_HEAD), w_out.astype(jnp.bfloat16))
    return pl.pallas_call(
        _rwkv_out_kernel, grid=(T // tm,),
        in_specs=[row(RW_DIM)] * 4 + [row(MLA_DIM)] + [whole(a) for a in consts]
                 + [row(D), pl.BlockSpec((1, 1, D), lambda i: (i, 0, 0))],
        out_specs=row(D),
        out_shape=jax.ShapeDtypeStruct((T, D), jnp.float32),
        compiler_params=pltpu.CompilerParams(dimension_semantics=("arbitrary",), vmem_limit_bytes=VMEM_LIMIT),
        name="rwkv_out_residual",
    )(y0, y1, bonus, gate, mla_out, *consts, x, gate_t)


def rwkv_scans(prep, B, L, s0s):
    heads = lambda t: t.reshape(B, L, RW_HEADS, RW_HEAD)
    r, kk, v, w_0, w_1, kd_0, kd_1, b_0, b_1 = [heads(t) for t in prep]
    if s0s is None:
        z = jnp.zeros((B, RW_HEADS, RW_HEAD, RW_HEAD), jnp.float32)
        s0s = (z, z)
    ys, finals = rwkv_scan_both(r, kk, v, [w_0, w_1], [kd_0, kd_1], [b_0, b_1], s0s)
    return [y.reshape(B * L, RW_DIM) for y in ys], finals


def _segment_ones(n, seg):
    h = np.arange(n) // seg
    return jnp.asarray(h[:, None] == h[None, :], jnp.bfloat16)


def _mla_prep_kernel(x_ref, cos_ref, sin_ref, qnorm_ref, qup_ref, kvnorm_ref, wk_ref, wv_ref, place_ref, qn_ref, kn_ref,
                     ones_ref, q_out, k_out, v_out, ckv_out, *, rope, norm_kv):
    bf16 = jnp.bfloat16
    f32 = jnp.float32
    x = x_ref[...]
    qc = x[:, :Q_LORA]
    kvc = x[:, Q_LORA:Q_LORA + KV_LORA]
    kpe = x[:, Q_LORA + KV_LORA:]
    rms = lambda t, g: t * lax.rsqrt(jnp.mean(t * t, axis=-1, keepdims=True) + EPS) * g
    ones_seg = ones_ref[...]
    head_norm = lambda t, g: t * lax.rsqrt(_head_sums(t * t, ones_seg) * (1.0 / QK_HEAD) + EPS) * g
    q = jnp.dot(rms(qc, qnorm_ref[...]).astype(bf16), qup_ref[...], preferred_element_type=f32)
    ckv = rms(kvc, kvnorm_ref[...]) if norm_kv else kvc
    ckv_out[...] = ckv
    c16 = ckv.astype(bf16)
    kpe_hi, kpe_lo = _split_bf16(kpe)
    k = (jnp.dot(c16, wk_ref[...], preferred_element_type=f32)
         + jnp.dot(kpe_hi, place_ref[...], preferred_element_type=f32)
         + jnp.dot(kpe_lo, place_ref[...], preferred_element_type=f32))
    v_out[...] = jnp.dot(c16, wv_ref[...], preferred_element_type=f32)
    q = head_norm(q, qn_ref[...])
    k = head_norm(k, kn_ref[...])
    if rope:
        n = q.shape[1]
        even = lax.broadcasted_iota(jnp.int32, (1, n), 1) % 2 == 0
        swap = lambda t: jnp.where(even, pltpu.roll(t, n - 1, 1), pltpu.roll(t, 1, 1))
        cos = cos_ref[...]
        sin = sin_ref[...]
        q = q * cos + swap(q) * sin
        k = k * cos + swap(k) * sin
    q_out[...] = q
    k_out[...] = k


def mla_prep(x, L, rope, weights, norm_kv=True):
    q_norm, q_up, kv_norm, kv_up, qn, kn = weights
    T, C = x.shape
    tm = min(ROW_TILE, L)
    nq = MLA_HEADS * QK_HEAD
    kv4 = kv_up.reshape(KV_LORA, MLA_HEADS, QK_NOPE + V_HEAD)
    wk = jnp.pad(kv4[:, :, :QK_NOPE], ((0, 0), (0, 0), (0, QK_ROPE))).reshape(KV_LORA, nq).astype(jnp.bfloat16)
    wv = kv4[:, :, QK_NOPE:].reshape(KV_LORA, MLA_HEADS * V_HEAD).astype(jnp.bfloat16)
    slot = np.arange(nq) % QK_HEAD - QK_NOPE
    place = jnp.asarray(np.arange(QK_ROPE)[:, None] == slot[None, :], jnp.bfloat16)
    if rope is not None:
        cos, sin = rope
        pair = np.maximum(slot, 0) // 2
        sign = np.where(slot % 2 == 0, -1.0, 1.0).astype(np.float32)
        cos_f = jnp.where(slot >= 0, cos[:, pair], 1.0)
        sin_f = jnp.where(slot >= 0, sin[:, pair] * sign, 0.0)
    else:
        cos_f = sin_f = jnp.zeros((tm, nq), jnp.float32)
    tiles_per_seq = max(L // tm, 1)
    rope_blk = pl.BlockSpec((tm, nq), (lambda i: (i % tiles_per_seq, 0)) if rope is not None else (lambda i: (0, 0)))
    row2 = lambda a: a.reshape(1, -1)
    whole = lambda a: pl.BlockSpec(a.shape, lambda i: (0,) * a.ndim)
    consts = (row2(q_norm), q_up.astype(jnp.bfloat16), row2(kv_norm), wk, wv, place,
              row2(jnp.tile(qn, MLA_HEADS)), row2(jnp.tile(kn, MLA_HEADS)), _segment_ones(nq, QK_HEAD))
    out = lambda n: pl.BlockSpec((tm, n), lambda i: (i, 0))
    sd = lambda n: jax.ShapeDtypeStruct((T, n), jnp.float32)
    return pl.pallas_call(
        partial(_mla_prep_kernel, rope=rope is not None, norm_kv=norm_kv), grid=(T // tm,),
        in_specs=[pl.BlockSpec((tm, C), lambda i: (i, 0)), rope_blk, rope_blk] + [whole(a) for a in consts],
        out_specs=[out(nq), out(nq), out(MLA_HEADS * V_HEAD), out(KV_LORA)],
        out_shape=[sd(nq), sd(nq), sd(MLA_HEADS * V_HEAD), sd(KV_LORA)],
        compiler_params=pltpu.CompilerParams(dimension_semantics=("arbitrary",), vmem_limit_bytes=VMEM_LIMIT),
        name="mla_prep",
    )(x, cos_f, sin_f, *consts)


def mla_mixer(mla, B, L, ctx, weights):
    q, k, v, ckv = mla_prep(mla, L, None if ctx is None else ctx[2], weights)
    seq = lambda t: t.reshape(B, L, -1)
    kv_ctx = None
    if ctx is not None:
        c_ckv, c_kpe = ctx[0], ctx[1]
        Lc = c_ckv.shape[1]
        xc = jnp.concatenate([jnp.zeros((B, Lc, Q_LORA), jnp.float32), c_ckv, c_kpe], axis=-1).reshape(B * Lc, -1)
        _, kc, vc, _ = mla_prep(xc, Lc, None, weights, norm_kv=False)
        kv_ctx = (kc.reshape(B, Lc, -1), vc.reshape(B, Lc, -1))
    att = attend(seq(q), seq(k), seq(v), kv_ctx)
    return att.reshape(B * L, -1), ckv.reshape(B, L, -1), mla[:, Q_LORA + KV_LORA:].reshape(B, L, -1)


def _hyena_prep_kernel(proj_ref, prev_ref, next_ref, first_ref, last_ref, cw_ref, cb_ref, x0_ref, z_ref):
    D = x0_ref.shape[1]
    x, prev, nxt = _with_neighbours(proj_ref, prev_ref, next_ref, first_ref, last_ref, proj_ref.shape[1])
    u = prev * cw_ref[0:1, :] + x * cw_ref[1:2, :] + nxt * cw_ref[2:3, :] + cb_ref[...]
    x0_ref[...] = u[:, :D]
    z_ref[...] = u[:, D:2 * D] * u[:, 2 * D:]


def hyena_prep(proj, L, conv_w, conv_b, tm):
    T, C = proj.shape
    D = C // 3
    nb_specs, flags = _neighbour_specs(T, C, L, tm)
    out = pl.BlockSpec((tm, D), lambda i: (i, 0))
    return pl.pallas_call(
        _hyena_prep_kernel, grid=(T // tm,),
        in_specs=nb_specs + [pl.BlockSpec((3, C), lambda i: (0, 0)), pl.BlockSpec((1, C), lambda i: (0, 0))],
        out_specs=[out, out],
        out_shape=[jax.ShapeDtypeStruct((T, D), jnp.float32)] * 2,
        compiler_params=pltpu.CompilerParams(dimension_semantics=("arbitrary",), vmem_limit_bytes=VMEM_LIMIT),
        name="hyena_prep",
    )(proj, proj, proj, *flags, conv_w, conv_b[None])


def _hyena_out_kernel(x0_ref, conv_ref, z_ref, bias_ref, w_ref, x_ref, mod_ref, o_ref):
    a = x0_ref[...] * (conv_ref[...] + z_ref[...] * bias_ref[...])
    o_ref[...] = x_ref[...] + mod_ref[0] * jnp.dot(a.astype(jnp.bfloat16), w_ref[...],
                                                   preferred_element_type=jnp.float32)


def hyena_out_residual(x0, conv, zin, bias, w_out, x, gate_t, tm):
    T, D = x.shape
    row = pl.BlockSpec((tm, D), lambda i: (i, 0))
    return pl.pallas_call(
        _hyena_out_kernel, grid=(T // tm,),
        in_specs=[row, row, row, pl.BlockSpec((1, D), lambda i: (0, 0)), pl.BlockSpec((D, D), lambda i: (0, 0)),
                  row, pl.BlockSpec((1, 1, D), lambda i: (i, 0, 0))],
        out_specs=row,
        out_shape=jax.ShapeDtypeStruct((T, D), jnp.float32),
        compiler_params=pltpu.CompilerParams(dimension_semantics=("arbitrary",), vmem_limit_bytes=VMEM_LIMIT),
        name="hyena_out_residual",
    )(x0, conv, zin, bias[None], w_out.astype(jnp.bfloat16), x, gate_t)


def hyena_filters(L, w1, b1, w2, b2, w3, freq):
    f32 = jnp.float32
    u = jnp.arange(2 * L, dtype=jnp.int32)[:, None]
    t = jnp.where(u < L, u, 2 * L - u).astype(f32)
    t_unit = t / (L - 1)
    bands = jnp.linspace(1e-4, HY_BANDS - 1, HY_BANDS, dtype=f32)
    ang = 2.0 * math.pi * t * bands / L
    zpos = jnp.concatenate([t_unit, jnp.cos(ang), -jnp.sin(ang)], axis=-1)
    fr = freq.astype(f32)
    hid = jnp.sin(fr * (_mm(zpos, w1.astype(f32), split=True) + b1.astype(f32)))
    hid = jnp.sin(fr * (_mm(hid, w2.astype(f32), split=True) + b2.astype(f32)))
    filt = _mm(hid, w3.astype(f32), split=True)
    deltas = jnp.linspace(math.log(HY_TARGET) / HY_FAST, math.log(HY_TARGET) / HY_SLOW, D_MODEL, dtype=f32)
    window = jnp.exp(-t_unit * jnp.abs(deltas))
    circ = jnp.where(u < L, filt[:, :D_MODEL], filt[:, D_MODEL:]) * window
    circ = jnp.where(u == L, 0.0, circ)
    return circ / jnp.sum(jnp.abs(circ), axis=0, keepdims=True)


PEER_N = PEER_KEYS * PEER_KEYS
GATE_LANES = 2 * LANES
PEER_ROUTE_TOKENS = 256
PEER_EXPERT_TOKENS = 512
ROW_TILE = 512
SEQ_TILE = 256


def _top_vals(s, n, with_rank):
    vals = []
    rank = jnp.full(s.shape, float(n), jnp.float32) if with_rank else None
    for a in range(n):
        m = jnp.max(s, axis=0, keepdims=True)
        vals.append(m)
        hit = s == m
        if with_rank:
            rank = jnp.where(hit, float(a), rank)
        s = jnp.where(hit, -jnp.inf, s)
    return vals, rank


def _peer_route_kernel(x_ref, g_ref, sh_ref, sc_ref, wq_hi_ref, wq_lo_ref, k_hi_ref, k_lo_ref,
                       n1_ref, e1_ref, r2_ref, e2_ref, h_ref, s_scr):
    tb = x_ref.shape[0]
    half = PEER_DKEY // 2
    n_tiles = tb // LANES
    h_hi, h_lo = _split_bf16(_rms_mod(x_ref[...], g_ref[...], sh_ref[0], sc_ref[0]))
    h_ref[...] = h_hi
    q_hi, q_lo = _split_bf16(_dot3(h_hi, h_lo, wq_hi_ref[...], wq_lo_ref[...]))
    for hh in range(PEER_HEADS):
        for p in range(2):
            cols = slice((2 * hh + p) * half, (2 * hh + p + 1) * half)
            k_hi = k_hi_ref[hh, p]
            s_scr[hh, p] = (lax.dot_general(k_hi, q_hi[:, cols], _NT, preferred_element_type=jnp.float32)
                            + lax.dot_general(k_lo_ref[hh, p], q_hi[:, cols], _NT, preferred_element_type=jnp.float32)
                            + lax.dot_general(k_hi, q_lo[:, cols], _NT, preferred_element_type=jnp.float32))

    K = PEER_TOPK
    G = SUBLANES

    def tile(it, carry):
        hh = it // n_tiles
        ln = pl.ds(pl.multiple_of((it % n_tiles) * LANES, LANES), LANES)
        s1 = s_scr[hh, 0, :, ln]
        s2 = s_scr[hh, 1, :, ln]
        v1, _ = _top_vals(s1, K, False)
        v2l, r2 = _top_vals(s2, K, True)
        v2 = jnp.concatenate(v2l, axis=0)
        cand = [v1[0] + v2[:G], v1[0] + v2[G:]]
        cand += [v1[a] + v2[:G] for a in range(1, G)]
        cand += [jnp.concatenate(v1[G:], axis=0) + v2[0:1]]
        c = cand
        for k in range(K):
            m = c[0]
            for ci in c[1:]:
                m = jnp.maximum(m, ci)
            m = jnp.max(m, axis=0, keepdims=True)
            if k + 1 < K:
                c = [jnp.where(ci == m, -jnp.inf, ci) for ci in c]
        tau = m
        top = v1[0] + v2[0:1]
        keep = [ci >= tau for ci in cand]
        z = jnp.zeros_like(tau)
        for ci, ki in zip(cand, keep):
            z = z + jnp.sum(jnp.where(ki, jnp.exp(ci - top), 0.0), axis=0, keepdims=True)
        cnt = [jnp.sum(jnp.where(ki, 1.0, 0.0), axis=0, keepdims=True) for ki in keep[:G + 1]]
        tail = jnp.where(keep[G + 1], 1.0, 0.0)
        n_a = [cnt[0] + cnt[1]] + cnt[2:] + [tail[a:a + 1] for a in range(G)]
        n1 = jnp.zeros_like(s1)
        for a in range(K):
            n1 = jnp.where(s1 == v1[a], n_a[a], n1)
        n1_ref[hh, :, ln] = n1
        e1_ref[hh, :, ln] = jnp.exp(s1 - v1[0]) / z
        r2_ref[hh, :, ln] = r2.astype(jnp.bfloat16)
        e2_ref[hh, :, ln] = jnp.exp(s2 - v2[0:1]).astype(jnp.bfloat16)
        return carry

    lax.fori_loop(0, PEER_HEADS * n_tiles, tile, 0, unroll=4)


def peer_route(x, g, shift_t, scale_t, wq_hi, wq_lo, k_hi, k_lo, tb=PEER_ROUTE_TOKENS):
    T = x.shape[0]
    nh = PEER_HEADS
    blk = pl.BlockSpec((nh, PEER_KEYS, tb), lambda t: (0, 0, t))
    sd = lambda dt: jax.ShapeDtypeStruct((nh, PEER_KEYS, T), dt)
    whole = lambda a: pl.BlockSpec(a.shape, lambda t: (0,) * a.ndim)
    mod = pl.BlockSpec((1, 1, D_MODEL), lambda t: (t, 0, 0))
    return pl.pallas_call(
        _peer_route_kernel,
        grid=(T // tb,),
        in_specs=[pl.BlockSpec((tb, D_MODEL), lambda t: (t, 0)), whole(g), mod, mod,
                  whole(wq_hi), whole(wq_lo), whole(k_hi), whole(k_lo)],
        out_specs=[blk, blk, blk, blk, pl.BlockSpec((tb, D_MODEL), lambda t: (t, 0))],
        out_shape=[sd(jnp.float32), sd(jnp.float32), sd(jnp.bfloat16), sd(jnp.bfloat16),
                   jax.ShapeDtypeStruct((T, D_MODEL), jnp.bfloat16)],
        scratch_shapes=[pltpu.VMEM((nh, 2, PEER_KEYS, tb), jnp.float32)],
        compiler_params=pltpu.CompilerParams(dimension_semantics=("arbitrary",),
                                             vmem_limit_bytes=VMEM_LIMIT),
        name="peer_route",
    )(x, g, shift_t, scale_t, wq_hi, wq_lo, k_hi, k_lo)


def _gelu_tanh(x):
    hx = 0.5 * x
    return hx * jnp.tanh(x * (x * x * (0.7978845608028654 * 0.044715) + 0.7978845608028654)) + hx


def _peer_expert_kernel(h_ref, u_ref, vt_even_ref, vt_prev_ref, vt_last_ref, n1_ref, e1_ref, r2_ref, e2_ref,
                        x_ref, gate_ref, o_ref, acc_ref, a0_scr, a1_scr, w0_scr, w1_scr):
    c = pl.program_id(1)
    ec, tb = a0_scr.shape
    n_i = ec // PEER_KEYS
    bf16 = jnp.bfloat16
    f32 = jnp.float32

    @pl.when(c == 0)
    def _():
        acc_ref[...] = jnp.zeros_like(acc_ref)
        w1_scr[...] = jnp.zeros_like(w1_scr)

    def gate_times_act(a_scr, w_scr, chunk):
        igrp = pl.ds(pl.multiple_of(chunk * n_i, SUBLANES), SUBLANES)
        for lt in range(tb // GATE_LANES):
            ln = slice(lt * GATE_LANES, (lt + 1) * GATE_LANES)
            for ii in range(n_i):
                rows = slice(ii * PEER_KEYS, (ii + 1) * PEER_KEYS)
                g = jnp.zeros((PEER_KEYS, GATE_LANES), bf16)
                for hh in range(PEER_HEADS):
                    n1 = jnp.broadcast_to(n1_ref[hh, igrp, ln][ii:ii + 1], (PEER_KEYS, GATE_LANES)).astype(bf16)
                    e1 = jnp.broadcast_to(e1_ref[hh, igrp, ln][ii:ii + 1], (PEER_KEYS, GATE_LANES)).astype(bf16)
                    g = g + jnp.where(r2_ref[hh, :, ln] < n1, e2_ref[hh, :, ln] * e1, jnp.zeros((), bf16))
                w_scr[rows, ln] = g * _gelu_tanh(a_scr[rows, ln])

    h = h_ref[...]
    a0_scr[...] = lax.dot_general(u_ref[:ec], h, _NT, preferred_element_type=f32).astype(bf16)
    acc_ref[...] += jnp.dot(vt_prev_ref[...], w1_scr[...], preferred_element_type=f32)
    gate_times_act(a0_scr, w0_scr, 2 * c)
    a1_scr[...] = lax.dot_general(u_ref[ec:], h, _NT, preferred_element_type=f32).astype(bf16)
    acc_ref[...] += jnp.dot(vt_even_ref[...], w0_scr[...], preferred_element_type=f32)
    gate_times_act(a1_scr, w1_scr, 2 * c + 1)

    @pl.when(c == pl.num_programs(1) - 1)
    def _():
        y = acc_ref[...] + jnp.dot(vt_last_ref[...], w1_scr[...], preferred_element_type=f32)
        o_ref[...] = x_ref[...] + gate_ref[0] * y.T


def peer_experts(h_bf16, u_bf16, vt_bf16, n1, e1, r2, e2, x, gate_t, tb=PEER_EXPERT_TOKENS, ec=SUBLANES * PEER_KEYS):
    T = h_bf16.shape[0]
    n_steps = PEER_N // (2 * ec)
    rblk = pl.BlockSpec((PEER_HEADS, PEER_KEYS, tb), lambda t, c: (0, 0, t))
    vt_blk = lambda chunk_of: pl.BlockSpec((D_MODEL, ec), lambda t, c: (0, chunk_of(c)))
    return pl.pallas_call(
        _peer_expert_kernel,
        grid=(T // tb, n_steps),
        in_specs=[pl.BlockSpec((tb, D_MODEL), lambda t, c: (t, 0)),
                  pl.BlockSpec((2 * ec, D_MODEL), lambda t, c: (c, 0)),
                  vt_blk(lambda c: 2 * c),
                  vt_blk(lambda c: jnp.maximum(2 * c - 1, 0)),
                  vt_blk(lambda c: 2 * n_steps - 1),
                  rblk, rblk, rblk, rblk,
                  pl.BlockSpec((tb, D_MODEL), lambda t, c: (t, 0)),
                  pl.BlockSpec((1, 1, D_MODEL), lambda t, c: (t, 0, 0))],
        out_specs=pl.BlockSpec((tb, D_MODEL), lambda t, c: (t, 0)),
        out_shape=jax.ShapeDtypeStruct((T, D_MODEL), jnp.float32),
        scratch_shapes=[pltpu.VMEM((D_MODEL, tb), jnp.float32)] + [pltpu.VMEM((ec, tb), jnp.bfloat16)] * 4,
        compiler_params=pltpu.CompilerParams(dimension_semantics=("arbitrary", "arbitrary"),
                                             vmem_limit_bytes=VMEM_LIMIT),
        name="peer_experts",
    )(h_bf16, u_bf16, vt_bf16, vt_bf16, vt_bf16, n1, e1, r2, e2, x, gate_t)


def _cast_table_kernel(x_ref, o_ref, *, transpose):
    x = x_ref[0]
    o_ref[...] = (x.T if transpose else x).astype(o_ref.dtype)


def cast_table(tabs, li, transpose, rows=1024):
    _, N, D = tabs.shape
    out_spec = pl.BlockSpec((D, rows), lambda i: (0, i)) if transpose else pl.BlockSpec((rows, D), lambda i: (i, 0))
    return pl.pallas_call(
        partial(_cast_table_kernel, transpose=transpose), grid=(N // rows,),
        in_specs=[pl.BlockSpec((1, rows, D), lambda i: (li, i, 0))],
        out_specs=out_spec,
        out_shape=jax.ShapeDtypeStruct((D, N) if transpose else (N, D), jnp.bfloat16),
        compiler_params=pltpu.CompilerParams(dimension_semantics=("arbitrary",), vmem_limit_bytes=VMEM_LIMIT),
        name="cast_table",
    )(tabs)


def peer_weights(w_q, sub_keys, u_tabs, v_tabs, li):
    return (_split_bf16(w_q[li]) + _split_bf16(sub_keys[li])
            + (cast_table(u_tabs, li, transpose=False), cast_table(v_tabs, li, transpose=True)))


def peer_block(x, g, mods_route, mods_expert, weights):
    wq_hi, wq_lo, k_hi, k_lo, u_bf16, vt_bf16 = weights
    n1, e1, r2, e2, h_bf16 = peer_route(x, g[None], mods_route[0], mods_route[1], wq_hi, wq_lo, k_hi, k_lo)
    return peer_experts(h_bf16, u_bf16, vt_bf16, n1, e1, r2, e2, x, mods_expert)


def kernel(x_prompt, x_sample, state_rwkv_fwd, state_rwkv_bwd, cache_mla_ckv, cache_mla_kpe, c, c_ctx,
           norm_g, w_mod, b_mod, ab_w_in, rw_mu, rw_w0, rw_w2, rw_a0, rw_a2, rw_g2, rw_k_k, rw_k_a, rw_r_k,
           rw_lnx_g, rw_lnx_b, mla_q_norm, mla_q_up, mla_kv_norm, mla_kv_up, mla_qn, mla_kn, ab_w_out,
           hy_w_in, hy_b_in, hy_conv_w, hy_conv_b, hy_f_w1, hy_f_b1, hy_f_w2, hy_f_b2, hy_f_w3, hy_f_freq,
           hy_bias, hy_w_out, peer_w_q, peer_keys, peer_u, peer_v):
    rope = axial_rope(x_sample.shape[1])
    D = D_MODEL
    groups = [dict(x=x_prompt.reshape(-1, D), B=x_prompt.shape[0], L=x_prompt.shape[1]),
              dict(x=x_sample.reshape(-1, D), B=x_sample.shape[0], L=x_sample.shape[1])]
    st_f, st_b, st_ckv, st_kpe = [], [], [], []
    for li in range(DEPTH):
        j = li // 2
        groups[0]["mod"] = (_mm(jax.nn.silu(c_ctx)[None], w_mod[li]) + b_mod[li]).reshape(1, 6, D)
        groups[1]["mod"] = (_mm(jax.nn.silu(c), w_mod[li]) + b_mod[li]).reshape(-1, 6, D)
        peer_w = peer_weights(peer_w_q, peer_keys, peer_u, peer_v, li)
        for gi, g in enumerate(groups):
            x, B, L = g["x"], g["B"], g["L"]
            mod = lambda i, tm: tile_mod(g["mod"][:, i], B * L, tm)
            if li % 2 == 0:
                proj = norm_mod_matmul(x, norm_g[li, 0], mod(0, ROW_TILE), mod(1, ROW_TILE), ab_w_in[j],
                                       jnp.zeros((ab_w_in.shape[-1],), jnp.float32), ROW_TILE)
                prep = rwkv_prep(proj, L, rw_mu[j], rw_w0[j], rw_w2[j], rw_a0[j], rw_a2[j], rw_g2[j],
                                 rw_k_k[j], rw_k_a[j], rw_r_k[j], SEQ_TILE)
                s0s = None if gi == 0 else (state_rwkv_fwd[:, j], state_rwkv_bwd[:, j])
                ys, (sf, sb) = rwkv_scans(prep[:9], B, L, s0s)
                ctx = None if gi == 0 else (cache_mla_ckv[:, j], cache_mla_kpe[:, j], rope)
                att, ckv, kpe = mla_mixer(proj[:, RW_IN:], B, L, ctx,
                                          (mla_q_norm[j], mla_q_up[j], mla_kv_norm[j], mla_kv_up[j], mla_qn[j], mla_kn[j]))
                if gi == 0:
                    st_f.append(sf)
                    st_b.append(sb)
                    st_ckv.append(ckv)
                    st_kpe.append(kpe)
                x = rwkv_out_residual(ys[0], ys[1], prep[10], prep[9], att, rw_lnx_g[j], rw_lnx_b[j],
                                      ab_w_out[j], x, mod(2, ROW_TILE), ROW_TILE)
            else:
                proj = norm_mod_matmul(x, norm_g[li, 0], mod(0, ROW_TILE), mod(1, ROW_TILE), hy_w_in[j], hy_b_in[j],
                                       ROW_TILE)
                x0, zin = hyena_prep(proj, L, hy_conv_w[j], hy_conv_b[j], SEQ_TILE)
                circ = hyena_filters(L, hy_f_w1[j], hy_f_b1[j], hy_f_w2[j], hy_f_b2[j], hy_f_w3[j], hy_f_freq[j])
                conv = hyena_longconv(zin.reshape(B, L, D), circ).reshape(B * L, D)
                x = hyena_out_residual(x0, conv, zin, hy_bias[j], hy_w_out[j], x, mod(2, ROW_TILE), ROW_TILE)
            g["x"] = peer_block(x, norm_g[li, 1], (mod(3, PEER_ROUTE_TOKENS), mod(4, PEER_ROUTE_TOKENS)),
                                mod(5, PEER_EXPERT_TOKENS), peer_w)
    xp = groups[0]["x"].reshape(x_prompt.shape)
    xs = groups[1]["x"].reshape(x_sample.shape)
    new_state_rwkv_fwd = jnp.stack(st_f, axis=1).astype(x_prompt.dtype)
    new_state_rwkv_bwd = jnp.stack(st_b, axis=1).astype(x_prompt.dtype)
    new_cache_mla_ckv = jnp.stack(st_ckv, axis=1)
    new_cache_mla_kpe = jnp.stack(st_kpe, axis=1)
    return (xp, xs, new_state_rwkv_fwd, new_state_rwkv_bwd, new_cache_mla_ckv, new_cache_mla_kpe)
```

```python
import math
from functools import lru_cache, partial

import jax
import jax.numpy as jnp
import numpy as np
from jax import lax
from jax.experimental import pallas as pl
from jax.experimental.pallas import tpu as pltpu

D_MODEL = 1024
DEPTH = 2
GRID_W = 64
EPS = 1e-6
RW_HEADS = 8
RW_HEAD = 64
RW_DIM = RW_HEADS * RW_HEAD
W_LORA = 64
A_LORA = 64
G_LORA = 128
LNX_EPS = 64e-5
RW_IN = 3 * RW_DIM + W_LORA + A_LORA + G_LORA
RW_SPLITS = (RW_DIM, 2 * RW_DIM, 3 * RW_DIM, 3 * RW_DIM + W_LORA, 3 * RW_DIM + W_LORA + A_LORA)
MLA_HEADS = 4
QK_NOPE = 128
QK_ROPE = 64
QK_HEAD = QK_NOPE + QK_ROPE
V_HEAD = 128
Q_LORA = 256
KV_LORA = 128
MLA_DIM = MLA_HEADS * V_HEAD
ROPE_THETA = 10000.0
HY_BANDS = 16
HY_TARGET = 1e-2
HY_FAST = 0.3
HY_SLOW = 1.5
PEER_KEYS = 128
PEER_HEADS = 8
PEER_DKEY = 256
PEER_TOPK = 16

LANES = 128
SUBLANES = 8
V7X_VMEM_BYTES = 64 * 1024 * 1024
VMEM_LIMIT = V7X_VMEM_BYTES * 7 // 8
_NT = (((1,), (1,)), ((), ()))


def _split_bf16(x):
    hi = x.astype(jnp.bfloat16)
    lo = (x - hi.astype(jnp.float32)).astype(jnp.bfloat16)
    return hi, lo


def _dot3(a_hi, a_lo, b_hi, b_lo):
    f32 = jnp.float32
    return (jnp.dot(a_hi, b_hi, preferred_element_type=f32) + jnp.dot(a_lo, b_hi, preferred_element_type=f32)
            + jnp.dot(a_hi, b_lo, preferred_element_type=f32))


def _mm_kernel(a_ref, b_ref, o_ref, *, split):
    if split:
        a_hi, a_lo = _split_bf16(a_ref[...])
        b_hi, b_lo = _split_bf16(b_ref[...])
        o_ref[...] = _dot3(a_hi, a_lo, b_hi, b_lo)
    else:
        o_ref[...] = jnp.dot(a_ref[...].astype(jnp.bfloat16), b_ref[...].astype(jnp.bfloat16),
                             preferred_element_type=jnp.float32)


def _mm(a, b, tm=512, tn=512, split=False):
    lead = a.shape[:-1]
    K = a.shape[-1]
    N = b.shape[-1]
    a2 = a.reshape(-1, K)
    if K % LANES:
        kp = -K % LANES
        a2 = jnp.pad(a2, ((0, 0), (0, kp)))
        b = jnp.pad(b, ((0, kp), (0, 0)))
        K += kp
    M = a2.shape[0]
    tm = min(tm, M)
    tn = min(tn, N)
    if N % tn:
        tn = N
    assert M % tm == 0 and N % tn == 0
    out = pl.pallas_call(
        partial(_mm_kernel, split=split),
        grid=(M // tm, N // tn),
        in_specs=[pl.BlockSpec((tm, K), lambda i, j: (i, 0)),
                  pl.BlockSpec((K, tn), lambda i, j: (0, j))],
        out_specs=pl.BlockSpec((tm, tn), lambda i, j: (i, j)),
        out_shape=jax.ShapeDtypeStruct((M, N), jnp.float32),
        name="matmul",
    )(a2, b)
    return out.reshape(*lead, N)


def axial_rope(L):
    rows = L // GRID_W
    row = jnp.repeat(jnp.arange(rows, dtype=jnp.float32), GRID_W)
    col = jnp.tile(jnp.arange(GRID_W, dtype=jnp.float32), rows)
    n_freq = QK_ROPE // 4
    inv = ROPE_THETA ** (-jnp.arange(n_freq, dtype=jnp.float32) / n_freq)
    ang = jnp.concatenate([row[:, None] * inv, col[:, None] * inv], axis=-1)
    return jnp.cos(ang), jnp.sin(ang)


def _mla_attn_kernel(q_ref, k_ref, v_ref, *rest, with_ctx):
    if with_ctx:
        kc_ref, vc_ref, o_ref = rest
    else:
        (o_ref,) = rest
    bf16 = jnp.bfloat16
    f32 = jnp.float32
    scale = QK_HEAD ** -0.5
    for h in range(MLA_HEADS):
        qs = slice(h * QK_HEAD, (h + 1) * QK_HEAD)
        vs = slice(h * V_HEAD, (h + 1) * V_HEAD)
        q = q_ref[0, :, qs].astype(bf16)
        s = lax.dot_general(q, k_ref[0, :, qs].astype(bf16), _NT, preferred_element_type=f32) * scale
        m = jnp.max(s, axis=-1, keepdims=True)
        if with_ctx:
            sc = lax.dot_general(q, kc_ref[0, :, qs].astype(bf16), _NT, preferred_element_type=f32) * scale
            m = jnp.maximum(m, jnp.max(sc, axis=-1, keepdims=True))
            pc = jnp.exp(sc - m)
        p = jnp.exp(s - m)
        l = jnp.sum(p, axis=-1, keepdims=True)
        if with_ctx:
            l = l + jnp.sum(pc, axis=-1, keepdims=True)
        o = jnp.dot(p.astype(bf16), v_ref[0, :, vs].astype(bf16), preferred_element_type=f32)
        if with_ctx:
            o = o + jnp.dot(pc.astype(bf16), vc_ref[0, :, vs].astype(bf16), preferred_element_type=f32)
        o_ref[0, :, vs] = o / l


def attend(q, k, v, ctx=None, tq=256):
    B, Lq, _ = q.shape
    tq = min(tq, Lq)
    kv = [k, v] + (list(ctx) if ctx is not None else [])
    full = lambda a: pl.BlockSpec((1,) + a.shape[1:], lambda b, i: (b, 0, 0))
    return pl.pallas_call(
        partial(_mla_attn_kernel, with_ctx=ctx is not None), grid=(B, Lq // tq),
        in_specs=[pl.BlockSpec((1, tq, MLA_HEADS * QK_HEAD), lambda b, i: (b, i, 0))] + [full(a) for a in kv],
        out_specs=pl.BlockSpec((1, tq, MLA_HEADS * V_HEAD), lambda b, i: (b, i, 0)),
        out_shape=jax.ShapeDtypeStruct((B, Lq, MLA_HEADS * V_HEAD), jnp.float32),
        compiler_params=pltpu.CompilerParams(dimension_semantics=("arbitrary", "arbitrary"),
                                             vmem_limit_bytes=VMEM_LIMIT),
        name="mla_attend",
    )(q, *kv)


def _first_row_block(fn):
    pl.when(pl.program_id(2) == 0)(fn)


def _dft_pair_kernel(c_tab, s_tab, x_ref, oc_ref, os_ref, x16):
    def _():
        x16[...] = x_ref[0].astype(jnp.bfloat16)
    _first_row_block(_)
    oc_ref[0] = jnp.dot(c_tab[...], x16[...], preferred_element_type=jnp.float32)
    os_ref[0] = jnp.dot(s_tab[...], x16[...], preferred_element_type=jnp.float32)


def _dft_spectral_kernel(c_tab, s_tab, x_ref, hr_ref, hi_ref, yr_ref, yi_ref, x16):
    def _():
        x16[...] = x_ref[0].astype(jnp.bfloat16)
    _first_row_block(_)
    rb, L = c_tab.shape
    zc = jnp.dot(c_tab[...], x16[...], preferred_element_type=jnp.float32)
    zs = jnp.dot(s_tab[...], x16[...], preferred_element_type=jnp.float32)
    hr = hr_ref[...]
    hi = hi_ref[...]
    f = pl.program_id(2) * rb + lax.broadcasted_iota(jnp.int32, zc.shape, 0)
    yr_ref[0] = jnp.where(f == 0, zc * hr * (0.5 / L), (zc * hr - zs * hi) * (1.0 / L))
    yi_ref[0] = jnp.where(f == 0, zs * hi * (0.5 / L), (zc * hi + zs * hr) * (1.0 / L))


def _dft_sum_kernel(c_tab, s_tab, x_ref, y_ref, o_ref, x16, y16):
    def _():
        x16[...] = x_ref[0].astype(jnp.bfloat16)
        y16[...] = y_ref[0].astype(jnp.bfloat16)
    _first_row_block(_)
    o_ref[0] = (jnp.dot(c_tab[...], x16[...], preferred_element_type=jnp.float32)
                + jnp.dot(s_tab[...], y16[...], preferred_element_type=jnp.float32))


@lru_cache(maxsize=None)
def dft_tables(L):
    f = np.arange(L, dtype=np.int64)
    ang = ((f[:, None] * f[None, :]) % (2 * L)).astype(np.float64) * (math.pi / L)
    c = np.cos(ang).astype(jnp.bfloat16)
    s = -np.sin(ang)
    alt = np.where(f % 2 == 0, 1.0, -1.0)
    s_ana = np.where(f[:, None] == 0, alt[None, :], s).astype(jnp.bfloat16)
    s_syn = np.where(f[None, :] == 0, alt[:, None], s).astype(jnp.bfloat16)
    return (c, s_ana), (c, s_syn)


def _dft_call(body, tabs, xs, n_out, name, row_inputs=(), rb=1024, nb=512):
    B, L, N = xs[0].shape
    rb = min(rb, L)
    nb = min(nb, N)
    tab = pl.BlockSpec((rb, L), lambda b, n, i: (i, 0))
    xin = pl.BlockSpec((1, L, nb), lambda b, n, i: (b, 0, n))
    rin = pl.BlockSpec((rb, nb), lambda b, n, i: (i, n))
    out = pl.BlockSpec((1, rb, nb), lambda b, n, i: (b, i, n))
    sd = jax.ShapeDtypeStruct((B, L, N), jnp.float32)
    return pl.pallas_call(
        body, grid=(B, N // nb, L // rb),
        in_specs=[tab] * len(tabs) + [xin] * len(xs) + [rin] * len(row_inputs),
        out_specs=[out] * n_out if n_out > 1 else out,
        out_shape=[sd] * n_out if n_out > 1 else sd,
        scratch_shapes=[pltpu.VMEM((L, nb), jnp.bfloat16)] * len(xs),
        compiler_params=pltpu.CompilerParams(dimension_semantics=("arbitrary",) * 3,
                                             vmem_limit_bytes=VMEM_LIMIT),
        name=name,
    )(*tabs, *xs, *row_inputs)


def hyena_longconv(zin, circ):
    B, L, D = zin.shape
    tabs, tabs_syn = dft_tables(L)
    alt = jnp.where(jnp.arange(L) % 2 == 0, 1.0, -1.0).astype(jnp.float32)[:, None]
    fc, fs = _dft_call(_dft_pair_kernel, tabs, [jnp.concatenate([circ[:L], circ[L:]], axis=-1)[None]], 2, "dft_filter")
    hr = fc[0, :, :D] + alt * fc[0, :, D:]
    hi = fs[0, :, :D] + alt * fs[0, :, D:]
    yr, yi = _dft_call(_dft_spectral_kernel, tabs, [zin], 2, "dft_analysis", row_inputs=(hr, hi))
    return _dft_call(_dft_sum_kernel, tabs_syn, [yr, yi], 1, "dft_synthesis")


SCAN_ACCS = 4


def _rwkv_scan_kernel(rf_ref, kkf_ref, vf_ref, wf_ref, kdf_ref, bf_ref, rb_ref, kkb_ref, vb_ref, wb_ref, kdb_ref, bb_ref,
                      s0f_ref, s0b_ref, yf_ref, yb_ref, sff_ref, sfb_ref, sf_scr, sb_scr):
    c = pl.program_id(0)
    tc, nk, _ = rf_ref.shape
    nv = vf_ref.shape[1]
    groups = RW_HEAD // nk

    @pl.when(c == 0)
    def _():
        sf_scr[...] = s0f_ref[...]
        sb_scr[...] = s0b_ref[...]

    def row(ref, tt, k):
        return jnp.broadcast_to(ref[tt, pl.ds(k, 1), :], (nv, LANES))

    def all_groups(p):
        tot = p
        for q in range(1, groups):
            tot = tot + pltpu.roll(p, q * (LANES // groups), 1)
        return tot

    def one_step(tt, r_ref, kk_ref, v_ref, w_ref, kd_ref, b_ref, s_scr, y_ref):
        parts = [None] * SCAN_ACCS
        for k in range(nk):
            term = s_scr[k] * row(kk_ref, tt, k)
            parts[k % SCAN_ACCS] = term if parts[k % SCAN_ACCS] is None else parts[k % SCAN_ACCS] + term
        sa = -all_groups((parts[0] + parts[1]) + (parts[2] + parts[3]))
        vv = v_ref[tt]
        parts = [None] * SCAN_ACCS
        for k in range(nk):
            s_new = s_scr[k] * row(w_ref, tt, k) + sa * row(b_ref, tt, k) + vv * row(kd_ref, tt, k)
            s_scr[k] = s_new
            term = s_new * row(r_ref, tt, k)
            parts[k % SCAN_ACCS] = term if parts[k % SCAN_ACCS] is None else parts[k % SCAN_ACCS] + term
        y_ref[tt] = all_groups((parts[0] + parts[1]) + (parts[2] + parts[3]))

    def step(t, carry):
        one_step(t, rf_ref, kkf_ref, vf_ref, wf_ref, kdf_ref, bf_ref, sf_scr, yf_ref)
        one_step(tc - 1 - t, rb_ref, kkb_ref, vb_ref, wb_ref, kdb_ref, bb_ref, sb_scr, yb_ref)
        return carry

    lax.fori_loop(0, tc, step, 0, unroll=4 if groups > 1 else 2)

    @pl.when(c == pl.num_programs(0) - 1)
    def _():
        sff_ref[...] = sf_scr[...]
        sfb_ref[...] = sb_scr[...]


def rwkv_scan(r, kk, v, wkb_fwd, wkb_bwd, s0_fwd, s0_bwd, tc=32):
    L, nk, _ = r.shape
    nv = v.shape[1]
    assert L % tc == 0 and RW_HEAD % nk == 0
    nc = L // tc
    fwd = lambda rows: pl.BlockSpec((tc, rows, LANES), lambda c: (c, 0, 0))
    bwd = lambda rows: pl.BlockSpec((tc, rows, LANES), lambda c: (nc - 1 - c, 0, 0))
    state = pl.BlockSpec((nk, nv, LANES), lambda c: (0, 0, 0))
    y_sd = jax.ShapeDtypeStruct((L, nv, LANES), jnp.float32)
    s_sd = jax.ShapeDtypeStruct((nk, nv, LANES), jnp.float32)
    return pl.pallas_call(
        _rwkv_scan_kernel,
        grid=(nc,),
        in_specs=[fwd(nk), fwd(nk), fwd(nv), fwd(nk), fwd(nk), fwd(nk),
                  bwd(nk), bwd(nk), bwd(nv), bwd(nk), bwd(nk), bwd(nk), state, state],
        out_specs=[fwd(nv), bwd(nv), state, state],
        out_shape=[y_sd, y_sd, s_sd, s_sd],
        scratch_shapes=[pltpu.VMEM((nk, nv, LANES), jnp.float32)] * 2,
        compiler_params=pltpu.CompilerParams(dimension_semantics=("arbitrary",), vmem_limit_bytes=VMEM_LIMIT),
        name="rwkv_scan",
    )(r, kk, v, *wkb_fwd, r, kk, v, *wkb_bwd, s0_fwd, s0_bwd)


def k_to_lanes(x, ksplit):
    B, L, H, N = x.shape
    nk = N // ksplit
    return x.reshape(B, L, H, ksplit, nk).transpose(1, 4, 3, 0, 2).reshape(L, nk, ksplit * B * H)


def v_to_lanes(x, ksplit):
    B, L, H, N = x.shape
    return jnp.tile(x.transpose(1, 3, 0, 2).reshape(L, N, B * H), (1, 1, ksplit))


def v_from_lanes(y, B, H):
    L, N, _ = y.shape
    return y[:, :, :B * H].reshape(L, N, B, H).transpose(2, 0, 3, 1)


def state_to_lanes(s, ksplit):
    B, H, N, K = s.shape
    nk = K // ksplit
    return s.reshape(B, H, N, ksplit, nk).transpose(4, 2, 3, 0, 1).reshape(nk, N, ksplit * B * H)


def state_from_lanes(s, B, H, ksplit):
    nk, N, _ = s.shape
    return s.reshape(nk, N, ksplit, B, H).transpose(3, 4, 1, 2, 0).reshape(B, H, N, ksplit * nk)


def rwkv_scan_both(rh, kk, vh, decay2, kd2, b2, s0_2, tc=32):
    B, L, H, N = rh.shape
    ksplit = LANES // (B * H)
    assert B * H * ksplit == LANES
    wkb = [[k_to_lanes(t[d], ksplit) for t in (decay2, kd2, b2)] for d in range(2)]
    y_f, y_b, sf_f, sf_b = rwkv_scan(k_to_lanes(rh, ksplit), k_to_lanes(kk, ksplit), v_to_lanes(vh, ksplit),
                                     wkb[0], wkb[1], state_to_lanes(s0_2[0], ksplit), state_to_lanes(s0_2[1], ksplit),
                                     tc=tc)
    return ([v_from_lanes(y_f, B, H), v_from_lanes(y_b, B, H)],
            [state_from_lanes(sf_f, B, H, ksplit), state_from_lanes(sf_b, B, H, ksplit)])


def _rms_mod(x, g, shift, scale):
    y = x * lax.rsqrt(jnp.mean(x * x, axis=-1, keepdims=True) + EPS)
    return (y * g) * (1.0 + scale) + shift


def tile_mod(m, n_rows, tm):
    return jnp.repeat(m, n_rows // m.shape[0] // tm, axis=0)[:, None, :]


def _norm_mod_matmul_kernel(x_ref, g_ref, sh_ref, sc_ref, w_ref, b_ref, o_ref):
    h = _rms_mod(x_ref[...], g_ref[...], sh_ref[0], sc_ref[0])
    o_ref[...] = jnp.dot(h.astype(jnp.bfloat16), w_ref[...], preferred_element_type=jnp.float32) + b_ref[...]


def norm_mod_matmul(x, g, shift_t, scale_t, w, b, tm):
    T, D = x.shape
    N = w.shape[1]
    assert T % tm == 0 and shift_t.shape[0] == T // tm
    mod = pl.BlockSpec((1, 1, D), lambda i: (i, 0, 0))
    return pl.pallas_call(
        _norm_mod_matmul_kernel, grid=(T // tm,),
        in_specs=[pl.BlockSpec((tm, D), lambda i: (i, 0)), pl.BlockSpec((1, D), lambda i: (0, 0)), mod, mod,
                  pl.BlockSpec((D, N), lambda i: (0, 0)), pl.BlockSpec((1, N), lambda i: (0, 0))],
        out_specs=pl.BlockSpec((tm, N), lambda i: (i, 0)),
        out_shape=jax.ShapeDtypeStruct((T, N), jnp.float32),
        compiler_params=pltpu.CompilerParams(dimension_semantics=("arbitrary",), vmem_limit_bytes=VMEM_LIMIT),
        name="norm_mod_matmul",
    )(x, g[None], shift_t, scale_t, w.astype(jnp.bfloat16), b[None])


def _head_sums(x, ones_bd):
    hi, lo = _split_bf16(x)
    return (jnp.dot(hi, ones_bd, preferred_element_type=jnp.float32)
            + jnp.dot(lo, ones_bd, preferred_element_type=jnp.float32))


def _with_neighbours(x_ref, prev_ref, next_ref, first_ref, last_ref, ncols):
    tm = x_ref.shape[0]
    x = x_ref[:, :ncols]
    row = lax.broadcasted_iota(jnp.int32, (tm, 1), 0)
    prev_row = prev_ref[SUBLANES - 1:SUBLANES, :ncols] * (1.0 - first_ref[0, :, :1])
    next_row = next_ref[0:1, :ncols] * (1.0 - last_ref[0, :, :1])
    prev = jnp.where(row == 0, prev_row, pltpu.roll(x, 1, 0))
    nxt = jnp.where(row == tm - 1, next_row, pltpu.roll(x, tm - 1, 0))
    return x, prev, nxt


def _neighbour_specs(T, C, L, tm):
    assert L % tm == 0 and T % L == 0 and tm % SUBLANES == 0
    n_tiles = T // tm
    hb = tm // SUBLANES
    start = np.arange(n_tiles) * tm % L
    first = np.broadcast_to((start == 0).astype(np.float32)[:, None, None], (n_tiles, 1, LANES))
    last = np.broadcast_to((start + tm == L).astype(np.float32)[:, None, None], (n_tiles, 1, LANES))
    flag = pl.BlockSpec((1, 1, LANES), lambda i: (i, 0, 0))
    specs = [pl.BlockSpec((tm, C), lambda i: (i, 0)),
             pl.BlockSpec((SUBLANES, C), lambda i: (jnp.maximum(i * hb - 1, 0), 0)),
             pl.BlockSpec((SUBLANES, C), lambda i: (jnp.minimum((i + 1) * hb, T // SUBLANES - 1), 0)),
             flag, flag]
    return specs, (jnp.asarray(first), jnp.asarray(last))


def _rwkv_prep_kernel(proj_ref, prev_ref, next_ref, first_ref, last_ref, mu_ref, w0_ref, w2_ref, a0_ref, a2_ref,
                      g2_ref, kk_ref, ka_ref, rk_ref, ones_ref,
                      r_out, kk_out, v_out, w_out0, w_out1, kd_out0, kd_out1, b_out0, b_out1, gate_out, bonus_out):
    bf16 = jnp.bfloat16
    f32 = jnp.float32
    x, prev, nxt = _with_neighbours(proj_ref, prev_ref, next_ref, first_ref, last_ref, RW_IN)
    rw = x + mu_ref[...] * (0.5 * (prev + nxt) - x)
    r = rw[:, RW_SPLITS[0] - RW_DIM:RW_SPLITS[0]]
    k = rw[:, RW_SPLITS[0]:RW_SPLITS[1]]
    v = rw[:, RW_SPLITS[1]:RW_SPLITS[2]]
    wd = rw[:, RW_SPLITS[2]:RW_SPLITS[3]]
    ad = rw[:, RW_SPLITS[3]:RW_SPLITS[4]]
    gd = rw[:, RW_SPLITS[4]:]
    ones_bd = ones_ref[...]
    kk = k * kk_ref[...]
    kk = kk / jnp.maximum(jnp.sqrt(_head_sums(kk * kk, ones_bd)), 1e-12)
    tanh_wd = jnp.tanh(wd).astype(bf16)
    ad16 = ad.astype(bf16)
    rk = r * rk_ref[...]
    bonus = jnp.zeros_like(r)
    for d, (w_o, kd_o, b_o) in enumerate(((w_out0, kd_out0, b_out0), (w_out1, kd_out1, b_out1))):
        w_raw = w0_ref[d:d + 1, :] + jnp.dot(tanh_wd, w2_ref[d], preferred_element_type=f32)
        w_o[...] = jnp.exp(-jnp.exp(-jax.nn.softplus(-w_raw) - 0.5))
        lr = jax.nn.sigmoid(a0_ref[d:d + 1, :] + jnp.dot(ad16, a2_ref[d], preferred_element_type=f32))
        kd = k * (1.0 + (lr - 1.0) * ka_ref[...])
        kd_o[...] = kd
        b_o[...] = kk * lr
        bonus = bonus + _head_sums(rk * kd, ones_bd)
    r_out[...] = r
    kk_out[...] = kk
    v_out[...] = v
    gate_out[...] = jnp.dot(jax.nn.sigmoid(gd).astype(bf16), g2_ref[...], preferred_element_type=f32)
    bonus_out[...] = bonus * v


def rwkv_prep(proj, L, mu, w0, w2, a0, a2, g2, k_k, k_a, r_k, tm):
    T, C = proj.shape
    nb_specs, flags = _neighbour_specs(T, C, L, tm)
    row2 = lambda a: a.reshape(1, -1)
    whole = lambda a: pl.BlockSpec(a.shape, lambda i: (0,) * a.ndim)
    out = pl.BlockSpec((tm, RW_DIM), lambda i: (i, 0))
    consts = (row2(mu), w0, w2.astype(jnp.bfloat16), a0, a2.astype(jnp.bfloat16), g2.astype(jnp.bfloat16),
              row2(k_k), row2(k_a), row2(r_k), _segment_ones(RW_DIM, RW_HEAD))
    return pl.pallas_call(
        _rwkv_prep_kernel, grid=(T // tm,),
        in_specs=nb_specs + [whole(a) for a in consts],
        out_specs=[out] * 11,
        out_shape=[jax.ShapeDtypeStruct((T, RW_DIM), jnp.float32)] * 11,
        compiler_params=pltpu.CompilerParams(dimension_semantics=("arbitrary",), vmem_limit_bytes=VMEM_LIMIT),
        name="rwkv_prep",
    )(proj, proj, proj, *flags, *consts)


def _rwkv_out_kernel(y0_ref, y1_ref, bonus_ref, gate_ref, mla_ref, lng_ref, lnb_ref, ones_ref, w_ref, x_ref, mod_ref,
                     o_ref):
    bf16 = jnp.bfloat16
    f32 = jnp.float32
    ones_bd = ones_ref[...]
    y = y0_ref[...] + y1_ref[...]
    dlt = y - _head_sums(y, ones_bd) * (1.0 / RW_HEAD)
    var = _head_sums(dlt * dlt, ones_bd) * (1.0 / RW_HEAD)
    yn = dlt * lax.rsqrt(var + LNX_EPS) * lng_ref[...] + lnb_ref[...]
    rw_out = (yn + bonus_ref[...]) * gate_ref[...]
    upd = (jnp.dot(rw_out.astype(bf16), w_ref[:RW_DIM], preferred_element_type=f32)
           + jnp.dot(mla_ref[...].astype(bf16), w_ref[RW_DIM:], preferred_element_type=f32))
    o_ref[...] = x_ref[...] + mod_ref[0] * upd


def rwkv_out_residual(y0, y1, bonus, gate, mla_out, lnx_g, lnx_b, w_out, x, gate_t, tm):
    T, D = x.shape
    row = lambda n: pl.BlockSpec((tm, n), lambda i: (i, 0))
    whole = lambda a: pl.BlockSpec(a.shape, lambda i: (0,) * a.ndim)
    consts = (lnx_g.reshape(1, -1), lnx_b.reshape(1, -1), _segment_ones(RW_DIM, RW_HEAD), w_out.astype(jnp.bfloat16))
    return pl.pallas_call(
        _rwkv_out_kernel, grid=(T // tm,),
        in_specs=[row(RW_DIM)] * 4 + [row(MLA_DIM)] + [whole(a) for a in consts]
                 + [row(D), pl.BlockSpec((1, 1, D), lambda i: (i, 0, 0))],
        out_specs=row(D),
        out_shape=jax.ShapeDtypeStruct((T, D), jnp.float32),
        compiler_params=pltpu.CompilerParams(dimension_semantics=("arbitrary",), vmem_limit_bytes=VMEM_LIMIT),
        name="rwkv_out_residual",
    )(y0, y1, bonus, gate, mla_out, *consts, x, gate_t)


def rwkv_scans(prep, B, L, s0s):
    heads = lambda t: t.reshape(B, L, RW_HEADS, RW_HEAD)
    r, kk, v, w_0, w_1, kd_0, kd_1, b_0, b_1 = [heads(t) for t in prep]
    if s0s is None:
        z = jnp.zeros((B, RW_HEADS, RW_HEAD, RW_HEAD), jnp.float32)
        s0s = (z, z)
    ys, finals = rwkv_scan_both(r, kk, v, [w_0, w_1], [kd_0, kd_1], [b_0, b_1], s0s)
    return [y.reshape(B * L, RW_DIM) for y in ys], finals


def _segment_ones(n, seg):
    h = np.arange(n) // seg
    return jnp.asarray(h[:, None] == h[None, :], jnp.bfloat16)


def _mla_prep_kernel(x_ref, cos_ref, sin_ref, qnorm_ref, qup_ref, kvnorm_ref, wk_ref, wv_ref, place_ref, qn_ref, kn_ref,
                     ones_ref, q_out, k_out, v_out, ckv_out, *, rope, norm_kv):
    bf16 = jnp.bfloat16
    f32 = jnp.float32
    x = x_ref[...]
    qc = x[:, :Q_LORA]
    kvc = x[:, Q_LORA:Q_LORA + KV_LORA]
    kpe = x[:, Q_LORA + KV_LORA:]
    rms = lambda t, g: t * lax.rsqrt(jnp.mean(t * t, axis=-1, keepdims=True) + EPS) * g
    ones_seg = ones_ref[...]
    head_norm = lambda t, g: t * lax.rsqrt(_head_sums(t * t, ones_seg) * (1.0 / QK_HEAD) + EPS) * g
    q = jnp.dot(rms(qc, qnorm_ref[...]).astype(bf16), qup_ref[...], preferred_element_type=f32)
    ckv = rms(kvc, kvnorm_ref[...]) if norm_kv else kvc
    ckv_out[...] = ckv
    c16 = ckv.astype(bf16)
    kpe_hi, kpe_lo = _split_bf16(kpe)
    k = (jnp.dot(c16, wk_ref[...], preferred_element_type=f32)
         + jnp.dot(kpe_hi, place_ref[...], preferred_element_type=f32)
         + jnp.dot(kpe_lo, place_ref[...], preferred_element_type=f32))
    v_out[...] = jnp.dot(c16, wv_ref[...], preferred_element_type=f32)
    q = head_norm(q, qn_ref[...])
    k = head_norm(k, kn_ref[...])
    if rope:
        n = q.shape[1]
        even = lax.broadcasted_iota(jnp.int32, (1, n), 1) % 2 == 0
        swap = lambda t: jnp.where(even, pltpu.roll(t, n - 1, 1), pltpu.roll(t, 1, 1))
        cos = cos_ref[...]
        sin = sin_ref[...]
        q = q * cos + swap(q) * sin
        k = k * cos + swap(k) * sin
    q_out[...] = q
    k_out[...] = k


def mla_prep(x, L, rope, weights, norm_kv=True):
    q_norm, q_up, kv_norm, kv_up, qn, kn = weights
    T, C = x.shape
    tm = min(ROW_TILE, L)
    nq = MLA_HEADS * QK_HEAD
    kv4 = kv_up.reshape(KV_LORA, MLA_HEADS, QK_NOPE + V_HEAD)
    wk = jnp.pad(kv4[:, :, :QK_NOPE], ((0, 0), (0, 0), (0, QK_ROPE))).reshape(KV_LORA, nq).astype(jnp.bfloat16)
    wv = kv4[:, :, QK_NOPE:].reshape(KV_LORA, MLA_HEADS * V_HEAD).astype(jnp.bfloat16)
    slot = np.arange(nq) % QK_HEAD - QK_NOPE
    place = jnp.asarray(np.arange(QK_ROPE)[:, None] == slot[None, :], jnp.bfloat16)
    if rope is not None:
        cos, sin = rope
        pair = np.maximum(slot, 0) // 2
        sign = np.where(slot % 2 == 0, -1.0, 1.0).astype(np.float32)
        cos_f = jnp.where(slot >= 0, cos[:, pair], 1.0)
        sin_f = jnp.where(slot >= 0, sin[:, pair] * sign, 0.0)
    else:
        cos_f = sin_f = jnp.zeros((tm, nq), jnp.float32)
    tiles_per_seq = max(L // tm, 1)
    rope_blk = pl.BlockSpec((tm, nq), (lambda i: (i % tiles_per_seq, 0)) if rope is not None else (lambda i: (0, 0)))
    row2 = lambda a: a.reshape(1, -1)
    whole = lambda a: pl.BlockSpec(a.shape, lambda i: (0,) * a.ndim)
    consts = (row2(q_norm), q_up.astype(jnp.bfloat16), row2(kv_norm), wk, wv, place,
              row2(jnp.tile(qn, MLA_HEADS)), row2(jnp.tile(kn, MLA_HEADS)), _segment_ones(nq, QK_HEAD))
    out = lambda n: pl.BlockSpec((tm, n), lambda i: (i, 0))
    sd = lambda n: jax.ShapeDtypeStruct((T, n), jnp.float32)
    return pl.pallas_call(
        partial(_mla_prep_kernel, rope=rope is not None, norm_kv=norm_kv), grid=(T // tm,),
        in_specs=[pl.BlockSpec((tm, C), lambda i: (i, 0)), rope_blk, rope_blk] + [whole(a) for a in consts],
        out_specs=[out(nq), out(nq), out(MLA_HEADS * V_HEAD), out(KV_LORA)],
        out_shape=[sd(nq), sd(nq), sd(MLA_HEADS * V_HEAD), sd(KV_LORA)],
        compiler_params=pltpu.CompilerParams(dimension_semantics=("arbitrary",), vmem_limit_bytes=VMEM_LIMIT),
        name="mla_prep",
    )(x, cos_f, sin_f, *consts)


def mla_mixer(mla, B, L, ctx, weights):
    q, k, v, ckv = mla_prep(mla, L, None if ctx is None else ctx[2], weights)
    seq = lambda t: t.reshape(B, L, -1)
    kv_ctx = None
    if ctx is not None:
        c_ckv, c_kpe = ctx[0], ctx[1]
        Lc = c_ckv.shape[1]
        xc = jnp.concatenate([jnp.zeros((B, Lc, Q_LORA), jnp.float32), c_ckv, c_kpe], axis=-1).reshape(B * Lc, -1)
        _, kc, vc, _ = mla_prep(xc, Lc, None, weights, norm_kv=False)
        kv_ctx = (kc.reshape(B, Lc, -1), vc.reshape(B, Lc, -1))
    att = attend(seq(q), seq(k), seq(v), kv_ctx)
    return att.reshape(B * L, -1), ckv.reshape(B, L, -1), mla[:, Q_LORA + KV_LORA:].reshape(B, L, -1)


def _hyena_prep_kernel(proj_ref, prev_ref, next_ref, first_ref, last_ref, cw_ref, cb_ref, x0_ref, z_ref):
    D = x0_ref.shape[1]
    x, prev, nxt = _with_neighbours(proj_ref, prev_ref, next_ref, first_ref, last_ref, proj_ref.shape[1])
    u = prev * cw_ref[0:1, :] + x * cw_ref[1:2, :] + nxt * cw_ref[2:3, :] + cb_ref[...]
    x0_ref[...] = u[:, :D]
    z_ref[...] = u[:, D:2 * D] * u[:, 2 * D:]


def hyena_prep(proj, L, conv_w, conv_b, tm):
    T, C = proj.shape
    D = C // 3
    nb_specs, flags = _neighbour_specs(T, C, L, tm)
    out = pl.BlockSpec((tm, D), lambda i: (i, 0))
    return pl.pallas_call(
        _hyena_prep_kernel, grid=(T // tm,),
        in_specs=nb_specs + [pl.BlockSpec((3, C), lambda i: (0, 0)), pl.BlockSpec((1, C), lambda i: (0, 0))],
        out_specs=[out, out],
        out_shape=[jax.ShapeDtypeStruct((T, D), jnp.float32)] * 2,
        compiler_params=pltpu.CompilerParams(dimension_semantics=("arbitrary",), vmem_limit_bytes=VMEM_LIMIT),
        name="hyena_prep",
    )(proj, proj, proj, *flags, conv_w, conv_b[None])


def _hyena_out_kernel(x0_ref, conv_ref, z_ref, bias_ref, w_ref, x_ref, mod_ref, o_ref):
    a = x0_ref[...] * (conv_ref[...] + z_ref[...] * bias_ref[...])
    o_ref[...] = x_ref[...] + mod_ref[0] * jnp.dot(a.astype(jnp.bfloat16), w_ref[...],
                                                   preferred_element_type=jnp.float32)


def hyena_out_residual(x0, conv, zin, bias, w_out, x, gate_t, tm):
    T, D = x.shape
    row = pl.BlockSpec((tm, D), lambda i: (i, 0))
    return pl.pallas_call(
        _hyena_out_kernel, grid=(T // tm,),
        in_specs=[row, row, row, pl.BlockSpec((1, D), lambda i: (0, 0)), pl.BlockSpec((D, D), lambda i: (0, 0)),
                  row, pl.BlockSpec((1, 1, D), lambda i: (i, 0, 0))],
        out_specs=row,
        out_shape=jax.ShapeDtypeStruct((T, D), jnp.float32),
        compiler_params=pltpu.CompilerParams(dimension_semantics=("arbitrary",), vmem_limit_bytes=VMEM_LIMIT),
        name="hyena_out_residual",
    )(x0, conv, zin, bias[None], w_out.astype(jnp.bfloat16), x, gate_t)


def hyena_filters(L, w1, b1, w2, b2, w3, freq):
    f32 = jnp.float32
    u = jnp.arange(2 * L, dtype=jnp.int32)[:, None]
    t = jnp.where(u < L, u, 2 * L - u).astype(f32)
    t_unit = t / (L - 1)
    bands = jnp.linspace(1e-4, HY_BANDS - 1, HY_BANDS, dtype=f32)
    ang = 2.0 * math.pi * t * bands / L
    zpos = jnp.concatenate([t_unit, jnp.cos(ang), -jnp.sin(ang)], axis=-1)
    fr = freq.astype(f32)
    hid = jnp.sin(fr * (_mm(zpos, w1.astype(f32), split=True) + b1.astype(f32)))
    hid = jnp.sin(fr * (_mm(hid, w2.astype(f32), split=True) + b2.astype(f32)))
    filt = _mm(hid, w3.astype(f32), split=True)
    deltas = jnp.linspace(math.log(HY_TARGET) / HY_FAST, math.log(HY_TARGET) / HY_SLOW, D_MODEL, dtype=f32)
    window = jnp.exp(-t_unit * jnp.abs(deltas))
    circ = jnp.where(u < L, filt[:, :D_MODEL], filt[:, D_MODEL:]) * window
    circ = jnp.where(u == L, 0.0, circ)
    return circ / jnp.sum(jnp.abs(circ), axis=0, keepdims=True)


PEER_N = PEER_KEYS * PEER_KEYS
GATE_LANES = 2 * LANES
PEER_ROUTE_TOKENS = 256
PEER_EXPERT_TOKENS = 512
ROW_TILE = 512
SEQ_TILE = 256


def _top_vals(s, n, with_rank):
    vals = []
    rank = jnp.full(s.shape, float(n), jnp.float32) if with_rank else None
    for a in range(n):
        m = jnp.max(s, axis=0, keepdims=True)
        vals.append(m)
        hit = s == m
        if with_rank:
            rank = jnp.where(hit, float(a), rank)
        s = jnp.where(hit, -jnp.inf, s)
    return vals, rank


def _peer_route_kernel(x_ref, g_ref, sh_ref, sc_ref, wq_hi_ref, wq_lo_ref, k_hi_ref, k_lo_ref,
                       n1_ref, e1_ref, r2_ref, e2_ref, h_ref, s_scr):
    tb = x_ref.shape[0]
    half = PEER_DKEY // 2
    n_tiles = tb // LANES
    h_hi, h_lo = _split_bf16(_rms_mod(x_ref[...], g_ref[...], sh_ref[0], sc_ref[0]))
    h_ref[...] = h_hi
    q_hi, q_lo = _split_bf16(_dot3(h_hi, h_lo, wq_hi_ref[...], wq_lo_ref[...]))
    for hh in range(PEER_HEADS):
        for p in range(2):
            cols = slice((2 * hh + p) * half, (2 * hh + p + 1) * half)
            k_hi = k_hi_ref[hh, p]
            s_scr[hh, p] = (lax.dot_general(k_hi, q_hi[:, cols], _NT, preferred_element_type=jnp.float32)
                            + lax.dot_general(k_lo_ref[hh, p], q_hi[:, cols], _NT, preferred_element_type=jnp.float32)
                            + lax.dot_general(k_hi, q_lo[:, cols], _NT, preferred_element_type=jnp.float32))

    K = PEER_TOPK
    G = SUBLANES

    def tile(it, carry):
        hh = it // n_tiles
        ln = pl.ds(pl.multiple_of((it % n_tiles) * LANES, LANES), LANES)
        s1 = s_scr[hh, 0, :, ln]
        s2 = s_scr[hh, 1, :, ln]
        v1, _ = _top_vals(s1, K, False)
        v2l, r2 = _top_vals(s2, K, True)
        v2 = jnp.concatenate(v2l, axis=0)
        cand = [v1[0] + v2[:G], v1[0] + v2[G:]]
        cand += [v1[a] + v2[:G] for a in range(1, G)]
        cand += [jnp.concatenate(v1[G:], axis=0) + v2[0:1]]
        c = cand
        for k in range(K):
            m = c[0]
            for ci in c[1:]:
                m = jnp.maximum(m, ci)
            m = jnp.max(m, axis=0, keepdims=True)
            if k + 1 < K:
                c = [jnp.where(ci == m, -jnp.inf, ci) for ci in c]
        tau = m
        top = v1[0] + v2[0:1]
        keep = [ci >= tau for ci in cand]
        z = jnp.zeros_like(tau)
        for ci, ki in zip(cand, keep):
            z = z + jnp.sum(jnp.where(ki, jnp.exp(ci - top), 0.0), axis=0, keepdims=True)
        cnt = [jnp.sum(jnp.where(ki, 1.0, 0.0), axis=0, keepdims=True) for ki in keep[:G + 1]]
        tail = jnp.where(keep[G + 1], 1.0, 0.0)
        n_a = [cnt[0] + cnt[1]] + cnt[2:] + [tail[a:a + 1] for a in range(G)]
        n1 = jnp.zeros_like(s1)
        for a in range(K):
            n1 = jnp.where(s1 == v1[a], n_a[a], n1)
        n1_ref[hh, :, ln] = n1
        e1_ref[hh, :, ln] = jnp.exp(s1 - v1[0]) / z
        r2_ref[hh, :, ln] = r2.astype(jnp.bfloat16)
        e2_ref[hh, :, ln] = jnp.exp(s2 - v2[0:1]).astype(jnp.bfloat16)
        return carry

    lax.fori_loop(0, PEER_HEADS * n_tiles, tile, 0, unroll=4)


def peer_route(x, g, shift_t, scale_t, wq_hi, wq_lo, k_hi, k_lo, tb=PEER_ROUTE_TOKENS):
    T = x.shape[0]
    assert T % tb == 0 and tb % LANES == 0
    nh = PEER_HEADS
    blk = pl.BlockSpec((nh, PEER_KEYS, tb), lambda t: (0, 0, t))
    sd = lambda dt: jax.ShapeDtypeStruct((nh, PEER_KEYS, T), dt)
    whole = lambda a: pl.BlockSpec(a.shape, lambda t: (0,) * a.ndim)
    mod = pl.BlockSpec((1, 1, D_MODEL), lambda t: (t, 0, 0))
    return pl.pallas_call(
        _peer_route_kernel,
        grid=(T // tb,),
        in_specs=[pl.BlockSpec((tb, D_MODEL), lambda t: (t, 0)), whole(g), mod, mod,
                  whole(wq_hi), whole(wq_lo), whole(k_hi), whole(k_lo)],
        out_specs=[blk, blk, blk, blk, pl.BlockSpec((tb, D_MODEL), lambda t: (t, 0))],
        out_shape=[sd(jnp.float32), sd(jnp.float32), sd(jnp.bfloat16), sd(jnp.bfloat16),
                   jax.ShapeDtypeStruct((T, D_MODEL), jnp.bfloat16)],
        scratch_shapes=[pltpu.VMEM((nh, 2, PEER_KEYS, tb), jnp.float32)],
        compiler_params=pltpu.CompilerParams(dimension_semantics=("arbitrary",),
                                             vmem_limit_bytes=VMEM_LIMIT),
        name="peer_route",
    )(x, g, shift_t, scale_t, wq_hi, wq_lo, k_hi, k_lo)


def _gelu_tanh(x):
    hx = 0.5 * x
    return hx * jnp.tanh(x * (x * x * (0.7978845608028654 * 0.044715) + 0.7978845608028654)) + hx


def _peer_expert_kernel(h_ref, u_ref, vt_even_ref, vt_prev_ref, vt_last_ref, n1_ref, e1_ref, r2_ref, e2_ref,
                        x_ref, gate_ref, o_ref, acc_ref, a0_scr, a1_scr, w0_scr, w1_scr):
    c = pl.program_id(1)
    ec, tb = a0_scr.shape
    n_i = ec // PEER_KEYS
    bf16 = jnp.bfloat16
    f32 = jnp.float32

    @pl.when(c == 0)
    def _():
        acc_ref[...] = jnp.zeros_like(acc_ref)
        w1_scr[...] = jnp.zeros_like(w1_scr)

    def gate_times_act(a_scr, w_scr, chunk):
        igrp = pl.ds(pl.multiple_of(chunk * n_i, SUBLANES), SUBLANES)
        for lt in range(tb // GATE_LANES):
            ln = slice(lt * GATE_LANES, (lt + 1) * GATE_LANES)
            for ii in range(n_i):
                rows = slice(ii * PEER_KEYS, (ii + 1) * PEER_KEYS)
                g = jnp.zeros((PEER_KEYS, GATE_LANES), bf16)
                for hh in range(PEER_HEADS):
                    n1 = jnp.broadcast_to(n1_ref[hh, igrp, ln][ii:ii + 1], (PEER_KEYS, GATE_LANES)).astype(bf16)
                    e1 = jnp.broadcast_to(e1_ref[hh, igrp, ln][ii:ii + 1], (PEER_KEYS, GATE_LANES)).astype(bf16)
                    g = g + jnp.where(r2_ref[hh, :, ln] < n1, e2_ref[hh, :, ln] * e1, jnp.zeros((), bf16))
                w_scr[rows, ln] = g * _gelu_tanh(a_scr[rows, ln])

    h = h_ref[...]
    a0_scr[...] = lax.dot_general(u_ref[:ec], h, _NT, preferred_element_type=f32).astype(bf16)
    acc_ref[...] += jnp.dot(vt_prev_ref[...], w1_scr[...], preferred_element_type=f32)
    gate_times_act(a0_scr, w0_scr, 2 * c)
    a1_scr[...] = lax.dot_general(u_ref[ec:], h, _NT, preferred_element_type=f32).astype(bf16)
    acc_ref[...] += jnp.dot(vt_even_ref[...], w0_scr[...], preferred_element_type=f32)
    gate_times_act(a1_scr, w1_scr, 2 * c + 1)

    @pl.when(c == pl.num_programs(1) - 1)
    def _():
        y = acc_ref[...] + jnp.dot(vt_last_ref[...], w1_scr[...], preferred_element_type=f32)
        o_ref[...] = x_ref[...] + gate_ref[0] * y.T


def peer_experts(h_bf16, u_bf16, vt_bf16, n1, e1, r2, e2, x, gate_t, tb=PEER_EXPERT_TOKENS, ec=SUBLANES * PEER_KEYS):
    T = h_bf16.shape[0]
    assert T % tb == 0 and tb % GATE_LANES == 0 and PEER_N % (2 * ec) == 0
    n_steps = PEER_N // (2 * ec)
    rblk = pl.BlockSpec((PEER_HEADS, PEER_KEYS, tb), lambda t, c: (0, 0, t))
    vt_blk = lambda chunk_of: pl.BlockSpec((D_MODEL, ec), lambda t, c: (0, chunk_of(c)))
    return pl.pallas_call(
        _peer_expert_kernel,
        grid=(T // tb, n_steps),
        in_specs=[pl.BlockSpec((tb, D_MODEL), lambda t, c: (t, 0)),
                  pl.BlockSpec((2 * ec, D_MODEL), lambda t, c: (c, 0)),
                  vt_blk(lambda c: 2 * c),
                  vt_blk(lambda c: jnp.maximum(2 * c - 1, 0)),
                  vt_blk(lambda c: 2 * n_steps - 1),
                  rblk, rblk, rblk, rblk,
                  pl.BlockSpec((tb, D_MODEL), lambda t, c: (t, 0)),
                  pl.BlockSpec((1, 1, D_MODEL), lambda t, c: (t, 0, 0))],
        out_specs=pl.BlockSpec((tb, D_MODEL), lambda t, c: (t, 0)),
        out_shape=jax.ShapeDtypeStruct((T, D_MODEL), jnp.float32),
        scratch_shapes=[pltpu.VMEM((D_MODEL, tb), jnp.float32)] + [pltpu.VMEM((ec, tb), jnp.bfloat16)] * 4,
        compiler_params=pltpu.CompilerParams(dimension_semantics=("arbitrary", "arbitrary"),
                                             vmem_limit_bytes=VMEM_LIMIT),
        name="peer_experts",
    )(h_bf16, u_bf16, vt_bf16, vt_bf16, vt_bf16, n1, e1, r2, e2, x, gate_t)


def _cast_table_kernel(x_ref, o_ref, *, transpose):
    x = x_ref[0]
    o_ref[...] = (x.T if transpose else x).astype(o_ref.dtype)


def cast_table(tabs, li, transpose, rows=1024):
    _, N, D = tabs.shape
    out_spec = pl.BlockSpec((D, rows), lambda i: (0, i)) if transpose else pl.BlockSpec((rows, D), lambda i: (i, 0))
    return pl.pallas_call(
        partial(_cast_table_kernel, transpose=transpose), grid=(N // rows,),
        in_specs=[pl.BlockSpec((1, rows, D), lambda i: (li, i, 0))],
        out_specs=out_spec,
        out_shape=jax.ShapeDtypeStruct((D, N) if transpose else (N, D), jnp.bfloat16),
        compiler_params=pltpu.CompilerParams(dimension_semantics=("arbitrary",), vmem_limit_bytes=VMEM_LIMIT),
        name="cast_table",
    )(tabs)


def peer_weights(w_q, sub_keys, u_tabs, v_tabs, li):
    return (_split_bf16(w_q[li]) + _split_bf16(sub_keys[li])
            + (cast_table(u_tabs, li, transpose=False), cast_table(v_tabs, li, transpose=True)))


def peer_block(x, g, mods_route, mods_expert, weights):
    wq_hi, wq_lo, k_hi, k_lo, u_bf16, vt_bf16 = weights
    n1, e1, r2, e2, h_bf16 = peer_route(x, g[None], mods_route[0], mods_route[1], wq_hi, wq_lo, k_hi, k_lo)
    return peer_experts(h_bf16, u_bf16, vt_bf16, n1, e1, r2, e2, x, mods_expert)


def kernel(x_prompt, x_sample, state_rwkv_fwd, state_rwkv_bwd, cache_mla_ckv, cache_mla_kpe, c, c_ctx,
           norm_g, w_mod, b_mod, ab_w_in, rw_mu, rw_w0, rw_w2, rw_a0, rw_a2, rw_g2, rw_k_k, rw_k_a, rw_r_k,
           rw_lnx_g, rw_lnx_b, mla_q_norm, mla_q_up, mla_kv_norm, mla_kv_up, mla_qn, mla_kn, ab_w_out,
           hy_w_in, hy_b_in, hy_conv_w, hy_conv_b, hy_f_w1, hy_f_b1, hy_f_w2, hy_f_b2, hy_f_w3, hy_f_freq,
           hy_bias, hy_w_out, peer_w_q, peer_keys, peer_u, peer_v):
    rope = axial_rope(x_sample.shape[1])
    D = D_MODEL
    groups = [dict(x=x_prompt.reshape(-1, D), B=x_prompt.shape[0], L=x_prompt.shape[1]),
              dict(x=x_sample.reshape(-1, D), B=x_sample.shape[0], L=x_sample.shape[1])]
    st_f, st_b, st_ckv, st_kpe = [], [], [], []
    for li in range(DEPTH):
        j = li // 2
        groups[0]["mod"] = (_mm(jax.nn.silu(c_ctx)[None], w_mod[li]) + b_mod[li]).reshape(1, 6, D)
        groups[1]["mod"] = (_mm(jax.nn.silu(c), w_mod[li]) + b_mod[li]).reshape(-1, 6, D)
        peer_w = peer_weights(peer_w_q, peer_keys, peer_u, peer_v, li)
        for gi, g in enumerate(groups):
            x, B, L = g["x"], g["B"], g["L"]
            mod = lambda i, tm: tile_mod(g["mod"][:, i], B * L, tm)
            if li % 2 == 0:
                proj = norm_mod_matmul(x, norm_g[li, 0], mod(0, ROW_TILE), mod(1, ROW_TILE), ab_w_in[j],
                                       jnp.zeros((ab_w_in.shape[-1],), jnp.float32), ROW_TILE)
                prep = rwkv_prep(proj, L, rw_mu[j], rw_w0[j], rw_w2[j], rw_a0[j], rw_a2[j], rw_g2[j],
                                 rw_k_k[j], rw_k_a[j], rw_r_k[j], SEQ_TILE)
                s0s = None if gi == 0 else (state_rwkv_fwd[:, j], state_rwkv_bwd[:, j])
                ys, (sf, sb) = rwkv_scans(prep[:9], B, L, s0s)
                ctx = None if gi == 0 else (cache_mla_ckv[:, j], cache_mla_kpe[:, j], rope)
                att, ckv, kpe = mla_mixer(proj[:, RW_IN:], B, L, ctx,
                                          (mla_q_norm[j], mla_q_up[j], mla_kv_norm[j], mla_kv_up[j], mla_qn[j], mla_kn[j]))
                if gi == 0:
                    st_f.append(sf)
                    st_b.append(sb)
                    st_ckv.append(ckv)
                    st_kpe.append(kpe)
                x = rwkv_out_residual(ys[0], ys[1], prep[10], prep[9], att, rw_lnx_g[j], rw_lnx_b[j],
                                      ab_w_out[j], x, mod(2, ROW_TILE), ROW_TILE)
            else:
                proj = norm_mod_matmul(x, norm_g[li, 0], mod(0, ROW_TILE), mod(1, ROW_TILE), hy_w_in[j], hy_b_in[j],
                                       ROW_TILE)
                x0, zin = hyena_prep(proj, L, hy_conv_w[j], hy_conv_b[j], SEQ_TILE)
                circ = hyena_filters(L, hy_f_w1[j], hy_f_b1[j], hy_f_w2[j], hy_f_b2[j], hy_f_w3[j], hy_f_freq[j])
                conv = hyena_longconv(zin.reshape(B, L, D), circ).reshape(B * L, D)
                x = hyena_out_residual(x0, conv, zin, hy_bias[j], hy_w_out[j], x, mod(2, ROW_TILE), ROW_TILE)
            g["x"] = peer_block(x, norm_g[li, 1], (mod(3, PEER_ROUTE_TOKENS), mod(4, PEER_ROUTE_TOKENS)),
                                mod(5, PEER_EXPERT_TOKENS), peer_w)
    xp = groups[0]["x"].reshape(x_prompt.shape)
    xs = groups[1]["x"].reshape(x_sample.shape)
    new_state_rwkv_fwd = jnp.stack(st_f, axis=1).astype(x_prompt.dtype)
    new_state_rwkv_bwd = jnp.stack(st_b, axis=1).astype(x_prompt.dtype)
    new_cache_mla_ckv = jnp.stack(st_ckv, axis=1)
    new_cache_mla_kpe = jnp.stack(st_kpe, axis=1)
    return (xp, xs, new_state_rwkv_fwd, new_state_rwkv_bwd, new_cache_mla_ckv, new_cache_mla_kpe)
```

```python
import math
from functools import lru_cache, partial

import jax
import jax.numpy as jnp
import numpy as np
from jax import lax
from jax.experimental import pallas as pl
from jax.experimental.pallas import tpu as pltpu

D_MODEL = 1024
DEPTH = 2
GRID_W = 64
EPS = 1e-6
RW_HEADS = 8
RW_HEAD = 64
RW_DIM = RW_HEADS * RW_HEAD
W_LORA = 64
A_LORA = 64
G_LORA = 128
LNX_EPS = 64e-5
RW_IN = 3 * RW_DIM + W_LORA + A_LORA + G_LORA
RW_SPLITS = (RW_DIM, 2 * RW_DIM, 3 * RW_DIM, 3 * RW_DIM + W_LORA, 3 * RW_DIM + W_LORA + A_LORA)
MLA_HEADS = 4
QK_NOPE = 128
QK_ROPE = 64
QK_HEAD = QK_NOPE + QK_ROPE
V_HEAD = 128
Q_LORA = 256
KV_LORA = 128
MLA_DIM = MLA_HEADS * V_HEAD
ROPE_THETA = 10000.0
HY_BANDS = 16
HY_TARGET = 1e-2
HY_FAST = 0.3
HY_SLOW = 1.5
PEER_KEYS = 128
PEER_HEADS = 8
PEER_DKEY = 256
PEER_TOPK = 16

LANES = 128
SUBLANES = 8
V7X_VMEM_BYTES = 64 * 1024 * 1024
VMEM_LIMIT = V7X_VMEM_BYTES * 7 // 8
_NT = (((1,), (1,)), ((), ()))


def _split_bf16(x):
    hi = x.astype(jnp.bfloat16)
    lo = (x - hi.astype(jnp.float32)).astype(jnp.bfloat16)
    return hi, lo


def _dot3(a_hi, a_lo, b_hi, b_lo):
    f32 = jnp.float32
    return (jnp.dot(a_hi, b_hi, preferred_element_type=f32) + jnp.dot(a_lo, b_hi, preferred_element_type=f32)
            + jnp.dot(a_hi, b_lo, preferred_element_type=f32))


def _mm_kernel(a_ref, b_ref, o_ref, *, split):
    if split:
        a_hi, a_lo = _split_bf16(a_ref[...])
        b_hi, b_lo = _split_bf16(b_ref[...])
        o_ref[...] = _dot3(a_hi, a_lo, b_hi, b_lo)
    else:
        o_ref[...] = jnp.dot(a_ref[...].astype(jnp.bfloat16), b_ref[...].astype(jnp.bfloat16),
                             preferred_element_type=jnp.float32)


def _mm(a, b, tm=512, tn=512, split=False):
    lead = a.shape[:-1]
    K = a.shape[-1]
    N = b.shape[-1]
    a2 = a.reshape(-1, K)
    if K % LANES:
        kp = -K % LANES
        a2 = jnp.pad(a2, ((0, 0), (0, kp)))
        b = jnp.pad(b, ((0, kp), (0, 0)))
        K += kp
    M = a2.shape[0]
    tm = min(tm, M)
    tn = min(tn, N)
    if N % tn:
        tn = N
    assert M % tm == 0 and N % tn == 0
    out = pl.pallas_call(
        partial(_mm_kernel, split=split),
        grid=(M // tm, N // tn),
        in_specs=[pl.BlockSpec((tm, K), lambda i, j: (i, 0)),
                  pl.BlockSpec((K, tn), lambda i, j: (0, j))],
        out_specs=pl.BlockSpec((tm, tn), lambda i, j: (i, j)),
        out_shape=jax.ShapeDtypeStruct((M, N), jnp.float32),
        name="matmul",
    )(a2, b)
    return out.reshape(*lead, N)


def axial_rope(L):
    rows = L // GRID_W
    row = jnp.repeat(jnp.arange(rows, dtype=jnp.float32), GRID_W)
    col = jnp.tile(jnp.arange(GRID_W, dtype=jnp.float32), rows)
    n_freq = QK_ROPE // 4
    inv = ROPE_THETA ** (-jnp.arange(n_freq, dtype=jnp.float32) / n_freq)
    ang = jnp.concatenate([row[:, None] * inv, col[:, None] * inv], axis=-1)
    return jnp.cos(ang), jnp.sin(ang)


def _mla_attn_kernel(q_ref, k_ref, v_ref, *rest, with_ctx):
    if with_ctx:
        kc_ref, vc_ref, o_ref = rest
    else:
        (o_ref,) = rest
    bf16 = jnp.bfloat16
    f32 = jnp.float32
    scale = QK_HEAD ** -0.5
    for h in range(MLA_HEADS):
        qs = slice(h * QK_HEAD, (h + 1) * QK_HEAD)
        vs = slice(h * V_HEAD, (h + 1) * V_HEAD)
        q = q_ref[0, :, qs].astype(bf16)
        s = lax.dot_general(q, k_ref[0, :, qs].astype(bf16), _NT, preferred_element_type=f32) * scale
        m = jnp.max(s, axis=-1, keepdims=True)
        if with_ctx:
            sc = lax.dot_general(q, kc_ref[0, :, qs].astype(bf16), _NT, preferred_element_type=f32) * scale
            m = jnp.maximum(m, jnp.max(sc, axis=-1, keepdims=True))
            pc = jnp.exp(sc - m)
        p = jnp.exp(s - m)
        l = jnp.sum(p, axis=-1, keepdims=True)
        if with_ctx:
            l = l + jnp.sum(pc, axis=-1, keepdims=True)
        o = jnp.dot(p.astype(bf16), v_ref[0, :, vs].astype(bf16), preferred_element_type=f32)
        if with_ctx:
            o = o + jnp.dot(pc.astype(bf16), vc_ref[0, :, vs].astype(bf16), preferred_element_type=f32)
        o_ref[0, :, vs] = o / l


def attend(q, k, v, ctx=None, tq=256):
    B, Lq, _ = q.shape
    tq = min(tq, Lq)
    kv = [k, v] + (list(ctx) if ctx is not None else [])
    full = lambda a: pl.BlockSpec((1,) + a.shape[1:], lambda b, i: (b, 0, 0))
    return pl.pallas_call(
        partial(_mla_attn_kernel, with_ctx=ctx is not None), grid=(B, Lq // tq),
        in_specs=[pl.BlockSpec((1, tq, MLA_HEADS * QK_HEAD), lambda b, i: (b, i, 0))] + [full(a) for a in kv],
        out_specs=pl.BlockSpec((1, tq, MLA_HEADS * V_HEAD), lambda b, i: (b, i, 0)),
        out_shape=jax.ShapeDtypeStruct((B, Lq, MLA_HEADS * V_HEAD), jnp.float32),
        compiler_params=pltpu.CompilerParams(dimension_semantics=("arbitrary", "arbitrary"),
                                             vmem_limit_bytes=VMEM_LIMIT),
        name="mla_attend",
    )(q, *kv)


def _first_row_block(fn):
    pl.when(pl.program_id(2) == 0)(fn)


def _dft_pair_kernel(c_tab, s_tab, x_ref, oc_ref, os_ref, x16):
    def _():
        x16[...] = x_ref[0].astype(jnp.bfloat16)
    _first_row_block(_)
    oc_ref[0] = jnp.dot(c_tab[...], x16[...], preferred_element_type=jnp.float32)
    os_ref[0] = jnp.dot(s_tab[...], x16[...], preferred_element_type=jnp.float32)


def _dft_spectral_kernel(c_tab, s_tab, x_ref, hr_ref, hi_ref, yr_ref, yi_ref, x16):
    def _():
        x16[...] = x_ref[0].astype(jnp.bfloat16)
    _first_row_block(_)
    rb, L = c_tab.shape
    zc = jnp.dot(c_tab[...], x16[...], preferred_element_type=jnp.float32)
    zs = jnp.dot(s_tab[...], x16[...], preferred_element_type=jnp.float32)
    hr = hr_ref[...]
    hi = hi_ref[...]
    f = pl.program_id(2) * rb + lax.broadcasted_iota(jnp.int32, zc.shape, 0)
    yr_ref[0] = jnp.where(f == 0, zc * hr * (0.5 / L), (zc * hr - zs * hi) * (1.0 / L))
    yi_ref[0] = jnp.where(f == 0, zs * hi * (0.5 / L), (zc * hi + zs * hr) * (1.0 / L))


def _dft_sum_kernel(c_tab, s_tab, x_ref, y_ref, o_ref, x16, y16):
    def _():
        x16[...] = x_ref[0].astype(jnp.bfloat16)
        y16[...] = y_ref[0].astype(jnp.bfloat16)
    _first_row_block(_)
    o_ref[0] = (jnp.dot(c_tab[...], x16[...], preferred_element_type=jnp.float32)
                + jnp.dot(s_tab[...], y16[...], preferred_element_type=jnp.float32))


@lru_cache(maxsize=None)
def dft_tables(L):
    f = np.arange(L, dtype=np.int64)
    ang = ((f[:, None] * f[None, :]) % (2 * L)).astype(np.float64) * (math.pi / L)
    c = np.cos(ang).astype(np.float32)
    s = -np.sin(ang)
    alt = np.where(f % 2 == 0, 1.0, -1.0)
    s_ana = np.where(f[:, None] == 0, alt[None, :], s).astype(np.float32)
    s_syn = np.where(f[None, :] == 0, alt[:, None], s).astype(np.float32)
    return (c, s_ana), (c, s_syn)


def _dft_call(body, tabs, xs, n_out, name, row_inputs=(), rb=1024, nb=512):
    B, L, N = xs[0].shape
    rb = min(rb, L)
    nb = min(nb, N)
    tab = pl.BlockSpec((rb, L), lambda b, n, i: (i, 0))
    xin = pl.BlockSpec((1, L, nb), lambda b, n, i: (b, 0, n))
    rin = pl.BlockSpec((rb, nb), lambda b, n, i: (i, n))
    out = pl.BlockSpec((1, rb, nb), lambda b, n, i: (b, i, n))
    sd = jax.ShapeDtypeStruct((B, L, N), jnp.float32)
    return pl.pallas_call(
        body, grid=(B, N // nb, L // rb),
        in_specs=[tab] * len(tabs) + [xin] * len(xs) + [rin] * len(row_inputs),
        out_specs=[out] * n_out if n_out > 1 else out,
        out_shape=[sd] * n_out if n_out > 1 else sd,
        scratch_shapes=[pltpu.VMEM((L, nb), jnp.bfloat16)] * len(xs),
        compiler_params=pltpu.CompilerParams(dimension_semantics=("arbitrary",) * 3,
                                             vmem_limit_bytes=VMEM_LIMIT),
        name=name,
    )(*tabs, *xs, *row_inputs)


def hyena_longconv(zin, circ):
    B, L, D = zin.shape
    tabs, tabs_syn = [tuple(jnp.asarray(t).astype(jnp.bfloat16) for t in pair) for pair in dft_tables(L)]
    alt = jnp.where(jnp.arange(L) % 2 == 0, 1.0, -1.0).astype(jnp.float32)[:, None]
    fc, fs = _dft_call(_dft_pair_kernel, tabs, [jnp.concatenate([circ[:L], circ[L:]], axis=-1)[None]], 2, "dft_filter")
    hr = fc[0, :, :D] + alt * fc[0, :, D:]
    hi = fs[0, :, :D] + alt * fs[0, :, D:]
    yr, yi = _dft_call(_dft_spectral_kernel, tabs, [zin], 2, "dft_analysis", row_inputs=(hr, hi))
    return _dft_call(_dft_sum_kernel, tabs_syn, [yr, yi], 1, "dft_synthesis")


SCAN_ACCS = 4


def _rwkv_scan_kernel(rf_ref, kkf_ref, vf_ref, wf_ref, kdf_ref, bf_ref, rb_ref, kkb_ref, vb_ref, wb_ref, kdb_ref, bb_ref,
                      s0f_ref, s0b_ref, yf_ref, yb_ref, sff_ref, sfb_ref, sf_scr, sb_scr):
    c = pl.program_id(0)
    tc, nk, _ = rf_ref.shape
    nv = vf_ref.shape[1]
    groups = RW_HEAD // nk

    @pl.when(c == 0)
    def _():
        sf_scr[...] = s0f_ref[...]
        sb_scr[...] = s0b_ref[...]

    def row(ref, tt, k):
        return jnp.broadcast_to(ref[tt, pl.ds(k, 1), :], (nv, LANES))

    def all_groups(p):
        tot = p
        for q in range(1, groups):
            tot = tot + pltpu.roll(p, q * (LANES // groups), 1)
        return tot

    def one_step(tt, r_ref, kk_ref, v_ref, w_ref, kd_ref, b_ref, s_scr, y_ref):
        parts = [None] * SCAN_ACCS
        for k in range(nk):
            term = s_scr[k] * row(kk_ref, tt, k)
            parts[k % SCAN_ACCS] = term if parts[k % SCAN_ACCS] is None else parts[k % SCAN_ACCS] + term
        sa = -all_groups((parts[0] + parts[1]) + (parts[2] + parts[3]))
        vv = v_ref[tt]
        parts = [None] * SCAN_ACCS
        for k in range(nk):
            s_new = s_scr[k] * row(w_ref, tt, k) + sa * row(b_ref, tt, k) + vv * row(kd_ref, tt, k)
            s_scr[k] = s_new
            term = s_new * row(r_ref, tt, k)
            parts[k % SCAN_ACCS] = term if parts[k % SCAN_ACCS] is None else parts[k % SCAN_ACCS] + term
        y_ref[tt] = all_groups((parts[0] + parts[1]) + (parts[2] + parts[3]))

    def step(t, carry):
        one_step(t, rf_ref, kkf_ref, vf_ref, wf_ref, kdf_ref, bf_ref, sf_scr, yf_ref)
        one_step(tc - 1 - t, rb_ref, kkb_ref, vb_ref, wb_ref, kdb_ref, bb_ref, sb_scr, yb_ref)
        return carry

    lax.fori_loop(0, tc, step, 0, unroll=4 if groups > 1 else 2)

    @pl.when(c == pl.num_programs(0) - 1)
    def _():
        sff_ref[...] = sf_scr[...]
        sfb_ref[...] = sb_scr[...]


def rwkv_scan(r, kk, v, wkb_fwd, wkb_bwd, s0_fwd, s0_bwd, tc=32):
    L, nk, _ = r.shape
    nv = v.shape[1]
    assert L % tc == 0 and RW_HEAD % nk == 0
    nc = L // tc
    fwd = lambda rows: pl.BlockSpec((tc, rows, LANES), lambda c: (c, 0, 0))
    bwd = lambda rows: pl.BlockSpec((tc, rows, LANES), lambda c: (nc - 1 - c, 0, 0))
    state = pl.BlockSpec((nk, nv, LANES), lambda c: (0, 0, 0))
    y_sd = jax.ShapeDtypeStruct((L, nv, LANES), jnp.float32)
    s_sd = jax.ShapeDtypeStruct((nk, nv, LANES), jnp.float32)
    return pl.pallas_call(
        _rwkv_scan_kernel,
        grid=(nc,),
        in_specs=[fwd(nk), fwd(nk), fwd(nv), fwd(nk), fwd(nk), fwd(nk),
                  bwd(nk), bwd(nk), bwd(nv), bwd(nk), bwd(nk), bwd(nk), state, state],
        out_specs=[fwd(nv), bwd(nv), state, state],
        out_shape=[y_sd, y_sd, s_sd, s_sd],
        scratch_shapes=[pltpu.VMEM((nk, nv, LANES), jnp.float32)] * 2,
        compiler_params=pltpu.CompilerParams(dimension_semantics=("arbitrary",), vmem_limit_bytes=VMEM_LIMIT),
        name="rwkv_scan",
    )(r, kk, v, *wkb_fwd, r, kk, v, *wkb_bwd, s0_fwd, s0_bwd)


def k_to_lanes(x, ksplit):
    B, L, H, N = x.shape
    nk = N // ksplit
    return x.reshape(B, L, H, ksplit, nk).transpose(1, 4, 3, 0, 2).reshape(L, nk, ksplit * B * H)


def v_to_lanes(x, ksplit):
    B, L, H, N = x.shape
    return jnp.tile(x.transpose(1, 3, 0, 2).reshape(L, N, B * H), (1, 1, ksplit))


def v_from_lanes(y, B, H):
    L, N, _ = y.shape
    return y[:, :, :B * H].reshape(L, N, B, H).transpose(2, 0, 3, 1)


def state_to_lanes(s, ksplit):
    B, H, N, K = s.shape
    nk = K // ksplit
    return s.reshape(B, H, N, ksplit, nk).transpose(4, 2, 3, 0, 1).reshape(nk, N, ksplit * B * H)


def state_from_lanes(s, B, H, ksplit):
    nk, N, _ = s.shape
    return s.reshape(nk, N, ksplit, B, H).transpose(3, 4, 1, 2, 0).reshape(B, H, N, ksplit * nk)


def rwkv_scan_both(rh, kk, vh, decay2, kd2, b2, s0_2, tc=32):
    B, L, H, N = rh.shape
    ksplit = LANES // (B * H)
    assert B * H * ksplit == LANES
    wkb = [[k_to_lanes(t[d], ksplit) for t in (decay2, kd2, b2)] for d in range(2)]
    y_f, y_b, sf_f, sf_b = rwkv_scan(k_to_lanes(rh, ksplit), k_to_lanes(kk, ksplit), v_to_lanes(vh, ksplit),
                                     wkb[0], wkb[1], state_to_lanes(s0_2[0], ksplit), state_to_lanes(s0_2[1], ksplit),
                                     tc=tc)
    return ([v_from_lanes(y_f, B, H), v_from_lanes(y_b, B, H)],
            [state_from_lanes(sf_f, B, H, ksplit), state_from_lanes(sf_b, B, H, ksplit)])


def _rms_mod(x, g, shift, scale):
    y = x * lax.rsqrt(jnp.mean(x * x, axis=-1, keepdims=True) + EPS)
    return (y * g) * (1.0 + scale) + shift


def tile_mod(m, n_rows, tm):
    return jnp.repeat(m, n_rows // m.shape[0] // tm, axis=0)[:, None, :]


def _norm_mod_matmul_kernel(x_ref, g_ref, sh_ref, sc_ref, w_ref, b_ref, o_ref):
    h = _rms_mod(x_ref[...], g_ref[...], sh_ref[0], sc_ref[0])
    o_ref[...] = jnp.dot(h.astype(jnp.bfloat16), w_ref[...], preferred_element_type=jnp.float32) + b_ref[...]


def norm_mod_matmul(x, g, shift_t, scale_t, w, b, tm):
    T, D = x.shape
    N = w.shape[1]
    assert T % tm == 0 and shift_t.shape[0] == T // tm
    mod = pl.BlockSpec((1, 1, D), lambda i: (i, 0, 0))
    return pl.pallas_call(
        _norm_mod_matmul_kernel, grid=(T // tm,),
        in_specs=[pl.BlockSpec((tm, D), lambda i: (i, 0)), pl.BlockSpec((1, D), lambda i: (0, 0)), mod, mod,
                  pl.BlockSpec((D, N), lambda i: (0, 0)), pl.BlockSpec((1, N), lambda i: (0, 0))],
        out_specs=pl.BlockSpec((tm, N), lambda i: (i, 0)),
        out_shape=jax.ShapeDtypeStruct((T, N), jnp.float32),
        compiler_params=pltpu.CompilerParams(dimension_semantics=("arbitrary",), vmem_limit_bytes=VMEM_LIMIT),
        name="norm_mod_matmul",
    )(x, g[None], shift_t, scale_t, w.astype(jnp.bfloat16), b[None])


def _head_sums(x, ones_bd):
    hi, lo = _split_bf16(x)
    return (jnp.dot(hi, ones_bd, preferred_element_type=jnp.float32)
            + jnp.dot(lo, ones_bd, preferred_element_type=jnp.float32))


def _with_neighbours(x_ref, prev_ref, next_ref, first_ref, last_ref, ncols):
    tm = x_ref.shape[0]
    x = x_ref[:, :ncols]
    row = lax.broadcasted_iota(jnp.int32, (tm, 1), 0)
    prev_row = prev_ref[SUBLANES - 1:SUBLANES, :ncols] * (1.0 - first_ref[0, :, :1])
    next_row = next_ref[0:1, :ncols] * (1.0 - last_ref[0, :, :1])
    prev = jnp.where(row == 0, prev_row, pltpu.roll(x, 1, 0))
    nxt = jnp.where(row == tm - 1, next_row, pltpu.roll(x, tm - 1, 0))
    return x, prev, nxt


def _neighbour_specs(T, C, L, tm):
    assert L % tm == 0 and T % L == 0 and tm % SUBLANES == 0
    n_tiles = T // tm
    hb = tm // SUBLANES
    start = np.arange(n_tiles) * tm % L
    first = np.broadcast_to((start == 0).astype(np.float32)[:, None, None], (n_tiles, 1, LANES))
    last = np.broadcast_to((start + tm == L).astype(np.float32)[:, None, None], (n_tiles, 1, LANES))
    flag = pl.BlockSpec((1, 1, LANES), lambda i: (i, 0, 0))
    specs = [pl.BlockSpec((tm, C), lambda i: (i, 0)),
             pl.BlockSpec((SUBLANES, C), lambda i: (jnp.maximum(i * hb - 1, 0), 0)),
             pl.BlockSpec((SUBLANES, C), lambda i: (jnp.minimum((i + 1) * hb, T // SUBLANES - 1), 0)),
             flag, flag]
    return specs, (jnp.asarray(first), jnp.asarray(last))


def _rwkv_prep_kernel(proj_ref, prev_ref, next_ref, first_ref, last_ref, mu_ref, w0_ref, w2_ref, a0_ref, a2_ref,
                      g2_ref, kk_ref, ka_ref, rk_ref, ones_ref,
                      r_out, kk_out, v_out, w_out0, w_out1, kd_out0, kd_out1, b_out0, b_out1, gate_out, bonus_out):
    bf16 = jnp.bfloat16
    f32 = jnp.float32
    x, prev, nxt = _with_neighbours(proj_ref, prev_ref, next_ref, first_ref, last_ref, RW_IN)
    rw = x + mu_ref[...] * (0.5 * (prev + nxt) - x)
    r = rw[:, RW_SPLITS[0] - RW_DIM:RW_SPLITS[0]]
    k = rw[:, RW_SPLITS[0]:RW_SPLITS[1]]
    v = rw[:, RW_SPLITS[1]:RW_SPLITS[2]]
    wd = rw[:, RW_SPLITS[2]:RW_SPLITS[3]]
    ad = rw[:, RW_SPLITS[3]:RW_SPLITS[4]]
    gd = rw[:, RW_SPLITS[4]:]
    ones_bd = ones_ref[...]
    kk = k * kk_ref[...]
    kk = kk / jnp.maximum(jnp.sqrt(_head_sums(kk * kk, ones_bd)), 1e-12)
    tanh_wd = jnp.tanh(wd).astype(bf16)
    ad16 = ad.astype(bf16)
    rk = r * rk_ref[...]
    bonus = jnp.zeros_like(r)
    for d, (w_o, kd_o, b_o) in enumerate(((w_out0, kd_out0, b_out0), (w_out1, kd_out1, b_out1))):
        w_raw = w0_ref[d:d + 1, :] + jnp.dot(tanh_wd, w2_ref[d], preferred_element_type=f32)
        w_o[...] = jnp.exp(-jnp.exp(-jax.nn.softplus(-w_raw) - 0.5))
        lr = jax.nn.sigmoid(a0_ref[d:d + 1, :] + jnp.dot(ad16, a2_ref[d], preferred_element_type=f32))
        kd = k * (1.0 + (lr - 1.0) * ka_ref[...])
        kd_o[...] = kd
        b_o[...] = kk * lr
        bonus = bonus + _head_sums(rk * kd, ones_bd)
    r_out[...] = r
    kk_out[...] = kk
    v_out[...] = v
    gate_out[...] = jnp.dot(jax.nn.sigmoid(gd).astype(bf16), g2_ref[...], preferred_element_type=f32)
    bonus_out[...] = bonus * v


def rwkv_prep(proj, L, mu, w0, w2, a0, a2, g2, k_k, k_a, r_k, tm):
    T, C = proj.shape
    nb_specs, flags = _neighbour_specs(T, C, L, tm)
    row2 = lambda a: a.reshape(1, -1)
    whole = lambda a: pl.BlockSpec(a.shape, lambda i: (0,) * a.ndim)
    out = pl.BlockSpec((tm, RW_DIM), lambda i: (i, 0))
    consts = (row2(mu), w0, w2.astype(jnp.bfloat16), a0, a2.astype(jnp.bfloat16), g2.astype(jnp.bfloat16),
              row2(k_k), row2(k_a), row2(r_k), _segment_ones(RW_DIM, RW_HEAD))
    return pl.pallas_call(
        _rwkv_prep_kernel, grid=(T // tm,),
        in_specs=nb_specs + [whole(a) for a in consts],
        out_specs=[out] * 11,
        out_shape=[jax.ShapeDtypeStruct((T, RW_DIM), jnp.float32)] * 11,
        compiler_params=pltpu.CompilerParams(dimension_semantics=("arbitrary",), vmem_limit_bytes=VMEM_LIMIT),
        name="rwkv_prep",
    )(proj, proj, proj, *flags, *consts)


def _rwkv_out_kernel(y0_ref, y1_ref, bonus_ref, gate_ref, mla_ref, lng_ref, lnb_ref, ones_ref, w_ref, x_ref, mod_ref,
                     o_ref):
    bf16 = jnp.bfloat16
    f32 = jnp.float32
    ones_bd = ones_ref[...]
    y = y0_ref[...] + y1_ref[...]
    dlt = y - _head_sums(y, ones_bd) * (1.0 / RW_HEAD)
    var = _head_sums(dlt * dlt, ones_bd) * (1.0 / RW_HEAD)
    yn = dlt * lax.rsqrt(var + LNX_EPS) * lng_ref[...] + lnb_ref[...]
    rw_out = (yn + bonus_ref[...]) * gate_ref[...]
    upd = (jnp.dot(rw_out.astype(bf16), w_ref[:RW_DIM], preferred_element_type=f32)
           + jnp.dot(mla_ref[...].astype(bf16), w_ref[RW_DIM:], preferred_element_type=f32))
    o_ref[...] = x_ref[...] + mod_ref[0] * upd


def rwkv_out_residual(y0, y1, bonus, gate, mla_out, lnx_g, lnx_b, w_out, x, gate_t, tm):
    T, D = x.shape
    row = lambda n: pl.BlockSpec((tm, n), lambda i: (i, 0))
    whole = lambda a: pl.BlockSpec(a.shape, lambda i: (0,) * a.ndim)
    consts = (lnx_g.reshape(1, -1), lnx_b.reshape(1, -1), _segment_ones(RW_DIM, RW_HEAD), w_out.astype(jnp.bfloat16))
    return pl.pallas_call(
        _rwkv_out_kernel, grid=(T // tm,),
        in_specs=[row(RW_DIM)] * 4 + [row(MLA_DIM)] + [whole(a) for a in consts]
                 + [row(D), pl.BlockSpec((1, 1, D), lambda i: (i, 0, 0))],
        out_specs=row(D),
        out_shape=jax.ShapeDtypeStruct((T, D), jnp.float32),
        compiler_params=pltpu.CompilerParams(dimension_semantics=("arbitrary",), vmem_limit_bytes=VMEM_LIMIT),
        name="rwkv_out_residual",
    )(y0, y1, bonus, gate, mla_out, *consts, x, gate_t)


def rwkv_scans(prep, B, L, s0s):
    heads = lambda t: t.reshape(B, L, RW_HEADS, RW_HEAD)
    r, kk, v, w_0, w_1, kd_0, kd_1, b_0, b_1 = [heads(t) for t in prep]
    if s0s is None:
        z = jnp.zeros((B, RW_HEADS, RW_HEAD, RW_HEAD), jnp.float32)
        s0s = (z, z)
    ys, finals = rwkv_scan_both(r, kk, v, [w_0, w_1], [kd_0, kd_1], [b_0, b_1], s0s)
    return [y.reshape(B * L, RW_DIM) for y in ys], finals


def _segment_ones(n, seg):
    h = np.arange(n) // seg
    return jnp.asarray(h[:, None] == h[None, :], jnp.bfloat16)


def _mla_prep_kernel(x_ref, cos_ref, sin_ref, qnorm_ref, qup_ref, kvnorm_ref, wk_ref, wv_ref, place_ref, qn_ref, kn_ref,
                     ones_ref, q_out, k_out, v_out, ckv_out, *, rope, norm_kv):
    bf16 = jnp.bfloat16
    f32 = jnp.float32
    x = x_ref[...]
    qc = x[:, :Q_LORA]
    kvc = x[:, Q_LORA:Q_LORA + KV_LORA]
    kpe = x[:, Q_LORA + KV_LORA:]
    rms = lambda t, g: t * lax.rsqrt(jnp.mean(t * t, axis=-1, keepdims=True) + EPS) * g
    ones_seg = ones_ref[...]
    head_norm = lambda t, g: t * lax.rsqrt(_head_sums(t * t, ones_seg) * (1.0 / QK_HEAD) + EPS) * g
    q = jnp.dot(rms(qc, qnorm_ref[...]).astype(bf16), qup_ref[...], preferred_element_type=f32)
    ckv = rms(kvc, kvnorm_ref[...]) if norm_kv else kvc
    ckv_out[...] = ckv
    c16 = ckv.astype(bf16)
    kpe_hi, kpe_lo = _split_bf16(kpe)
    k = (jnp.dot(c16, wk_ref[...], preferred_element_type=f32)
         + jnp.dot(kpe_hi, place_ref[...], preferred_element_type=f32)
         + jnp.dot(kpe_lo, place_ref[...], preferred_element_type=f32))
    v_out[...] = jnp.dot(c16, wv_ref[...], preferred_element_type=f32)
    q = head_norm(q, qn_ref[...])
    k = head_norm(k, kn_ref[...])
    if rope:
        n = q.shape[1]
        even = lax.broadcasted_iota(jnp.int32, (1, n), 1) % 2 == 0
        swap = lambda t: jnp.where(even, pltpu.roll(t, n - 1, 1), pltpu.roll(t, 1, 1))
        cos = cos_ref[...]
        sin = sin_ref[...]
        q = q * cos + swap(q) * sin
        k = k * cos + swap(k) * sin
    q_out[...] = q
    k_out[...] = k


def mla_prep(x, L, rope, weights, norm_kv=True):
    q_norm, q_up, kv_norm, kv_up, qn, kn = weights
    T, C = x.shape
    tm = min(ROW_TILE, L)
    nq = MLA_HEADS * QK_HEAD
    kv4 = kv_up.reshape(KV_LORA, MLA_HEADS, QK_NOPE + V_HEAD)
    wk = jnp.pad(kv4[:, :, :QK_NOPE], ((0, 0), (0, 0), (0, QK_ROPE))).reshape(KV_LORA, nq).astype(jnp.bfloat16)
    wv = kv4[:, :, QK_NOPE:].reshape(KV_LORA, MLA_HEADS * V_HEAD).astype(jnp.bfloat16)
    slot = np.arange(nq) % QK_HEAD - QK_NOPE
    place = jnp.asarray(np.arange(QK_ROPE)[:, None] == slot[None, :], jnp.bfloat16)
    if rope is not None:
        cos, sin = rope
        pair = np.maximum(slot, 0) // 2
        sign = np.where(slot % 2 == 0, -1.0, 1.0).astype(np.float32)
        cos_f = jnp.where(slot >= 0, cos[:, pair], 1.0)
        sin_f = jnp.where(slot >= 0, sin[:, pair] * sign, 0.0)
    else:
        cos_f = sin_f = jnp.zeros((tm, nq), jnp.float32)
    tiles_per_seq = max(L // tm, 1)
    rope_blk = pl.BlockSpec((tm, nq), (lambda i: (i % tiles_per_seq, 0)) if rope is not None else (lambda i: (0, 0)))
    row2 = lambda a: a.reshape(1, -1)
    whole = lambda a: pl.BlockSpec(a.shape, lambda i: (0,) * a.ndim)
    consts = (row2(q_norm), q_up.astype(jnp.bfloat16), row2(kv_norm), wk, wv, place,
              row2(jnp.tile(qn, MLA_HEADS)), row2(jnp.tile(kn, MLA_HEADS)), _segment_ones(nq, QK_HEAD))
    out = lambda n: pl.BlockSpec((tm, n), lambda i: (i, 0))
    sd = lambda n: jax.ShapeDtypeStruct((T, n), jnp.float32)
    return pl.pallas_call(
        partial(_mla_prep_kernel, rope=rope is not None, norm_kv=norm_kv), grid=(T // tm,),
        in_specs=[pl.BlockSpec((tm, C), lambda i: (i, 0)), rope_blk, rope_blk] + [whole(a) for a in consts],
        out_specs=[out(nq), out(nq), out(MLA_HEADS * V_HEAD), out(KV_LORA)],
        out_shape=[sd(nq), sd(nq), sd(MLA_HEADS * V_HEAD), sd(KV_LORA)],
        compiler_params=pltpu.CompilerParams(dimension_semantics=("arbitrary",), vmem_limit_bytes=VMEM_LIMIT),
        name="mla_prep",
    )(x, cos_f, sin_f, *consts)


def mla_mixer(mla, B, L, ctx, weights):
    q, k, v, ckv = mla_prep(mla, L, None if ctx is None else ctx[2], weights)
    seq = lambda t: t.reshape(B, L, -1)
    kv_ctx = None
    if ctx is not None:
        c_ckv, c_kpe = ctx[0], ctx[1]
        Lc = c_ckv.shape[1]
        xc = jnp.concatenate([jnp.zeros((B, Lc, Q_LORA), jnp.float32), c_ckv, c_kpe], axis=-1).reshape(B * Lc, -1)
        _, kc, vc, _ = mla_prep(xc, Lc, None, weights, norm_kv=False)
        kv_ctx = (kc.reshape(B, Lc, -1), vc.reshape(B, Lc, -1))
    att = attend(seq(q), seq(k), seq(v), kv_ctx)
    return att.reshape(B * L, -1), ckv.reshape(B, L, -1), mla[:, Q_LORA + KV_LORA:].reshape(B, L, -1)


def _hyena_prep_kernel(proj_ref, prev_ref, next_ref, first_ref, last_ref, cw_ref, cb_ref, x0_ref, z_ref):
    D = x0_ref.shape[1]
    x, prev, nxt = _with_neighbours(proj_ref, prev_ref, next_ref, first_ref, last_ref, proj_ref.shape[1])
    u = prev * cw_ref[0:1, :] + x * cw_ref[1:2, :] + nxt * cw_ref[2:3, :] + cb_ref[...]
    x0_ref[...] = u[:, :D]
    z_ref[...] = u[:, D:2 * D] * u[:, 2 * D:]


def hyena_prep(proj, L, conv_w, conv_b, tm):
    T, C = proj.shape
    D = C // 3
    nb_specs, flags = _neighbour_specs(T, C, L, tm)
    out = pl.BlockSpec((tm, D), lambda i: (i, 0))
    return pl.pallas_call(
        _hyena_prep_kernel, grid=(T // tm,),
        in_specs=nb_specs + [pl.BlockSpec((3, C), lambda i: (0, 0)), pl.BlockSpec((1, C), lambda i: (0, 0))],
        out_specs=[out, out],
        out_shape=[jax.ShapeDtypeStruct((T, D), jnp.float32)] * 2,
        compiler_params=pltpu.CompilerParams(dimension_semantics=("arbitrary",), vmem_limit_bytes=VMEM_LIMIT),
        name="hyena_prep",
    )(proj, proj, proj, *flags, conv_w, conv_b[None])


def _hyena_out_kernel(x0_ref, conv_ref, z_ref, bias_ref, w_ref, x_ref, mod_ref, o_ref):
    a = x0_ref[...] * (conv_ref[...] + z_ref[...] * bias_ref[...])
    o_ref[...] = x_ref[...] + mod_ref[0] * jnp.dot(a.astype(jnp.bfloat16), w_ref[...],
                                                   preferred_element_type=jnp.float32)


def hyena_out_residual(x0, conv, zin, bias, w_out, x, gate_t, tm):
    T, D = x.shape
    row = pl.BlockSpec((tm, D), lambda i: (i, 0))
    return pl.pallas_call(
        _hyena_out_kernel, grid=(T // tm,),
        in_specs=[row, row, row, pl.BlockSpec((1, D), lambda i: (0, 0)), pl.BlockSpec((D, D), lambda i: (0, 0)),
                  row, pl.BlockSpec((1, 1, D), lambda i: (i, 0, 0))],
        out_specs=row,
        out_shape=jax.ShapeDtypeStruct((T, D), jnp.float32),
        compiler_params=pltpu.CompilerParams(dimension_semantics=("arbitrary",), vmem_limit_bytes=VMEM_LIMIT),
        name="hyena_out_residual",
    )(x0, conv, zin, bias[None], w_out.astype(jnp.bfloat16), x, gate_t)


def hyena_filters(L, w1, b1, w2, b2, w3, freq):
    f32 = jnp.float32
    u = jnp.arange(2 * L, dtype=jnp.int32)[:, None]
    t = jnp.where(u < L, u, 2 * L - u).astype(f32)
    t_unit = t / (L - 1)
    bands = jnp.linspace(1e-4, HY_BANDS - 1, HY_BANDS, dtype=f32)
    ang = 2.0 * math.pi * t * bands / L
    zpos = jnp.concatenate([t_unit, jnp.cos(ang), -jnp.sin(ang)], axis=-1)
    fr = freq.astype(f32)
    hid = jnp.sin(fr * (_mm(zpos, w1.astype(f32), split=True) + b1.astype(f32)))
    hid = jnp.sin(fr * (_mm(hid, w2.astype(f32), split=True) + b2.astype(f32)))
    filt = _mm(hid, w3.astype(f32), split=True)
    deltas = jnp.linspace(math.log(HY_TARGET) / HY_FAST, math.log(HY_TARGET) / HY_SLOW, D_MODEL, dtype=f32)
    window = jnp.exp(-t_unit * jnp.abs(deltas))
    circ = jnp.where(u < L, filt[:, :D_MODEL], filt[:, D_MODEL:]) * window
    circ = jnp.where(u == L, 0.0, circ)
    return circ / jnp.sum(jnp.abs(circ), axis=0, keepdims=True)


PEER_N = PEER_KEYS * PEER_KEYS
GATE_LANES = 2 * LANES
PEER_ROUTE_TOKENS = 256
PEER_EXPERT_TOKENS = 512
ROW_TILE = 1024
SEQ_TILE = 256


def _top_vals(s, n, with_rank):
    vals = []
    rank = jnp.full(s.shape, float(n), jnp.float32) if with_rank else None
    for a in range(n):
        m = jnp.max(s, axis=0, keepdims=True)
        vals.append(m)
        hit = s == m
        if with_rank:
            rank = jnp.where(hit, float(a), rank)
        s = jnp.where(hit, -jnp.inf, s)
    return vals, rank


def _peer_route_kernel(x_ref, g_ref, sh_ref, sc_ref, wq_hi_ref, wq_lo_ref, k_hi_ref, k_lo_ref,
                       n1_ref, e1_ref, r2_ref, e2_ref, h_ref, s_scr):
    tb = x_ref.shape[0]
    half = PEER_DKEY // 2
    n_tiles = tb // LANES
    h_hi, h_lo = _split_bf16(_rms_mod(x_ref[...], g_ref[...], sh_ref[0], sc_ref[0]))
    h_ref[...] = h_hi
    q_hi, q_lo = _split_bf16(_dot3(h_hi, h_lo, wq_hi_ref[...], wq_lo_ref[...]))
    for hh in range(PEER_HEADS):
        for p in range(2):
            cols = slice((2 * hh + p) * half, (2 * hh + p + 1) * half)
            k_hi = k_hi_ref[hh, p]
            s_scr[hh, p] = (lax.dot_general(k_hi, q_hi[:, cols], _NT, preferred_element_type=jnp.float32)
                            + lax.dot_general(k_lo_ref[hh, p], q_hi[:, cols], _NT, preferred_element_type=jnp.float32)
                            + lax.dot_general(k_hi, q_lo[:, cols], _NT, preferred_element_type=jnp.float32))

    K = PEER_TOPK
    G = SUBLANES

    def tile(it, carry):
        hh = it // n_tiles
        ln = pl.ds(pl.multiple_of((it % n_tiles) * LANES, LANES), LANES)
        s1 = s_scr[hh, 0, :, ln]
        s2 = s_scr[hh, 1, :, ln]
        v1, _ = _top_vals(s1, K, False)
        v2l, r2 = _top_vals(s2, K, True)
        v2 = jnp.concatenate(v2l, axis=0)
        cand = [v1[0] + v2[:G], v1[0] + v2[G:]]
        cand += [v1[a] + v2[:G] for a in range(1, G)]
        cand += [jnp.concatenate(v1[G:], axis=0) + v2[0:1]]
        c = cand
        for k in range(K):
            m = c[0]
            for ci in c[1:]:
                m = jnp.maximum(m, ci)
            m = jnp.max(m, axis=0, keepdims=True)
            if k + 1 < K:
                c = [jnp.where(ci == m, -jnp.inf, ci) for ci in c]
        tau = m
        top = v1[0] + v2[0:1]
        keep = [ci >= tau for ci in cand]
        z = jnp.zeros_like(tau)
        for ci, ki in zip(cand, keep):
            z = z + jnp.sum(jnp.where(ki, jnp.exp(ci - top), 0.0), axis=0, keepdims=True)
        cnt = [jnp.sum(jnp.where(ki, 1.0, 0.0), axis=0, keepdims=True) for ki in keep[:G + 1]]
        tail = jnp.where(keep[G + 1], 1.0, 0.0)
        n_a = [cnt[0] + cnt[1]] + cnt[2:] + [tail[a:a + 1] for a in range(G)]
        n1 = jnp.zeros_like(s1)
        for a in range(K):
            n1 = jnp.where(s1 == v1[a], n_a[a], n1)
        n1_ref[hh, :, ln] = n1
        e1_ref[hh, :, ln] = jnp.exp(s1 - v1[0]) / z
        r2_ref[hh, :, ln] = r2.astype(jnp.bfloat16)
        e2_ref[hh, :, ln] = jnp.exp(s2 - v2[0:1]).astype(jnp.bfloat16)
        return carry

    lax.fori_loop(0, PEER_HEADS * n_tiles, tile, 0, unroll=4)


def peer_route(x, g, shift_t, scale_t, wq_hi, wq_lo, k_hi, k_lo, tb=PEER_ROUTE_TOKENS):
    T = x.shape[0]
    assert T % tb == 0 and tb % LANES == 0
    nh = PEER_HEADS
    blk = pl.BlockSpec((nh, PEER_KEYS, tb), lambda t: (0, 0, t))
    sd = lambda dt: jax.ShapeDtypeStruct((nh, PEER_KEYS, T), dt)
    whole = lambda a: pl.BlockSpec(a.shape, lambda t: (0,) * a.ndim)
    mod = pl.BlockSpec((1, 1, D_MODEL), lambda t: (t, 0, 0))
    return pl.pallas_call(
        _peer_route_kernel,
        grid=(T // tb,),
        in_specs=[pl.BlockSpec((tb, D_MODEL), lambda t: (t, 0)), whole(g), mod, mod,
                  whole(wq_hi), whole(wq_lo), whole(k_hi), whole(k_lo)],
        out_specs=[blk, blk, blk, blk, pl.BlockSpec((tb, D_MODEL), lambda t: (t, 0))],
        out_shape=[sd(jnp.float32), sd(jnp.float32), sd(jnp.bfloat16), sd(jnp.bfloat16),
                   jax.ShapeDtypeStruct((T, D_MODEL), jnp.bfloat16)],
        scratch_shapes=[pltpu.VMEM((nh, 2, PEER_KEYS, tb), jnp.float32)],
        compiler_params=pltpu.CompilerParams(dimension_semantics=("arbitrary",),
                                             vmem_limit_bytes=VMEM_LIMIT),
        name="peer_route",
    )(x, g, shift_t, scale_t, wq_hi, wq_lo, k_hi, k_lo)


def _gelu_tanh(x):
    hx = 0.5 * x
    return hx * jnp.tanh(x * (x * x * (0.7978845608028654 * 0.044715) + 0.7978845608028654)) + hx


def _peer_expert_kernel(h_ref, u_ref, vt_even_ref, vt_prev_ref, vt_last_ref, n1_ref, e1_ref, r2_ref, e2_ref,
                        x_ref, gate_ref, o_ref, acc_ref, a0_scr, a1_scr, w0_scr, w1_scr):
    c = pl.program_id(1)
    ec, tb = a0_scr.shape
    n_i = ec // PEER_KEYS
    bf16 = jnp.bfloat16
    f32 = jnp.float32

    @pl.when(c == 0)
    def _():
        acc_ref[...] = jnp.zeros_like(acc_ref)
        w1_scr[...] = jnp.zeros_like(w1_scr)

    def gate_times_act(a_scr, w_scr, chunk):
        igrp = pl.ds(pl.multiple_of(chunk * n_i, SUBLANES), SUBLANES)
        for lt in range(tb // GATE_LANES):
            ln = slice(lt * GATE_LANES, (lt + 1) * GATE_LANES)
            for ii in range(n_i):
                rows = slice(ii * PEER_KEYS, (ii + 1) * PEER_KEYS)
                g = jnp.zeros((PEER_KEYS, GATE_LANES), bf16)
                for hh in range(PEER_HEADS):
                    n1 = jnp.broadcast_to(n1_ref[hh, igrp, ln][ii:ii + 1], (PEER_KEYS, GATE_LANES)).astype(bf16)
                    e1 = jnp.broadcast_to(e1_ref[hh, igrp, ln][ii:ii + 1], (PEER_KEYS, GATE_LANES)).astype(bf16)
                    g = g + jnp.where(r2_ref[hh, :, ln] < n1, e2_ref[hh, :, ln] * e1, jnp.zeros((), bf16))
                w_scr[rows, ln] = g * _gelu_tanh(a_scr[rows, ln])

    h = h_ref[...]
    a0_scr[...] = lax.dot_general(u_ref[:ec], h, _NT, preferred_element_type=f32).astype(bf16)
    acc_ref[...] += jnp.dot(vt_prev_ref[...], w1_scr[...], preferred_element_type=f32)
    gate_times_act(a0_scr, w0_scr, 2 * c)
    a1_scr[...] = lax.dot_general(u_ref[ec:], h, _NT, preferred_element_type=f32).astype(bf16)
    acc_ref[...] += jnp.dot(vt_even_ref[...], w0_scr[...], preferred_element_type=f32)
    gate_times_act(a1_scr, w1_scr, 2 * c + 1)

    @pl.when(c == pl.num_programs(1) - 1)
    def _():
        y = acc_ref[...] + jnp.dot(vt_last_ref[...], w1_scr[...], preferred_element_type=f32)
        o_ref[...] = x_ref[...] + gate_ref[0] * y.T


def peer_experts(h_bf16, u_bf16, vt_bf16, n1, e1, r2, e2, x, gate_t, tb=PEER_EXPERT_TOKENS, ec=SUBLANES * PEER_KEYS):
    T = h_bf16.shape[0]
    assert T % tb == 0 and tb % GATE_LANES == 0 and PEER_N % (2 * ec) == 0
    n_steps = PEER_N // (2 * ec)
    rblk = pl.BlockSpec((PEER_HEADS, PEER_KEYS, tb), lambda t, c: (0, 0, t))
    vt_blk = lambda chunk_of: pl.BlockSpec((D_MODEL, ec), lambda t, c: (0, chunk_of(c)))
    return pl.pallas_call(
        _peer_expert_kernel,
        grid=(T // tb, n_steps),
        in_specs=[pl.BlockSpec((tb, D_MODEL), lambda t, c: (t, 0)),
                  pl.BlockSpec((2 * ec, D_MODEL), lambda t, c: (c, 0)),
                  vt_blk(lambda c: 2 * c),
                  vt_blk(lambda c: jnp.maximum(2 * c - 1, 0)),
                  vt_blk(lambda c: 2 * n_steps - 1),
                  rblk, rblk, rblk, rblk,
                  pl.BlockSpec((tb, D_MODEL), lambda t, c: (t, 0)),
                  pl.BlockSpec((1, 1, D_MODEL), lambda t, c: (t, 0, 0))],
        out_specs=pl.BlockSpec((tb, D_MODEL), lambda t, c: (t, 0)),
        out_shape=jax.ShapeDtypeStruct((T, D_MODEL), jnp.float32),
        scratch_shapes=[pltpu.VMEM((D_MODEL, tb), jnp.float32)] + [pltpu.VMEM((ec, tb), jnp.bfloat16)] * 4,
        compiler_params=pltpu.CompilerParams(dimension_semantics=("arbitrary", "arbitrary"),
                                             vmem_limit_bytes=VMEM_LIMIT),
        name="peer_experts",
    )(h_bf16, u_bf16, vt_bf16, vt_bf16, vt_bf16, n1, e1, r2, e2, x, gate_t)


def _cast_table_kernel(x_ref, o_ref, *, transpose):
    x = x_ref[0]
    o_ref[...] = (x.T if transpose else x).astype(o_ref.dtype)


def cast_table(tabs, li, transpose, rows=1024):
    _, N, D = tabs.shape
    out_spec = pl.BlockSpec((D, rows), lambda i: (0, i)) if transpose else pl.BlockSpec((rows, D), lambda i: (i, 0))
    return pl.pallas_call(
        partial(_cast_table_kernel, transpose=transpose), grid=(N // rows,),
        in_specs=[pl.BlockSpec((1, rows, D), lambda i: (li, i, 0))],
        out_specs=out_spec,
        out_shape=jax.ShapeDtypeStruct((D, N) if transpose else (N, D), jnp.bfloat16),
        compiler_params=pltpu.CompilerParams(dimension_semantics=("arbitrary",), vmem_limit_bytes=VMEM_LIMIT),
        name="cast_table",
    )(tabs)


def peer_weights(w_q, sub_keys, u_tabs, v_tabs, li):
    return (_split_bf16(w_q[li]) + _split_bf16(sub_keys[li])
            + (cast_table(u_tabs, li, transpose=False), cast_table(v_tabs, li, transpose=True)))


def peer_block(x, g, mods_route, mods_expert, weights):
    wq_hi, wq_lo, k_hi, k_lo, u_bf16, vt_bf16 = weights
    n1, e1, r2, e2, h_bf16 = peer_route(x, g[None], mods_route[0], mods_route[1], wq_hi, wq_lo, k_hi, k_lo)
    return peer_experts(h_bf16, u_bf16, vt_bf16, n1, e1, r2, e2, x, mods_expert)


def kernel(x_prompt, x_sample, state_rwkv_fwd, state_rwkv_bwd, cache_mla_ckv, cache_mla_kpe, c, c_ctx,
           norm_g, w_mod, b_mod, ab_w_in, rw_mu, rw_w0, rw_w2, rw_a0, rw_a2, rw_g2, rw_k_k, rw_k_a, rw_r_k,
           rw_lnx_g, rw_lnx_b, mla_q_norm, mla_q_up, mla_kv_norm, mla_kv_up, mla_qn, mla_kn, ab_w_out,
           hy_w_in, hy_b_in, hy_conv_w, hy_conv_b, hy_f_w1, hy_f_b1, hy_f_w2, hy_f_b2, hy_f_w3, hy_f_freq,
           hy_bias, hy_w_out, peer_w_q, peer_keys, peer_u, peer_v):
    rope = axial_rope(x_sample.shape[1])
    D = D_MODEL
    groups = [dict(x=x_prompt.reshape(-1, D), B=x_prompt.shape[0], L=x_prompt.shape[1]),
              dict(x=x_sample.reshape(-1, D), B=x_sample.shape[0], L=x_sample.shape[1])]
    st_f, st_b, st_ckv, st_kpe = [], [], [], []
    for li in range(DEPTH):
        j = li // 2
        groups[0]["mod"] = (_mm(jax.nn.silu(c_ctx)[None], w_mod[li]) + b_mod[li]).reshape(1, 6, D)
        groups[1]["mod"] = (_mm(jax.nn.silu(c), w_mod[li]) + b_mod[li]).reshape(-1, 6, D)
        peer_w = peer_weights(peer_w_q, peer_keys, peer_u, peer_v, li)
        for gi, g in enumerate(groups):
            x, B, L = g["x"], g["B"], g["L"]
            mod = lambda i, tm: tile_mod(g["mod"][:, i], B * L, tm)
            if li % 2 == 0:
                proj = norm_mod_matmul(x, norm_g[li, 0], mod(0, ROW_TILE), mod(1, ROW_TILE), ab_w_in[j],
                                       jnp.zeros((ab_w_in.shape[-1],), jnp.float32), ROW_TILE)
                prep = rwkv_prep(proj, L, rw_mu[j], rw_w0[j], rw_w2[j], rw_a0[j], rw_a2[j], rw_g2[j],
                                 rw_k_k[j], rw_k_a[j], rw_r_k[j], SEQ_TILE)
                s0s = None if gi == 0 else (state_rwkv_fwd[:, j], state_rwkv_bwd[:, j])
                ys, (sf, sb) = rwkv_scans(prep[:9], B, L, s0s)
                ctx = None if gi == 0 else (cache_mla_ckv[:, j], cache_mla_kpe[:, j], rope)
                att, ckv, kpe = mla_mixer(proj[:, RW_IN:], B, L, ctx,
                                          (mla_q_norm[j], mla_q_up[j], mla_kv_norm[j], mla_kv_up[j], mla_qn[j], mla_kn[j]))
                if gi == 0:
                    st_f.append(sf)
                    st_b.append(sb)
                    st_ckv.append(ckv)
                    st_kpe.append(kpe)
                x = rwkv_out_residual(ys[0], ys[1], prep[10], prep[9], att, rw_lnx_g[j], rw_lnx_b[j],
                                      ab_w_out[j], x, mod(2, ROW_TILE), ROW_TILE)
            else:
                proj = norm_mod_matmul(x, norm_g[li, 0], mod(0, ROW_TILE), mod(1, ROW_TILE), hy_w_in[j], hy_b_in[j],
                                       ROW_TILE)
                x0, zin = hyena_prep(proj, L, hy_conv_w[j], hy_conv_b[j], SEQ_TILE)
                circ = hyena_filters(L, hy_f_w1[j], hy_f_b1[j], hy_f_w2[j], hy_f_b2[j], hy_f_w3[j], hy_f_freq[j])
                conv = hyena_longconv(zin.reshape(B, L, D), circ).reshape(B * L, D)
                x = hyena_out_residual(x0, conv, zin, hy_bias[j], hy_w_out[j], x, mod(2, ROW_TILE), ROW_TILE)
            g["x"] = peer_block(x, norm_g[li, 1], (mod(3, PEER_ROUTE_TOKENS), mod(4, PEER_ROUTE_TOKENS)),
                                mod(5, PEER_EXPERT_TOKENS), peer_w)
    xp = groups[0]["x"].reshape(x_prompt.shape)
    xs = groups[1]["x"].reshape(x_sample.shape)
    new_state_rwkv_fwd = jnp.stack(st_f, axis=1).astype(x_prompt.dtype)
    new_state_rwkv_bwd = jnp.stack(st_b, axis=1).astype(x_prompt.dtype)
    new_cache_mla_ckv = jnp.stack(st_ckv, axis=1)
    new_cache_mla_kpe = jnp.stack(st_kpe, axis=1)
    return (xp, xs, new_state_rwkv_fwd, new_state_rwkv_bwd, new_cache_mla_ckv, new_cache_mla_kpe)
```
